```python
import math
import jax, jax.numpy as jnp
from jax import lax
import numpy as np

D_MODEL = 1024
BATCH = 16
SEQ = 4096
DEPTH = 2

GRID_W = 64
CTX_LEN = 256
EPS = 1e-6
N_MOD = 6
N_BRANCH = 3
CHUNK = 128
A_GROUPS = 8
A_GROUP_DIM = 64
A_WIDTH = A_GROUPS * A_GROUP_DIM
B_GROUPS = 8
B_WIDTH = 512
CONV_W = 3
C_HEADS = 8
C_HEAD_DIM = 64
C_V_DIM = 2 * C_HEAD_DIM
C_QK_WIDTH = C_HEADS * 2 * C_HEAD_DIM
C_V_WIDTH = C_HEADS * C_V_DIM
Q_BLOCK = 128
ROPE_BASE = 10000.0
OFF_AU = 0
OFF_AV = OFF_AU + A_WIDTH
OFF_BB = OFF_AV + A_WIDTH
OFF_BC = OFF_BB + B_WIDTH
OFF_BH = OFF_BC + B_WIDTH
OFF_CQ = OFF_BH + B_WIDTH
OFF_CK = OFF_CQ + C_QK_WIDTH
OFF_CV = OFF_CK + C_QK_WIDTH
OFF_GATE = OFF_CV + C_V_WIDTH
IN_COLS = OFF_GATE + N_BRANCH * D_MODEL
SPLITS = [OFF_AV, OFF_BB, OFF_BC, OFF_BH, OFF_CQ, OFF_CK, OFF_CV, OFF_GATE]
D_FF_DENSE = 2816
N_EXPERTS = 8
TOP_K = 2
D_FF_EXPERT = 3584
N_DENSE_LAYERS = (DEPTH + 1) // 2
N_MOE_LAYERS = DEPTH // 2

kernel_name = 'hybrid_gated_diffusion_block'


def rmsnorm(x, g):
    xf = x.astype(jnp.float32)
    y = xf * lax.rsqrt(jnp.mean(xf * xf, axis=-1, keepdims=True) + EPS)
    return (y * g).astype(x.dtype)


def layernorm(x, g, b):
    xf = x.astype(jnp.float32)
    mu = jnp.mean(xf, axis=-1, keepdims=True)
    var = jnp.mean(jnp.square(xf - mu), axis=-1, keepdims=True)
    return ((xf - mu) * lax.rsqrt(var + EPS) * g + b).astype(x.dtype)


def modulate(x, g, shift, scale):
    return rmsnorm(x, g) * (1 + scale) + shift


def axial_rope_tables(n):
    rows = n // GRID_W
    r = jnp.repeat(jnp.arange(rows, dtype=jnp.float32), GRID_W)
    col = jnp.tile(jnp.arange(GRID_W, dtype=jnp.float32), rows)
    quarter = C_HEAD_DIM // 4
    inv = ROPE_BASE ** (-jnp.arange(quarter, dtype=jnp.float32) / quarter)
    ar = r[:, None] * inv
    ac = col[:, None] * inv
    ang = jnp.concatenate([ar, ar, ac, ac], axis=-1)
    return jnp.cos(ang), jnp.sin(ang)


def apply_rope(t, cos, sin):
    ts = t.reshape(t.shape[:-1] + (2, 2, C_HEAD_DIM // 4))
    rot = jnp.stack([-ts[..., 1, :], ts[..., 0, :]], axis=-2).reshape(t.shape)
    cb = cos[None, :, None, None, :]
    sb = sin[None, :, None, None, :]
    return (t * cb + rot * sb).astype(t.dtype)


def qk_heads(t):
    b, n, _ = t.shape
    return t.reshape(b, n, C_HEADS, 2, C_HEAD_DIM)


def v_heads(t):
    b, n, _ = t.shape
    return t.reshape(b, n, C_HEADS, C_V_DIM)


def diff_attention(q, k, v, lam):
    s = jnp.einsum('bqhmd,bkhmd->bhmqk', q, k).astype(jnp.float32) * (C_HEAD_DIM ** -0.5)
    p = jax.nn.softmax(s, axis=-1)
    a = p[:, :, 0] - lam * p[:, :, 1]
    return jnp.einsum('bhqk,bkhe->bqhe', a.astype(v.dtype), v)


def diff_attention_latent(q, k_lat, v_lat, k_ctx, v_ctx, lam):
    k = jnp.concatenate([k_lat, k_ctx], axis=1)
    v = jnp.concatenate([v_lat, v_ctx], axis=1)
    b, n = q.shape[:2]
    qb = jnp.moveaxis(q.reshape((b, n // Q_BLOCK, Q_BLOCK) + q.shape[2:]), 1, 0)
    out = lax.map(lambda qblk: diff_attention(qblk, k, v, lam), qb)
    return jnp.moveaxis(out, 0, 1).reshape(b, n, C_HEADS, C_V_DIM)


def diff_head_out(o, g, lam_init):
    b, n = o.shape[:2]
    return (rmsnorm(o, g) * (1 - lam_init)).reshape(b, n, C_V_WIDTH)


def chunk_spatial_gate(u, v, ln_g, ln_b, ws, bs):
    b, n, _ = v.shape
    vc = layernorm(v, ln_g, ln_b).reshape(b, n // CHUNK, CHUNK, A_GROUPS, A_GROUP_DIM)
    mixed = jnp.einsum('gts,bcsge->bctge', ws, vc) + bs.T[:, :, None]
    return u * mixed.reshape(b, n, A_WIDTH)


def short_gated_conv(bg, cg, h, w):
    z = cg * h
    n = z.shape[1]
    pad = CONV_W // 2
    zp = jnp.pad(z, ((0, 0), (pad, pad), (0, 0)))
    conv = w[0] * zp[:, 0:n]
    for j in range(1, CONV_W):
        conv = conv + w[j] * zp[:, j:j + n]
    return bg * conv


def local_mixers(p, ln_g, ln_b, ws, bs, conv_w):
    ya = chunk_spatial_gate(jax.nn.gelu(p[0]), jax.nn.gelu(p[1]), ln_g, ln_b, ws, bs)
    yb = short_gated_conv(p[2], p[3], p[4], conv_w)
    return ya, yb


def merge_branches(ya, yb, yc, gate_pre, b_gate, w_a_out, w_b_out, w_c_out, w_o):
    ga, gb, gc = jnp.split(jax.nn.sigmoid(gate_pre + b_gate), N_BRANCH, axis=-1)
    y = ga * (ya @ w_a_out) + gb * (yb @ w_b_out) + gc * (yc @ w_c_out)
    return y @ w_o


def swiglu(h, wg, wu, wd):
    return (jax.nn.silu(h @ wg) * (h @ wu)) @ wd


def moe_swiglu(h, w_router, wg, wu, wd):
    logits = (h @ w_router).astype(jnp.float32)
    top_v, top_i = lax.top_k(logits, TOP_K)
    top_w = jax.nn.softmax(top_v, axis=-1)
    comb = jnp.sum(jax.nn.one_hot(top_i, N_EXPERTS, dtype=jnp.float32) * top_w[..., None], axis=-2)
    comb = comb.astype(h.dtype)
    out = jnp.zeros_like(h)
    for e in range(N_EXPERTS):
        out = out + comb[..., e:e + 1] * swiglu(h, wg[e], wu[e], wd[e])
    return out


def channel_mixer(h, l, ff_w_gate, ff_w_up, ff_w_down, moe_w_router, moe_w_gate, moe_w_up, moe_w_down):
    i = l // 2
    if l % 2 == 0:
        return swiglu(h, ff_w_gate[i], ff_w_up[i], ff_w_down[i])
    return moe_swiglu(h, moe_w_router[i], moe_w_gate[i], moe_w_up[i], moe_w_down[i])


def _normal(key, shape, scale):
    return jax.random.normal(key, shape, jnp.float32) * scale


def setup_inputs(seed: int = 0) -> dict:
    key = jax.random.key(seed)
    ks = iter(jax.random.split(key, 32))
    D = D_MODEL
    L = DEPTH
    return {
        'x': _normal(next(ks), (BATCH, SEQ, D), 1.0),
        'c': _normal(next(ks), (BATCH, D), 1.0),
        'ctx': _normal(next(ks), (BATCH, CTX_LEN, D), 1.0),
        'c_ctx': _normal(next(ks), (D,), 1.0),
        'w_ada': _normal(next(ks), (L, D, N_MOD * D), 0.5 * D ** -0.5),
        'b_ada': _normal(next(ks), (L, N_MOD * D), 0.02),
        'norm1_g': 1.0 + _normal(next(ks), (L, D), 0.02),
        'norm2_g': 1.0 + _normal(next(ks), (L, D), 0.02),
        'w_in': _normal(next(ks), (L, D, IN_COLS), D ** -0.5),
        'b_gate': _normal(next(ks), (L, N_BRANCH * D), 0.02),
        'a_ln_g': 1.0 + _normal(next(ks), (L, A_WIDTH), 0.02),
        'a_ln_b': _normal(next(ks), (L, A_WIDTH), 0.02),
        'a_ws': _normal(next(ks), (L, A_GROUPS, CHUNK, CHUNK), CHUNK ** -0.5),
        'a_bs': 1.0 + _normal(next(ks), (L, A_GROUPS, CHUNK), 0.02),
        'b_conv': _normal(next(ks), (L, CONV_W, B_WIDTH), CONV_W ** -0.5),
        'c_lambda': _normal(next(ks), (L, 4, C_HEAD_DIM), 0.1),
        'c_subln_g': 1.0 + _normal(next(ks), (L, C_V_DIM), 0.02),
        'w_a_out': _normal(next(ks), (L, A_WIDTH, D), A_WIDTH ** -0.5),
        'w_b_out': _normal(next(ks), (L, B_WIDTH, D), B_WIDTH ** -0.5),
        'w_c_out': _normal(next(ks), (L, C_V_WIDTH, D), C_V_WIDTH ** -0.5),
        'w_o': _normal(next(ks), (L, D, D), D ** -0.5),
        'ff_w_gate': _normal(next(ks), (N_DENSE_LAYERS, D, D_FF_DENSE), D ** -0.5),
        'ff_w_up': _normal(next(ks), (N_DENSE_LAYERS, D, D_FF_DENSE), D ** -0.5),
        'ff_w_down': _normal(next(ks), (N_DENSE_LAYERS, D_FF_DENSE, D), D_FF_DENSE ** -0.5),
        'moe_w_router': _normal(next(ks), (N_MOE_LAYERS, D, N_EXPERTS), D ** -0.5),
        'moe_w_gate': _normal(next(ks), (N_MOE_LAYERS, N_EXPERTS, D, D_FF_EXPERT), D ** -0.5),
        'moe_w_up': _normal(next(ks), (N_MOE_LAYERS, N_EXPERTS, D, D_FF_EXPERT), D ** -0.5),
        'moe_w_down': _normal(next(ks), (N_MOE_LAYERS, N_EXPERTS, D_FF_EXPERT, D), D_FF_EXPERT ** -0.5),
        'final_norm_g': 1.0 + _normal(next(ks), (D,), 0.02),
    }


def reference(x, c, ctx, c_ctx, w_ada, b_ada, norm1_g, norm2_g, w_in, b_gate, a_ln_g, a_ln_b,
              a_ws, a_bs, b_conv, c_lambda, c_subln_g, w_a_out, w_b_out, w_c_out, w_o,
              ff_w_gate, ff_w_up, ff_w_down, moe_w_router, moe_w_gate, moe_w_up, moe_w_down,
              final_norm_g):
    n = x.shape[1]
    cos, sin = axial_rope_tables(n)
    xc = ctx
    for l in range(DEPTH):
        last = l == DEPTH - 1
        mod = jax.nn.silu(c) @ w_ada[l] + b_ada[l]
        mod_c = jax.nn.silu(c_ctx) @ w_ada[l] + b_ada[l]
        sh1, sc1, g1, sh2, sc2, g2 = jnp.split(mod[:, None, :], N_MOD, axis=-1)
        csh1, csc1, cg1, csh2, csc2, cg2 = jnp.split(mod_c, N_MOD, axis=-1)
        lam_init = 0.8 - 0.6 * math.exp(-0.3 * l)
        lq1, lk1, lq2, lk2 = c_lambda[l]
        lam = jnp.exp(jnp.sum(lq1 * lk1)) - jnp.exp(jnp.sum(lq2 * lk2)) + lam_init

        h = modulate(x, norm1_g[l], sh1, sc1)
        hc = modulate(xc, norm1_g[l], csh1, csc1)
        p = jnp.split(h @ w_in[l], SPLITS, axis=-1)
        if last:
            kv_c = hc @ w_in[l][:, OFF_CK:OFF_GATE]
            kc_raw, vc_raw = jnp.split(kv_c, [C_QK_WIDTH], axis=-1)
        else:
            pc = jnp.split(hc @ w_in[l], SPLITS, axis=-1)
            kc_raw, vc_raw = pc[6], pc[7]
        k_ctx = qk_heads(kc_raw)
        v_ctx = v_heads(vc_raw)

        ya, yb = local_mixers(p, a_ln_g[l], a_ln_b[l], a_ws[l], a_bs[l], b_conv[l])
        q = apply_rope(qk_heads(p[5]), cos, sin)
        k = apply_rope(qk_heads(p[6]), cos, sin)
        yc = diff_attention_latent(q, k, v_heads(p[7]), k_ctx, v_ctx, lam)
        yc = diff_head_out(yc, c_subln_g[l], lam_init)
        x = x + g1 * merge_branches(ya, yb, yc, p[8], b_gate[l], w_a_out[l], w_b_out[l], w_c_out[l], w_o[l])

        x = x + g2 * channel_mixer(modulate(x, norm2_g[l], sh2, sc2), l, ff_w_gate, ff_w_up, ff_w_down,
                                   moe_w_router, moe_w_gate, moe_w_up, moe_w_down)

        if not last:
            ya_c, yb_c = local_mixers(pc, a_ln_g[l], a_ln_b[l], a_ws[l], a_bs[l], b_conv[l])
            yc_c = diff_head_out(diff_attention(qk_heads(pc[5]), k_ctx, v_ctx, lam), c_subln_g[l], lam_init)
            xc = xc + cg1 * merge_branches(ya_c, yb_c, yc_c, pc[8], b_gate[l], w_a_out[l], w_b_out[l],
                                           w_c_out[l], w_o[l])
            xc = xc + cg2 * channel_mixer(modulate(xc, norm2_g[l], csh2, csc2), l, ff_w_gate, ff_w_up,
                                          ff_w_down, moe_w_router, moe_w_gate, moe_w_up, moe_w_down)
    return rmsnorm(x, final_norm_g)
```

```python
import functools
import math

import jax
import jax.numpy as jnp
from jax import lax
from jax.experimental import pallas as pl
from jax.experimental.pallas import tpu as pltpu

F32 = jnp.float32
BF16 = jnp.bfloat16

EPS = 1e-6
GRID_W = 64
N_MOD = 6
N_BRANCH = 3
CHUNK = 128
A_GROUPS = 8
A_GROUP_DIM = 64
A_WIDTH = A_GROUPS * A_GROUP_DIM
B_WIDTH = 512
CONV_W = 3
C_HEADS = 8
C_HEAD_DIM = 64
C_V_DIM = 2 * C_HEAD_DIM
ROPE_BASE = 10000.0
TOP_K = 2
LOG2E = 1.4426950408889634

LANES = 128
BF16_SUBLANES = 16
V7X_VMEM_LIMIT = 56 * 1024 * 1024

COL_BLOCK = 512


def _cparams(sem):
    return pltpu.CompilerParams(dimension_semantics=sem, vmem_limit_bytes=V7X_VMEM_LIMIT)


def _resident(shape, index_map):
    return pl.BlockSpec(shape, index_map, pipeline_mode=pl.Buffered(1))


def _silu(v):
    return v * jax.nn.sigmoid(v)


def _modulate(x, g, sh, sc):
    ms = jnp.mean(x * x, axis=-1, keepdims=True)
    y = x * lax.rsqrt(ms + EPS)
    return (y * g) * (1.0 + sc) + sh


def _rmsnorm(x, g):
    ms = jnp.mean(x * x, axis=-1, keepdims=True)
    return x * lax.rsqrt(ms + EPS) * g


def _ada_kernel(c_ref, w_ref, b_ref, o_ref):
    s = _silu(c_ref[...])
    o_ref[0] = jnp.dot(s, w_ref[0], preferred_element_type=F32,
                       precision=lax.Precision.HIGHEST) + b_ref[0]


def _ada_call(cc, w_ada, b_ada):
    depth, d, cols = w_ada.shape
    rows = cc.shape[0]
    tn = 1536
    return pl.pallas_call(
        _ada_kernel,
        grid=(depth, cols // tn),
        in_specs=[
            pl.BlockSpec((rows, d), lambda l, j: (0, 0)),
            pl.BlockSpec((1, d, tn), lambda l, j: (l, 0, j)),
            pl.BlockSpec((1, 1, tn), lambda l, j: (l, 0, j)),
        ],
        out_specs=pl.BlockSpec((1, rows, tn), lambda l, j: (l, 0, j)),
        out_shape=jax.ShapeDtypeStruct((depth, rows, cols), F32),
        compiler_params=_cparams(("parallel", "parallel")),
        name="ada_proj",
    )(cc, w_ada, b_ada.reshape(depth, 1, cols))


def _mod_kernel(x_ref, g_ref, sh_ref, sc_ref, o_ref):
    o_ref[0] = _modulate(x_ref[0], g_ref[...], sh_ref[0], sc_ref[0]).astype(BF16)


def _mod_call(x, g, sh, sc):
    nb, n, d = x.shape
    tm = min(1024, n)
    vec = pl.BlockSpec((1, 1, d), lambda b, i: (b, 0, 0))
    return pl.pallas_call(
        _mod_kernel,
        grid=(nb, n // tm),
        in_specs=[pl.BlockSpec((1, tm, d), lambda b, i: (b, i, 0)),
                  pl.BlockSpec((1, d), lambda b, i: (0, 0)), vec, vec],
        out_specs=pl.BlockSpec((1, tm, d), lambda b, i: (b, i, 0)),
        out_shape=jax.ShapeDtypeStruct((nb, n, d), BF16),
        compiler_params=_cparams(("parallel", "parallel")),
        name="modulate",
    )(x, g.reshape(1, d), sh, sc)


class _Cols:
    def __init__(self, d):
        self.au = 0
        self.av = self.au + A_WIDTH // COL_BLOCK
        self.bb = self.av + A_WIDTH // COL_BLOCK
        self.bc = self.bb + B_WIDTH // COL_BLOCK
        self.bh = self.bc + B_WIDTH // COL_BLOCK
        self.cq = self.bh + B_WIDTH // COL_BLOCK
        qk = C_HEADS * 2 * C_HEAD_DIM // COL_BLOCK
        self.ck = self.cq + qk
        self.cv = self.ck + qk
        self.gate = self.cv + C_HEADS * C_V_DIM // COL_BLOCK
        self.end = self.gate + N_BRANCH * d // COL_BLOCK

    def out(self, jw):
        return (jw - self.gate) % self.end


def _in_kernel(h_ref, w_ref, bg_ref, cos_ref, sa_ref, sb_ref, o_ref, *, cols, j0, rope, qscale):
    j = pl.program_id(2) + j0
    acc = jnp.dot(h_ref[0], w_ref[...], preferred_element_type=F32)

    @pl.when(j < cols.bb)
    def _():
        o_ref[0] = jax.nn.gelu(acc).astype(BF16)

    @pl.when(((j >= cols.bb) & (j < cols.cq)) | ((j >= cols.cv) & (j < cols.gate)))
    def _():
        o_ref[0] = acc.astype(BF16)

    @pl.when((j >= cols.cq) & (j < cols.cv))
    def _():
        scale = jnp.where(j < cols.ck, qscale, 1.0).astype(F32)
        if rope:
            cos, sa, sb = cos_ref[...], sa_ref[...], sb_ref[...]
            for s in range(COL_BLOCK // LANES):
                t = acc[:, s * LANES:(s + 1) * LANES]
                r = (t * cos + pltpu.roll(t, LANES - 16, 1) * sa + pltpu.roll(t, 16, 1) * sb)
                o_ref[0, :, s * LANES:(s + 1) * LANES] = (r * scale).astype(BF16)
        else:
            o_ref[0] = (acc * scale).astype(BF16)

    @pl.when(j >= cols.gate)
    def _():
        o_ref[0] = jax.nn.sigmoid(acc + bg_ref[...]).astype(BF16)


def _in_call(h, w, b_gate, tables, *, j0, nj, rope):
    nb, n, d = h.shape
    cols = _Cols(d)
    tm = min(2048, n)
    full = (j0 == 0) and (nj == cols.end)
    if rope:
        cos, sa, sb = tables
    else:
        cos = sa = sb = jnp.zeros((tm, LANES), F32)
    tab = pl.BlockSpec((tm, LANES), lambda b, i, j: (i if rope else 0, 0))

    def out_idx(b, i, j):
        if full:
            return (b, i, jnp.where(j >= cols.gate, j - cols.gate, j + cols.end - cols.gate))
        return (b, i, j)
    kern = functools.partial(_in_kernel, cols=cols, j0=j0, rope=rope,
                             qscale=C_HEAD_DIM ** -0.5 * LOG2E)
    return pl.pallas_call(
        kern,
        grid=(nb, n // tm, nj),
        in_specs=[
            pl.BlockSpec((1, tm, d), lambda b, i, j: (b, i, 0)),
            pl.BlockSpec((d, COL_BLOCK), lambda b, i, j: (0, j + j0)),
            pl.BlockSpec((1, COL_BLOCK), lambda b, i, j: (0, jnp.maximum(j + j0 - cols.gate, 0))),
            tab, tab, tab,
        ],
        out_specs=pl.BlockSpec((1, tm, COL_BLOCK), out_idx),
        out_shape=jax.ShapeDtypeStruct((nb, n, nj * COL_BLOCK), BF16),
        compiler_params=_cparams(("parallel", "parallel", "arbitrary")),
        name="in_proj",
    )(h, w, b_gate.reshape(1, -1), cos, sa, sb)


def _rope_tables(n):
    rows = n // GRID_W
    r = jnp.repeat(jnp.arange(rows, dtype=F32), GRID_W)
    col = jnp.tile(jnp.arange(GRID_W, dtype=F32), rows)
    quarter = C_HEAD_DIM // 4
    inv = ROPE_BASE ** (-jnp.arange(quarter, dtype=F32) / quarter)
    ar = r[:, None] * inv
    ac = col[:, None] * inv
    ang = jnp.concatenate([ar, ar, ac, ac], axis=-1)
    ang = jnp.tile(ang, (1, LANES // C_HEAD_DIM))
    cos, sin = jnp.cos(ang), jnp.sin(ang)
    first_of_pair = (jnp.arange(LANES) // quarter) % 2 == 0
    sa = jnp.where(first_of_pair, -sin, 0.0)
    sb = jnp.where(first_of_pair, 0.0, sin)
    return cos, sa, sb


def _local_kernel(u_ref, v_ref, bg_ref, cg_ref, hh_ref, cgp_ref, hhp_ref, cgn_ref, hhn_ref,
                  lng_ref, lnb_ref, ws_ref, bias_ref, cw_ref, ya_ref, yb_ref, *, tm, seq):
    i = pl.program_id(1)
    lane = lax.broadcasted_iota(jnp.int32, (CHUNK, LANES), 1)
    lo = lane < A_GROUP_DIM
    for c in range(tm // CHUNK):
        rows = pl.ds(c * CHUNK, CHUNK)
        v = v_ref[0, rows, :].astype(F32)
        mu = jnp.mean(v, axis=-1, keepdims=True)
        var = jnp.mean(jnp.square(v - mu), axis=-1, keepdims=True)
        vn = ((v - mu) * lax.rsqrt(var + EPS) * lng_ref[...] + lnb_ref[...]).astype(BF16)
        for k in range(A_WIDTH // LANES):
            blk = vn[:, k * LANES:(k + 1) * LANES]
            zero = jnp.zeros_like(blk)
            mixed = (jnp.dot(ws_ref[2 * k], jnp.where(lo, blk, zero), preferred_element_type=F32)
                     + jnp.dot(ws_ref[2 * k + 1], jnp.where(lo, zero, blk), preferred_element_type=F32)
                     + bias_ref[:, k * LANES:(k + 1) * LANES])
            u = u_ref[0, rows, k * LANES:(k + 1) * LANES].astype(F32)
            ya_ref[0, rows, k * LANES:(k + 1) * LANES] = (u * mixed).astype(BF16)

    z = cg_ref[0].astype(F32) * hh_ref[0].astype(F32)
    last = BF16_SUBLANES - 1
    z_prev = cgp_ref[0, last:last + 1, :].astype(F32) * hhp_ref[0, last:last + 1, :].astype(F32)
    z_next = cgn_ref[0, 0:1, :].astype(F32) * hhn_ref[0, 0:1, :].astype(F32)
    row = lax.broadcasted_iota(jnp.int32, (tm, 1), 0)
    pos = (i * tm) % seq + row
    zm1 = jnp.where(row == 0, z_prev, pltpu.roll(z, 1, 0))
    zm1 = jnp.where(pos == 0, 0.0, zm1)
    zp1 = jnp.where(row == tm - 1, z_next, pltpu.roll(z, tm - 1, 0))
    zp1 = jnp.where(pos == seq - 1, 0.0, zp1)
    conv = cw_ref[0:1, :] * zm1 + cw_ref[1:2, :] * z + cw_ref[2:3, :] * zp1
    yb_ref[0] = (bg_ref[0].astype(F32) * conv).astype(BF16)


def _local_call(p, pos, ln_g, ln_b, ws, bs, conv_w, *, seq):
    nb, n, _ = p.shape
    au, av, bb, bc, bh = pos
    tm = min(512, seq)
    hb = tm // BF16_SUBLANES
    nhb = n // BF16_SUBLANES

    def main(cb):
        return pl.BlockSpec((1, tm, COL_BLOCK), lambda b, i: (b, i, cb))

    def prev(cb):
        return pl.BlockSpec((1, BF16_SUBLANES, COL_BLOCK),
                            lambda b, i: (b, jnp.maximum(i * hb - 1, 0), cb))

    def nxt(cb):
        return pl.BlockSpec((1, BF16_SUBLANES, COL_BLOCK),
                            lambda b, i: (b, jnp.minimum((i + 1) * hb, nhb - 1), cb))

    bias = jnp.repeat(bs.T, A_GROUP_DIM, axis=1)
    out = jax.ShapeDtypeStruct((nb, n, A_WIDTH), BF16)
    kern = functools.partial(_local_kernel, tm=tm, seq=seq)
    return pl.pallas_call(
        kern,
        grid=(nb, n // tm),
        in_specs=[main(au), main(av), main(bb), main(bc), main(bh),
                  prev(bc), prev(bh), nxt(bc), nxt(bh),
                  pl.BlockSpec((1, A_WIDTH), lambda b, i: (0, 0)),
                  pl.BlockSpec((1, A_WIDTH), lambda b, i: (0, 0)),
                  pl.BlockSpec((A_GROUPS, CHUNK, CHUNK), lambda b, i: (0, 0, 0)),
                  pl.BlockSpec((CHUNK, A_WIDTH), lambda b, i: (0, 0)),
                  pl.BlockSpec((CONV_W, B_WIDTH), lambda b, i: (0, 0))],
        out_specs=[pl.BlockSpec((1, tm, A_WIDTH), lambda b, i: (b, i, 0)),
                   pl.BlockSpec((1, tm, B_WIDTH), lambda b, i: (b, i, 0))],
        out_shape=[out, out],
        compiler_params=_cparams(("parallel", "parallel")),
        name="local_mixers",
    )(p, p, p, p, p, p, p, p, p, ln_g.reshape(1, -1), ln_b.reshape(1, -1), ws.astype(BF16),
      bias, conv_w)


def _attn_kernel(*refs, tq, ck, sizes, lam_init):
    cl_ref, g_ref, q_ref = refs[:3]
    kv_refs = refs[3:3 + 2 * len(sizes)]
    o_ref = refs[-1]

    q = q_ref[0]
    lane = lax.broadcasted_iota(jnp.int32, (tq, LANES), 1)
    first = lane < C_HEAD_DIM
    zero = jnp.zeros_like(q)
    qq = jnp.concatenate([jnp.where(first, q, zero), jnp.where(first, zero, q)], axis=0)

    def step(k, v, carry):
        m, l, acc = carry
        s = lax.dot_general(qq, k, (((1,), (1,)), ((), ())), preferred_element_type=F32)
        m_new = jnp.maximum(m, jnp.max(s, axis=-1, keepdims=True))
        alpha = jnp.exp2(m - m_new)
        p = jnp.exp2(s - m_new)
        l = alpha * l + jnp.sum(p, axis=-1, keepdims=True)
        acc = alpha * acc + jnp.dot(p.astype(BF16), v, preferred_element_type=F32)
        return m_new, l, acc

    carry = (jnp.full((2 * tq, 1), -jnp.inf, F32), jnp.zeros((2 * tq, 1), F32),
             jnp.zeros((2 * tq, C_V_DIM), F32))
    for src, nk in enumerate(sizes):
        k_ref, v_ref = kv_refs[2 * src], kv_refs[2 * src + 1]
        c = min(ck, nk)
        if nk // c == 1:
            carry = step(k_ref[0], v_ref[0], carry)
        else:
            def body(t, carry, k_ref=k_ref, v_ref=v_ref, c=c):
                rows = pl.ds(pl.multiple_of(t * c, c), c)
                return step(k_ref[0, rows, :], v_ref[0, rows, :], carry)
            carry = lax.fori_loop(0, nk // c, body, carry)
    m, l, acc = carry
    o2 = acc / l
    cl = cl_ref[...]
    lam = (jnp.exp(jnp.sum(cl[0:1] * cl[1:2], axis=-1, keepdims=True))
           - jnp.exp(jnp.sum(cl[2:3] * cl[3:4], axis=-1, keepdims=True)) + lam_init)
    o = o2[:tq] - lam * o2[tq:]
    o_ref[0] = (_rmsnorm(o, g_ref[...]) * (1.0 - lam_init)).astype(BF16)


def _attn_call(q_arr, q_cb, sources, c_lambda, subln_g, *, lam_init):
    nb, n, _ = q_arr.shape
    tq = min(256, n)
    sizes = tuple(a.shape[1] for a, _, _ in sources)
    in_specs = [
        pl.BlockSpec((4, C_HEAD_DIM), lambda b, h, i: (0, 0)),
        pl.BlockSpec((1, C_V_DIM), lambda b, h, i: (0, 0)),
        pl.BlockSpec((1, tq, LANES), lambda b, h, i: (b, i, q_cb + h)),
    ]
    args = [c_lambda, subln_g.reshape(1, -1), q_arr]
    for arr, kcb, vcb in sources:
        nk = arr.shape[1]
        in_specs.append(pl.BlockSpec((1, nk, LANES), lambda b, h, i, kcb=kcb: (b, 0, kcb + h)))
        in_specs.append(pl.BlockSpec((1, nk, LANES), lambda b, h, i, vcb=vcb: (b, 0, vcb + h)))
        args += [arr, arr]
    kern = functools.partial(_attn_kernel, tq=tq, ck=512, sizes=sizes, lam_init=lam_init)
    return pl.pallas_call(
        kern,
        grid=(nb, C_HEADS, n // tq),
        in_specs=in_specs,
        out_specs=pl.BlockSpec((1, tq, C_V_DIM), lambda b, h, i: (b, i, h)),
        out_shape=jax.ShapeDtypeStruct((nb, n, C_HEADS * C_V_DIM), BF16),
        compiler_params=_cparams(("parallel", "parallel", "arbitrary")),
        name="diff_attention",
    )(*args)


def _merge_kernel(x_ref, ya_ref, yb_ref, yc_ref, ga_ref, gb_ref, gc_ref, wa_ref, wb_ref, wc_ref,
                  wo_ref, g1_ref, n2_ref, sh2_ref, sc2_ref, xo_ref, ho_ref):
    a = jnp.dot(ya_ref[0], wa_ref[...], preferred_element_type=F32)
    b = jnp.dot(yb_ref[0], wb_ref[...], preferred_element_type=F32)
    c = jnp.dot(yc_ref[0], wc_ref[...], preferred_element_type=F32)
    y = (ga_ref[0].astype(F32) * a + gb_ref[0].astype(F32) * b + gc_ref[0].astype(F32) * c)
    m = jnp.dot(y.astype(BF16), wo_ref[...], preferred_element_type=F32)
    xn = x_ref[0] + g1_ref[0] * m
    xo_ref[0] = xn
    ho_ref[0] = _modulate(xn, n2_ref[...], sh2_ref[0], sc2_ref[0]).astype(BF16)


def _merge_call(x, ya, yb, yc, p, gate_cb, wa, wb, wc, wo, g1, n2g, sh2, sc2):
    nb, n, d = x.shape
    tm = min(512, n)

    def tok(w, cb=0):
        return pl.BlockSpec((1, tm, w), lambda b, i: (b, i, cb))

    vec = pl.BlockSpec((1, 1, d), lambda b, i: (b, 0, 0))
    return pl.pallas_call(
        _merge_kernel,
        grid=(nb, n // tm),
        in_specs=[tok(d), tok(A_WIDTH), tok(B_WIDTH), tok(d),
                  tok(d, gate_cb), tok(d, gate_cb + 1), tok(d, gate_cb + 2),
                  _resident(wa.shape, lambda b, i: (0, 0)), _resident(wb.shape, lambda b, i: (0, 0)),
                  _resident(wc.shape, lambda b, i: (0, 0)), _resident(wo.shape, lambda b, i: (0, 0)),
                  vec, pl.BlockSpec((1, d), lambda b, i: (0, 0)), vec, vec],
        out_specs=[tok(d), tok(d)],
        out_shape=[jax.ShapeDtypeStruct((nb, n, d), F32), jax.ShapeDtypeStruct((nb, n, d), BF16)],
        compiler_params=_cparams(("parallel", "parallel")),
        name="merge_branches",
    )(x, ya, yb, yc, p, p, p, wa, wb, wc, wo, g1, n2g.reshape(1, d), sh2, sc2)


def _ff_chunks(width, chunk=512):
    out, c0 = [], 0
    while c0 < width:
        cw = min(chunk, width - c0)
        out.append((c0, cw))
        c0 += cw
    return out


def _finish(xn, final, ng_ref, sh_ref, sc_ref, out_refs):
    if final:
        out_refs[0][0] = _rmsnorm(xn, ng_ref[...])
    else:
        out_refs[0][0] = xn
        out_refs[1][0] = _modulate(xn, ng_ref[...], sh_ref[0], sc_ref[0]).astype(BF16)


def _ffn_kernel(x_ref, h_ref, wg_ref, wu_ref, wd_ref, g2_ref, ng_ref, sh_ref, sc_ref, *out_refs,
                final):
    h = h_ref[0]
    acc = None
    for c0, cw in _ff_chunks(wg_ref.shape[1]):
        a = jnp.dot(h, wg_ref[:, c0:c0 + cw], preferred_element_type=F32)
        b = jnp.dot(h, wu_ref[:, c0:c0 + cw], preferred_element_type=F32)
        t = (_silu(a) * b).astype(BF16)
        part = jnp.dot(t, wd_ref[c0:c0 + cw, :], preferred_element_type=F32)
        acc = part if acc is None else acc + part
    xn = x_ref[0] + g2_ref[0] * acc
    _finish(xn, final, ng_ref, sh_ref, sc_ref, out_refs)


def _epilogue_specs(nb, n, d, tm, final, idx):
    tok = pl.BlockSpec((1, tm, d), idx)
    if final:
        return [tok], [jax.ShapeDtypeStruct((nb, n, d), F32)]
    return [tok, tok], [jax.ShapeDtypeStruct((nb, n, d), F32), jax.ShapeDtypeStruct((nb, n, d), BF16)]


def _ffn_call(x, h, wg, wu, wd, g2, ng, sh, sc, *, final):
    nb, n, d = x.shape
    tm = min(512, n)
    idx = lambda b, i: (b, i, 0)
    vec = pl.BlockSpec((1, 1, d), lambda b, i: (b, 0, 0))
    out_specs, out_shape = _epilogue_specs(nb, n, d, tm, final, idx)
    return pl.pallas_call(
        functools.partial(_ffn_kernel, final=final),
        grid=(nb, n // tm),
        in_specs=[pl.BlockSpec((1, tm, d), idx), pl.BlockSpec((1, tm, d), idx),
                  _resident(wg.shape, lambda b, i: (0, 0)), _resident(wu.shape, lambda b, i: (0, 0)),
                  _resident(wd.shape, lambda b, i: (0, 0)),
                  vec, pl.BlockSpec((1, d), lambda b, i: (0, 0)), vec, vec],
        out_specs=out_specs,
        out_shape=out_shape,
        compiler_params=_cparams(("parallel", "parallel")),
        name="dense_swiglu",
    )(x, h, wg, wu, wd, g2, ng.reshape(1, d), sh, sc)


def _moe_kernel(x_ref, h_ref, wr_ref, wg_ref, wu_ref, wd_ref, g2_ref, ng_ref, sh_ref, sc_ref,
                *rest, final, n_experts):
    out_refs, (comb_ref, acc_ref) = rest[:-2], rest[-2:]
    e, f = pl.program_id(2), pl.program_id(3)
    h = h_ref[0]
    tm = h.shape[0]
    lane = lax.broadcasted_iota(jnp.int32, (tm, LANES), 1)

    @pl.when((e == 0) & (f == 0))
    def _():
        logits = jnp.dot(h, wr_ref[...], preferred_element_type=F32)
        lg = jnp.where(lane < n_experts, logits, -jnp.inf)
        m1 = jnp.max(lg, axis=-1, keepdims=True)
        i1 = jnp.min(jnp.where(lg == m1, lane, LANES), axis=-1, keepdims=True)
        lg2 = jnp.where(lane == i1, -jnp.inf, lg)
        m2 = jnp.max(lg2, axis=-1, keepdims=True)
        i2 = jnp.min(jnp.where(lg2 == m2, lane, LANES), axis=-1, keepdims=True)
        e2 = jnp.exp(m2 - m1)
        w1 = 1.0 / (1.0 + e2)
        comb_ref[...] = jnp.where(lane == i1, w1, 0.0) + jnp.where(lane == i2, e2 * w1, 0.0)
        acc_ref[...] = jnp.zeros_like(acc_ref)

    ce = jnp.sum(jnp.where(lane == e, comb_ref[...], 0.0), axis=-1, keepdims=True)
    a = jnp.dot(h, wg_ref[0], preferred_element_type=F32)
    b = jnp.dot(h, wu_ref[0], preferred_element_type=F32)
    t = (_silu(a) * b * ce).astype(BF16)
    acc_ref[...] += jnp.dot(t, wd_ref[0], preferred_element_type=F32)

    @pl.when((e == pl.num_programs(2) - 1) & (f == pl.num_programs(3) - 1))
    def _():
        xn = x_ref[0] + g2_ref[0] * acc_ref[...]
        _finish(xn, final, ng_ref, sh_ref, sc_ref, out_refs)


def _moe_call(x, h, w_router, wg, wu, wd, g2, ng, sh, sc, *, final):
    nb, n, d = x.shape
    n_experts, _, dff = wg.shape
    tm = min(1024, n)
    tf = 512
    idx = lambda b, i, e, f: (b, i, 0)
    vec = pl.BlockSpec((1, 1, d), lambda b, i, e, f: (b, 0, 0))
    out_specs, out_shape = _epilogue_specs(nb, n, d, tm, final, idx)
    wr = jnp.zeros((d, LANES), BF16).at[:, :n_experts].set(w_router.astype(BF16))
    return pl.pallas_call(
        functools.partial(_moe_kernel, final=final, n_experts=n_experts),
        grid=(nb, n // tm, n_experts, dff // tf),
        in_specs=[pl.BlockSpec((1, tm, d), idx), pl.BlockSpec((1, tm, d), idx),
                  pl.BlockSpec((d, LANES), lambda b, i, e, f: (0, 0)),
                  pl.BlockSpec((1, d, tf), lambda b, i, e, f: (e, 0, f)),
                  pl.BlockSpec((1, d, tf), lambda b, i, e, f: (e, 0, f)),
                  pl.BlockSpec((1, tf, d), lambda b, i, e, f: (e, f, 0)),
                  vec, pl.BlockSpec((1, d), lambda b, i, e, f: (0, 0)), vec, vec],
        out_specs=out_specs,
        out_shape=out_shape,
        scratch_shapes=[pltpu.VMEM((tm, LANES), F32), pltpu.VMEM((tm, d), F32)],
        compiler_params=_cparams(("parallel", "parallel", "arbitrary", "arbitrary")),
        name="moe_swiglu",
    )(x, h, wr, wg, wu, wd, g2, ng.reshape(1, d), sh, sc)


def kernel(x, c, ctx, c_ctx, w_ada, b_ada, norm1_g, norm2_g, w_in, b_gate, a_ln_g, a_ln_b, a_ws,
           a_bs, b_conv, c_lambda, c_subln_g, w_a_out, w_b_out, w_c_out, w_o, ff_w_gate, ff_w_up,
           ff_w_down, moe_w_router, moe_w_gate, moe_w_up, moe_w_down, final_norm_g):
    bsz, n, d = x.shape
    nc = ctx.shape[1]
    depth = w_ada.shape[0]
    cols = _Cols(d)
    head_cb = COL_BLOCK // LANES
    gate_cb = cols.out(cols.gate) * COL_BLOCK // d
    mix_pos = tuple(cols.out(j) for j in (cols.au, cols.av, cols.bb, cols.bc, cols.bh))
    cq_l, ck_l, cv_l = (cols.out(j) * head_cb for j in (cols.cq, cols.ck, cols.cv))

    pad = (-(bsz + 1)) % 8
    cc = jnp.concatenate([c, c_ctx[None], jnp.zeros((pad, d), F32)], axis=0)
    mod = _ada_call(cc, w_ada, b_ada)

    def mods(l):
        lat = [mod[l, :bsz, k * d:(k + 1) * d].reshape(bsz, 1, d) for k in range(N_MOD)]
        con = [mod[l, bsz:bsz + 1, k * d:(k + 1) * d].reshape(1, 1, d) for k in range(N_MOD)]
        return lat, con

    tables = _rope_tables(n)
    w_in_b = w_in.astype(BF16)
    xl = x
    xc = ctx.reshape(1, bsz * nc, d)
    lat, con = mods(0)
    h = _mod_call(xl, norm1_g[0], lat[0], lat[1])
    hc = _mod_call(xc, norm1_g[0], con[0], con[1])

    for l in range(depth):
        last = l == depth - 1
        lam_init = 0.8 - 0.6 * math.exp(-0.3 * l)
        lat, con = mods(l)
        if not last:
            nlat, ncon = mods(l + 1)
            nxt_l = (norm1_g[l + 1], nlat[0], nlat[1])
            nxt_c = (norm1_g[l + 1], ncon[0], ncon[1])
        else:
            nxt_l = (final_norm_g, lat[0], lat[1])
            nxt_c = None

        p = _in_call(h, w_in_b[l], b_gate[l], tables, j0=0, nj=cols.end, rope=True)
        if last:
            pc = _in_call(hc, w_in_b[l], b_gate[l], tables, j0=cols.ck, nj=cols.gate - cols.ck,
                          rope=False)
            ck_c, cv_c = 0, (cols.cv - cols.ck) * head_cb
        else:
            pc = _in_call(hc, w_in_b[l], b_gate[l], tables, j0=0, nj=cols.end, rope=False)
            ck_c, cv_c = ck_l, cv_l
        pc_seq = pc.reshape(bsz, nc, -1)

        wa, wb, wc, wo = (w.astype(BF16) for w in (w_a_out[l], w_b_out[l], w_c_out[l], w_o[l]))

        def channel(xs, hs, g2, nxt, final):
            i = l // 2
            if l % 2 == 0:
                return _ffn_call(xs, hs, ff_w_gate[i].astype(BF16), ff_w_up[i].astype(BF16),
                                 ff_w_down[i].astype(BF16), g2, *nxt, final=final)
            return _moe_call(xs, hs, moe_w_router[i], moe_w_gate[i].astype(BF16),
                             moe_w_up[i].astype(BF16), moe_w_down[i].astype(BF16), g2, *nxt,
                             final=final)

        ya, yb = _local_call(p, mix_pos, a_ln_g[l], a_ln_b[l], a_ws[l], a_bs[l], b_conv[l], seq=n)
        yc = _attn_call(p, cq_l, [(p, ck_l, cv_l), (pc_seq, ck_c, cv_c)],
                        c_lambda[l], c_subln_g[l], lam_init=lam_init)
        x1, h2 = _merge_call(xl, ya, yb, yc, p, gate_cb, wa, wb, wc, wo, lat[2], norm2_g[l],
                             lat[3], lat[4])
        res = channel(x1, h2, lat[5], nxt_l, last)
        if last:
            return res[0]
        xl, h = res

        ya_c, yb_c = _local_call(pc_seq, mix_pos, a_ln_g[l], a_ln_b[l], a_ws[l], a_bs[l],
                                 b_conv[l], seq=nc)
        yc_c = _attn_call(pc_seq, cq_l, [(pc_seq, ck_c, cv_c)], c_lambda[l], c_subln_g[l],
                          lam_init=lam_init)
        flat = lambda a: a.reshape(1, bsz * nc, -1)
        xc1, hc2 = _merge_call(xc, flat(ya_c), flat(yb_c), flat(yc_c), pc, gate_cb, wa, wb, wc, wo,
                               con[2], norm2_g[l], con[3], con[4])
        xc, hc = channel(xc1, hc2, con[5], nxt_c, False)
```

```python
import functools
import math

import jax
import jax.numpy as jnp
from jax import lax
from jax.experimental import pallas as pl
from jax.experimental.pallas import tpu as pltpu

F32 = jnp.float32
BF16 = jnp.bfloat16

EPS = 1e-6
GRID_W = 64
N_MOD = 6
N_BRANCH = 3
CHUNK = 128
A_GROUPS = 8
A_GROUP_DIM = 64
A_WIDTH = A_GROUPS * A_GROUP_DIM
B_WIDTH = 512
CONV_W = 3
C_HEADS = 8
C_HEAD_DIM = 64
C_V_DIM = 2 * C_HEAD_DIM
ROPE_BASE = 10000.0
TOP_K = 2
LOG2E = 1.4426950408889634

LANES = 128
BF16_SUBLANES = 16
V7X_VMEM_LIMIT = 56 * 1024 * 1024

COL_BLOCK = 512


def _cparams(sem):
    return pltpu.CompilerParams(dimension_semantics=sem, vmem_limit_bytes=V7X_VMEM_LIMIT)


def _resident(shape, index_map):
    return pl.BlockSpec(shape, index_map, pipeline_mode=pl.Buffered(1))


def _silu(v):
    return v * jax.nn.sigmoid(v)


def _modulate(x, g, sh, sc):
    ms = jnp.mean(x * x, axis=-1, keepdims=True)
    y = x * lax.rsqrt(ms + EPS)
    return (y * g) * (1.0 + sc) + sh


def _rmsnorm(x, g):
    ms = jnp.mean(x * x, axis=-1, keepdims=True)
    return x * lax.rsqrt(ms + EPS) * g


def _ada_kernel(c_ref, w_ref, b_ref, o_ref):
    s = _silu(c_ref[...])
    o_ref[0] = jnp.dot(s, w_ref[0], preferred_element_type=F32,
                       precision=lax.Precision.HIGHEST) + b_ref[0]


def _ada_call(cc, w_ada, b_ada):
    depth, d, cols = w_ada.shape
    rows = cc.shape[0]
    tn = 1536
    return pl.pallas_call(
        _ada_kernel,
        grid=(depth, cols // tn),
        in_specs=[
            pl.BlockSpec((rows, d), lambda l, j: (0, 0)),
            pl.BlockSpec((1, d, tn), lambda l, j: (l, 0, j)),
            pl.BlockSpec((1, 1, tn), lambda l, j: (l, 0, j)),
        ],
        out_specs=pl.BlockSpec((1, rows, tn), lambda l, j: (l, 0, j)),
        out_shape=jax.ShapeDtypeStruct((depth, rows, cols), F32),
        compiler_params=_cparams(("parallel", "parallel")),
        name="ada_proj",
    )(cc, w_ada, b_ada.reshape(depth, 1, cols))


def _mod_kernel(x_ref, g_ref, sh_ref, sc_ref, o_ref):
    o_ref[0] = _modulate(x_ref[0], g_ref[...], sh_ref[0], sc_ref[0]).astype(BF16)


def _mod_call(x, g, sh, sc):
    nb, n, d = x.shape
    tm = min(1024, n)
    vec = pl.BlockSpec((1, 1, d), lambda b, i: (b, 0, 0))
    return pl.pallas_call(
        _mod_kernel,
        grid=(nb, n // tm),
        in_specs=[pl.BlockSpec((1, tm, d), lambda b, i: (b, i, 0)),
                  pl.BlockSpec((1, d), lambda b, i: (0, 0)), vec, vec],
        out_specs=pl.BlockSpec((1, tm, d), lambda b, i: (b, i, 0)),
        out_shape=jax.ShapeDtypeStruct((nb, n, d), BF16),
        compiler_params=_cparams(("parallel", "parallel")),
        name="modulate",
    )(x, g.reshape(1, d), sh, sc)


class _Cols:
    def __init__(self, d):
        self.au = 0
        self.av = self.au + A_WIDTH // COL_BLOCK
        self.bb = self.av + A_WIDTH // COL_BLOCK
        self.bc = self.bb + B_WIDTH // COL_BLOCK
        self.bh = self.bc + B_WIDTH // COL_BLOCK
        self.cq = self.bh + B_WIDTH // COL_BLOCK
        qk = C_HEADS * 2 * C_HEAD_DIM // COL_BLOCK
        self.ck = self.cq + qk
        self.cv = self.ck + qk
        self.gate = self.cv + C_HEADS * C_V_DIM // COL_BLOCK
        self.end = self.gate + N_BRANCH * d // COL_BLOCK

    def out(self, jw):
        return (jw - self.gate) % self.end


def _in_kernel(h_ref, w_ref, bg_ref, cos_ref, sa_ref, sb_ref, o_ref, *, cols, j0, rope, qscale):
    j = pl.program_id(2) + j0
    acc = jnp.dot(h_ref[0], w_ref[...], preferred_element_type=F32)

    @pl.when(j < cols.bb)
    def _():
        o_ref[0] = jax.nn.gelu(acc).astype(BF16)

    @pl.when(((j >= cols.bb) & (j < cols.cq)) | ((j >= cols.cv) & (j < cols.gate)))
    def _():
        o_ref[0] = acc.astype(BF16)

    @pl.when((j >= cols.cq) & (j < cols.cv))
    def _():
        scale = jnp.where(j < cols.ck, qscale, 1.0).astype(F32)
        if rope:
            cos, sa, sb = cos_ref[...], sa_ref[...], sb_ref[...]
            for s in range(COL_BLOCK // LANES):
                t = acc[:, s * LANES:(s + 1) * LANES]
                r = (t * cos + pltpu.roll(t, LANES - 16, 1) * sa + pltpu.roll(t, 16, 1) * sb)
                o_ref[0, :, s * LANES:(s + 1) * LANES] = (r * scale).astype(BF16)
        else:
            o_ref[0] = (acc * scale).astype(BF16)

    @pl.when(j >= cols.gate)
    def _():
        o_ref[0] = jax.nn.sigmoid(acc + bg_ref[...]).astype(BF16)


def _in_call(h, w, b_gate, tables, *, j0, nj, rope):
    nb, n, d = h.shape
    cols = _Cols(d)
    tm = min(2048, n)
    full = (j0 == 0) and (nj == cols.end)
    if rope:
        cos, sa, sb = tables
    else:
        cos = sa = sb = jnp.zeros((tm, LANES), F32)
    tab = pl.BlockSpec((tm, LANES), lambda b, i, j: (i if rope else 0, 0))

    def out_idx(b, i, j):
        if full:
            return (b, i, jnp.where(j >= cols.gate, j - cols.gate, j + cols.end - cols.gate))
        return (b, i, j)
    kern = functools.partial(_in_kernel, cols=cols, j0=j0, rope=rope,
                             qscale=C_HEAD_DIM ** -0.5 * LOG2E)
    return pl.pallas_call(
        kern,
        grid=(nb, n // tm, nj),
        in_specs=[
            pl.BlockSpec((1, tm, d), lambda b, i, j: (b, i, 0)),
            pl.BlockSpec((d, COL_BLOCK), lambda b, i, j: (0, j + j0)),
            pl.BlockSpec((1, COL_BLOCK), lambda b, i, j: (0, jnp.maximum(j + j0 - cols.gate, 0))),
            tab, tab, tab,
        ],
        out_specs=pl.BlockSpec((1, tm, COL_BLOCK), out_idx),
        out_shape=jax.ShapeDtypeStruct((nb, n, nj * COL_BLOCK), BF16),
        compiler_params=_cparams(("parallel", "parallel", "arbitrary")),
        name="in_proj",
    )(h, w, b_gate.reshape(1, -1), cos, sa, sb)


def _rope_tables(n):
    rows = n // GRID_W
    r = jnp.repeat(jnp.arange(rows, dtype=F32), GRID_W)
    col = jnp.tile(jnp.arange(GRID_W, dtype=F32), rows)
    quarter = C_HEAD_DIM // 4
    inv = ROPE_BASE ** (-jnp.arange(quarter, dtype=F32) / quarter)
    ar = r[:, None] * inv
    ac = col[:, None] * inv
    ang = jnp.concatenate([ar, ar, ac, ac], axis=-1)
    ang = jnp.tile(ang, (1, LANES // C_HEAD_DIM))
    cos, sin = jnp.cos(ang), jnp.sin(ang)
    first_of_pair = (jnp.arange(LANES) // quarter) % 2 == 0
    sa = jnp.where(first_of_pair, -sin, 0.0)
    sb = jnp.where(first_of_pair, 0.0, sin)
    return cos, sa, sb


def _local_kernel(u_ref, v_ref, bg_ref, cg_ref, hh_ref, cgp_ref, hhp_ref, cgn_ref, hhn_ref,
                  lng_ref, lnb_ref, ws_ref, bias_ref, cw_ref, ya_ref, yb_ref, *, tm, seq):
    i = pl.program_id(1)
    lane = lax.broadcasted_iota(jnp.int32, (CHUNK, LANES), 1)
    lo = lane < A_GROUP_DIM
    for c in range(tm // CHUNK):
        rows = pl.ds(c * CHUNK, CHUNK)
        v = v_ref[0, rows, :].astype(F32)
        mu = jnp.mean(v, axis=-1, keepdims=True)
        var = jnp.mean(jnp.square(v - mu), axis=-1, keepdims=True)
        vn = ((v - mu) * lax.rsqrt(var + EPS) * lng_ref[...] + lnb_ref[...]).astype(BF16)
        for k in range(A_WIDTH // LANES):
            blk = vn[:, k * LANES:(k + 1) * LANES]
            zero = jnp.zeros_like(blk)
            mixed = (jnp.dot(ws_ref[2 * k], jnp.where(lo, blk, zero), preferred_element_type=F32)
                     + jnp.dot(ws_ref[2 * k + 1], jnp.where(lo, zero, blk), preferred_element_type=F32)
                     + bias_ref[:, k * LANES:(k + 1) * LANES])
            u = u_ref[0, rows, k * LANES:(k + 1) * LANES].astype(F32)
            ya_ref[0, rows, k * LANES:(k + 1) * LANES] = (u * mixed).astype(BF16)

    z = cg_ref[0].astype(F32) * hh_ref[0].astype(F32)
    last = BF16_SUBLANES - 1
    z_prev = cgp_ref[0, last:last + 1, :].astype(F32) * hhp_ref[0, last:last + 1, :].astype(F32)
    z_next = cgn_ref[0, 0:1, :].astype(F32) * hhn_ref[0, 0:1, :].astype(F32)
    row = lax.broadcasted_iota(jnp.int32, (tm, 1), 0)
    pos = (i * tm) % seq + row
    zm1 = jnp.where(row == 0, z_prev, pltpu.roll(z, 1, 0))
    zm1 = jnp.where(pos == 0, 0.0, zm1)
    zp1 = jnp.where(row == tm - 1, z_next, pltpu.roll(z, tm - 1, 0))
    zp1 = jnp.where(pos == seq - 1, 0.0, zp1)
    conv = cw_ref[0:1, :] * zm1 + cw_ref[1:2, :] * z + cw_ref[2:3, :] * zp1
    yb_ref[0] = (bg_ref[0].astype(F32) * conv).astype(BF16)


def _local_call(p, pos, ln_g, ln_b, ws, bs, conv_w, *, seq):
    nb, n, _ = p.shape
    au, av, bb, bc, bh = pos
    tm = min(512, seq)
    hb = tm // BF16_SUBLANES
    nhb = n // BF16_SUBLANES

    def main(cb):
        return pl.BlockSpec((1, tm, COL_BLOCK), lambda b, i: (b, i, cb))

    def prev(cb):
        return pl.BlockSpec((1, BF16_SUBLANES, COL_BLOCK),
                            lambda b, i: (b, jnp.maximum(i * hb - 1, 0), cb))

    def nxt(cb):
        return pl.BlockSpec((1, BF16_SUBLANES, COL_BLOCK),
                            lambda b, i: (b, jnp.minimum((i + 1) * hb, nhb - 1), cb))

    bias = jnp.repeat(bs.T, A_GROUP_DIM, axis=1)
    out = jax.ShapeDtypeStruct((nb, n, A_WIDTH), BF16)
    kern = functools.partial(_local_kernel, tm=tm, seq=seq)
    return pl.pallas_call(
        kern,
        grid=(nb, n // tm),
        in_specs=[main(au), main(av), main(bb), main(bc), main(bh),
                  prev(bc), prev(bh), nxt(bc), nxt(bh),
                  pl.BlockSpec((1, A_WIDTH), lambda b, i: (0, 0)),
                  pl.BlockSpec((1, A_WIDTH), lambda b, i: (0, 0)),
                  pl.BlockSpec((A_GROUPS, CHUNK, CHUNK), lambda b, i: (0, 0, 0)),
                  pl.BlockSpec((CHUNK, A_WIDTH), lambda b, i: (0, 0)),
                  pl.BlockSpec((CONV_W, B_WIDTH), lambda b, i: (0, 0))],
        out_specs=[pl.BlockSpec((1, tm, A_WIDTH), lambda b, i: (b, i, 0)),
                   pl.BlockSpec((1, tm, B_WIDTH), lambda b, i: (b, i, 0))],
        out_shape=[out, out],
        compiler_params=_cparams(("parallel", "parallel")),
        name="local_mixers",
    )(p, p, p, p, p, p, p, p, p, ln_g.reshape(1, -1), ln_b.reshape(1, -1), ws.astype(BF16),
      bias, conv_w)


def _attn_kernel(*refs, tq, ck, sizes, lam_init):
    nsrc = len(sizes)
    cl_ref, g_ref, q_ref = refs[:3]
    kv_refs = refs[3:3 + 2 * nsrc]
    o_ref = refs[3 + 2 * nsrc]
    vt_refs = refs[4 + 2 * nsrc:]

    @pl.when(pl.program_id(2) == 0)
    def _():
        for src in range(nsrc):
            vt_refs[src][...] = kv_refs[2 * src + 1][0].astype(F32).T.astype(BF16)

    q = q_ref[0]
    lane = lax.broadcasted_iota(jnp.int32, (tq, LANES), 1)
    first = lane < C_HEAD_DIM
    zero = jnp.zeros_like(q)
    qq = jnp.concatenate([jnp.where(first, q, zero), jnp.where(first, zero, q)], axis=0)

    m = jnp.full((1, 2 * tq), -jnp.inf, F32)
    l = jnp.zeros((1, 2 * tq), F32)
    acc = jnp.zeros((C_V_DIM, 2 * tq), F32)
    chunks = [(kv_refs[2 * src], vt_refs[src], c0, min(ck, nk))
              for src, nk in enumerate(sizes) for c0 in range(0, nk, min(ck, nk))]

    def scores(chunk):
        k_ref, _, c0, c = chunk
        return lax.dot_general(k_ref[0, c0:c0 + c, :], qq, (((1,), (1,)), ((), ())),
                               preferred_element_type=F32)

    s_next = scores(chunks[0])
    for t, (_, vt_ref, c0, c) in enumerate(chunks):
        s = s_next
        if t + 1 < len(chunks):
            s_next = scores(chunks[t + 1])
        m_new = jnp.maximum(m, jnp.max(s, axis=0, keepdims=True))
        alpha = jnp.exp2(m - m_new)
        p = jnp.exp2(s - m_new)
        l = alpha * l + jnp.sum(p, axis=0, keepdims=True)
        acc = alpha * acc + jnp.dot(vt_ref[:, c0:c0 + c], p.astype(BF16),
                                    preferred_element_type=F32)
        m = m_new
    o2 = acc / l
    cl = cl_ref[...]
    lam = (jnp.exp(jnp.sum(cl[0:1] * cl[1:2], axis=-1, keepdims=True))
           - jnp.exp(jnp.sum(cl[2:3] * cl[3:4], axis=-1, keepdims=True)) + lam_init)
    o = o2[:, :tq] - lam * o2[:, tq:]
    ms = jnp.mean(o * o, axis=0, keepdims=True)
    y = o * lax.rsqrt(ms + EPS) * (g_ref[...] * (1.0 - lam_init))
    o_ref[0] = y.T.astype(BF16)


def _attn_call(q_arr, q_cb, sources, c_lambda, subln_g, *, lam_init):
    nb, n, _ = q_arr.shape
    tq = min(256, n)
    sizes = tuple(a.shape[1] for a, _, _ in sources)
    in_specs = [
        pl.BlockSpec((4, C_HEAD_DIM), lambda b, h, i: (0, 0)),
        pl.BlockSpec((C_V_DIM, 1), lambda b, h, i: (0, 0)),
        pl.BlockSpec((1, tq, LANES), lambda b, h, i: (b, i, q_cb + h)),
    ]
    args = [c_lambda, subln_g.reshape(-1, 1), q_arr]
    for arr, kcb, vcb in sources:
        nk = arr.shape[1]
        in_specs.append(pl.BlockSpec((1, nk, LANES), lambda b, h, i, kcb=kcb: (b, 0, kcb + h)))
        in_specs.append(pl.BlockSpec((1, nk, LANES), lambda b, h, i, vcb=vcb: (b, 0, vcb + h)))
        args += [arr, arr]
    kern = functools.partial(_attn_kernel, tq=tq, ck=1024, sizes=sizes, lam_init=lam_init)
    return pl.pallas_call(
        kern,
        grid=(nb, C_HEADS, n // tq),
        in_specs=in_specs,
        out_specs=pl.BlockSpec((1, tq, C_V_DIM), lambda b, h, i: (b, i, h)),
        out_shape=jax.ShapeDtypeStruct((nb, n, C_HEADS * C_V_DIM), BF16),
        scratch_shapes=[pltpu.VMEM((C_V_DIM, nk), BF16) for nk in sizes],
        compiler_params=_cparams(("parallel", "parallel", "arbitrary")),
        name="diff_attention",
    )(*args)


def _merge_kernel(x_ref, ya_ref, yb_ref, yc_ref, ga_ref, gb_ref, gc_ref, wa_ref, wb_ref, wc_ref,
                  wo_ref, g1_ref, n2_ref, sh2_ref, sc2_ref, xo_ref, ho_ref):
    a = jnp.dot(ya_ref[0], wa_ref[...], preferred_element_type=F32)
    b = jnp.dot(yb_ref[0], wb_ref[...], preferred_element_type=F32)
    c = jnp.dot(yc_ref[0], wc_ref[...], preferred_element_type=F32)
    y = (ga_ref[0].astype(F32) * a + gb_ref[0].astype(F32) * b + gc_ref[0].astype(F32) * c)
    m = jnp.dot(y.astype(BF16), wo_ref[...], preferred_element_type=F32)
    xn = x_ref[0] + g1_ref[0] * m
    xo_ref[0] = xn
    ho_ref[0] = _modulate(xn, n2_ref[...], sh2_ref[0], sc2_ref[0]).astype(BF16)


def _merge_call(x, ya, yb, yc, p, gate_cb, wa, wb, wc, wo, g1, n2g, sh2, sc2):
    nb, n, d = x.shape
    tm = min(512, n)

    def tok(w, cb=0):
        return pl.BlockSpec((1, tm, w), lambda b, i: (b, i, cb))

    vec = pl.BlockSpec((1, 1, d), lambda b, i: (b, 0, 0))
    return pl.pallas_call(
        _merge_kernel,
        grid=(nb, n // tm),
        in_specs=[tok(d), tok(A_WIDTH), tok(B_WIDTH), tok(d),
                  tok(d, gate_cb), tok(d, gate_cb + 1), tok(d, gate_cb + 2),
                  _resident(wa.shape, lambda b, i: (0, 0)), _resident(wb.shape, lambda b, i: (0, 0)),
                  _resident(wc.shape, lambda b, i: (0, 0)), _resident(wo.shape, lambda b, i: (0, 0)),
                  vec, pl.BlockSpec((1, d), lambda b, i: (0, 0)), vec, vec],
        out_specs=[tok(d), tok(d)],
        out_shape=[jax.ShapeDtypeStruct((nb, n, d), F32), jax.ShapeDtypeStruct((nb, n, d), BF16)],
        compiler_params=_cparams(("parallel", "parallel")),
        name="merge_branches",
    )(x, ya, yb, yc, p, p, p, wa, wb, wc, wo, g1, n2g.reshape(1, d), sh2, sc2)


def _ff_chunks(width, chunk=512):
    out, c0 = [], 0
    while c0 < width:
        cw = min(chunk, width - c0)
        out.append((c0, cw))
        c0 += cw
    return out


def _finish(xn, final, ng_ref, sh_ref, sc_ref, out_refs):
    if final:
        out_refs[0][0] = _rmsnorm(xn, ng_ref[...])
    else:
        out_refs[0][0] = xn
        out_refs[1][0] = _modulate(xn, ng_ref[...], sh_ref[0], sc_ref[0]).astype(BF16)


def _ffn_kernel(x_ref, h_ref, wg_ref, wu_ref, wd_ref, g2_ref, ng_ref, sh_ref, sc_ref, *out_refs,
                final):
    h = h_ref[0]
    acc = None
    for c0, cw in _ff_chunks(wg_ref.shape[1]):
        a = jnp.dot(h, wg_ref[:, c0:c0 + cw], preferred_element_type=F32)
        b = jnp.dot(h, wu_ref[:, c0:c0 + cw], preferred_element_type=F32)
        t = (_silu(a) * b).astype(BF16)
        part = jnp.dot(t, wd_ref[c0:c0 + cw, :], preferred_element_type=F32)
        acc = part if acc is None else acc + part
    xn = x_ref[0] + g2_ref[0] * acc
    _finish(xn, final, ng_ref, sh_ref, sc_ref, out_refs)


def _epilogue_specs(nb, n, d, tm, final, idx):
    tok = pl.BlockSpec((1, tm, d), idx)
    if final:
        return [tok], [jax.ShapeDtypeStruct((nb, n, d), F32)]
    return [tok, tok], [jax.ShapeDtypeStruct((nb, n, d), F32), jax.ShapeDtypeStruct((nb, n, d), BF16)]


def _ffn_call(x, h, wg, wu, wd, g2, ng, sh, sc, *, final):
    nb, n, d = x.shape
    tm = min(512, n)
    idx = lambda b, i: (b, i, 0)
    vec = pl.BlockSpec((1, 1, d), lambda b, i: (b, 0, 0))
    out_specs, out_shape = _epilogue_specs(nb, n, d, tm, final, idx)
    return pl.pallas_call(
        functools.partial(_ffn_kernel, final=final),
        grid=(nb, n // tm),
        in_specs=[pl.BlockSpec((1, tm, d), idx), pl.BlockSpec((1, tm, d), idx),
                  _resident(wg.shape, lambda b, i: (0, 0)), _resident(wu.shape, lambda b, i: (0, 0)),
                  _resident(wd.shape, lambda b, i: (0, 0)),
                  vec, pl.BlockSpec((1, d), lambda b, i: (0, 0)), vec, vec],
        out_specs=out_specs,
        out_shape=out_shape,
        compiler_params=_cparams(("parallel", "parallel")),
        name="dense_swiglu",
    )(x, h, wg, wu, wd, g2, ng.reshape(1, d), sh, sc)


def _moe_kernel(x_ref, h_ref, wr_ref, wg_ref, wu_ref, wd_ref, g2_ref, ng_ref, sh_ref, sc_ref,
                *rest, final, n_experts):
    out_refs, (comb_ref, acc_ref) = rest[:-2], rest[-2:]
    e, f = pl.program_id(2), pl.program_id(3)
    h = h_ref[0]
    tm = h.shape[0]
    lane = lax.broadcasted_iota(jnp.int32, (tm, LANES), 1)

    @pl.when((e == 0) & (f == 0))
    def _():
        logits = jnp.dot(h, wr_ref[...], preferred_element_type=F32)
        lg = jnp.where(lane < n_experts, logits, -jnp.inf)
        m1 = jnp.max(lg, axis=-1, keepdims=True)
        i1 = jnp.min(jnp.where(lg == m1, lane, LANES), axis=-1, keepdims=True)
        lg2 = jnp.where(lane == i1, -jnp.inf, lg)
        m2 = jnp.max(lg2, axis=-1, keepdims=True)
        i2 = jnp.min(jnp.where(lg2 == m2, lane, LANES), axis=-1, keepdims=True)
        e2 = jnp.exp(m2 - m1)
        w1 = 1.0 / (1.0 + e2)
        comb_ref[...] = jnp.where(lane == i1, w1, 0.0) + jnp.where(lane == i2, e2 * w1, 0.0)
        acc_ref[...] = jnp.zeros_like(acc_ref)

    ce = jnp.sum(jnp.where(lane == e, comb_ref[...], 0.0), axis=-1, keepdims=True)
    a = jnp.dot(h, wg_ref[0], preferred_element_type=F32)
    b = jnp.dot(h, wu_ref[0], preferred_element_type=F32)
    t = (_silu(a) * b * ce).astype(BF16)
    acc_ref[...] += jnp.dot(t, wd_ref[0], preferred_element_type=F32)

    @pl.when((e == pl.num_programs(2) - 1) & (f == pl.num_programs(3) - 1))
    def _():
        xn = x_ref[0] + g2_ref[0] * acc_ref[...]
        _finish(xn, final, ng_ref, sh_ref, sc_ref, out_refs)


def _moe_call(x, h, w_router, wg, wu, wd, g2, ng, sh, sc, *, final):
    nb, n, d = x.shape
    n_experts, _, dff = wg.shape
    tm = min(1024, n)
    tf = 512
    idx = lambda b, i, e, f: (b, i, 0)
    vec = pl.BlockSpec((1, 1, d), lambda b, i, e, f: (b, 0, 0))
    out_specs, out_shape = _epilogue_specs(nb, n, d, tm, final, idx)
    wr = jnp.zeros((d, LANES), BF16).at[:, :n_experts].set(w_router.astype(BF16))
    return pl.pallas_call(
        functools.partial(_moe_kernel, final=final, n_experts=n_experts),
        grid=(nb, n // tm, n_experts, dff // tf),
        in_specs=[pl.BlockSpec((1, tm, d), idx), pl.BlockSpec((1, tm, d), idx),
                  pl.BlockSpec((d, LANES), lambda b, i, e, f: (0, 0)),
                  pl.BlockSpec((1, d, tf), lambda b, i, e, f: (e, 0, f)),
                  pl.BlockSpec((1, d, tf), lambda b, i, e, f: (e, 0, f)),
                  pl.BlockSpec((1, tf, d), lambda b, i, e, f: (e, f, 0)),
                  vec, pl.BlockSpec((1, d), lambda b, i, e, f: (0, 0)), vec, vec],
        out_specs=out_specs,
        out_shape=out_shape,
        scratch_shapes=[pltpu.VMEM((tm, LANES), F32), pltpu.VMEM((tm, d), F32)],
        compiler_params=_cparams(("parallel", "parallel", "arbitrary", "arbitrary")),
        name="moe_swiglu",
    )(x, h, wr, wg, wu, wd, g2, ng.reshape(1, d), sh, sc)


def kernel(x, c, ctx, c_ctx, w_ada, b_ada, norm1_g, norm2_g, w_in, b_gate, a_ln_g, a_ln_b, a_ws,
           a_bs, b_conv, c_lambda, c_subln_g, w_a_out, w_b_out, w_c_out, w_o, ff_w_gate, ff_w_up,
           ff_w_down, moe_w_router, moe_w_gate, moe_w_up, moe_w_down, final_norm_g):
    bsz, n, d = x.shape
    nc = ctx.shape[1]
    depth = w_ada.shape[0]
    cols = _Cols(d)
    head_cb = COL_BLOCK // LANES
    gate_cb = cols.out(cols.gate) * COL_BLOCK // d
    mix_pos = tuple(cols.out(j) for j in (cols.au, cols.av, cols.bb, cols.bc, cols.bh))
    cq_l, ck_l, cv_l = (cols.out(j) * head_cb for j in (cols.cq, cols.ck, cols.cv))

    pad = (-(bsz + 1)) % 8
    cc = jnp.concatenate([c, c_ctx[None], jnp.zeros((pad, d), F32)], axis=0)
    mod = _ada_call(cc, w_ada, b_ada)

    def mods(l):
        lat = [mod[l, :bsz, k * d:(k + 1) * d].reshape(bsz, 1, d) for k in range(N_MOD)]
        con = [mod[l, bsz:bsz + 1, k * d:(k + 1) * d].reshape(1, 1, d) for k in range(N_MOD)]
        return lat, con

    tables = _rope_tables(n)
    w_in_b = w_in.astype(BF16)
    xl = x
    xc = ctx.reshape(1, bsz * nc, d)
    lat, con = mods(0)
    h = _mod_call(xl, norm1_g[0], lat[0], lat[1])
    hc = _mod_call(xc, norm1_g[0], con[0], con[1])

    for l in range(depth):
        last = l == depth - 1
        lam_init = 0.8 - 0.6 * math.exp(-0.3 * l)
        lat, con = mods(l)
        if not last:
            nlat, ncon = mods(l + 1)
            nxt_l = (norm1_g[l + 1], nlat[0], nlat[1])
            nxt_c = (norm1_g[l + 1], ncon[0], ncon[1])
        else:
            nxt_l = (final_norm_g, lat[0], lat[1])
            nxt_c = None

        p = _in_call(h, w_in_b[l], b_gate[l], tables, j0=0, nj=cols.end, rope=True)
        if last:
            pc = _in_call(hc, w_in_b[l], b_gate[l], tables, j0=cols.ck, nj=cols.gate - cols.ck,
                          rope=False)
            ck_c, cv_c = 0, (cols.cv - cols.ck) * head_cb
        else:
            pc = _in_call(hc, w_in_b[l], b_gate[l], tables, j0=0, nj=cols.end, rope=False)
            ck_c, cv_c = ck_l, cv_l
        pc_seq = pc.reshape(bsz, nc, -1)

        wa, wb, wc, wo = (w.astype(BF16) for w in (w_a_out[l], w_b_out[l], w_c_out[l], w_o[l]))

        def channel(xs, hs, g2, nxt, final):
            i = l // 2
            if l % 2 == 0:
                return _ffn_call(xs, hs, ff_w_gate[i].astype(BF16), ff_w_up[i].astype(BF16),
                                 ff_w_down[i].astype(BF16), g2, *nxt, final=final)
            return _moe_call(xs, hs, moe_w_router[i], moe_w_gate[i].astype(BF16),
                             moe_w_up[i].astype(BF16), moe_w_down[i].astype(BF16), g2, *nxt,
                             final=final)

        ya, yb = _local_call(p, mix_pos, a_ln_g[l], a_ln_b[l], a_ws[l], a_bs[l], b_conv[l], seq=n)
        yc = _attn_call(p, cq_l, [(p, ck_l, cv_l), (pc_seq, ck_c, cv_c)],
                        c_lambda[l], c_subln_g[l], lam_init=lam_init)
        x1, h2 = _merge_call(xl, ya, yb, yc, p, gate_cb, wa, wb, wc, wo, lat[2], norm2_g[l],
                             lat[3], lat[4])
        res = channel(x1, h2, lat[5], nxt_l, last)
        if last:
            return res[0]
        xl, h = res

        ya_c, yb_c = _local_call(pc_seq, mix_pos, a_ln_g[l], a_ln_b[l], a_ws[l], a_bs[l],
                                 b_conv[l], seq=nc)
        yc_c = _attn_call(pc_seq, cq_l, [(pc_seq, ck_c, cv_c)], c_lambda[l], c_subln_g[l],
                          lam_init=lam_init)
        flat = lambda a: a.reshape(1, bsz * nc, -1)
        xc1, hc2 = _merge_call(xc, flat(ya_c), flat(yb_c), flat(yc_c), pc, gate_cb, wa, wb, wc, wo,
                               con[2], norm2_g[l], con[3], con[4])
        xc, hc = channel(xc1, hc2, con[5], nxt_c, False)
```

```python
import functools
import math

import jax
import jax.numpy as jnp
from jax import lax
from jax.experimental import pallas as pl
from jax.experimental.pallas import tpu as pltpu

F32 = jnp.float32
BF16 = jnp.bfloat16

EPS = 1e-6
GRID_W = 64
N_MOD = 6
N_BRANCH = 3
CHUNK = 128
A_GROUPS = 8
A_GROUP_DIM = 64
A_WIDTH = A_GROUPS * A_GROUP_DIM
B_WIDTH = 512
CONV_W = 3
C_HEADS = 8
C_HEAD_DIM = 64
C_V_DIM = 2 * C_HEAD_DIM
ROPE_BASE = 10000.0
TOP_K = 2
LOG2E = 1.4426950408889634

LANES = 128
BF16_SUBLANES = 16
V7X_VMEM_LIMIT = 56 * 1024 * 1024

COL_BLOCK = 512


def _cparams(sem):
    return pltpu.CompilerParams(dimension_semantics=sem, vmem_limit_bytes=V7X_VMEM_LIMIT)


def _resident(shape, index_map):
    return pl.BlockSpec(shape, index_map, pipeline_mode=pl.Buffered(1))


def _silu(v):
    return v * jax.nn.sigmoid(v)


def _modulate(x, g, sh, sc):
    ms = jnp.mean(x * x, axis=-1, keepdims=True)
    y = x * lax.rsqrt(ms + EPS)
    return (y * g) * (1.0 + sc) + sh


def _rmsnorm(x, g):
    ms = jnp.mean(x * x, axis=-1, keepdims=True)
    return x * lax.rsqrt(ms + EPS) * g


def _ada_kernel(c_ref, w_ref, b_ref, o_ref):
    s = _silu(c_ref[...])
    o_ref[0] = jnp.dot(s, w_ref[0], preferred_element_type=F32,
                       precision=lax.Precision.HIGHEST) + b_ref[0]


def _ada_call(cc, w_ada, b_ada):
    depth, d, cols = w_ada.shape
    rows = cc.shape[0]
    tn = 1536
    return pl.pallas_call(
        _ada_kernel,
        grid=(depth, cols // tn),
        in_specs=[
            pl.BlockSpec((rows, d), lambda l, j: (0, 0)),
            pl.BlockSpec((1, d, tn), lambda l, j: (l, 0, j)),
            pl.BlockSpec((1, 1, tn), lambda l, j: (l, 0, j)),
        ],
        out_specs=pl.BlockSpec((1, rows, tn), lambda l, j: (l, 0, j)),
        out_shape=jax.ShapeDtypeStruct((depth, rows, cols), F32),
        compiler_params=_cparams(("parallel", "parallel")),
        name="ada_proj",
    )(cc, w_ada, b_ada.reshape(depth, 1, cols))


def _mod_kernel(x_ref, g_ref, sh_ref, sc_ref, o_ref):
    o_ref[0] = _modulate(x_ref[0], g_ref[...], sh_ref[0], sc_ref[0]).astype(BF16)


def _mod_call(x, g, sh, sc):
    nb, n, d = x.shape
    tm = min(1024, n)
    vec = pl.BlockSpec((1, 1, d), lambda b, i: (b, 0, 0))
    return pl.pallas_call(
        _mod_kernel,
        grid=(nb, n // tm),
        in_specs=[pl.BlockSpec((1, tm, d), lambda b, i: (b, i, 0)),
                  pl.BlockSpec((1, d), lambda b, i: (0, 0)), vec, vec],
        out_specs=pl.BlockSpec((1, tm, d), lambda b, i: (b, i, 0)),
        out_shape=jax.ShapeDtypeStruct((nb, n, d), BF16),
        compiler_params=_cparams(("parallel", "parallel")),
        name="modulate",
    )(x, g.reshape(1, d), sh, sc)


class _Cols:
    def __init__(self, d):
        self.au = 0
        self.av = self.au + A_WIDTH // COL_BLOCK
        self.bb = self.av + A_WIDTH // COL_BLOCK
        self.bc = self.bb + B_WIDTH // COL_BLOCK
        self.bh = self.bc + B_WIDTH // COL_BLOCK
        self.cq = self.bh + B_WIDTH // COL_BLOCK
        qk = C_HEADS * 2 * C_HEAD_DIM // COL_BLOCK
        self.ck = self.cq + qk
        self.cv = self.ck + qk
        self.gate = self.cv + C_HEADS * C_V_DIM // COL_BLOCK
        self.end = self.gate + N_BRANCH * d // COL_BLOCK

    def out(self, jw):
        return (jw - self.gate) % self.end


def _in_kernel(h_ref, w_ref, bg_ref, cos_ref, sa_ref, sb_ref, o_ref, *, cols, j0, rope, qscale):
    j = pl.program_id(2) + j0
    acc = jnp.dot(h_ref[0], w_ref[...], preferred_element_type=F32)

    @pl.when(j < cols.bb)
    def _():
        o_ref[0] = jax.nn.gelu(acc).astype(BF16)

    @pl.when(((j >= cols.bb) & (j < cols.cq)) | ((j >= cols.cv) & (j < cols.gate)))
    def _():
        o_ref[0] = acc.astype(BF16)

    @pl.when((j >= cols.cq) & (j < cols.cv))
    def _():
        scale = jnp.where(j < cols.ck, qscale, 1.0).astype(F32)
        if rope:
            cos, sa, sb = cos_ref[...], sa_ref[...], sb_ref[...]
            for s in range(COL_BLOCK // LANES):
                t = acc[:, s * LANES:(s + 1) * LANES]
                r = (t * cos + pltpu.roll(t, LANES - 16, 1) * sa + pltpu.roll(t, 16, 1) * sb)
                o_ref[0, :, s * LANES:(s + 1) * LANES] = (r * scale).astype(BF16)
        else:
            o_ref[0] = (acc * scale).astype(BF16)

    @pl.when(j >= cols.gate)
    def _():
        o_ref[0] = jax.nn.sigmoid(acc + bg_ref[...]).astype(BF16)


def _in_call(h, w, b_gate, tables, *, j0, nj, rope):
    nb, n, d = h.shape
    cols = _Cols(d)
    tm = min(2048, n)
    full = (j0 == 0) and (nj == cols.end)
    if rope:
        cos, sa, sb = tables
    else:
        cos = sa = sb = jnp.zeros((tm, LANES), F32)
    tab = pl.BlockSpec((tm, LANES), lambda b, i, j: (i if rope else 0, 0))

    def out_idx(b, i, j):
        if full:
            return (b, i, jnp.where(j >= cols.gate, j - cols.gate, j + cols.end - cols.gate))
        return (b, i, j)
    kern = functools.partial(_in_kernel, cols=cols, j0=j0, rope=rope,
                             qscale=C_HEAD_DIM ** -0.5 * LOG2E)
    return pl.pallas_call(
        kern,
        grid=(nb, n // tm, nj),
        in_specs=[
            pl.BlockSpec((1, tm, d), lambda b, i, j: (b, i, 0)),
            pl.BlockSpec((d, COL_BLOCK), lambda b, i, j: (0, j + j0)),
            pl.BlockSpec((1, COL_BLOCK), lambda b, i, j: (0, jnp.maximum(j + j0 - cols.gate, 0))),
            tab, tab, tab,
        ],
        out_specs=pl.BlockSpec((1, tm, COL_BLOCK), out_idx),
        out_shape=jax.ShapeDtypeStruct((nb, n, nj * COL_BLOCK), BF16),
        compiler_params=_cparams(("parallel", "parallel", "arbitrary")),
        name="in_proj",
    )(h, w, b_gate.reshape(1, -1), cos, sa, sb)


def _rope_tables(n):
    rows = n // GRID_W
    r = jnp.repeat(jnp.arange(rows, dtype=F32), GRID_W)
    col = jnp.tile(jnp.arange(GRID_W, dtype=F32), rows)
    quarter = C_HEAD_DIM // 4
    inv = ROPE_BASE ** (-jnp.arange(quarter, dtype=F32) / quarter)
    ar = r[:, None] * inv
    ac = col[:, None] * inv
    ang = jnp.concatenate([ar, ar, ac, ac], axis=-1)
    ang = jnp.tile(ang, (1, LANES // C_HEAD_DIM))
    cos, sin = jnp.cos(ang), jnp.sin(ang)
    first_of_pair = (jnp.arange(LANES) // quarter) % 2 == 0
    sa = jnp.where(first_of_pair, -sin, 0.0)
    sb = jnp.where(first_of_pair, 0.0, sin)
    return cos, sa, sb


def _local_kernel(u_ref, v_ref, bg_ref, cg_ref, hh_ref, cgp_ref, hhp_ref, cgn_ref, hhn_ref,
                  lng_ref, lnb_ref, ws_ref, bias_ref, cw_ref, ya_ref, yb_ref, *, tm, seq):
    i = pl.program_id(1)
    lane = lax.broadcasted_iota(jnp.int32, (CHUNK, LANES), 1)
    lo = lane < A_GROUP_DIM
    for c in range(tm // CHUNK):
        rows = pl.ds(c * CHUNK, CHUNK)
        v = v_ref[0, rows, :].astype(F32)
        mu = jnp.mean(v, axis=-1, keepdims=True)
        var = jnp.mean(jnp.square(v - mu), axis=-1, keepdims=True)
        vn = ((v - mu) * lax.rsqrt(var + EPS) * lng_ref[...] + lnb_ref[...]).astype(BF16)
        for k in range(A_WIDTH // LANES):
            blk = vn[:, k * LANES:(k + 1) * LANES]
            zero = jnp.zeros_like(blk)
            mixed = (jnp.dot(ws_ref[2 * k], jnp.where(lo, blk, zero), preferred_element_type=F32)
                     + jnp.dot(ws_ref[2 * k + 1], jnp.where(lo, zero, blk), preferred_element_type=F32)
                     + bias_ref[:, k * LANES:(k + 1) * LANES])
            u = u_ref[0, rows, k * LANES:(k + 1) * LANES].astype(F32)
            ya_ref[0, rows, k * LANES:(k + 1) * LANES] = (u * mixed).astype(BF16)

    z = cg_ref[0].astype(F32) * hh_ref[0].astype(F32)
    last = BF16_SUBLANES - 1
    z_prev = cgp_ref[0, last:last + 1, :].astype(F32) * hhp_ref[0, last:last + 1, :].astype(F32)
    z_next = cgn_ref[0, 0:1, :].astype(F32) * hhn_ref[0, 0:1, :].astype(F32)
    row = lax.broadcasted_iota(jnp.int32, (tm, 1), 0)
    pos = (i * tm) % seq + row
    zm1 = jnp.where(row == 0, z_prev, pltpu.roll(z, 1, 0))
    zm1 = jnp.where(pos == 0, 0.0, zm1)
    zp1 = jnp.where(row == tm - 1, z_next, pltpu.roll(z, tm - 1, 0))
    zp1 = jnp.where(pos == seq - 1, 0.0, zp1)
    conv = cw_ref[0:1, :] * zm1 + cw_ref[1:2, :] * z + cw_ref[2:3, :] * zp1
    yb_ref[0] = (bg_ref[0].astype(F32) * conv).astype(BF16)


def _local_call(p, pos, ln_g, ln_b, ws, bs, conv_w, *, seq):
    nb, n, _ = p.shape
    au, av, bb, bc, bh = pos
    tm = min(512, seq)
    hb = tm // BF16_SUBLANES
    nhb = n // BF16_SUBLANES

    def main(cb):
        return pl.BlockSpec((1, tm, COL_BLOCK), lambda b, i: (b, i, cb))

    def prev(cb):
        return pl.BlockSpec((1, BF16_SUBLANES, COL_BLOCK),
                            lambda b, i: (b, jnp.maximum(i * hb - 1, 0), cb))

    def nxt(cb):
        return pl.BlockSpec((1, BF16_SUBLANES, COL_BLOCK),
                            lambda b, i: (b, jnp.minimum((i + 1) * hb, nhb - 1), cb))

    bias = jnp.repeat(bs.T, A_GROUP_DIM, axis=1)
    out = jax.ShapeDtypeStruct((nb, n, A_WIDTH), BF16)
    kern = functools.partial(_local_kernel, tm=tm, seq=seq)
    return pl.pallas_call(
        kern,
        grid=(nb, n // tm),
        in_specs=[main(au), main(av), main(bb), main(bc), main(bh),
                  prev(bc), prev(bh), nxt(bc), nxt(bh),
                  pl.BlockSpec((1, A_WIDTH), lambda b, i: (0, 0)),
                  pl.BlockSpec((1, A_WIDTH), lambda b, i: (0, 0)),
                  pl.BlockSpec((A_GROUPS, CHUNK, CHUNK), lambda b, i: (0, 0, 0)),
                  pl.BlockSpec((CHUNK, A_WIDTH), lambda b, i: (0, 0)),
                  pl.BlockSpec((CONV_W, B_WIDTH), lambda b, i: (0, 0))],
        out_specs=[pl.BlockSpec((1, tm, A_WIDTH), lambda b, i: (b, i, 0)),
                   pl.BlockSpec((1, tm, B_WIDTH), lambda b, i: (b, i, 0))],
        out_shape=[out, out],
        compiler_params=_cparams(("parallel", "parallel")),
        name="local_mixers",
    )(p, p, p, p, p, p, p, p, p, ln_g.reshape(1, -1), ln_b.reshape(1, -1), ws.astype(BF16),
      bias, conv_w)


def _attn_kernel(*refs, tq, ck, sizes, lam_init):
    nsrc = len(sizes)
    cl_ref, g_ref, q_ref = refs[:3]
    kv_refs = refs[3:3 + 2 * nsrc]
    o_ref = refs[3 + 2 * nsrc]
    vt_refs = refs[4 + 2 * nsrc:]

    @pl.when(pl.program_id(2) == 0)
    def _():
        for src in range(nsrc):
            vt_refs[src][...] = kv_refs[2 * src + 1][0].astype(F32).T.astype(BF16)

    q = q_ref[0]
    lane = lax.broadcasted_iota(jnp.int32, (tq, LANES), 1)
    first = lane < C_HEAD_DIM
    zero = jnp.zeros_like(q)
    qq = jnp.concatenate([jnp.where(first, q, zero), jnp.where(first, zero, q)], axis=0)

    m = jnp.full((1, 2 * tq), -jnp.inf, F32)
    l = jnp.zeros((1, 2 * tq), F32)
    acc = jnp.zeros((C_V_DIM, 2 * tq), F32)
    chunks = [(kv_refs[2 * src], vt_refs[src], c0, min(ck, nk))
              for src, nk in enumerate(sizes) for c0 in range(0, nk, min(ck, nk))]

    def scores(chunk):
        k_ref, _, c0, c = chunk
        return lax.dot_general(k_ref[0, c0:c0 + c, :], qq, (((1,), (1,)), ((), ())),
                               preferred_element_type=F32)

    s_next = scores(chunks[0])
    for t, (_, vt_ref, c0, c) in enumerate(chunks):
        s = s_next
        if t + 1 < len(chunks):
            s_next = scores(chunks[t + 1])
        m_new = jnp.maximum(m, jnp.max(s, axis=0, keepdims=True))
        alpha = jnp.exp2(m - m_new)
        p = jnp.exp2(s - m_new)
        l = alpha * l + jnp.sum(p, axis=0, keepdims=True)
        acc = alpha * acc + jnp.dot(vt_ref[:, c0:c0 + c], p.astype(BF16),
                                    preferred_element_type=F32)
        m = m_new
    o2 = acc / l
    cl = cl_ref[...]
    lam = (jnp.exp(jnp.sum(cl[0:1] * cl[1:2], axis=-1, keepdims=True))
           - jnp.exp(jnp.sum(cl[2:3] * cl[3:4], axis=-1, keepdims=True)) + lam_init)
    o = o2[:, :tq] - lam * o2[:, tq:]
    ms = jnp.mean(o * o, axis=0, keepdims=True)
    y = o * lax.rsqrt(ms + EPS) * (g_ref[...] * (1.0 - lam_init))
    o_ref[0] = y.T.astype(BF16)


def _attn_call(q_arr, q_cb, sources, c_lambda, subln_g, *, lam_init):
    nb, n, _ = q_arr.shape
    tq = min(256, n)
    sizes = tuple(a.shape[1] for a, _, _ in sources)
    in_specs = [
        pl.BlockSpec((4, C_HEAD_DIM), lambda b, h, i: (0, 0)),
        pl.BlockSpec((C_V_DIM, 1), lambda b, h, i: (0, 0)),
        pl.BlockSpec((1, tq, LANES), lambda b, h, i: (b, i, q_cb + h)),
    ]
    args = [c_lambda, subln_g.reshape(-1, 1), q_arr]
    for arr, kcb, vcb in sources:
        nk = arr.shape[1]
        in_specs.append(pl.BlockSpec((1, nk, LANES), lambda b, h, i, kcb=kcb: (b, 0, kcb + h)))
        in_specs.append(pl.BlockSpec((1, nk, LANES), lambda b, h, i, vcb=vcb: (b, 0, vcb + h)))
        args += [arr, arr]
    kern = functools.partial(_attn_kernel, tq=tq, ck=1024, sizes=sizes, lam_init=lam_init)
    return pl.pallas_call(
        kern,
        grid=(nb, C_HEADS, n // tq),
        in_specs=in_specs,
        out_specs=pl.BlockSpec((1, tq, C_V_DIM), lambda b, h, i: (b, i, h)),
        out_shape=jax.ShapeDtypeStruct((nb, n, C_HEADS * C_V_DIM), BF16),
        scratch_shapes=[pltpu.VMEM((C_V_DIM, nk), BF16) for nk in sizes],
        compiler_params=_cparams(("parallel", "parallel", "arbitrary")),
        name="diff_attention",
    )(*args)


def _merge_kernel(x_ref, ya_ref, yb_ref, yc_ref, ga_ref, gb_ref, gc_ref, wa_ref, wb_ref, wc_ref,
                  wo_ref, g1_ref, n2_ref, sh2_ref, sc2_ref, xo_ref, ho_ref):
    a = jnp.dot(ya_ref[0], wa_ref[...], preferred_element_type=F32)
    b = jnp.dot(yb_ref[0], wb_ref[...], preferred_element_type=F32)
    c = jnp.dot(yc_ref[0], wc_ref[...], preferred_element_type=F32)
    y = (ga_ref[0].astype(F32) * a + gb_ref[0].astype(F32) * b + gc_ref[0].astype(F32) * c)
    m = jnp.dot(y.astype(BF16), wo_ref[...], preferred_element_type=F32)
    xn = x_ref[0] + g1_ref[0] * m
    xo_ref[0] = xn
    ho_ref[0] = _modulate(xn, n2_ref[...], sh2_ref[0], sc2_ref[0]).astype(BF16)


def _merge_call(x, ya, yb, yc, p, gate_cb, wa, wb, wc, wo, g1, n2g, sh2, sc2):
    nb, n, d = x.shape
    tm = min(512, n)

    def tok(w, cb=0):
        return pl.BlockSpec((1, tm, w), lambda b, i: (b, i, cb))

    vec = pl.BlockSpec((1, 1, d), lambda b, i: (b, 0, 0))
    return pl.pallas_call(
        _merge_kernel,
        grid=(nb, n // tm),
        in_specs=[tok(d), tok(A_WIDTH), tok(B_WIDTH), tok(d),
                  tok(d, gate_cb), tok(d, gate_cb + 1), tok(d, gate_cb + 2),
                  _resident(wa.shape, lambda b, i: (0, 0)), _resident(wb.shape, lambda b, i: (0, 0)),
                  _resident(wc.shape, lambda b, i: (0, 0)), _resident(wo.shape, lambda b, i: (0, 0)),
                  vec, pl.BlockSpec((1, d), lambda b, i: (0, 0)), vec, vec],
        out_specs=[tok(d), tok(d)],
        out_shape=[jax.ShapeDtypeStruct((nb, n, d), F32), jax.ShapeDtypeStruct((nb, n, d), BF16)],
        compiler_params=_cparams(("parallel", "parallel")),
        name="merge_branches",
    )(x, ya, yb, yc, p, p, p, wa, wb, wc, wo, g1, n2g.reshape(1, d), sh2, sc2)


def _ff_chunks(width, chunk=512):
    out, c0 = [], 0
    while c0 < width:
        cw = min(chunk, width - c0)
        out.append((c0, cw))
        c0 += cw
    return out


def _finish(xn, final, ng_ref, sh_ref, sc_ref, out_refs):
    if final:
        out_refs[0][0] = _rmsnorm(xn, ng_ref[...])
    else:
        out_refs[0][0] = xn
        out_refs[1][0] = _modulate(xn, ng_ref[...], sh_ref[0], sc_ref[0]).astype(BF16)


def _ffn_kernel(x_ref, h_ref, wg_ref, wu_ref, wd_ref, g2_ref, ng_ref, sh_ref, sc_ref, *out_refs,
                final):
    h = h_ref[0]
    acc = None
    for c0, cw in _ff_chunks(wg_ref.shape[1]):
        a = jnp.dot(h, wg_ref[:, c0:c0 + cw], preferred_element_type=F32)
        b = jnp.dot(h, wu_ref[:, c0:c0 + cw], preferred_element_type=F32)
        t = (_silu(a) * b).astype(BF16)
        part = jnp.dot(t, wd_ref[c0:c0 + cw, :], preferred_element_type=F32)
        acc = part if acc is None else acc + part
    xn = x_ref[0] + g2_ref[0] * acc
    _finish(xn, final, ng_ref, sh_ref, sc_ref, out_refs)


def _epilogue_specs(nb, n, d, tm, final, idx):
    tok = pl.BlockSpec((1, tm, d), idx)
    if final:
        return [tok], [jax.ShapeDtypeStruct((nb, n, d), F32)]
    return [tok, tok], [jax.ShapeDtypeStruct((nb, n, d), F32), jax.ShapeDtypeStruct((nb, n, d), BF16)]


def _ffn_call(x, h, wg, wu, wd, g2, ng, sh, sc, *, final):
    nb, n, d = x.shape
    tm = min(512, n)
    idx = lambda b, i: (b, i, 0)
    vec = pl.BlockSpec((1, 1, d), lambda b, i: (b, 0, 0))
    out_specs, out_shape = _epilogue_specs(nb, n, d, tm, final, idx)
    return pl.pallas_call(
        functools.partial(_ffn_kernel, final=final),
        grid=(nb, n // tm),
        in_specs=[pl.BlockSpec((1, tm, d), idx), pl.BlockSpec((1, tm, d), idx),
                  _resident(wg.shape, lambda b, i: (0, 0)), _resident(wu.shape, lambda b, i: (0, 0)),
                  _resident(wd.shape, lambda b, i: (0, 0)),
                  vec, pl.BlockSpec((1, d), lambda b, i: (0, 0)), vec, vec],
        out_specs=out_specs,
        out_shape=out_shape,
        compiler_params=_cparams(("parallel", "parallel")),
        name="dense_swiglu",
    )(x, h, wg, wu, wd, g2, ng.reshape(1, d), sh, sc)


MOE_TILE = 2048
MOE_WIN = 512
MOE_SLAB = 256
MOE_ROWS = 272
MOE_FF = 512
CNT_ROWS = 8


def _route_kernel(h_ref, wr_ref, comb_ref, rank_ref, rankt_ref, cnt_ref, *, n_experts, win):
    h = h_ref[0]
    tm = h.shape[0]
    lane = lax.broadcasted_iota(jnp.int32, (tm, LANES), 1)
    logits = jnp.dot(h, wr_ref[...], preferred_element_type=F32)
    lg = jnp.where(lane < n_experts, logits, -jnp.inf)
    m1 = jnp.max(lg, axis=-1, keepdims=True)
    i1 = jnp.min(jnp.where(lg == m1, lane, LANES), axis=-1, keepdims=True)
    lg2 = jnp.where(lane == i1, -jnp.inf, lg)
    m2 = jnp.max(lg2, axis=-1, keepdims=True)
    i2 = jnp.min(jnp.where(lg2 == m2, lane, LANES), axis=-1, keepdims=True)
    e2 = jnp.exp(m2 - m1)
    w1 = 1.0 / (1.0 + e2)
    comb_ref[0] = jnp.where(lane == i1, w1, 0.0) + jnp.where(lane == i2, e2 * w1, 0.0)
    sel = (lane == i1) | (lane == i2)
    tri = jnp.where(lax.broadcasted_iota(jnp.int32, (win, win), 0)
                    > lax.broadcasted_iota(jnp.int32, (win, win), 1), 1.0, 0.0).astype(BF16)
    base = jnp.zeros((1, LANES), F32)
    bases = []
    for w in range(tm // win):
        rows = slice(w * win, (w + 1) * win)
        sw = jnp.where(sel[rows], 1.0, 0.0)
        excl = jnp.dot(tri, sw.astype(BF16), preferred_element_type=F32)
        rank_ref[0, rows, :] = jnp.where(sel[rows], excl + base, -1.0)
        bases.append(base)
        base = base + jnp.sum(sw, axis=0, keepdims=True)
    bases.append(base)
    bases += [jnp.zeros((1, LANES), F32)] * (CNT_ROWS - len(bases))
    cnt_ref[0] = jnp.concatenate(bases, axis=0).astype(jnp.int32)
    rankt_ref[0] = rank_ref[0].T[:CNT_ROWS]


def _route_call(h, w_router, tm):
    nb, n, d = h.shape
    n_experts = w_router.shape[1]
    assert n_experts <= CNT_ROWS and tm // MOE_WIN + 1 <= CNT_ROWS
    nt = n // tm
    wr = jnp.zeros((d, LANES), BF16).at[:, :n_experts].set(w_router.astype(BF16))
    tok = pl.BlockSpec((1, tm, LANES), lambda b, i: (b, i, 0))
    per_tile = lambda r, c: pl.BlockSpec((1, r, c), lambda b, i: (b * nt + i, 0, 0))
    return pl.pallas_call(
        functools.partial(_route_kernel, n_experts=n_experts, win=min(MOE_WIN, tm)),
        grid=(nb, nt),
        in_specs=[pl.BlockSpec((1, tm, d), lambda b, i: (b, i, 0)),
                  pl.BlockSpec((d, LANES), lambda b, i: (0, 0))],
        out_specs=[tok, tok, per_tile(CNT_ROWS, tm), per_tile(CNT_ROWS, LANES)],
        out_shape=[jax.ShapeDtypeStruct((nb, n, LANES), F32), jax.ShapeDtypeStruct((nb, n, LANES), F32),
                   jax.ShapeDtypeStruct((nb * nt, CNT_ROWS, tm), F32),
                   jax.ShapeDtypeStruct((nb * nt, CNT_ROWS, LANES), jnp.int32)],
        compiler_params=_cparams(("parallel", "parallel")),
        name="moe_route",
    )(h, wr)


def _moe_kernel(cnt_ref, x_ref, h_ref, comb_ref, rank_ref, rankt_ref, wg_ref, wu_ref, wd_ref, g2_ref,
                ng_ref, sh_ref, sc_ref, *rest, final, n_experts, win, slab, rblk):
    out_refs, (hc_ref, yc_ref) = rest[:-2], rest[-2:]
    acc_ref = out_refs[0].at[0]
    b, i, e, f = (pl.program_id(k) for k in range(4))
    tm, d = h_ref.shape[1], h_ref.shape[2]
    nw = tm // win
    tile = b * pl.num_programs(1) + i
    cbase = tile * (CNT_ROWS * n_experts)

    def count(w):
        return cnt_ref[cbase + w * n_experts + e]

    def slabs(w):
        row0 = (count(w) // BF16_SUBLANES) * BF16_SUBLANES
        return row0, (count(w + 1) - row0 + slab - 1) // slab

    @pl.when((e == 0) & (f == 0))
    def _():
        acc_ref[...] = jnp.zeros_like(acc_ref)

    @pl.when(f == 0)
    def _():
        hc_ref[...] = jnp.zeros_like(hc_ref)
        yc_ref[...] = jnp.zeros_like(yc_ref)
        for w in range(nw):
            row0, nsl = slabs(w)
            rt = rankt_ref[0, 0, :, w * win:(w + 1) * win]
            hw = h_ref[0, w * win:(w + 1) * win, :]

            def body(s, carry, row0=row0, rt=rt, hw=hw):
                r0 = pl.multiple_of(row0 + s * slab, BF16_SUBLANES)
                rid = (r0 + lax.broadcasted_iota(jnp.int32, (slab, win), 0)).astype(F32)
                onehot = jnp.where(rt == rid, 1.0, 0.0).astype(BF16)
                rows = pl.ds(r0, slab)
                got = jnp.dot(onehot, hw, preferred_element_type=F32)
                hc_ref[rows, :] = (hc_ref[rows, :].astype(F32) + got).astype(BF16)
                return carry

            lax.fori_loop(0, nsl, body, 0)

    total = count(nw)

    def ffn(r, carry):
        rows = pl.ds(pl.multiple_of(r * rblk, BF16_SUBLANES), rblk)
        hb = hc_ref[rows, :]
        a = jnp.dot(hb, wg_ref[0], preferred_element_type=F32)
        u = jnp.dot(hb, wu_ref[0], preferred_element_type=F32)
        t = (_silu(a) * u).astype(BF16)
        yc_ref[rows, :] += jnp.dot(t, wd_ref[0], preferred_element_type=F32)
        return carry

    lax.fori_loop(0, (total + rblk - 1) // rblk, ffn, 0)

    @pl.when(f == pl.num_programs(3) - 1)
    def _():
        for w in range(nw):
            row0, nsl = slabs(w)
            trows = slice(w * win, (w + 1) * win)
            mine = lax.broadcasted_iota(jnp.int32, (win, LANES), 1) == e
            rcol = jnp.sum(jnp.where(mine, rank_ref[0, trows, :], 0.0), axis=-1, keepdims=True)
            wcol = jnp.sum(jnp.where(mine, comb_ref[0, trows, :], 0.0), axis=-1, keepdims=True)

            def body(s, carry, row0=row0, rcol=rcol, wcol=wcol, trows=trows):
                r0 = pl.multiple_of(row0 + s * slab, BF16_SUBLANES)
                cid = (r0 + lax.broadcasted_iota(jnp.int32, (win, slab), 1)).astype(F32)
                onehot = jnp.where(rcol == cid, 1.0, 0.0).astype(BF16)
                yb = yc_ref[pl.ds(r0, slab), :].astype(BF16)
                acc_ref[trows, :] += wcol * jnp.dot(onehot, yb, preferred_element_type=F32)
                return carry

            lax.fori_loop(0, nsl, body, 0)

    @pl.when((e == pl.num_programs(2) - 1) & (f == pl.num_programs(3) - 1))
    def _():
        xn = x_ref[0] + g2_ref[0] * acc_ref[...]
        _finish(xn, final, ng_ref, sh_ref, sc_ref, out_refs)


def _moe_call(x, h, w_router, wg, wu, wd, g2, ng, sh, sc, *, final):
    nb, n, d = x.shape
    n_experts, _, dff = wg.shape
    tm = min(MOE_TILE, n)
    win = min(MOE_WIN, tm)
    nt = n // tm
    comb, rank, rankt, cnt = _route_call(h, w_router, tm)
    cnt = cnt[:, :, :n_experts].reshape(-1)
    rankt = rankt.reshape(nb * nt, CNT_ROWS, 1, tm)
    cap_ffn = -(-tm // MOE_ROWS) * MOE_ROWS
    cap = -(-max(cap_ffn, tm + MOE_SLAB) // BF16_SUBLANES) * BF16_SUBLANES

    idx = lambda b, i, e, f, c: (b, i, 0)
    vec = pl.BlockSpec((1, 1, d), lambda b, i, e, f, c: (b, 0, 0))
    out_specs, out_shape = _epilogue_specs(nb, n, d, tm, final, idx)
    grid_spec = pltpu.PrefetchScalarGridSpec(
        num_scalar_prefetch=1,
        grid=(nb, nt, n_experts, dff // MOE_FF),
        in_specs=[_resident((1, tm, d), idx), _resident((1, tm, d), idx),
                  _resident((1, tm, LANES), idx), _resident((1, tm, LANES), idx),
                  pl.BlockSpec((1, 1, 1, tm), lambda b, i, e, f, c: (b * nt + i, e, 0, 0)),
                  pl.BlockSpec((1, d, MOE_FF), lambda b, i, e, f, c: (e, 0, f)),
                  pl.BlockSpec((1, d, MOE_FF), lambda b, i, e, f, c: (e, 0, f)),
                  pl.BlockSpec((1, MOE_FF, d), lambda b, i, e, f, c: (e, f, 0)),
                  vec, pl.BlockSpec((1, d), lambda b, i, e, f, c: (0, 0)), vec, vec],
        out_specs=out_specs,
        scratch_shapes=[pltpu.VMEM((cap, d), BF16), pltpu.VMEM((cap, d), F32)],
    )
    return pl.pallas_call(
        functools.partial(_moe_kernel, final=final, n_experts=n_experts, win=win, slab=MOE_SLAB,
                          rblk=MOE_ROWS),
        grid_spec=grid_spec,
        out_shape=out_shape,
        compiler_params=_cparams(("parallel", "parallel", "arbitrary", "arbitrary")),
        name="moe_swiglu",
    )(cnt, x, h, comb, rank, rankt, wg, wu, wd, g2, ng.reshape(1, d), sh, sc)


def kernel(x, c, ctx, c_ctx, w_ada, b_ada, norm1_g, norm2_g, w_in, b_gate, a_ln_g, a_ln_b, a_ws,
           a_bs, b_conv, c_lambda, c_subln_g, w_a_out, w_b_out, w_c_out, w_o, ff_w_gate, ff_w_up,
           ff_w_down, moe_w_router, moe_w_gate, moe_w_up, moe_w_down, final_norm_g):
    bsz, n, d = x.shape
    nc = ctx.shape[1]
    depth = w_ada.shape[0]
    cols = _Cols(d)
    head_cb = COL_BLOCK // LANES
    gate_cb = cols.out(cols.gate) * COL_BLOCK // d
    mix_pos = tuple(cols.out(j) for j in (cols.au, cols.av, cols.bb, cols.bc, cols.bh))
    cq_l, ck_l, cv_l = (cols.out(j) * head_cb for j in (cols.cq, cols.ck, cols.cv))

    pad = (-(bsz + 1)) % 8
    cc = jnp.concatenate([c, c_ctx[None], jnp.zeros((pad, d), F32)], axis=0)
    mod = _ada_call(cc, w_ada, b_ada)

    def mods(l):
        lat = [mod[l, :bsz, k * d:(k + 1) * d].reshape(bsz, 1, d) for k in range(N_MOD)]
        con = [mod[l, bsz:bsz + 1, k * d:(k + 1) * d].reshape(1, 1, d) for k in range(N_MOD)]
        return lat, con

    tables = _rope_tables(n)
    w_in_b = w_in.astype(BF16)
    xl = x
    xc = ctx.reshape(1, bsz * nc, d)
    lat, con = mods(0)
    h = _mod_call(xl, norm1_g[0], lat[0], lat[1])
    hc = _mod_call(xc, norm1_g[0], con[0], con[1])

    for l in range(depth):
        last = l == depth - 1
        lam_init = 0.8 - 0.6 * math.exp(-0.3 * l)
        lat, con = mods(l)
        if not last:
            nlat, ncon = mods(l + 1)
            nxt_l = (norm1_g[l + 1], nlat[0], nlat[1])
            nxt_c = (norm1_g[l + 1], ncon[0], ncon[1])
        else:
            nxt_l = (final_norm_g, lat[0], lat[1])
            nxt_c = None

        p = _in_call(h, w_in_b[l], b_gate[l], tables, j0=0, nj=cols.end, rope=True)
        if last:
            pc = _in_call(hc, w_in_b[l], b_gate[l], tables, j0=cols.ck, nj=cols.gate - cols.ck,
                          rope=False)
            ck_c, cv_c = 0, (cols.cv - cols.ck) * head_cb
        else:
            pc = _in_call(hc, w_in_b[l], b_gate[l], tables, j0=0, nj=cols.end, rope=False)
            ck_c, cv_c = ck_l, cv_l
        pc_seq = pc.reshape(bsz, nc, -1)

        wa, wb, wc, wo = (w.astype(BF16) for w in (w_a_out[l], w_b_out[l], w_c_out[l], w_o[l]))

        def channel(xs, hs, g2, nxt, final):
            i = l // 2
            if l % 2 == 0:
                return _ffn_call(xs, hs, ff_w_gate[i].astype(BF16), ff_w_up[i].astype(BF16),
                                 ff_w_down[i].astype(BF16), g2, *nxt, final=final)
            return _moe_call(xs, hs, moe_w_router[i], moe_w_gate[i].astype(BF16),
                             moe_w_up[i].astype(BF16), moe_w_down[i].astype(BF16), g2, *nxt,
                             final=final)

        ya, yb = _local_call(p, mix_pos, a_ln_g[l], a_ln_b[l], a_ws[l], a_bs[l], b_conv[l], seq=n)
        yc = _attn_call(p, cq_l, [(p, ck_l, cv_l), (pc_seq, ck_c, cv_c)],
                        c_lambda[l], c_subln_g[l], lam_init=lam_init)
        x1, h2 = _merge_call(xl, ya, yb, yc, p, gate_cb, wa, wb, wc, wo, lat[2], norm2_g[l],
                             lat[3], lat[4])
        res = channel(x1, h2, lat[5], nxt_l, last)
        if last:
            return res[0]
        xl, h = res

        ya_c, yb_c = _local_call(pc_seq, mix_pos, a_ln_g[l], a_ln_b[l], a_ws[l], a_bs[l],
                                 b_conv[l], seq=nc)
        yc_c = _attn_call(pc_seq, cq_l, [(pc_seq, ck_c, cv_c)], c_lambda[l], c_subln_g[l],
                          lam_init=lam_init)
        flat = lambda a: a.reshape(1, bsz * nc, -1)
        xc1, hc2 = _merge_call(xc, flat(ya_c), flat(yb_c), flat(yc_c), pc, gate_cb, wa, wb, wc, wo,
                               con[2], norm2_g[l], con[3], con[4])
        xc, hc = channel(xc1, hc2, con[5], nxt_c, False)
```

```python
import functools
import math

import jax
import jax.numpy as jnp
from jax import lax
from jax.experimental import pallas as pl
from jax.experimental.pallas import tpu as pltpu

F32 = jnp.float32
BF16 = jnp.bfloat16

EPS = 1e-6
GRID_W = 64
N_MOD = 6
N_BRANCH = 3
CHUNK = 128
A_GROUPS = 8
A_GROUP_DIM = 64
A_WIDTH = A_GROUPS * A_GROUP_DIM
B_WIDTH = 512
CONV_W = 3
C_HEADS = 8
C_HEAD_DIM = 64
C_V_DIM = 2 * C_HEAD_DIM
ROPE_BASE = 10000.0
TOP_K = 2
LOG2E = 1.4426950408889634

LANES = 128
BF16_SUBLANES = 16
V7X_VMEM_LIMIT = 56 * 1024 * 1024

COL_BLOCK = 512


def _cparams(sem):
    return pltpu.CompilerParams(dimension_semantics=sem, vmem_limit_bytes=V7X_VMEM_LIMIT)


def _resident(shape, index_map):
    return pl.BlockSpec(shape, index_map, pipeline_mode=pl.Buffered(1))


def _sigmoid(v):
    return 0.5 * (1.0 + jnp.tanh(0.5 * v))


def _silu(v):
    return v * _sigmoid(v)


def _modulate(x, g, sh, sc):
    ms = jnp.mean(x * x, axis=-1, keepdims=True)
    y = x * lax.rsqrt(ms + EPS)
    return (y * g) * (1.0 + sc) + sh


def _rmsnorm(x, g):
    ms = jnp.mean(x * x, axis=-1, keepdims=True)
    return x * lax.rsqrt(ms + EPS) * g


def _ada_kernel(c_ref, w_ref, b_ref, o_ref):
    s = _silu(c_ref[...])
    o_ref[0] = jnp.dot(s, w_ref[0], preferred_element_type=F32,
                       precision=lax.Precision.HIGHEST) + b_ref[0]


def _ada_call(cc, w_ada, b_ada):
    depth, d, cols = w_ada.shape
    rows = cc.shape[0]
    tn = 1536
    return pl.pallas_call(
        _ada_kernel,
        grid=(depth, cols // tn),
        in_specs=[
            pl.BlockSpec((rows, d), lambda l, j: (0, 0)),
            pl.BlockSpec((1, d, tn), lambda l, j: (l, 0, j)),
            pl.BlockSpec((1, 1, tn), lambda l, j: (l, 0, j)),
        ],
        out_specs=pl.BlockSpec((1, rows, tn), lambda l, j: (l, 0, j)),
        out_shape=jax.ShapeDtypeStruct((depth, rows, cols), F32),
        compiler_params=_cparams(("parallel", "parallel")),
        name="ada_proj",
    )(cc, w_ada, b_ada.reshape(depth, 1, cols))


def _mod_kernel(x_ref, g_ref, sh_ref, sc_ref, o_ref):
    o_ref[0] = _modulate(x_ref[0], g_ref[...], sh_ref[0], sc_ref[0]).astype(BF16)


def _mod_call(x, g, sh, sc):
    nb, n, d = x.shape
    tm = min(1024, n)
    vec = pl.BlockSpec((1, 1, d), lambda b, i: (b, 0, 0))
    return pl.pallas_call(
        _mod_kernel,
        grid=(nb, n // tm),
        in_specs=[pl.BlockSpec((1, tm, d), lambda b, i: (b, i, 0)),
                  pl.BlockSpec((1, d), lambda b, i: (0, 0)), vec, vec],
        out_specs=pl.BlockSpec((1, tm, d), lambda b, i: (b, i, 0)),
        out_shape=jax.ShapeDtypeStruct((nb, n, d), BF16),
        compiler_params=_cparams(("parallel", "parallel")),
        name="modulate",
    )(x, g.reshape(1, d), sh, sc)


class _Cols:
    def __init__(self, d):
        self.au = 0
        self.av = self.au + A_WIDTH // COL_BLOCK
        self.bb = self.av + A_WIDTH // COL_BLOCK
        self.bc = self.bb + B_WIDTH // COL_BLOCK
        self.bh = self.bc + B_WIDTH // COL_BLOCK
        self.cq = self.bh + B_WIDTH // COL_BLOCK
        qk = C_HEADS * 2 * C_HEAD_DIM // COL_BLOCK
        self.ck = self.cq + qk
        self.cv = self.ck + qk
        self.gate = self.cv + C_HEADS * C_V_DIM // COL_BLOCK
        self.end = self.gate + N_BRANCH * d // COL_BLOCK

    def out(self, jw):
        return (jw - self.gate) % self.end


def _in_kernel(h_ref, w_ref, bg_ref, cos_ref, sa_ref, sb_ref, o_ref, *, cols, j0, rope, qscale,
               sub):
    j = pl.program_id(2) + j0
    tm = h_ref.shape[1]

    def run(epilogue):
        def mm(r):
            return jnp.dot(h_ref[0, r * sub:(r + 1) * sub, :], w_ref[...],
                           preferred_element_type=F32)
        nxt = mm(0)
        for r in range(tm // sub):
            acc = nxt
            if r + 1 < tm // sub:
                nxt = mm(r + 1)
            epilogue(acc, slice(r * sub, (r + 1) * sub))

    @pl.when(j < cols.bb)
    def _():
        def gelu(acc, rows):
            o_ref[0, rows, :] = jax.nn.gelu(acc).astype(BF16)
        run(gelu)

    @pl.when(((j >= cols.bb) & (j < cols.cq)) | ((j >= cols.cv) & (j < cols.gate)))
    def _():
        def plain(acc, rows):
            o_ref[0, rows, :] = acc.astype(BF16)
        run(plain)

    @pl.when((j >= cols.cq) & (j < cols.cv))
    def _():
        scale = jnp.where(j < cols.ck, qscale, 1.0).astype(F32)

        def rotary(acc, rows):
            if not rope:
                o_ref[0, rows, :] = (acc * scale).astype(BF16)
                return
            cos, sa, sb = cos_ref[rows, :] * scale, sa_ref[rows, :] * scale, sb_ref[rows, :] * scale
            for s in range(COL_BLOCK // LANES):
                t = acc[:, s * LANES:(s + 1) * LANES]
                r = (t * cos + pltpu.roll(t, LANES - 16, 1) * sa + pltpu.roll(t, 16, 1) * sb)
                o_ref[0, rows, s * LANES:(s + 1) * LANES] = r.astype(BF16)
        run(rotary)

    @pl.when(j >= cols.gate)
    def _():
        def gate(acc, rows):
            o_ref[0, rows, :] = _sigmoid(acc + bg_ref[...]).astype(BF16)
        run(gate)


def _in_call(h, w, b_gate, tables, *, j0, nj, rope):
    nb, n, d = h.shape
    cols = _Cols(d)
    tm = min(2048, n)
    full = (j0 == 0) and (nj == cols.end)
    if rope:
        cos, sa, sb = tables
    else:
        cos = sa = sb = jnp.zeros((tm, LANES), F32)
    tab = pl.BlockSpec((tm, LANES), lambda b, i, j: (i if rope else 0, 0))

    def out_idx(b, i, j):
        if full:
            return (b, i, jnp.where(j >= cols.gate, j - cols.gate, j + cols.end - cols.gate))
        return (b, i, j)
    kern = functools.partial(_in_kernel, cols=cols, j0=j0, rope=rope,
                             qscale=C_HEAD_DIM ** -0.5 * LOG2E, sub=min(512, tm))
    return pl.pallas_call(
        kern,
        grid=(nb, n // tm, nj),
        in_specs=[
            pl.BlockSpec((1, tm, d), lambda b, i, j: (b, i, 0)),
            pl.BlockSpec((d, COL_BLOCK), lambda b, i, j: (0, j + j0)),
            pl.BlockSpec((1, COL_BLOCK), lambda b, i, j: (0, jnp.maximum(j + j0 - cols.gate, 0))),
            tab, tab, tab,
        ],
        out_specs=pl.BlockSpec((1, tm, COL_BLOCK), out_idx),
        out_shape=jax.ShapeDtypeStruct((nb, n, nj * COL_BLOCK), BF16),
        compiler_params=_cparams(("parallel", "parallel", "arbitrary")),
        name="in_proj",
    )(h, w, b_gate.reshape(1, -1), cos, sa, sb)


def _rope_tables(n):
    rows = n // GRID_W
    r = jnp.repeat(jnp.arange(rows, dtype=F32), GRID_W)
    col = jnp.tile(jnp.arange(GRID_W, dtype=F32), rows)
    quarter = C_HEAD_DIM // 4
    inv = ROPE_BASE ** (-jnp.arange(quarter, dtype=F32) / quarter)
    ar = r[:, None] * inv
    ac = col[:, None] * inv
    ang = jnp.concatenate([ar, ar, ac, ac], axis=-1)
    ang = jnp.tile(ang, (1, LANES // C_HEAD_DIM))
    cos, sin = jnp.cos(ang), jnp.sin(ang)
    first_of_pair = (jnp.arange(LANES) // quarter) % 2 == 0
    sa = jnp.where(first_of_pair, -sin, 0.0)
    sb = jnp.where(first_of_pair, 0.0, sin)
    return cos, sa, sb


def _local_kernel(u_ref, v_ref, bg_ref, cg_ref, hh_ref, cgp_ref, hhp_ref, cgn_ref, hhn_ref,
                  lng_ref, lnb_ref, ws_ref, bias_ref, cw_ref, ya_ref, yb_ref, *, tm, seq):
    i = pl.program_id(1)
    lane = lax.broadcasted_iota(jnp.int32, (CHUNK, LANES), 1)
    lo = lane < A_GROUP_DIM
    for c in range(tm // CHUNK):
        rows = pl.ds(c * CHUNK, CHUNK)
        v = v_ref[0, rows, :].astype(F32)
        mu = jnp.mean(v, axis=-1, keepdims=True)
        var = jnp.mean(jnp.square(v - mu), axis=-1, keepdims=True)
        vn = ((v - mu) * lax.rsqrt(var + EPS) * lng_ref[...] + lnb_ref[...]).astype(BF16)
        for k in range(A_WIDTH // LANES):
            blk = vn[:, k * LANES:(k + 1) * LANES]
            zero = jnp.zeros_like(blk)
            mixed = (jnp.dot(ws_ref[2 * k], jnp.where(lo, blk, zero), preferred_element_type=F32)
                     + jnp.dot(ws_ref[2 * k + 1], jnp.where(lo, zero, blk), preferred_element_type=F32)
                     + bias_ref[:, k * LANES:(k + 1) * LANES])
            u = u_ref[0, rows, k * LANES:(k + 1) * LANES].astype(F32)
            ya_ref[0, rows, k * LANES:(k + 1) * LANES] = (u * mixed).astype(BF16)

    z = cg_ref[0].astype(F32) * hh_ref[0].astype(F32)
    last = BF16_SUBLANES - 1
    z_prev = cgp_ref[0, last:last + 1, :].astype(F32) * hhp_ref[0, last:last + 1, :].astype(F32)
    z_next = cgn_ref[0, 0:1, :].astype(F32) * hhn_ref[0, 0:1, :].astype(F32)
    row = lax.broadcasted_iota(jnp.int32, (tm, 1), 0)
    pos = (i * tm) % seq + row
    zm1 = jnp.where(row == 0, z_prev, pltpu.roll(z, 1, 0))
    zm1 = jnp.where(pos == 0, 0.0, zm1)
    zp1 = jnp.where(row == tm - 1, z_next, pltpu.roll(z, tm - 1, 0))
    zp1 = jnp.where(pos == seq - 1, 0.0, zp1)
    conv = cw_ref[0:1, :] * zm1 + cw_ref[1:2, :] * z + cw_ref[2:3, :] * zp1
    yb_ref[0] = (bg_ref[0].astype(F32) * conv).astype(BF16)


def _local_call(p, pos, ln_g, ln_b, ws, bs, conv_w, *, seq):
    nb, n, _ = p.shape
    au, av, bb, bc, bh = pos
    tm = min(512, seq)
    hb = tm // BF16_SUBLANES
    nhb = n // BF16_SUBLANES

    def main(cb):
        return pl.BlockSpec((1, tm, COL_BLOCK), lambda b, i: (b, i, cb))

    def prev(cb):
        return pl.BlockSpec((1, BF16_SUBLANES, COL_BLOCK),
                            lambda b, i: (b, jnp.maximum(i * hb - 1, 0), cb))

    def nxt(cb):
        return pl.BlockSpec((1, BF16_SUBLANES, COL_BLOCK),
                            lambda b, i: (b, jnp.minimum((i + 1) * hb, nhb - 1), cb))

    bias = jnp.repeat(bs.T, A_GROUP_DIM, axis=1)
    out = jax.ShapeDtypeStruct((nb, n, A_WIDTH), BF16)
    kern = functools.partial(_local_kernel, tm=tm, seq=seq)
    return pl.pallas_call(
        kern,
        grid=(nb, n // tm),
        in_specs=[main(au), main(av), main(bb), main(bc), main(bh),
                  prev(bc), prev(bh), nxt(bc), nxt(bh),
                  pl.BlockSpec((1, A_WIDTH), lambda b, i: (0, 0)),
                  pl.BlockSpec((1, A_WIDTH), lambda b, i: (0, 0)),
                  pl.BlockSpec((A_GROUPS, CHUNK, CHUNK), lambda b, i: (0, 0, 0)),
                  pl.BlockSpec((CHUNK, A_WIDTH), lambda b, i: (0, 0)),
                  pl.BlockSpec((CONV_W, B_WIDTH), lambda b, i: (0, 0))],
        out_specs=[pl.BlockSpec((1, tm, A_WIDTH), lambda b, i: (b, i, 0)),
                   pl.BlockSpec((1, tm, B_WIDTH), lambda b, i: (b, i, 0))],
        out_shape=[out, out],
        compiler_params=_cparams(("parallel", "parallel")),
        name="local_mixers",
    )(p, p, p, p, p, p, p, p, p, ln_g.reshape(1, -1), ln_b.reshape(1, -1), ws.astype(BF16),
      bias, conv_w)


def _attn_kernel(*refs, tq, ck, sizes, lam_init):
    nsrc = len(sizes)
    cl_ref, g_ref, q_ref = refs[:3]
    kv_refs = refs[3:3 + 2 * nsrc]
    o_ref = refs[3 + 2 * nsrc]
    vt_refs = refs[4 + 2 * nsrc:]

    @pl.when(pl.program_id(2) == 0)
    def _():
        for src in range(nsrc):
            vt_refs[src][...] = kv_refs[2 * src + 1][0].astype(F32).T.astype(BF16)

    q = q_ref[0]
    lane = lax.broadcasted_iota(jnp.int32, (tq, LANES), 1)
    first = lane < C_HEAD_DIM
    zero = jnp.zeros_like(q)
    qs = [jnp.where(first, q, zero), jnp.where(first, zero, q)]
    ms = [jnp.full((1, tq), -jnp.inf, F32) for _ in qs]
    ls = [jnp.zeros((1, tq), F32) for _ in qs]
    accs = [jnp.zeros((C_V_DIM, tq), F32) for _ in qs]
    chunks = [(kv_refs[2 * src], vt_refs[src], c0, min(ck, nk))
              for src, nk in enumerate(sizes) for c0 in range(0, nk, min(ck, nk))]

    def scores(chunk, qm):
        k_ref, _, c0, c = chunk
        return lax.dot_general(k_ref[0, c0:c0 + c, :], qm, (((1,), (1,)), ((), ())),
                               preferred_element_type=F32)

    s_next = [scores(chunks[0], qm) for qm in qs]
    for t, (_, vt_ref, c0, c) in enumerate(chunks):
        s_cur = s_next
        if t + 1 < len(chunks):
            s_next = [scores(chunks[t + 1], qm) for qm in qs]
        vt = vt_ref[:, c0:c0 + c]
        for j, s in enumerate(s_cur):
            m_new = jnp.maximum(ms[j], jnp.max(s, axis=0, keepdims=True))
            alpha = jnp.exp2(ms[j] - m_new)
            p = jnp.exp2(s - m_new)
            ls[j] = alpha * ls[j] + jnp.sum(p, axis=0, keepdims=True)
            accs[j] = alpha * accs[j] + jnp.dot(vt, p.astype(BF16), preferred_element_type=F32)
            ms[j] = m_new
    cl = cl_ref[...]
    lam = (jnp.exp(jnp.sum(cl[0:1] * cl[1:2], axis=-1, keepdims=True))
           - jnp.exp(jnp.sum(cl[2:3] * cl[3:4], axis=-1, keepdims=True)) + lam_init)
    o = accs[0] / ls[0] - lam * (accs[1] / ls[1])
    ms = jnp.mean(o * o, axis=0, keepdims=True)
    y = o * lax.rsqrt(ms + EPS) * (g_ref[...] * (1.0 - lam_init))
    o_ref[0] = y.T.astype(BF16)


def _attn_call(q_arr, q_cb, sources, c_lambda, subln_g, *, lam_init):
    nb, n, _ = q_arr.shape
    tq = min(512, n)
    sizes = tuple(a.shape[1] for a, _, _ in sources)
    in_specs = [
        pl.BlockSpec((4, C_HEAD_DIM), lambda b, h, i: (0, 0)),
        pl.BlockSpec((C_V_DIM, 1), lambda b, h, i: (0, 0)),
        pl.BlockSpec((1, tq, LANES), lambda b, h, i: (b, i, q_cb + h)),
    ]
    args = [c_lambda, subln_g.reshape(-1, 1), q_arr]
    for arr, kcb, vcb in sources:
        nk = arr.shape[1]
        in_specs.append(pl.BlockSpec((1, nk, LANES), lambda b, h, i, kcb=kcb: (b, 0, kcb + h)))
        in_specs.append(pl.BlockSpec((1, nk, LANES), lambda b, h, i, vcb=vcb: (b, 0, vcb + h)))
        args += [arr, arr]
    kern = functools.partial(_attn_kernel, tq=tq, ck=512, sizes=sizes, lam_init=lam_init)
    return pl.pallas_call(
        kern,
        grid=(nb, C_HEADS, n // tq),
        in_specs=in_specs,
        out_specs=pl.BlockSpec((1, tq, C_V_DIM), lambda b, h, i: (b, i, h)),
        out_shape=jax.ShapeDtypeStruct((nb, n, C_HEADS * C_V_DIM), BF16),
        scratch_shapes=[pltpu.VMEM((C_V_DIM, nk), BF16) for nk in sizes],
        compiler_params=_cparams(("parallel", "parallel", "arbitrary")),
        name="diff_attention",
    )(*args)


def _merge_kernel(x_ref, ya_ref, yb_ref, yc_ref, ga_ref, gb_ref, gc_ref, wa_ref, wb_ref, wc_ref,
                  wo_ref, g1_ref, n2_ref, sh2_ref, sc2_ref, xo_ref, ho_ref):
    a = jnp.dot(ya_ref[0], wa_ref[...], preferred_element_type=F32)
    b = jnp.dot(yb_ref[0], wb_ref[...], preferred_element_type=F32)
    c = jnp.dot(yc_ref[0], wc_ref[...], preferred_element_type=F32)
    y = (ga_ref[0].astype(F32) * a + gb_ref[0].astype(F32) * b + gc_ref[0].astype(F32) * c)
    m = jnp.dot(y.astype(BF16), wo_ref[...], preferred_element_type=F32)
    xn = x_ref[0] + g1_ref[0] * m
    xo_ref[0] = xn
    ho_ref[0] = _modulate(xn, n2_ref[...], sh2_ref[0], sc2_ref[0]).astype(BF16)


def _merge_call(x, ya, yb, yc, p, gate_cb, wa, wb, wc, wo, g1, n2g, sh2, sc2):
    nb, n, d = x.shape
    tm = min(512, n)

    def tok(w, cb=0):
        return pl.BlockSpec((1, tm, w), lambda b, i: (b, i, cb))

    vec = pl.BlockSpec((1, 1, d), lambda b, i: (b, 0, 0))
    return pl.pallas_call(
        _merge_kernel,
        grid=(nb, n // tm),
        in_specs=[tok(d), tok(A_WIDTH), tok(B_WIDTH), tok(d),
                  tok(d, gate_cb), tok(d, gate_cb + 1), tok(d, gate_cb + 2),
                  _resident(wa.shape, lambda b, i: (0, 0)), _resident(wb.shape, lambda b, i: (0, 0)),
                  _resident(wc.shape, lambda b, i: (0, 0)), _resident(wo.shape, lambda b, i: (0, 0)),
                  vec, pl.BlockSpec((1, d), lambda b, i: (0, 0)), vec, vec],
        out_specs=[tok(d), tok(d)],
        out_shape=[jax.ShapeDtypeStruct((nb, n, d), F32), jax.ShapeDtypeStruct((nb, n, d), BF16)],
        compiler_params=_cparams(("parallel", "parallel")),
        name="merge_branches",
    )(x, ya, yb, yc, p, p, p, wa, wb, wc, wo, g1, n2g.reshape(1, d), sh2, sc2)


def _ff_chunks(width, chunk=512):
    out, c0 = [], 0
    while c0 < width:
        cw = min(chunk, width - c0)
        out.append((c0, cw))
        c0 += cw
    return out


def _finish(xn, final, ng_ref, sh_ref, sc_ref, out_refs):
    if final:
        out_refs[0][0] = _rmsnorm(xn, ng_ref[...])
    else:
        out_refs[0][0] = xn
        out_refs[1][0] = _modulate(xn, ng_ref[...], sh_ref[0], sc_ref[0]).astype(BF16)


def _ffn_kernel(x_ref, h_ref, wg_ref, wu_ref, wd_ref, g2_ref, ng_ref, sh_ref, sc_ref, *out_refs,
                final):
    h = h_ref[0]
    acc = None
    for c0, cw in _ff_chunks(wg_ref.shape[1]):
        a = jnp.dot(h, wg_ref[:, c0:c0 + cw], preferred_element_type=F32)
        b = jnp.dot(h, wu_ref[:, c0:c0 + cw], preferred_element_type=F32)
        t = (_silu(a) * b).astype(BF16)
        part = jnp.dot(t, wd_ref[c0:c0 + cw, :], preferred_element_type=F32)
        acc = part if acc is None else acc + part
    xn = x_ref[0] + g2_ref[0] * acc
    _finish(xn, final, ng_ref, sh_ref, sc_ref, out_refs)


def _epilogue_specs(nb, n, d, tm, final, idx):
    tok = pl.BlockSpec((1, tm, d), idx)
    if final:
        return [tok], [jax.ShapeDtypeStruct((nb, n, d), F32)]
    return [tok, tok], [jax.ShapeDtypeStruct((nb, n, d), F32), jax.ShapeDtypeStruct((nb, n, d), BF16)]


def _ffn_call(x, h, wg, wu, wd, g2, ng, sh, sc, *, final):
    nb, n, d = x.shape
    tm = min(512, n)
    idx = lambda b, i: (b, i, 0)
    vec = pl.BlockSpec((1, 1, d), lambda b, i: (b, 0, 0))
    out_specs, out_shape = _epilogue_specs(nb, n, d, tm, final, idx)
    return pl.pallas_call(
        functools.partial(_ffn_kernel, final=final),
        grid=(nb, n // tm),
        in_specs=[pl.BlockSpec((1, tm, d), idx), pl.BlockSpec((1, tm, d), idx),
                  _resident(wg.shape, lambda b, i: (0, 0)), _resident(wu.shape, lambda b, i: (0, 0)),
                  _resident(wd.shape, lambda b, i: (0, 0)),
                  vec, pl.BlockSpec((1, d), lambda b, i: (0, 0)), vec, vec],
        out_specs=out_specs,
        out_shape=out_shape,
        compiler_params=_cparams(("parallel", "parallel")),
        name="dense_swiglu",
    )(x, h, wg, wu, wd, g2, ng.reshape(1, d), sh, sc)


MOE_TILE = 2048
MOE_WIN = 512
MOE_SLAB = 256
MOE_ROWS = 272
MOE_FF = 512
CNT_ROWS = 8


def _route_kernel(h_ref, wr_ref, comb_ref, rank_ref, rankt_ref, cnt_ref, *, n_experts, win):
    h = h_ref[0]
    tm = h.shape[0]
    lane = lax.broadcasted_iota(jnp.int32, (tm, LANES), 1)
    logits = jnp.dot(h, wr_ref[...], preferred_element_type=F32)
    lg = jnp.where(lane < n_experts, logits, -jnp.inf)
    m1 = jnp.max(lg, axis=-1, keepdims=True)
    i1 = jnp.min(jnp.where(lg == m1, lane, LANES), axis=-1, keepdims=True)
    lg2 = jnp.where(lane == i1, -jnp.inf, lg)
    m2 = jnp.max(lg2, axis=-1, keepdims=True)
    i2 = jnp.min(jnp.where(lg2 == m2, lane, LANES), axis=-1, keepdims=True)
    e2 = jnp.exp(m2 - m1)
    w1 = 1.0 / (1.0 + e2)
    comb_ref[0] = jnp.where(lane == i1, w1, 0.0) + jnp.where(lane == i2, e2 * w1, 0.0)
    sel = (lane == i1) | (lane == i2)
    tri = jnp.where(lax.broadcasted_iota(jnp.int32, (win, win), 0)
                    > lax.broadcasted_iota(jnp.int32, (win, win), 1), 1.0, 0.0).astype(BF16)
    base = jnp.zeros((1, LANES), F32)
    bases = []
    for w in range(tm // win):
        rows = slice(w * win, (w + 1) * win)
        sw = jnp.where(sel[rows], 1.0, 0.0)
        excl = jnp.dot(tri, sw.astype(BF16), preferred_element_type=F32)
        rank_ref[0, rows, :] = jnp.where(sel[rows], excl + base, -1.0)
        bases.append(base)
        base = base + jnp.sum(sw, axis=0, keepdims=True)
    bases.append(base)
    bases += [jnp.zeros((1, LANES), F32)] * (CNT_ROWS - len(bases))
    cnt_ref[0] = jnp.concatenate(bases, axis=0).astype(jnp.int32)
    rankt_ref[0] = rank_ref[0].T[:CNT_ROWS]


def _route_call(h, w_router, tm):
    nb, n, d = h.shape
    n_experts = w_router.shape[1]
    assert n_experts <= CNT_ROWS and tm // MOE_WIN + 1 <= CNT_ROWS
    nt = n // tm
    wr = jnp.zeros((d, LANES), BF16).at[:, :n_experts].set(w_router.astype(BF16))
    tok = pl.BlockSpec((1, tm, LANES), lambda b, i: (b, i, 0))
    per_tile = lambda r, c: pl.BlockSpec((1, r, c), lambda b, i: (b * nt + i, 0, 0))
    return pl.pallas_call(
        functools.partial(_route_kernel, n_experts=n_experts, win=min(MOE_WIN, tm)),
        grid=(nb, nt),
        in_specs=[pl.BlockSpec((1, tm, d), lambda b, i: (b, i, 0)),
                  pl.BlockSpec((d, LANES), lambda b, i: (0, 0))],
        out_specs=[tok, tok, per_tile(CNT_ROWS, tm), per_tile(CNT_ROWS, LANES)],
        out_shape=[jax.ShapeDtypeStruct((nb, n, LANES), F32), jax.ShapeDtypeStruct((nb, n, LANES), F32),
                   jax.ShapeDtypeStruct((nb * nt, CNT_ROWS, tm), F32),
                   jax.ShapeDtypeStruct((nb * nt, CNT_ROWS, LANES), jnp.int32)],
        compiler_params=_cparams(("parallel", "parallel")),
        name="moe_route",
    )(h, wr)


def _moe_kernel(cnt_ref, x_ref, h_ref, comb_ref, rank_ref, rankt_ref, wg_ref, wu_ref, wd_ref, g2_ref,
                ng_ref, sh_ref, sc_ref, *rest, final, n_experts, win, slab, rblk):
    out_refs, (hc_ref, yc_ref) = rest[:-2], rest[-2:]
    acc_ref = out_refs[0].at[0]
    b, i, e, f = (pl.program_id(k) for k in range(4))
    tm, d = h_ref.shape[1], h_ref.shape[2]
    nw = tm // win
    tile = b * pl.num_programs(1) + i
    cbase = tile * (CNT_ROWS * n_experts)

    def count(w):
        return cnt_ref[cbase + w * n_experts + e]

    def slabs(w):
        row0 = (count(w) // BF16_SUBLANES) * BF16_SUBLANES
        return row0, (count(w + 1) - row0 + slab - 1) // slab

    @pl.when((e == 0) & (f == 0))
    def _():
        acc_ref[...] = jnp.zeros_like(acc_ref)

    total = count(nw)

    @pl.when(f == 0)
    def _():
        def zero(s, carry):
            rows = pl.ds(pl.multiple_of(s * slab, slab), slab)
            hc_ref[rows, :] = jnp.zeros((slab, d), BF16)
            yc_ref[rows, :] = jnp.zeros((slab, d), F32)
            return carry

        lax.fori_loop(0, jnp.minimum((total + rblk + 2 * slab) // slab, hc_ref.shape[0] // slab),
                      zero, 0)
        for w in range(nw):
            row0, nsl = slabs(w)
            rt = rankt_ref[0, 0, :, w * win:(w + 1) * win]
            hw = h_ref[0, w * win:(w + 1) * win, :]

            def body(s, carry, row0=row0, rt=rt, hw=hw):
                r0 = pl.multiple_of(row0 + s * slab, BF16_SUBLANES)
                rid = (r0 + lax.broadcasted_iota(jnp.int32, (slab, win), 0)).astype(F32)
                onehot = jnp.where(rt == rid, 1.0, 0.0).astype(BF16)
                rows = pl.ds(r0, slab)
                got = jnp.dot(onehot, hw, preferred_element_type=F32)
                hc_ref[rows, :] = (hc_ref[rows, :].astype(F32) + got).astype(BF16)
                return carry

            lax.fori_loop(0, nsl, body, 0)

    def ffn(r, carry):
        rows = pl.ds(pl.multiple_of(r * rblk, BF16_SUBLANES), rblk)
        hb = hc_ref[rows, :]
        a = jnp.dot(hb, wg_ref[0, 0], preferred_element_type=F32)
        u = jnp.dot(hb, wu_ref[0, 0], preferred_element_type=F32)
        t = (_silu(a) * u).astype(BF16)
        yc_ref[rows, :] += jnp.dot(t, wd_ref[0], preferred_element_type=F32)
        return carry

    lax.fori_loop(0, (total + rblk - 1) // rblk, ffn, 0)

    @pl.when(f == pl.num_programs(3) - 1)
    def _():
        for w in range(nw):
            row0, nsl = slabs(w)
            trows = slice(w * win, (w + 1) * win)
            mine = lax.broadcasted_iota(jnp.int32, (win, LANES), 1) == e
            rcol = jnp.sum(jnp.where(mine, rank_ref[0, trows, :], 0.0), axis=-1, keepdims=True)
            wcol = jnp.sum(jnp.where(mine, comb_ref[0, trows, :], 0.0), axis=-1, keepdims=True)

            def body(s, carry, row0=row0, rcol=rcol, wcol=wcol, trows=trows):
                r0 = pl.multiple_of(row0 + s * slab, BF16_SUBLANES)
                cid = (r0 + lax.broadcasted_iota(jnp.int32, (win, slab), 1)).astype(F32)
                onehot = jnp.where(rcol == cid, 1.0, 0.0).astype(BF16)
                yb = yc_ref[pl.ds(r0, slab), :].astype(BF16)
                acc_ref[trows, :] += wcol * jnp.dot(onehot, yb, preferred_element_type=F32)
                return carry

            lax.fori_loop(0, nsl, body, 0)

    @pl.when((e == pl.num_programs(2) - 1) & (f == pl.num_programs(3) - 1))
    def _():
        xn = x_ref[0] + g2_ref[0] * acc_ref[...]
        _finish(xn, final, ng_ref, sh_ref, sc_ref, out_refs)


def _moe_call(x, h, w_router, wg, wu, wd, g2, ng, sh, sc, *, final):
    nb, n, d = x.shape
    n_experts, _, dff = wg.shape
    tm = min(MOE_TILE, n)
    win = min(MOE_WIN, tm)
    nt = n // tm
    comb, rank, rankt, cnt = _route_call(h, w_router, tm)
    cnt = cnt[:, :, :n_experts].reshape(-1)
    blocked = lambda w: w.reshape(n_experts, d, dff // MOE_FF, MOE_FF).transpose(0, 2, 1, 3)
    wg, wu = blocked(wg), blocked(wu)
    rankt = rankt.reshape(nb * nt, CNT_ROWS, 1, tm)
    cap_ffn = -(-tm // MOE_ROWS) * MOE_ROWS
    cap = -(-max(cap_ffn, tm + MOE_SLAB) // MOE_SLAB) * MOE_SLAB

    idx = lambda b, i, e, f, c: (b, i, 0)
    vec = pl.BlockSpec((1, 1, d), lambda b, i, e, f, c: (b, 0, 0))
    out_specs, out_shape = _epilogue_specs(nb, n, d, tm, final, idx)
    grid_spec = pltpu.PrefetchScalarGridSpec(
        num_scalar_prefetch=1,
        grid=(nb, nt, n_experts, dff // MOE_FF),
        in_specs=[_resident((1, tm, d), idx), _resident((1, tm, d), idx),
                  _resident((1, tm, LANES), idx), _resident((1, tm, LANES), idx),
                  pl.BlockSpec((1, 1, 1, tm), lambda b, i, e, f, c: (b * nt + i, e, 0, 0)),
                  pl.BlockSpec((1, 1, d, MOE_FF), lambda b, i, e, f, c: (e, f, 0, 0)),
                  pl.BlockSpec((1, 1, d, MOE_FF), lambda b, i, e, f, c: (e, f, 0, 0)),
                  pl.BlockSpec((1, MOE_FF, d), lambda b, i, e, f, c: (e, f, 0)),
                  vec, pl.BlockSpec((1, d), lambda b, i, e, f, c: (0, 0)), vec, vec],
        out_specs=out_specs,
        scratch_shapes=[pltpu.VMEM((cap, d), BF16), pltpu.VMEM((cap, d), F32)],
    )
    return pl.pallas_call(
        functools.partial(_moe_kernel, final=final, n_experts=n_experts, win=win, slab=MOE_SLAB,
                          rblk=MOE_ROWS),
        grid_spec=grid_spec,
        out_shape=out_shape,
        compiler_params=_cparams(("parallel", "parallel", "arbitrary", "arbitrary")),
        name="moe_swiglu",
    )(cnt, x, h, comb, rank, rankt, wg, wu, wd, g2, ng.reshape(1, d), sh, sc)


def kernel(x, c, ctx, c_ctx, w_ada, b_ada, norm1_g, norm2_g, w_in, b_gate, a_ln_g, a_ln_b, a_ws,
           a_bs, b_conv, c_lambda, c_subln_g, w_a_out, w_b_out, w_c_out, w_o, ff_w_gate, ff_w_up,
           ff_w_down, moe_w_router, moe_w_gate, moe_w_up, moe_w_down, final_norm_g):
    bsz, n, d = x.shape
    nc = ctx.shape[1]
    depth = w_ada.shape[0]
    cols = _Cols(d)
    head_cb = COL_BLOCK // LANES
    gate_cb = cols.out(cols.gate) * COL_BLOCK // d
    mix_pos = tuple(cols.out(j) for j in (cols.au, cols.av, cols.bb, cols.bc, cols.bh))
    cq_l, ck_l, cv_l = (cols.out(j) * head_cb for j in (cols.cq, cols.ck, cols.cv))

    pad = (-(bsz + 1)) % 8
    cc = jnp.concatenate([c, c_ctx[None], jnp.zeros((pad, d), F32)], axis=0)
    mod = _ada_call(cc, w_ada, b_ada)

    def mods(l):
        lat = [mod[l, :bsz, k * d:(k + 1) * d].reshape(bsz, 1, d) for k in range(N_MOD)]
        con = [mod[l, bsz:bsz + 1, k * d:(k + 1) * d].reshape(1, 1, d) for k in range(N_MOD)]
        return lat, con

    tables = _rope_tables(n)
    w_in_b = w_in.astype(BF16)
    xl = x
    xc = ctx.reshape(1, bsz * nc, d)
    lat, con = mods(0)
    h = _mod_call(xl, norm1_g[0], lat[0], lat[1])
    hc = _mod_call(xc, norm1_g[0], con[0], con[1])

    for l in range(depth):
        last = l == depth - 1
        lam_init = 0.8 - 0.6 * math.exp(-0.3 * l)
        lat, con = mods(l)
        if not last:
            nlat, ncon = mods(l + 1)
            nxt_l = (norm1_g[l + 1], nlat[0], nlat[1])
            nxt_c = (norm1_g[l + 1], ncon[0], ncon[1])
        else:
            nxt_l = (final_norm_g, lat[0], lat[1])
            nxt_c = None

        p = _in_call(h, w_in_b[l], b_gate[l], tables, j0=0, nj=cols.end, rope=True)
        if last:
            pc = _in_call(hc, w_in_b[l], b_gate[l], tables, j0=cols.ck, nj=cols.gate - cols.ck,
                          rope=False)
            ck_c, cv_c = 0, (cols.cv - cols.ck) * head_cb
        else:
            pc = _in_call(hc, w_in_b[l], b_gate[l], tables, j0=0, nj=cols.end, rope=False)
            ck_c, cv_c = ck_l, cv_l
        pc_seq = pc.reshape(bsz, nc, -1)

        wa, wb, wc, wo = (w.astype(BF16) for w in (w_a_out[l], w_b_out[l], w_c_out[l], w_o[l]))

        def channel(xs, hs, g2, nxt, final):
            i = l // 2
            if l % 2 == 0:
                return _ffn_call(xs, hs, ff_w_gate[i].astype(BF16), ff_w_up[i].astype(BF16),
                                 ff_w_down[i].astype(BF16), g2, *nxt, final=final)
            return _moe_call(xs, hs, moe_w_router[i], moe_w_gate[i].astype(BF16),
                             moe_w_up[i].astype(BF16), moe_w_down[i].astype(BF16), g2, *nxt,
                             final=final)

        ya, yb = _local_call(p, mix_pos, a_ln_g[l], a_ln_b[l], a_ws[l], a_bs[l], b_conv[l], seq=n)
        yc = _attn_call(p, cq_l, [(p, ck_l, cv_l), (pc_seq, ck_c, cv_c)],
                        c_lambda[l], c_subln_g[l], lam_init=lam_init)
        x1, h2 = _merge_call(xl, ya, yb, yc, p, gate_cb, wa, wb, wc, wo, lat[2], norm2_g[l],
                             lat[3], lat[4])
        res = channel(x1, h2, lat[5], nxt_l, last)
        if last:
            return res[0]
        xl, h = res

        ya_c, yb_c = _local_call(pc_seq, mix_pos, a_ln_g[l], a_ln_b[l], a_ws[l], a_bs[l],
                                 b_conv[l], seq=nc)
        yc_c = _attn_call(pc_seq, cq_l, [(pc_seq, ck_c, cv_c)], c_lambda[l], c_subln_g[l],
                          lam_init=lam_init)
        flat = lambda a: a.reshape(1, bsz * nc, -1)
        xc1, hc2 = _merge_call(xc, flat(ya_c), flat(yb_c), flat(yc_c), pc, gate_cb, wa, wb, wc, wo,
                               con[2], norm2_g[l], con[3], con[4])
        xc, hc = channel(xc1, hc2, con[5], nxt_c, False)
```

```python
import functools
import math

import jax
import jax.numpy as jnp
from jax import lax
from jax.experimental import pallas as pl
from jax.experimental.pallas import tpu as pltpu

F32 = jnp.float32
BF16 = jnp.bfloat16

EPS = 1e-6
GRID_W = 64
N_MOD = 6
N_BRANCH = 3
CHUNK = 128
A_GROUPS = 8
A_GROUP_DIM = 64
A_WIDTH = A_GROUPS * A_GROUP_DIM
B_WIDTH = 512
CONV_W = 3
C_HEADS = 8
C_HEAD_DIM = 64
C_V_DIM = 2 * C_HEAD_DIM
ROPE_BASE = 10000.0
TOP_K = 2
LOG2E = 1.4426950408889634

LANES = 128
BF16_SUBLANES = 16
V7X_VMEM_LIMIT = 56 * 1024 * 1024

COL_BLOCK = 512


def _cparams(sem):
    return pltpu.CompilerParams(dimension_semantics=sem, vmem_limit_bytes=V7X_VMEM_LIMIT)


def _resident(shape, index_map):
    return pl.BlockSpec(shape, index_map, pipeline_mode=pl.Buffered(1))


def _sigmoid(v):
    return 0.5 * (1.0 + jnp.tanh(0.5 * v))


def _silu(v):
    return v * _sigmoid(v)


def _modulate(x, g, sh, sc):
    ms = jnp.mean(x * x, axis=-1, keepdims=True)
    y = x * lax.rsqrt(ms + EPS)
    return (y * g) * (1.0 + sc) + sh


def _rmsnorm(x, g):
    ms = jnp.mean(x * x, axis=-1, keepdims=True)
    return x * lax.rsqrt(ms + EPS) * g


def _ada_kernel(c_ref, w_ref, b_ref, o_ref):
    s = _silu(c_ref[...])
    o_ref[0] = jnp.dot(s, w_ref[0], preferred_element_type=F32,
                       precision=lax.Precision.HIGHEST) + b_ref[0]


def _ada_call(cc, w_ada, b_ada):
    depth, d, cols = w_ada.shape
    rows = cc.shape[0]
    tn = 1536
    return pl.pallas_call(
        _ada_kernel,
        grid=(depth, cols // tn),
        in_specs=[
            pl.BlockSpec((rows, d), lambda l, j: (0, 0)),
            pl.BlockSpec((1, d, tn), lambda l, j: (l, 0, j)),
            pl.BlockSpec((1, 1, tn), lambda l, j: (l, 0, j)),
        ],
        out_specs=pl.BlockSpec((1, rows, tn), lambda l, j: (l, 0, j)),
        out_shape=jax.ShapeDtypeStruct((depth, rows, cols), F32),
        compiler_params=_cparams(("parallel", "parallel")),
        name="ada_proj",
    )(cc, w_ada, b_ada.reshape(depth, 1, cols))


def _mod_kernel(x_ref, g_ref, sh_ref, sc_ref, o_ref):
    o_ref[0] = _modulate(x_ref[0], g_ref[...], sh_ref[0], sc_ref[0]).astype(BF16)


def _mod_call(x, g, sh, sc):
    nb, n, d = x.shape
    tm = min(1024, n)
    vec = pl.BlockSpec((1, 1, d), lambda b, i: (b, 0, 0))
    return pl.pallas_call(
        _mod_kernel,
        grid=(nb, n // tm),
        in_specs=[pl.BlockSpec((1, tm, d), lambda b, i: (b, i, 0)),
                  pl.BlockSpec((1, d), lambda b, i: (0, 0)), vec, vec],
        out_specs=pl.BlockSpec((1, tm, d), lambda b, i: (b, i, 0)),
        out_shape=jax.ShapeDtypeStruct((nb, n, d), BF16),
        compiler_params=_cparams(("parallel", "parallel")),
        name="modulate",
    )(x, g.reshape(1, d), sh, sc)


class _Cols:
    def __init__(self, d):
        self.au = 0
        self.av = self.au + A_WIDTH // COL_BLOCK
        self.bb = self.av + A_WIDTH // COL_BLOCK
        self.bc = self.bb + B_WIDTH // COL_BLOCK
        self.bh = self.bc + B_WIDTH // COL_BLOCK
        self.cq = self.bh + B_WIDTH // COL_BLOCK
        qk = C_HEADS * 2 * C_HEAD_DIM // COL_BLOCK
        self.ck = self.cq + qk
        self.cv = self.ck + qk
        self.gate = self.cv + C_HEADS * C_V_DIM // COL_BLOCK
        self.end = self.gate + N_BRANCH * d // COL_BLOCK

    def out(self, jw):
        return (jw - self.gate) % self.end


def _in_kernel(h_ref, w_ref, bg_ref, cos_ref, sa_ref, sb_ref, o_ref, *, cols, j0, rope, qscale,
               sub):
    j = pl.program_id(2) + j0
    tm = h_ref.shape[1]

    def run(epilogue):
        def mm(r):
            return jnp.dot(h_ref[0, r * sub:(r + 1) * sub, :], w_ref[...],
                           preferred_element_type=F32)
        nxt = mm(0)
        for r in range(tm // sub):
            acc = nxt
            if r + 1 < tm // sub:
                nxt = mm(r + 1)
            epilogue(acc, slice(r * sub, (r + 1) * sub))

    @pl.when(j < cols.bb)
    def _():
        def gelu(acc, rows):
            o_ref[0, rows, :] = jax.nn.gelu(acc).astype(BF16)
        run(gelu)

    @pl.when(((j >= cols.bb) & (j < cols.cq)) | ((j >= cols.cv) & (j < cols.gate)))
    def _():
        def plain(acc, rows):
            o_ref[0, rows, :] = acc.astype(BF16)
        run(plain)

    @pl.when((j >= cols.cq) & (j < cols.cv))
    def _():
        scale = jnp.where(j < cols.ck, qscale, 1.0).astype(F32)

        def rotary(acc, rows):
            if not rope:
                o_ref[0, rows, :] = (acc * scale).astype(BF16)
                return
            cos, sa, sb = cos_ref[rows, :] * scale, sa_ref[rows, :] * scale, sb_ref[rows, :] * scale
            for s in range(COL_BLOCK // LANES):
                t = acc[:, s * LANES:(s + 1) * LANES]
                r = (t * cos + pltpu.roll(t, LANES - 16, 1) * sa + pltpu.roll(t, 16, 1) * sb)
                o_ref[0, rows, s * LANES:(s + 1) * LANES] = r.astype(BF16)
        run(rotary)

    @pl.when(j >= cols.gate)
    def _():
        def gate(acc, rows):
            o_ref[0, rows, :] = _sigmoid(acc + bg_ref[...]).astype(BF16)
        run(gate)


def _in_call(h, w, b_gate, tables, *, j0, nj, rope):
    nb, n, d = h.shape
    cols = _Cols(d)
    tm = min(2048, n)
    full = (j0 == 0) and (nj == cols.end)
    if rope:
        cos, sa, sb = tables
    else:
        cos = sa = sb = jnp.zeros((tm, LANES), F32)
    tab = pl.BlockSpec((tm, LANES), lambda b, i, j: (i if rope else 0, 0))

    def out_idx(b, i, j):
        if full:
            return (b, i, jnp.where(j >= cols.gate, j - cols.gate, j + cols.end - cols.gate))
        return (b, i, j)
    kern = functools.partial(_in_kernel, cols=cols, j0=j0, rope=rope,
                             qscale=C_HEAD_DIM ** -0.5 * LOG2E, sub=min(512, tm))
    return pl.pallas_call(
        kern,
        grid=(nb, n // tm, nj),
        in_specs=[
            pl.BlockSpec((1, tm, d), lambda b, i, j: (b, i, 0)),
            pl.BlockSpec((d, COL_BLOCK), lambda b, i, j: (0, j + j0)),
            pl.BlockSpec((1, COL_BLOCK), lambda b, i, j: (0, jnp.maximum(j + j0 - cols.gate, 0))),
            tab, tab, tab,
        ],
        out_specs=pl.BlockSpec((1, tm, COL_BLOCK), out_idx),
        out_shape=jax.ShapeDtypeStruct((nb, n, nj * COL_BLOCK), BF16),
        compiler_params=_cparams(("parallel", "parallel", "arbitrary")),
        name="in_proj",
    )(h, w, b_gate.reshape(1, -1), cos, sa, sb)


def _rope_tables(n):
    rows = n // GRID_W
    r = jnp.repeat(jnp.arange(rows, dtype=F32), GRID_W)
    col = jnp.tile(jnp.arange(GRID_W, dtype=F32), rows)
    quarter = C_HEAD_DIM // 4
    inv = ROPE_BASE ** (-jnp.arange(quarter, dtype=F32) / quarter)
    ar = r[:, None] * inv
    ac = col[:, None] * inv
    ang = jnp.concatenate([ar, ar, ac, ac], axis=-1)
    ang = jnp.tile(ang, (1, LANES // C_HEAD_DIM))
    cos, sin = jnp.cos(ang), jnp.sin(ang)
    first_of_pair = (jnp.arange(LANES) // quarter) % 2 == 0
    sa = jnp.where(first_of_pair, -sin, 0.0)
    sb = jnp.where(first_of_pair, 0.0, sin)
    return cos, sa, sb


def _local_kernel(u_ref, v_ref, bg_ref, cg_ref, hh_ref, cgp_ref, hhp_ref, cgn_ref, hhn_ref,
                  lng_ref, lnb_ref, ws_ref, bias_ref, cw_ref, ya_ref, yb_ref, *, tm, seq):
    i = pl.program_id(1)
    lane = lax.broadcasted_iota(jnp.int32, (CHUNK, LANES), 1)
    lo = lane < A_GROUP_DIM
    for c in range(tm // CHUNK):
        rows = pl.ds(c * CHUNK, CHUNK)
        v = v_ref[0, rows, :].astype(F32)
        mu = jnp.mean(v, axis=-1, keepdims=True)
        var = jnp.mean(jnp.square(v - mu), axis=-1, keepdims=True)
        vn = ((v - mu) * lax.rsqrt(var + EPS) * lng_ref[...] + lnb_ref[...]).astype(BF16)
        for k in range(A_WIDTH // LANES):
            blk = vn[:, k * LANES:(k + 1) * LANES]
            zero = jnp.zeros_like(blk)
            mixed = (jnp.dot(ws_ref[2 * k], jnp.where(lo, blk, zero), preferred_element_type=F32)
                     + jnp.dot(ws_ref[2 * k + 1], jnp.where(lo, zero, blk), preferred_element_type=F32)
                     + bias_ref[:, k * LANES:(k + 1) * LANES])
            u = u_ref[0, rows, k * LANES:(k + 1) * LANES].astype(F32)
            ya_ref[0, rows, k * LANES:(k + 1) * LANES] = (u * mixed).astype(BF16)

    z = cg_ref[0].astype(F32) * hh_ref[0].astype(F32)
    last = BF16_SUBLANES - 1
    z_prev = cgp_ref[0, last:last + 1, :].astype(F32) * hhp_ref[0, last:last + 1, :].astype(F32)
    z_next = cgn_ref[0, 0:1, :].astype(F32) * hhn_ref[0, 0:1, :].astype(F32)
    row = lax.broadcasted_iota(jnp.int32, (tm, 1), 0)
    pos = (i * tm) % seq + row
    zm1 = jnp.where(row == 0, z_prev, pltpu.roll(z, 1, 0))
    zm1 = jnp.where(pos == 0, 0.0, zm1)
    zp1 = jnp.where(row == tm - 1, z_next, pltpu.roll(z, tm - 1, 0))
    zp1 = jnp.where(pos == seq - 1, 0.0, zp1)
    conv = cw_ref[0:1, :] * zm1 + cw_ref[1:2, :] * z + cw_ref[2:3, :] * zp1
    yb_ref[0] = (bg_ref[0].astype(F32) * conv).astype(BF16)


def _local_call(p, pos, ln_g, ln_b, ws, bs, conv_w, *, seq):
    nb, n, _ = p.shape
    au, av, bb, bc, bh = pos
    tm = min(512, seq)
    hb = tm // BF16_SUBLANES
    nhb = n // BF16_SUBLANES

    def main(cb):
        return pl.BlockSpec((1, tm, COL_BLOCK), lambda b, i: (b, i, cb))

    def prev(cb):
        return pl.BlockSpec((1, BF16_SUBLANES, COL_BLOCK),
                            lambda b, i: (b, jnp.maximum(i * hb - 1, 0), cb))

    def nxt(cb):
        return pl.BlockSpec((1, BF16_SUBLANES, COL_BLOCK),
                            lambda b, i: (b, jnp.minimum((i + 1) * hb, nhb - 1), cb))

    bias = jnp.repeat(bs.T, A_GROUP_DIM, axis=1)
    out = jax.ShapeDtypeStruct((nb, n, A_WIDTH), BF16)
    kern = functools.partial(_local_kernel, tm=tm, seq=seq)
    return pl.pallas_call(
        kern,
        grid=(nb, n // tm),
        in_specs=[main(au), main(av), main(bb), main(bc), main(bh),
                  prev(bc), prev(bh), nxt(bc), nxt(bh),
                  pl.BlockSpec((1, A_WIDTH), lambda b, i: (0, 0)),
                  pl.BlockSpec((1, A_WIDTH), lambda b, i: (0, 0)),
                  pl.BlockSpec((A_GROUPS, CHUNK, CHUNK), lambda b, i: (0, 0, 0)),
                  pl.BlockSpec((CHUNK, A_WIDTH), lambda b, i: (0, 0)),
                  pl.BlockSpec((CONV_W, B_WIDTH), lambda b, i: (0, 0))],
        out_specs=[pl.BlockSpec((1, tm, A_WIDTH), lambda b, i: (b, i, 0)),
                   pl.BlockSpec((1, tm, B_WIDTH), lambda b, i: (b, i, 0))],
        out_shape=[out, out],
        compiler_params=_cparams(("parallel", "parallel")),
        name="local_mixers",
    )(p, p, p, p, p, p, p, p, p, ln_g.reshape(1, -1), ln_b.reshape(1, -1), ws.astype(BF16),
      bias, conv_w)


def _attn_kernel(*refs, tq, ck, sizes, lam_init):
    nsrc = len(sizes)
    cl_ref, g_ref, q_ref = refs[:3]
    kv_refs = refs[3:3 + 2 * nsrc]
    o_ref = refs[3 + 2 * nsrc]
    vt_refs = refs[4 + 2 * nsrc:4 + 3 * nsrc]
    kmax_ref = refs[4 + 3 * nsrc]
    half = lax.broadcasted_iota(jnp.int32, (1, LANES), 1) < C_HEAD_DIM

    @pl.when(pl.program_id(2) == 0)
    def _():
        kmax = [jnp.zeros((1, 1), F32), jnp.zeros((1, 1), F32)]
        for src in range(nsrc):
            vt_refs[src][...] = kv_refs[2 * src + 1][0].astype(F32).T.astype(BF16)
            ksq = jnp.square(kv_refs[2 * src][0].astype(F32))
            for j, keep in enumerate((half, jnp.logical_not(half))):
                n2 = jnp.sum(jnp.where(keep, ksq, 0.0), axis=-1, keepdims=True)
                kmax[j] = jnp.maximum(kmax[j], jnp.max(n2, axis=0, keepdims=True))
        for j in range(2):
            kmax_ref[j:j + 1, :] = jnp.broadcast_to(jnp.sqrt(kmax[j]), (1, LANES))

    q = q_ref[0]
    zero = jnp.zeros_like(q)
    qs = [jnp.where(half, q, zero), jnp.where(half, zero, q)]
    chunks = [(kv_refs[2 * src], vt_refs[src], c0, min(ck, nk))
              for src, nk in enumerate(sizes) for c0 in range(0, nk, min(ck, nk))]

    def scores(chunk, qm):
        k_ref, _, c0, c = chunk
        return lax.dot_general(k_ref[0, c0:c0 + c, :], qm, (((1,), (1,)), ((), ())),
                               preferred_element_type=F32)

    def finish(accs, ls):
        cl = cl_ref[...]
        lam = (jnp.exp(jnp.sum(cl[0:1] * cl[1:2], axis=-1, keepdims=True))
               - jnp.exp(jnp.sum(cl[2:3] * cl[3:4], axis=-1, keepdims=True)) + lam_init)
        o = accs[0] / ls[0] - lam * (accs[1] / ls[1])
        ms = jnp.mean(o * o, axis=0, keepdims=True)
        y = o * lax.rsqrt(ms + EPS) * (g_ref[...] * (1.0 - lam_init))
        o_ref[0] = y.T.astype(BF16)

    qsq = jnp.square(q.astype(F32).T)
    refs_r = [jnp.sqrt(jnp.sum(qsq[j * C_HEAD_DIM:(j + 1) * C_HEAD_DIM], axis=0, keepdims=True))
              * kmax_ref[j:j + 1, 0:1] for j in range(2)]
    ls = [jnp.zeros((1, tq), F32) for _ in qs]
    accs = [jnp.zeros((C_V_DIM, tq), F32) for _ in qs]
    s_next = [scores(chunks[0], qm) for qm in qs]
    for t, (_, vt_ref, c0, c) in enumerate(chunks):
        s_cur = s_next
        if t + 1 < len(chunks):
            s_next = [scores(chunks[t + 1], qm) for qm in qs]
        vt = vt_ref[:, c0:c0 + c]
        for j, s in enumerate(s_cur):
            p = jnp.exp2(s - refs_r[j])
            ls[j] = ls[j] + jnp.sum(p, axis=0, keepdims=True)
            accs[j] = accs[j] + jnp.dot(vt, p.astype(BF16), preferred_element_type=F32)
    finish(accs, ls)

    healthy = jnp.min(jnp.minimum(ls[0], ls[1])) >= 2.0 ** -60

    @pl.when(jnp.logical_not(healthy))
    def _():
        ms = [jnp.full((1, tq), -jnp.inf, F32) for _ in qs]
        ls = [jnp.zeros((1, tq), F32) for _ in qs]
        accs = [jnp.zeros((C_V_DIM, tq), F32) for _ in qs]
        for chunk in chunks:
            _, vt_ref, c0, c = chunk
            vt = vt_ref[:, c0:c0 + c]
            for j, qm in enumerate(qs):
                s = scores(chunk, qm)
                m_new = jnp.maximum(ms[j], jnp.max(s, axis=0, keepdims=True))
                alpha = jnp.exp2(ms[j] - m_new)
                p = jnp.exp2(s - m_new)
                ls[j] = alpha * ls[j] + jnp.sum(p, axis=0, keepdims=True)
                accs[j] = alpha * accs[j] + jnp.dot(vt, p.astype(BF16),
                                                    preferred_element_type=F32)
                ms[j] = m_new
        finish(accs, ls)


def _attn_call(q_arr, q_cb, sources, c_lambda, subln_g, *, lam_init):
    nb, n, _ = q_arr.shape
    tq = min(512, n)
    sizes = tuple(a.shape[1] for a, _, _ in sources)
    in_specs = [
        pl.BlockSpec((4, C_HEAD_DIM), lambda b, h, i: (0, 0)),
        pl.BlockSpec((C_V_DIM, 1), lambda b, h, i: (0, 0)),
        pl.BlockSpec((1, tq, LANES), lambda b, h, i: (b, i, q_cb + h)),
    ]
    args = [c_lambda, subln_g.reshape(-1, 1), q_arr]
    for arr, kcb, vcb in sources:
        nk = arr.shape[1]
        in_specs.append(pl.BlockSpec((1, nk, LANES), lambda b, h, i, kcb=kcb: (b, 0, kcb + h)))
        in_specs.append(pl.BlockSpec((1, nk, LANES), lambda b, h, i, vcb=vcb: (b, 0, vcb + h)))
        args += [arr, arr]
    kern = functools.partial(_attn_kernel, tq=tq, ck=2048, sizes=sizes, lam_init=lam_init)
    return pl.pallas_call(
        kern,
        grid=(nb, C_HEADS, n // tq),
        in_specs=in_specs,
        out_specs=pl.BlockSpec((1, tq, C_V_DIM), lambda b, h, i: (b, i, h)),
        out_shape=jax.ShapeDtypeStruct((nb, n, C_HEADS * C_V_DIM), BF16),
        scratch_shapes=[pltpu.VMEM((C_V_DIM, nk), BF16) for nk in sizes]
        + [pltpu.VMEM((8, LANES), F32)],
        compiler_params=_cparams(("parallel", "parallel", "arbitrary")),
        name="diff_attention",
    )(*args)


def _merge_kernel(x_ref, ya_ref, yb_ref, yc_ref, ga_ref, gb_ref, gc_ref, wa_ref, wb_ref, wc_ref,
                  wo_ref, g1_ref, n2_ref, sh2_ref, sc2_ref, xo_ref, ho_ref):
    a = jnp.dot(ya_ref[0], wa_ref[...], preferred_element_type=F32)
    b = jnp.dot(yb_ref[0], wb_ref[...], preferred_element_type=F32)
    c = jnp.dot(yc_ref[0], wc_ref[...], preferred_element_type=F32)
    y = (ga_ref[0].astype(F32) * a + gb_ref[0].astype(F32) * b + gc_ref[0].astype(F32) * c)
    m = jnp.dot(y.astype(BF16), wo_ref[...], preferred_element_type=F32)
    xn = x_ref[0] + g1_ref[0] * m
    xo_ref[0] = xn
    ho_ref[0] = _modulate(xn, n2_ref[...], sh2_ref[0], sc2_ref[0]).astype(BF16)


def _merge_call(x, ya, yb, yc, p, gate_cb, wa, wb, wc, wo, g1, n2g, sh2, sc2):
    nb, n, d = x.shape
    tm = min(512, n)

    def tok(w, cb=0):
        return pl.BlockSpec((1, tm, w), lambda b, i: (b, i, cb))

    vec = pl.BlockSpec((1, 1, d), lambda b, i: (b, 0, 0))
    return pl.pallas_call(
        _merge_kernel,
        grid=(nb, n // tm),
        in_specs=[tok(d), tok(A_WIDTH), tok(B_WIDTH), tok(d),
                  tok(d, gate_cb), tok(d, gate_cb + 1), tok(d, gate_cb + 2),
                  _resident(wa.shape, lambda b, i: (0, 0)), _resident(wb.shape, lambda b, i: (0, 0)),
                  _resident(wc.shape, lambda b, i: (0, 0)), _resident(wo.shape, lambda b, i: (0, 0)),
                  vec, pl.BlockSpec((1, d), lambda b, i: (0, 0)), vec, vec],
        out_specs=[tok(d), tok(d)],
        out_shape=[jax.ShapeDtypeStruct((nb, n, d), F32), jax.ShapeDtypeStruct((nb, n, d), BF16)],
        compiler_params=_cparams(("parallel", "parallel")),
        name="merge_branches",
    )(x, ya, yb, yc, p, p, p, wa, wb, wc, wo, g1, n2g.reshape(1, d), sh2, sc2)


def _ff_chunks(width, chunk=512):
    out, c0 = [], 0
    while c0 < width:
        cw = min(chunk, width - c0)
        out.append((c0, cw))
        c0 += cw
    return out


def _finish(xn, final, ng_ref, sh_ref, sc_ref, out_refs):
    if final:
        out_refs[0][0] = _rmsnorm(xn, ng_ref[...])
    else:
        out_refs[0][0] = xn
        out_refs[1][0] = _modulate(xn, ng_ref[...], sh_ref[0], sc_ref[0]).astype(BF16)


def _ffn_kernel(x_ref, h_ref, wg_ref, wu_ref, wd_ref, g2_ref, ng_ref, sh_ref, sc_ref, *out_refs,
                final):
    h = h_ref[0]
    acc = None
    for c0, cw in _ff_chunks(wg_ref.shape[1]):
        a = jnp.dot(h, wg_ref[:, c0:c0 + cw], preferred_element_type=F32)
        b = jnp.dot(h, wu_ref[:, c0:c0 + cw], preferred_element_type=F32)
        t = (_silu(a) * b).astype(BF16)
        part = jnp.dot(t, wd_ref[c0:c0 + cw, :], preferred_element_type=F32)
        acc = part if acc is None else acc + part
    xn = x_ref[0] + g2_ref[0] * acc
    _finish(xn, final, ng_ref, sh_ref, sc_ref, out_refs)


def _epilogue_specs(nb, n, d, tm, final, idx):
    tok = pl.BlockSpec((1, tm, d), idx)
    if final:
        return [tok], [jax.ShapeDtypeStruct((nb, n, d), F32)]
    return [tok, tok], [jax.ShapeDtypeStruct((nb, n, d), F32), jax.ShapeDtypeStruct((nb, n, d), BF16)]


def _ffn_call(x, h, wg, wu, wd, g2, ng, sh, sc, *, final):
    nb, n, d = x.shape
    tm = min(512, n)
    idx = lambda b, i: (b, i, 0)
    vec = pl.BlockSpec((1, 1, d), lambda b, i: (b, 0, 0))
    out_specs, out_shape = _epilogue_specs(nb, n, d, tm, final, idx)
    return pl.pallas_call(
        functools.partial(_ffn_kernel, final=final),
        grid=(nb, n // tm),
        in_specs=[pl.BlockSpec((1, tm, d), idx), pl.BlockSpec((1, tm, d), idx),
                  _resident(wg.shape, lambda b, i: (0, 0)), _resident(wu.shape, lambda b, i: (0, 0)),
                  _resident(wd.shape, lambda b, i: (0, 0)),
                  vec, pl.BlockSpec((1, d), lambda b, i: (0, 0)), vec, vec],
        out_specs=out_specs,
        out_shape=out_shape,
        compiler_params=_cparams(("parallel", "parallel")),
        name="dense_swiglu",
    )(x, h, wg, wu, wd, g2, ng.reshape(1, d), sh, sc)


MOE_TILE = 2048
MOE_WIN = 512
MOE_SLAB = 256
MOE_ROWS = 544
MOE_FF = 512
CNT_ROWS = 8


def _route_kernel(h_ref, wr_ref, comb_ref, rank_ref, rankt_ref, cnt_ref, *, n_experts, win):
    h = h_ref[0]
    tm = h.shape[0]
    lane = lax.broadcasted_iota(jnp.int32, (tm, LANES), 1)
    logits = jnp.dot(h, wr_ref[...], preferred_element_type=F32)
    lg = jnp.where(lane < n_experts, logits, -jnp.inf)
    m1 = jnp.max(lg, axis=-1, keepdims=True)
    i1 = jnp.min(jnp.where(lg == m1, lane, LANES), axis=-1, keepdims=True)
    lg2 = jnp.where(lane == i1, -jnp.inf, lg)
    m2 = jnp.max(lg2, axis=-1, keepdims=True)
    i2 = jnp.min(jnp.where(lg2 == m2, lane, LANES), axis=-1, keepdims=True)
    e2 = jnp.exp(m2 - m1)
    w1 = 1.0 / (1.0 + e2)
    comb_ref[0] = jnp.where(lane == i1, w1, 0.0) + jnp.where(lane == i2, e2 * w1, 0.0)
    sel = (lane == i1) | (lane == i2)
    tri = jnp.where(lax.broadcasted_iota(jnp.int32, (win, win), 0)
                    > lax.broadcasted_iota(jnp.int32, (win, win), 1), 1.0, 0.0).astype(BF16)
    base = jnp.zeros((1, LANES), F32)
    bases = []
    for w in range(tm // win):
        rows = slice(w * win, (w + 1) * win)
        sw = jnp.where(sel[rows], 1.0, 0.0)
        excl = jnp.dot(tri, sw.astype(BF16), preferred_element_type=F32)
        rank_ref[0, rows, :] = jnp.where(sel[rows], excl + base, -1.0)
        bases.append(base)
        base = base + jnp.sum(sw, axis=0, keepdims=True)
    bases.append(base)
    bases += [jnp.zeros((1, LANES), F32)] * (CNT_ROWS - len(bases))
    cnt_ref[0] = jnp.concatenate(bases, axis=0).astype(jnp.int32)
    rankt_ref[0] = rank_ref[0].T[:CNT_ROWS]


def _route_call(h, w_router, tm):
    nb, n, d = h.shape
    n_experts = w_router.shape[1]
    assert n_experts <= CNT_ROWS and tm // MOE_WIN + 1 <= CNT_ROWS
    nt = n // tm
    wr = jnp.zeros((d, LANES), BF16).at[:, :n_experts].set(w_router.astype(BF16))
    tok = pl.BlockSpec((1, tm, LANES), lambda b, i: (b, i, 0))
    per_tile = lambda r, c: pl.BlockSpec((1, r, c), lambda b, i: (b * nt + i, 0, 0))
    return pl.pallas_call(
        functools.partial(_route_kernel, n_experts=n_experts, win=min(MOE_WIN, tm)),
        grid=(nb, nt),
        in_specs=[pl.BlockSpec((1, tm, d), lambda b, i: (b, i, 0)),
                  pl.BlockSpec((d, LANES), lambda b, i: (0, 0))],
        out_specs=[tok, tok, per_tile(CNT_ROWS, tm), per_tile(CNT_ROWS, LANES)],
        out_shape=[jax.ShapeDtypeStruct((nb, n, LANES), F32), jax.ShapeDtypeStruct((nb, n, LANES), F32),
                   jax.ShapeDtypeStruct((nb * nt, CNT_ROWS, tm), F32),
                   jax.ShapeDtypeStruct((nb * nt, CNT_ROWS, LANES), jnp.int32)],
        compiler_params=_cparams(("parallel", "parallel")),
        name="moe_route",
    )(h, wr)


def _moe_kernel(cnt_ref, x_ref, h_ref, comb_ref, rank_ref, rankt_ref, wg_ref, wu_ref, wd_ref, g2_ref,
                ng_ref, sh_ref, sc_ref, *rest, final, n_experts, win, slab, rblk):
    out_refs, (hc_ref, yc_ref) = rest[:-2], rest[-2:]
    acc_ref = out_refs[0].at[0]
    b, i, e, f = (pl.program_id(k) for k in range(4))
    tm, d = h_ref.shape[1], h_ref.shape[2]
    nw = tm // win
    tile = b * pl.num_programs(1) + i
    cbase = tile * (CNT_ROWS * n_experts)

    def count(w):
        return cnt_ref[cbase + w * n_experts + e]

    def slabs(w):
        row0 = (count(w) // BF16_SUBLANES) * BF16_SUBLANES
        return row0, (count(w + 1) - row0 + slab - 1) // slab

    @pl.when((e == 0) & (f == 0))
    def _():
        acc_ref[...] = jnp.zeros_like(acc_ref)

    total = count(nw)

    @pl.when(f == 0)
    def _():
        def zero(s, carry):
            rows = pl.ds(pl.multiple_of(s * slab, slab), slab)
            hc_ref[rows, :] = jnp.zeros((slab, d), BF16)
            yc_ref[rows, :] = jnp.zeros((slab, d), F32)
            return carry

        lax.fori_loop(0, jnp.minimum((total + rblk + 2 * slab) // slab, hc_ref.shape[0] // slab),
                      zero, 0)
        for w in range(nw):
            row0, nsl = slabs(w)
            rt = rankt_ref[0, 0, :, w * win:(w + 1) * win]
            hw = h_ref[0, w * win:(w + 1) * win, :]

            def body(s, carry, row0=row0, rt=rt, hw=hw):
                r0 = pl.multiple_of(row0 + s * slab, BF16_SUBLANES)
                rid = (r0 + lax.broadcasted_iota(jnp.int32, (slab, win), 0)).astype(F32)
                onehot = jnp.where(rt == rid, 1.0, 0.0).astype(BF16)
                rows = pl.ds(r0, slab)
                got = jnp.dot(onehot, hw, preferred_element_type=F32)
                hc_ref[rows, :] = (hc_ref[rows, :].astype(F32) + got).astype(BF16)
                return carry

            lax.fori_loop(0, nsl, body, 0)

    def ffn(r, carry):
        rows = pl.ds(pl.multiple_of(r * rblk, BF16_SUBLANES), rblk)
        hb = hc_ref[rows, :]
        a = jnp.dot(hb, wg_ref[0, 0], preferred_element_type=F32)
        u = jnp.dot(hb, wu_ref[0, 0], preferred_element_type=F32)
        t = (_silu(a) * u).astype(BF16)
        yc_ref[rows, :] += jnp.dot(t, wd_ref[0], preferred_element_type=F32)
        return carry

    lax.fori_loop(0, (total + rblk - 1) // rblk, ffn, 0)

    @pl.when(f == pl.num_programs(3) - 1)
    def _():
        for w in range(nw):
            row0, nsl = slabs(w)
            trows = slice(w * win, (w + 1) * win)
            mine = lax.broadcasted_iota(jnp.int32, (win, LANES), 1) == e
            rcol = jnp.sum(jnp.where(mine, rank_ref[0, trows, :], 0.0), axis=-1, keepdims=True)
            wcol = jnp.sum(jnp.where(mine, comb_ref[0, trows, :], 0.0), axis=-1, keepdims=True)

            def body(s, carry, row0=row0, rcol=rcol, wcol=wcol, trows=trows):
                r0 = pl.multiple_of(row0 + s * slab, BF16_SUBLANES)
                cid = (r0 + lax.broadcasted_iota(jnp.int32, (win, slab), 1)).astype(F32)
                onehot = jnp.where(rcol == cid, 1.0, 0.0).astype(BF16)
                yb = yc_ref[pl.ds(r0, slab), :].astype(BF16)
                acc_ref[trows, :] += wcol * jnp.dot(onehot, yb, preferred_element_type=F32)
                return carry

            lax.fori_loop(0, nsl, body, 0)

    @pl.when((e == pl.num_programs(2) - 1) & (f == pl.num_programs(3) - 1))
    def _():
        xn = x_ref[0] + g2_ref[0] * acc_ref[...]
        _finish(xn, final, ng_ref, sh_ref, sc_ref, out_refs)


def _moe_call(x, h, w_router, wg, wu, wd, g2, ng, sh, sc, *, final):
    nb, n, d = x.shape
    n_experts, _, dff = wg.shape
    tm = min(MOE_TILE, n)
    win = min(MOE_WIN, tm)
    nt = n // tm
    comb, rank, rankt, cnt = _route_call(h, w_router, tm)
    cnt = cnt[:, :, :n_experts].reshape(-1)
    blocked = lambda w: w.reshape(n_experts, d, dff // MOE_FF, MOE_FF).transpose(0, 2, 1, 3)
    wg, wu = blocked(wg), blocked(wu)
    rankt = rankt.reshape(nb * nt, CNT_ROWS, 1, tm)
    cap_ffn = -(-tm // MOE_ROWS) * MOE_ROWS
    cap = -(-max(cap_ffn, tm + MOE_SLAB) // MOE_SLAB) * MOE_SLAB

    idx = lambda b, i, e, f, c: (b, i, 0)
    vec = pl.BlockSpec((1, 1, d), lambda b, i, e, f, c: (b, 0, 0))
    out_specs, out_shape = _epilogue_specs(nb, n, d, tm, final, idx)
    grid_spec = pltpu.PrefetchScalarGridSpec(
        num_scalar_prefetch=1,
        grid=(nb, nt, n_experts, dff // MOE_FF),
        in_specs=[_resident((1, tm, d), idx), _resident((1, tm, d), idx),
                  _resident((1, tm, LANES), idx), _resident((1, tm, LANES), idx),
                  pl.BlockSpec((1, 1, 1, tm), lambda b, i, e, f, c: (b * nt + i, e, 0, 0)),
                  pl.BlockSpec((1, 1, d, MOE_FF), lambda b, i, e, f, c: (e, f, 0, 0)),
                  pl.BlockSpec((1, 1, d, MOE_FF), lambda b, i, e, f, c: (e, f, 0, 0)),
                  pl.BlockSpec((1, MOE_FF, d), lambda b, i, e, f, c: (e, f, 0)),
                  vec, pl.BlockSpec((1, d), lambda b, i, e, f, c: (0, 0)), vec, vec],
        out_specs=out_specs,
        scratch_shapes=[pltpu.VMEM((cap, d), BF16), pltpu.VMEM((cap, d), F32)],
    )
    return pl.pallas_call(
        functools.partial(_moe_kernel, final=final, n_experts=n_experts, win=win, slab=MOE_SLAB,
                          rblk=MOE_ROWS),
        grid_spec=grid_spec,
        out_shape=out_shape,
        compiler_params=_cparams(("parallel", "parallel", "arbitrary", "arbitrary")),
        name="moe_swiglu",
    )(cnt, x, h, comb, rank, rankt, wg, wu, wd, g2, ng.reshape(1, d), sh, sc)


def kernel(x, c, ctx, c_ctx, w_ada, b_ada, norm1_g, norm2_g, w_in, b_gate, a_ln_g, a_ln_b, a_ws,
           a_bs, b_conv, c_lambda, c_subln_g, w_a_out, w_b_out, w_c_out, w_o, ff_w_gate, ff_w_up,
           ff_w_down, moe_w_router, moe_w_gate, moe_w_up, moe_w_down, final_norm_g):
    bsz, n, d = x.shape
    nc = ctx.shape[1]
    depth = w_ada.shape[0]
    cols = _Cols(d)
    head_cb = COL_BLOCK // LANES
    gate_cb = cols.out(cols.gate) * COL_BLOCK // d
    mix_pos = tuple(cols.out(j) for j in (cols.au, cols.av, cols.bb, cols.bc, cols.bh))
    cq_l, ck_l, cv_l = (cols.out(j) * head_cb for j in (cols.cq, cols.ck, cols.cv))

    pad = (-(bsz + 1)) % 8
    cc = jnp.concatenate([c, c_ctx[None], jnp.zeros((pad, d), F32)], axis=0)
    mod = _ada_call(cc, w_ada, b_ada)

    def mods(l):
        lat = [mod[l, :bsz, k * d:(k + 1) * d].reshape(bsz, 1, d) for k in range(N_MOD)]
        con = [mod[l, bsz:bsz + 1, k * d:(k + 1) * d].reshape(1, 1, d) for k in range(N_MOD)]
        return lat, con

    tables = _rope_tables(n)
    w_in_b = w_in.astype(BF16)
    xl = x
    xc = ctx.reshape(1, bsz * nc, d)
    lat, con = mods(0)
    h = _mod_call(xl, norm1_g[0], lat[0], lat[1])
    hc = _mod_call(xc, norm1_g[0], con[0], con[1])

    for l in range(depth):
        last = l == depth - 1
        lam_init = 0.8 - 0.6 * math.exp(-0.3 * l)
        lat, con = mods(l)
        if not last:
            nlat, ncon = mods(l + 1)
            nxt_l = (norm1_g[l + 1], nlat[0], nlat[1])
            nxt_c = (norm1_g[l + 1], ncon[0], ncon[1])
        else:
            nxt_l = (final_norm_g, lat[0], lat[1])
            nxt_c = None

        p = _in_call(h, w_in_b[l], b_gate[l], tables, j0=0, nj=cols.end, rope=True)
        if last:
            pc = _in_call(hc, w_in_b[l], b_gate[l], tables, j0=cols.ck, nj=cols.gate - cols.ck,
                          rope=False)
            ck_c, cv_c = 0, (cols.cv - cols.ck) * head_cb
        else:
            pc = _in_call(hc, w_in_b[l], b_gate[l], tables, j0=0, nj=cols.end, rope=False)
            ck_c, cv_c = ck_l, cv_l
        pc_seq = pc.reshape(bsz, nc, -1)

        wa, wb, wc, wo = (w.astype(BF16) for w in (w_a_out[l], w_b_out[l], w_c_out[l], w_o[l]))

        def channel(xs, hs, g2, nxt, final):
            i = l // 2
            if l % 2 == 0:
                return _ffn_call(xs, hs, ff_w_gate[i].astype(BF16), ff_w_up[i].astype(BF16),
                                 ff_w_down[i].astype(BF16), g2, *nxt, final=final)
            return _moe_call(xs, hs, moe_w_router[i], moe_w_gate[i].astype(BF16),
                             moe_w_up[i].astype(BF16), moe_w_down[i].astype(BF16), g2, *nxt,
                             final=final)

        ya, yb = _local_call(p, mix_pos, a_ln_g[l], a_ln_b[l], a_ws[l], a_bs[l], b_conv[l], seq=n)
        yc = _attn_call(p, cq_l, [(p, ck_l, cv_l), (pc_seq, ck_c, cv_c)],
                        c_lambda[l], c_subln_g[l], lam_init=lam_init)
        x1, h2 = _merge_call(xl, ya, yb, yc, p, gate_cb, wa, wb, wc, wo, lat[2], norm2_g[l],
                             lat[3], lat[4])
        res = channel(x1, h2, lat[5], nxt_l, last)
        if last:
            return res[0]
        xl, h = res

        ya_c, yb_c = _local_call(pc_seq, mix_pos, a_ln_g[l], a_ln_b[l], a_ws[l], a_bs[l],
                                 b_conv[l], seq=nc)
        yc_c = _attn_call(pc_seq, cq_l, [(pc_seq, ck_c, cv_c)], c_lambda[l], c_subln_g[l],
                          lam_init=lam_init)
        flat = lambda a: a.reshape(1, bsz * nc, -1)
        xc1, hc2 = _merge_call(xc, flat(ya_c), flat(yb_c), flat(yc_c), pc, gate_cb, wa, wb, wc, wo,
                               con[2], norm2_g[l], con[3], con[4])
        xc, hc = channel(xc1, hc2, con[5], nxt_c, False)
```

```python
import functools
import math

import jax
import jax.numpy as jnp
from jax import lax
from jax.experimental import pallas as pl
from jax.experimental.pallas import tpu as pltpu

F32 = jnp.float32
BF16 = jnp.bfloat16

EPS = 1e-6
GRID_W = 64
N_MOD = 6
N_BRANCH = 3
CHUNK = 128
A_GROUPS = 8
A_GROUP_DIM = 64
A_WIDTH = A_GROUPS * A_GROUP_DIM
B_WIDTH = 512
CONV_W = 3
C_HEADS = 8
C_HEAD_DIM = 64
C_V_DIM = 2 * C_HEAD_DIM
ROPE_BASE = 10000.0
TOP_K = 2
LOG2E = 1.4426950408889634

LANES = 128
BF16_SUBLANES = 16
V7X_VMEM_LIMIT = 56 * 1024 * 1024

COL_BLOCK = 512


def _cparams(sem):
    return pltpu.CompilerParams(dimension_semantics=sem, vmem_limit_bytes=V7X_VMEM_LIMIT)


def _resident(shape, index_map):
    return pl.BlockSpec(shape, index_map, pipeline_mode=pl.Buffered(1))


def _sigmoid(v):
    return 0.5 * (1.0 + jnp.tanh(0.5 * v))


def _silu(v):
    return v * _sigmoid(v)


def _modulate(x, g, sh, sc):
    ms = jnp.mean(x * x, axis=-1, keepdims=True)
    y = x * lax.rsqrt(ms + EPS)
    return (y * g) * (1.0 + sc) + sh


def _rmsnorm(x, g):
    ms = jnp.mean(x * x, axis=-1, keepdims=True)
    return x * lax.rsqrt(ms + EPS) * g


def _ada_kernel(c_ref, w_ref, b_ref, o_ref):
    s = _silu(c_ref[...])
    o_ref[0] = jnp.dot(s, w_ref[0], preferred_element_type=F32,
                       precision=lax.Precision.HIGHEST) + b_ref[0]


def _ada_call(cc, w_ada, b_ada):
    depth, d, cols = w_ada.shape
    rows = cc.shape[0]
    tn = 1536
    return pl.pallas_call(
        _ada_kernel,
        grid=(depth, cols // tn),
        in_specs=[
            pl.BlockSpec((rows, d), lambda l, j: (0, 0)),
            pl.BlockSpec((1, d, tn), lambda l, j: (l, 0, j)),
            pl.BlockSpec((1, 1, tn), lambda l, j: (l, 0, j)),
        ],
        out_specs=pl.BlockSpec((1, rows, tn), lambda l, j: (l, 0, j)),
        out_shape=jax.ShapeDtypeStruct((depth, rows, cols), F32),
        compiler_params=_cparams(("parallel", "parallel")),
        name="ada_proj",
    )(cc, w_ada, b_ada.reshape(depth, 1, cols))


def _mod_kernel(x_ref, g_ref, sh_ref, sc_ref, o_ref):
    o_ref[0] = _modulate(x_ref[0], g_ref[...], sh_ref[0], sc_ref[0]).astype(BF16)


def _mod_call(x, g, sh, sc):
    nb, n, d = x.shape
    tm = min(1024, n)
    vec = pl.BlockSpec((1, 1, d), lambda b, i: (b, 0, 0))
    return pl.pallas_call(
        _mod_kernel,
        grid=(nb, n // tm),
        in_specs=[pl.BlockSpec((1, tm, d), lambda b, i: (b, i, 0)),
                  pl.BlockSpec((1, d), lambda b, i: (0, 0)), vec, vec],
        out_specs=pl.BlockSpec((1, tm, d), lambda b, i: (b, i, 0)),
        out_shape=jax.ShapeDtypeStruct((nb, n, d), BF16),
        compiler_params=_cparams(("parallel", "parallel")),
        name="modulate",
    )(x, g.reshape(1, d), sh, sc)


class _Cols:
    def __init__(self, d):
        self.au = 0
        self.av = self.au + A_WIDTH // COL_BLOCK
        self.bb = self.av + A_WIDTH // COL_BLOCK
        self.bc = self.bb + B_WIDTH // COL_BLOCK
        self.bh = self.bc + B_WIDTH // COL_BLOCK
        self.cq = self.bh + B_WIDTH // COL_BLOCK
        qk = C_HEADS * 2 * C_HEAD_DIM // COL_BLOCK
        self.ck = self.cq + qk
        self.cv = self.ck + qk
        self.gate = self.cv + C_HEADS * C_V_DIM // COL_BLOCK
        self.end = self.gate + N_BRANCH * d // COL_BLOCK

    def out(self, jw):
        return (jw - self.gate) % self.end


def _in_kernel(h_ref, w_ref, bg_ref, cos_ref, sa_ref, sb_ref, o_ref, *, cols, j0, rope, qscale,
               sub):
    j = pl.program_id(2) + j0
    tm = h_ref.shape[1]

    def run(epilogue):
        def mm(r):
            return jnp.dot(h_ref[0, r * sub:(r + 1) * sub, :], w_ref[...],
                           preferred_element_type=F32)
        nxt = mm(0)
        for r in range(tm // sub):
            acc = nxt
            if r + 1 < tm // sub:
                nxt = mm(r + 1)
            epilogue(acc, slice(r * sub, (r + 1) * sub))

    @pl.when(j < cols.bb)
    def _():
        def gelu(acc, rows):
            o_ref[0, rows, :] = jax.nn.gelu(acc).astype(BF16)
        run(gelu)

    @pl.when(((j >= cols.bb) & (j < cols.cq)) | ((j >= cols.cv) & (j < cols.gate)))
    def _():
        def plain(acc, rows):
            o_ref[0, rows, :] = acc.astype(BF16)
        run(plain)

    @pl.when((j >= cols.cq) & (j < cols.cv))
    def _():
        scale = jnp.where(j < cols.ck, qscale, 1.0).astype(F32)

        def rotary(acc, rows):
            if not rope:
                o_ref[0, rows, :] = (acc * scale).astype(BF16)
                return
            cos, sa, sb = cos_ref[rows, :] * scale, sa_ref[rows, :] * scale, sb_ref[rows, :] * scale
            for s in range(COL_BLOCK // LANES):
                t = acc[:, s * LANES:(s + 1) * LANES]
                r = (t * cos + pltpu.roll(t, LANES - 16, 1) * sa + pltpu.roll(t, 16, 1) * sb)
                o_ref[0, rows, s * LANES:(s + 1) * LANES] = r.astype(BF16)
        run(rotary)

    @pl.when(j >= cols.gate)
    def _():
        def gate(acc, rows):
            o_ref[0, rows, :] = _sigmoid(acc + bg_ref[...]).astype(BF16)
        run(gate)


def _in_call(h, w, b_gate, tables, *, j0, nj, rope):
    nb, n, d = h.shape
    cols = _Cols(d)
    tm = min(2048, n)
    full = (j0 == 0) and (nj == cols.end)
    if rope:
        cos, sa, sb = tables
    else:
        cos = sa = sb = jnp.zeros((tm, LANES), F32)
    tab = pl.BlockSpec((tm, LANES), lambda b, i, j: (i if rope else 0, 0))

    def out_idx(b, i, j):
        if full:
            return (b, i, jnp.where(j >= cols.gate, j - cols.gate, j + cols.end - cols.gate))
        return (b, i, j)
    kern = functools.partial(_in_kernel, cols=cols, j0=j0, rope=rope,
                             qscale=C_HEAD_DIM ** -0.5 * LOG2E, sub=min(256, tm))
    return pl.pallas_call(
        kern,
        grid=(nb, n // tm, nj),
        in_specs=[
            pl.BlockSpec((1, tm, d), lambda b, i, j: (b, i, 0)),
            pl.BlockSpec((d, COL_BLOCK), lambda b, i, j: (0, j + j0)),
            pl.BlockSpec((1, COL_BLOCK), lambda b, i, j: (0, jnp.maximum(j + j0 - cols.gate, 0))),
            tab, tab, tab,
        ],
        out_specs=pl.BlockSpec((1, tm, COL_BLOCK), out_idx),
        out_shape=jax.ShapeDtypeStruct((nb, n, nj * COL_BLOCK), BF16),
        compiler_params=_cparams(("parallel", "parallel", "arbitrary")),
        name="in_proj",
    )(h, w, b_gate.reshape(1, -1), cos, sa, sb)


def _rope_tables(n):
    rows = n // GRID_W
    r = jnp.repeat(jnp.arange(rows, dtype=F32), GRID_W)
    col = jnp.tile(jnp.arange(GRID_W, dtype=F32), rows)
    quarter = C_HEAD_DIM // 4
    inv = ROPE_BASE ** (-jnp.arange(quarter, dtype=F32) / quarter)
    ar = r[:, None] * inv
    ac = col[:, None] * inv
    ang = jnp.concatenate([ar, ar, ac, ac], axis=-1)
    ang = jnp.tile(ang, (1, LANES // C_HEAD_DIM))
    cos, sin = jnp.cos(ang), jnp.sin(ang)
    first_of_pair = (jnp.arange(LANES) // quarter) % 2 == 0
    sa = jnp.where(first_of_pair, -sin, 0.0)
    sb = jnp.where(first_of_pair, 0.0, sin)
    return cos, sa, sb


def _local_kernel(u_ref, v_ref, bg_ref, cg_ref, hh_ref, cgp_ref, hhp_ref, cgn_ref, hhn_ref,
                  lng_ref, lnb_ref, ws_ref, bias_ref, cw_ref, ya_ref, yb_ref, *, tm, seq):
    i = pl.program_id(1)
    lane = lax.broadcasted_iota(jnp.int32, (CHUNK, LANES), 1)
    lo = lane < A_GROUP_DIM
    for c in range(tm // CHUNK):
        rows = pl.ds(c * CHUNK, CHUNK)
        v = v_ref[0, rows, :].astype(F32)
        mu = jnp.mean(v, axis=-1, keepdims=True)
        var = jnp.mean(jnp.square(v - mu), axis=-1, keepdims=True)
        vn = ((v - mu) * lax.rsqrt(var + EPS) * lng_ref[...] + lnb_ref[...]).astype(BF16)
        for k in range(A_WIDTH // LANES):
            blk = vn[:, k * LANES:(k + 1) * LANES]
            zero = jnp.zeros_like(blk)
            mixed = (jnp.dot(ws_ref[2 * k], jnp.where(lo, blk, zero), preferred_element_type=F32)
                     + jnp.dot(ws_ref[2 * k + 1], jnp.where(lo, zero, blk), preferred_element_type=F32)
                     + bias_ref[:, k * LANES:(k + 1) * LANES])
            u = u_ref[0, rows, k * LANES:(k + 1) * LANES].astype(F32)
            ya_ref[0, rows, k * LANES:(k + 1) * LANES] = (u * mixed).astype(BF16)

    z = cg_ref[0].astype(F32) * hh_ref[0].astype(F32)
    last = BF16_SUBLANES - 1
    z_prev = cgp_ref[0, last:last + 1, :].astype(F32) * hhp_ref[0, last:last + 1, :].astype(F32)
    z_next = cgn_ref[0, 0:1, :].astype(F32) * hhn_ref[0, 0:1, :].astype(F32)
    row = lax.broadcasted_iota(jnp.int32, (tm, 1), 0)
    pos = (i * tm) % seq + row
    zm1 = jnp.where(row == 0, z_prev, pltpu.roll(z, 1, 0))
    zm1 = jnp.where(pos == 0, 0.0, zm1)
    zp1 = jnp.where(row == tm - 1, z_next, pltpu.roll(z, tm - 1, 0))
    zp1 = jnp.where(pos == seq - 1, 0.0, zp1)
    conv = cw_ref[0:1, :] * zm1 + cw_ref[1:2, :] * z + cw_ref[2:3, :] * zp1
    yb_ref[0] = (bg_ref[0].astype(F32) * conv).astype(BF16)


def _local_call(p, pos, ln_g, ln_b, ws, bs, conv_w, *, seq):
    nb, n, _ = p.shape
    au, av, bb, bc, bh = pos
    tm = min(512, seq)
    hb = tm // BF16_SUBLANES
    nhb = n // BF16_SUBLANES

    def main(cb):
        return pl.BlockSpec((1, tm, COL_BLOCK), lambda b, i: (b, i, cb))

    def prev(cb):
        return pl.BlockSpec((1, BF16_SUBLANES, COL_BLOCK),
                            lambda b, i: (b, jnp.maximum(i * hb - 1, 0), cb))

    def nxt(cb):
        return pl.BlockSpec((1, BF16_SUBLANES, COL_BLOCK),
                            lambda b, i: (b, jnp.minimum((i + 1) * hb, nhb - 1), cb))

    bias = jnp.repeat(bs.T, A_GROUP_DIM, axis=1)
    out = jax.ShapeDtypeStruct((nb, n, A_WIDTH), BF16)
    kern = functools.partial(_local_kernel, tm=tm, seq=seq)
    return pl.pallas_call(
        kern,
        grid=(nb, n // tm),
        in_specs=[main(au), main(av), main(bb), main(bc), main(bh),
                  prev(bc), prev(bh), nxt(bc), nxt(bh),
                  pl.BlockSpec((1, A_WIDTH), lambda b, i: (0, 0)),
                  pl.BlockSpec((1, A_WIDTH), lambda b, i: (0, 0)),
                  pl.BlockSpec((A_GROUPS, CHUNK, CHUNK), lambda b, i: (0, 0, 0)),
                  pl.BlockSpec((CHUNK, A_WIDTH), lambda b, i: (0, 0)),
                  pl.BlockSpec((CONV_W, B_WIDTH), lambda b, i: (0, 0))],
        out_specs=[pl.BlockSpec((1, tm, A_WIDTH), lambda b, i: (b, i, 0)),
                   pl.BlockSpec((1, tm, B_WIDTH), lambda b, i: (b, i, 0))],
        out_shape=[out, out],
        compiler_params=_cparams(("parallel", "parallel")),
        name="local_mixers",
    )(p, p, p, p, p, p, p, p, p, ln_g.reshape(1, -1), ln_b.reshape(1, -1), ws.astype(BF16),
      bias, conv_w)


def _attn_kernel(*refs, tq, ck, sizes, lam_init):
    nsrc = len(sizes)
    cl_ref, g_ref, q_ref = refs[:3]
    kv_refs = refs[3:3 + 2 * nsrc]
    o_ref = refs[3 + 2 * nsrc]
    vt_refs = refs[4 + 2 * nsrc:4 + 3 * nsrc]
    kmax_ref = refs[4 + 3 * nsrc]
    half = lax.broadcasted_iota(jnp.int32, (1, LANES), 1) < C_HEAD_DIM

    @pl.when(pl.program_id(2) == 0)
    def _():
        kmax = [jnp.zeros((1, 1), F32), jnp.zeros((1, 1), F32)]
        for src in range(nsrc):
            vt_refs[src][...] = kv_refs[2 * src + 1][0].astype(F32).T.astype(BF16)
            ksq = jnp.square(kv_refs[2 * src][0].astype(F32))
            for j, keep in enumerate((half, jnp.logical_not(half))):
                n2 = jnp.sum(jnp.where(keep, ksq, 0.0), axis=-1, keepdims=True)
                kmax[j] = jnp.maximum(kmax[j], jnp.max(n2, axis=0, keepdims=True))
        for j in range(2):
            kmax_ref[j:j + 1, :] = jnp.broadcast_to(jnp.sqrt(kmax[j]), (1, LANES))

    q = q_ref[0]
    zero = jnp.zeros_like(q)
    qs = [jnp.where(half, q, zero), jnp.where(half, zero, q)]
    chunks = [(kv_refs[2 * src], vt_refs[src], c0, min(ck, nk))
              for src, nk in enumerate(sizes) for c0 in range(0, nk, min(ck, nk))]

    def scores(chunk, qm):
        k_ref, _, c0, c = chunk
        return lax.dot_general(k_ref[0, c0:c0 + c, :], qm, (((1,), (1,)), ((), ())),
                               preferred_element_type=F32)

    def finish(accs, ls):
        cl = cl_ref[...]
        lam = (jnp.exp(jnp.sum(cl[0:1] * cl[1:2], axis=-1, keepdims=True))
               - jnp.exp(jnp.sum(cl[2:3] * cl[3:4], axis=-1, keepdims=True)) + lam_init)
        o = accs[0] / ls[0] - lam * (accs[1] / ls[1])
        ms = jnp.mean(o * o, axis=0, keepdims=True)
        y = o * lax.rsqrt(ms + EPS) * (g_ref[...] * (1.0 - lam_init))
        o_ref[0] = y.T.astype(BF16)

    qsq = jnp.square(q.astype(F32).T)
    refs_r = [jnp.sqrt(jnp.sum(qsq[j * C_HEAD_DIM:(j + 1) * C_HEAD_DIM], axis=0, keepdims=True))
              * kmax_ref[j:j + 1, 0:1] for j in range(2)]
    ls = [jnp.zeros((1, tq), F32) for _ in qs]
    accs = [jnp.zeros((C_V_DIM, tq), F32) for _ in qs]
    s_next = [scores(chunks[0], qm) for qm in qs]
    for t, (_, vt_ref, c0, c) in enumerate(chunks):
        s_cur = s_next
        if t + 1 < len(chunks):
            s_next = [scores(chunks[t + 1], qm) for qm in qs]
        vt = vt_ref[:, c0:c0 + c]
        for j, s in enumerate(s_cur):
            p = jnp.exp2(s - refs_r[j])
            ls[j] = ls[j] + jnp.sum(p, axis=0, keepdims=True)
            accs[j] = accs[j] + jnp.dot(vt, p.astype(BF16), preferred_element_type=F32)
    finish(accs, ls)

    healthy = jnp.min(jnp.minimum(ls[0], ls[1])) >= 2.0 ** -60

    @pl.when(jnp.logical_not(healthy))
    def _():
        ms = [jnp.full((1, tq), -jnp.inf, F32) for _ in qs]
        ls = [jnp.zeros((1, tq), F32) for _ in qs]
        accs = [jnp.zeros((C_V_DIM, tq), F32) for _ in qs]
        for chunk in chunks:
            _, vt_ref, c0, c = chunk
            vt = vt_ref[:, c0:c0 + c]
            for j, qm in enumerate(qs):
                s = scores(chunk, qm)
                m_new = jnp.maximum(ms[j], jnp.max(s, axis=0, keepdims=True))
                alpha = jnp.exp2(ms[j] - m_new)
                p = jnp.exp2(s - m_new)
                ls[j] = alpha * ls[j] + jnp.sum(p, axis=0, keepdims=True)
                accs[j] = alpha * accs[j] + jnp.dot(vt, p.astype(BF16),
                                                    preferred_element_type=F32)
                ms[j] = m_new
        finish(accs, ls)


def _attn_call(q_arr, q_cb, sources, c_lambda, subln_g, *, lam_init):
    nb, n, _ = q_arr.shape
    tq = min(512, n)
    sizes = tuple(a.shape[1] for a, _, _ in sources)
    in_specs = [
        pl.BlockSpec((4, C_HEAD_DIM), lambda b, h, i: (0, 0)),
        pl.BlockSpec((C_V_DIM, 1), lambda b, h, i: (0, 0)),
        pl.BlockSpec((1, tq, LANES), lambda b, h, i: (b, i, q_cb + h)),
    ]
    args = [c_lambda, subln_g.reshape(-1, 1), q_arr]
    for arr, kcb, vcb in sources:
        nk = arr.shape[1]
        in_specs.append(pl.BlockSpec((1, nk, LANES), lambda b, h, i, kcb=kcb: (b, 0, kcb + h)))
        in_specs.append(pl.BlockSpec((1, nk, LANES), lambda b, h, i, vcb=vcb: (b, 0, vcb + h)))
        args += [arr, arr]
    kern = functools.partial(_attn_kernel, tq=tq, ck=2048, sizes=sizes, lam_init=lam_init)
    return pl.pallas_call(
        kern,
        grid=(nb, C_HEADS, n // tq),
        in_specs=in_specs,
        out_specs=pl.BlockSpec((1, tq, C_V_DIM), lambda b, h, i: (b, i, h)),
        out_shape=jax.ShapeDtypeStruct((nb, n, C_HEADS * C_V_DIM), BF16),
        scratch_shapes=[pltpu.VMEM((C_V_DIM, nk), BF16) for nk in sizes]
        + [pltpu.VMEM((8, LANES), F32)],
        compiler_params=_cparams(("parallel", "parallel", "arbitrary")),
        name="diff_attention",
    )(*args)


def _merge_kernel(x_ref, ya_ref, yb_ref, yc_ref, ga_ref, gb_ref, gc_ref, wa_ref, wb_ref, wc_ref,
                  wo_ref, g1_ref, n2_ref, sh2_ref, sc2_ref, xo_ref, ho_ref):
    a = jnp.dot(ya_ref[0], wa_ref[...], preferred_element_type=F32)
    b = jnp.dot(yb_ref[0], wb_ref[...], preferred_element_type=F32)
    c = jnp.dot(yc_ref[0], wc_ref[...], preferred_element_type=F32)
    y = (ga_ref[0].astype(F32) * a + gb_ref[0].astype(F32) * b + gc_ref[0].astype(F32) * c)
    m = jnp.dot(y.astype(BF16), wo_ref[...], preferred_element_type=F32)
    xn = x_ref[0] + g1_ref[0] * m
    xo_ref[0] = xn
    ho_ref[0] = _modulate(xn, n2_ref[...], sh2_ref[0], sc2_ref[0]).astype(BF16)


def _merge_call(x, ya, yb, yc, p, gate_cb, wa, wb, wc, wo, g1, n2g, sh2, sc2):
    nb, n, d = x.shape
    tm = min(512, n)

    def tok(w, cb=0):
        return pl.BlockSpec((1, tm, w), lambda b, i: (b, i, cb))

    vec = pl.BlockSpec((1, 1, d), lambda b, i: (b, 0, 0))
    return pl.pallas_call(
        _merge_kernel,
        grid=(nb, n // tm),
        in_specs=[tok(d), tok(A_WIDTH), tok(B_WIDTH), tok(d),
                  tok(d, gate_cb), tok(d, gate_cb + 1), tok(d, gate_cb + 2),
                  _resident(wa.shape, lambda b, i: (0, 0)), _resident(wb.shape, lambda b, i: (0, 0)),
                  _resident(wc.shape, lambda b, i: (0, 0)), _resident(wo.shape, lambda b, i: (0, 0)),
                  vec, pl.BlockSpec((1, d), lambda b, i: (0, 0)), vec, vec],
        out_specs=[tok(d), tok(d)],
        out_shape=[jax.ShapeDtypeStruct((nb, n, d), F32), jax.ShapeDtypeStruct((nb, n, d), BF16)],
        compiler_params=_cparams(("parallel", "parallel")),
        name="merge_branches",
    )(x, ya, yb, yc, p, p, p, wa, wb, wc, wo, g1, n2g.reshape(1, d), sh2, sc2)


def _ff_chunks(width, chunk=512):
    out, c0 = [], 0
    while c0 < width:
        cw = min(chunk, width - c0)
        out.append((c0, cw))
        c0 += cw
    return out


def _finish(xn, final, ng_ref, sh_ref, sc_ref, out_refs):
    if final:
        out_refs[0][0] = _rmsnorm(xn, ng_ref[...])
    else:
        out_refs[0][0] = xn
        out_refs[1][0] = _modulate(xn, ng_ref[...], sh_ref[0], sc_ref[0]).astype(BF16)


def _ffn_kernel(x_ref, h_ref, wg_ref, wu_ref, wd_ref, g2_ref, ng_ref, sh_ref, sc_ref, *out_refs,
                final):
    h = h_ref[0]
    acc = None
    for c0, cw in _ff_chunks(wg_ref.shape[1]):
        a = jnp.dot(h, wg_ref[:, c0:c0 + cw], preferred_element_type=F32)
        b = jnp.dot(h, wu_ref[:, c0:c0 + cw], preferred_element_type=F32)
        t = (_silu(a) * b).astype(BF16)
        part = jnp.dot(t, wd_ref[c0:c0 + cw, :], preferred_element_type=F32)
        acc = part if acc is None else acc + part
    xn = x_ref[0] + g2_ref[0] * acc
    _finish(xn, final, ng_ref, sh_ref, sc_ref, out_refs)


def _epilogue_specs(nb, n, d, tm, final, idx):
    tok = pl.BlockSpec((1, tm, d), idx)
    if final:
        return [tok], [jax.ShapeDtypeStruct((nb, n, d), F32)]
    return [tok, tok], [jax.ShapeDtypeStruct((nb, n, d), F32), jax.ShapeDtypeStruct((nb, n, d), BF16)]


def _ffn_call(x, h, wg, wu, wd, g2, ng, sh, sc, *, final):
    nb, n, d = x.shape
    tm = min(512, n)
    idx = lambda b, i: (b, i, 0)
    vec = pl.BlockSpec((1, 1, d), lambda b, i: (b, 0, 0))
    out_specs, out_shape = _epilogue_specs(nb, n, d, tm, final, idx)
    return pl.pallas_call(
        functools.partial(_ffn_kernel, final=final),
        grid=(nb, n // tm),
        in_specs=[pl.BlockSpec((1, tm, d), idx), pl.BlockSpec((1, tm, d), idx),
                  _resident(wg.shape, lambda b, i: (0, 0)), _resident(wu.shape, lambda b, i: (0, 0)),
                  _resident(wd.shape, lambda b, i: (0, 0)),
                  vec, pl.BlockSpec((1, d), lambda b, i: (0, 0)), vec, vec],
        out_specs=out_specs,
        out_shape=out_shape,
        compiler_params=_cparams(("parallel", "parallel")),
        name="dense_swiglu",
    )(x, h, wg, wu, wd, g2, ng.reshape(1, d), sh, sc)


MOE_TILE = 2048
MOE_WIN = 512
MOE_SLAB = 256
MOE_ROWS = 512
MOE_FF = 512
CNT_ROWS = 8


def _route_kernel(h_ref, wr_ref, comb_ref, rank_ref, rankt_ref, cnt_ref, *, n_experts, win):
    h = h_ref[0]
    tm = h.shape[0]
    lane = lax.broadcasted_iota(jnp.int32, (tm, LANES), 1)
    logits = jnp.dot(h, wr_ref[...], preferred_element_type=F32)
    lg = jnp.where(lane < n_experts, logits, -jnp.inf)
    m1 = jnp.max(lg, axis=-1, keepdims=True)
    i1 = jnp.min(jnp.where(lg == m1, lane, LANES), axis=-1, keepdims=True)
    lg2 = jnp.where(lane == i1, -jnp.inf, lg)
    m2 = jnp.max(lg2, axis=-1, keepdims=True)
    i2 = jnp.min(jnp.where(lg2 == m2, lane, LANES), axis=-1, keepdims=True)
    e2 = jnp.exp(m2 - m1)
    w1 = 1.0 / (1.0 + e2)
    comb_ref[0] = jnp.where(lane == i1, w1, 0.0) + jnp.where(lane == i2, e2 * w1, 0.0)
    sel = (lane == i1) | (lane == i2)
    tri = jnp.where(lax.broadcasted_iota(jnp.int32, (win, win), 0)
                    > lax.broadcasted_iota(jnp.int32, (win, win), 1), 1.0, 0.0).astype(BF16)
    base = jnp.zeros((1, LANES), F32)
    bases = []
    for w in range(tm // win):
        rows = slice(w * win, (w + 1) * win)
        sw = jnp.where(sel[rows], 1.0, 0.0)
        excl = jnp.dot(tri, sw.astype(BF16), preferred_element_type=F32)
        rank_ref[0, rows, :] = jnp.where(sel[rows], excl + base, -1.0)
        bases.append(base)
        base = base + jnp.sum(sw, axis=0, keepdims=True)
    bases.append(base)
    bases += [jnp.zeros((1, LANES), F32)] * (CNT_ROWS - len(bases))
    cnt_ref[0] = jnp.concatenate(bases, axis=0).astype(jnp.int32)
    rankt_ref[0] = rank_ref[0].T[:CNT_ROWS]


def _route_call(h, w_router, tm):
    nb, n, d = h.shape
    n_experts = w_router.shape[1]
    assert n_experts <= CNT_ROWS and tm // MOE_WIN + 1 <= CNT_ROWS
    nt = n // tm
    wr = jnp.zeros((d, LANES), BF16).at[:, :n_experts].set(w_router.astype(BF16))
    tok = pl.BlockSpec((1, tm, LANES), lambda b, i: (b, i, 0))
    per_tile = lambda r, c: pl.BlockSpec((1, r, c), lambda b, i: (b * nt + i, 0, 0))
    return pl.pallas_call(
        functools.partial(_route_kernel, n_experts=n_experts, win=min(MOE_WIN, tm)),
        grid=(nb, nt),
        in_specs=[pl.BlockSpec((1, tm, d), lambda b, i: (b, i, 0)),
                  pl.BlockSpec((d, LANES), lambda b, i: (0, 0))],
        out_specs=[tok, tok, per_tile(CNT_ROWS, tm), per_tile(CNT_ROWS, LANES)],
        out_shape=[jax.ShapeDtypeStruct((nb, n, LANES), F32), jax.ShapeDtypeStruct((nb, n, LANES), F32),
                   jax.ShapeDtypeStruct((nb * nt, CNT_ROWS, tm), F32),
                   jax.ShapeDtypeStruct((nb * nt, CNT_ROWS, LANES), jnp.int32)],
        compiler_params=_cparams(("parallel", "parallel")),
        name="moe_route",
    )(h, wr)


def _moe_kernel(cnt_ref, x_ref, h_ref, comb_ref, rank_ref, rankt_ref, wg_ref, wu_ref, wd_ref, g2_ref,
                ng_ref, sh_ref, sc_ref, *rest, final, n_experts, win, slab, rblk):
    out_refs, (hc_ref, yc_ref) = rest[:-2], rest[-2:]
    acc_ref = out_refs[0].at[0]
    b, i, e, f = (pl.program_id(k) for k in range(4))
    tm, d = h_ref.shape[1], h_ref.shape[2]
    nw = tm // win
    tile = b * pl.num_programs(1) + i
    cbase = tile * (CNT_ROWS * n_experts)

    def count(w):
        return cnt_ref[cbase + w * n_experts + e]

    def slabs(w):
        row0 = (count(w) // BF16_SUBLANES) * BF16_SUBLANES
        return row0, (count(w + 1) - row0 + slab - 1) // slab

    @pl.when((e == 0) & (f == 0))
    def _():
        acc_ref[...] = jnp.zeros_like(acc_ref)

    total = count(nw)

    @pl.when(f == 0)
    def _():
        def zero(s, carry):
            rows = pl.ds(pl.multiple_of(s * slab, slab), slab)
            hc_ref[rows, :] = jnp.zeros((slab, d), BF16)
            yc_ref[rows, :] = jnp.zeros((slab, d), F32)
            return carry

        lax.fori_loop(0, jnp.minimum((total + rblk + 2 * slab) // slab, hc_ref.shape[0] // slab),
                      zero, 0)
        for w in range(nw):
            row0, nsl = slabs(w)
            rt = rankt_ref[0, 0, :, w * win:(w + 1) * win]
            hw = h_ref[0, w * win:(w + 1) * win, :]

            def body(s, carry, row0=row0, rt=rt, hw=hw):
                r0 = pl.multiple_of(row0 + s * slab, BF16_SUBLANES)
                rid = (r0 + lax.broadcasted_iota(jnp.int32, (slab, win), 0)).astype(F32)
                onehot = jnp.where(rt == rid, 1.0, 0.0).astype(BF16)
                rows = pl.ds(r0, slab)
                got = jnp.dot(onehot, hw, preferred_element_type=F32)
                hc_ref[rows, :] = (hc_ref[rows, :].astype(F32) + got).astype(BF16)
                return carry

            lax.fori_loop(0, nsl, body, 0)

    def ffn(r0, size):
        rows = pl.ds(pl.multiple_of(r0, rblk // 4), size)
        hb = hc_ref[rows, :]
        a = jnp.dot(hb, wg_ref[0, 0], preferred_element_type=F32)
        u = jnp.dot(hb, wu_ref[0, 0], preferred_element_type=F32)
        t = (_silu(a) * u).astype(BF16)
        yc_ref[rows, :] += jnp.dot(t, wd_ref[0], preferred_element_type=F32)

    quarters = (total + rblk // 4 - 1) // (rblk // 4)
    nfull = (quarters + 1) // 4
    rest = jnp.maximum(quarters - 4 * nfull, 0)

    def full(r, carry):
        ffn(r * rblk, rblk)
        return carry

    lax.fori_loop(0, nfull, full, 0)

    @pl.when(rest == 2)
    def _():
        ffn(nfull * rblk, rblk // 2)

    @pl.when(rest == 1)
    def _():
        ffn(nfull * rblk, rblk // 4)

    @pl.when(f == pl.num_programs(3) - 1)
    def _():
        for w in range(nw):
            row0, nsl = slabs(w)
            trows = slice(w * win, (w + 1) * win)
            mine = lax.broadcasted_iota(jnp.int32, (win, LANES), 1) == e
            rcol = jnp.sum(jnp.where(mine, rank_ref[0, trows, :], 0.0), axis=-1, keepdims=True)
            wcol = jnp.sum(jnp.where(mine, comb_ref[0, trows, :], 0.0), axis=-1, keepdims=True)

            def body(s, carry, row0=row0, rcol=rcol, wcol=wcol, trows=trows):
                r0 = pl.multiple_of(row0 + s * slab, BF16_SUBLANES)
                cid = (r0 + lax.broadcasted_iota(jnp.int32, (win, slab), 1)).astype(F32)
                onehot = jnp.where(rcol == cid, 1.0, 0.0).astype(BF16)
                yb = yc_ref[pl.ds(r0, slab), :].astype(BF16)
                acc_ref[trows, :] += wcol * jnp.dot(onehot, yb, preferred_element_type=F32)
                return carry

            lax.fori_loop(0, nsl, body, 0)

    @pl.when((e == pl.num_programs(2) - 1) & (f == pl.num_programs(3) - 1))
    def _():
        xn = x_ref[0] + g2_ref[0] * acc_ref[...]
        _finish(xn, final, ng_ref, sh_ref, sc_ref, out_refs)


def _moe_call(x, h, w_router, wg, wu, wd, g2, ng, sh, sc, *, final):
    nb, n, d = x.shape
    n_experts, _, dff = wg.shape
    tm = min(MOE_TILE, n)
    win = min(MOE_WIN, tm)
    nt = n // tm
    comb, rank, rankt, cnt = _route_call(h, w_router, tm)
    cnt = cnt[:, :, :n_experts].reshape(-1)
    blocked = lambda w: w.reshape(n_experts, d, dff // MOE_FF, MOE_FF).transpose(0, 2, 1, 3)
    wg, wu = blocked(wg), blocked(wu)
    rankt = rankt.reshape(nb * nt, CNT_ROWS, 1, tm)
    cap_ffn = -(-tm // MOE_ROWS) * MOE_ROWS
    cap = -(-max(cap_ffn, tm + MOE_SLAB) // MOE_SLAB) * MOE_SLAB

    idx = lambda b, i, e, f, c: (b, i, 0)
    vec = pl.BlockSpec((1, 1, d), lambda b, i, e, f, c: (b, 0, 0))
    out_specs, out_shape = _epilogue_specs(nb, n, d, tm, final, idx)
    grid_spec = pltpu.PrefetchScalarGridSpec(
        num_scalar_prefetch=1,
        grid=(nb, nt, n_experts, dff // MOE_FF),
        in_specs=[_resident((1, tm, d), idx), _resident((1, tm, d), idx),
                  _resident((1, tm, LANES), idx), _resident((1, tm, LANES), idx),
                  pl.BlockSpec((1, 1, 1, tm), lambda b, i, e, f, c: (b * nt + i, e, 0, 0)),
                  pl.BlockSpec((1, 1, d, MOE_FF), lambda b, i, e, f, c: (e, f, 0, 0)),
                  pl.BlockSpec((1, 1, d, MOE_FF), lambda b, i, e, f, c: (e, f, 0, 0)),
                  pl.BlockSpec((1, MOE_FF, d), lambda b, i, e, f, c: (e, f, 0)),
                  vec, pl.BlockSpec((1, d), lambda b, i, e, f, c: (0, 0)), vec, vec],
        out_specs=out_specs,
        scratch_shapes=[pltpu.VMEM((cap, d), BF16), pltpu.VMEM((cap, d), F32)],
    )
    return pl.pallas_call(
        functools.partial(_moe_kernel, final=final, n_experts=n_experts, win=win, slab=MOE_SLAB,
                          rblk=MOE_ROWS),
        grid_spec=grid_spec,
        out_shape=out_shape,
        compiler_params=_cparams(("parallel", "parallel", "arbitrary", "arbitrary")),
        name="moe_swiglu",
    )(cnt, x, h, comb, rank, rankt, wg, wu, wd, g2, ng.reshape(1, d), sh, sc)


def kernel(x, c, ctx, c_ctx, w_ada, b_ada, norm1_g, norm2_g, w_in, b_gate, a_ln_g, a_ln_b, a_ws,
           a_bs, b_conv, c_lambda, c_subln_g, w_a_out, w_b_out, w_c_out, w_o, ff_w_gate, ff_w_up,
           ff_w_down, moe_w_router, moe_w_gate, moe_w_up, moe_w_down, final_norm_g):
    bsz, n, d = x.shape
    nc = ctx.shape[1]
    depth = w_ada.shape[0]
    cols = _Cols(d)
    head_cb = COL_BLOCK // LANES
    gate_cb = cols.out(cols.gate) * COL_BLOCK // d
    mix_pos = tuple(cols.out(j) for j in (cols.au, cols.av, cols.bb, cols.bc, cols.bh))
    cq_l, ck_l, cv_l = (cols.out(j) * head_cb for j in (cols.cq, cols.ck, cols.cv))

    pad = (-(bsz + 1)) % 8
    cc = jnp.concatenate([c, c_ctx[None], jnp.zeros((pad, d), F32)], axis=0)
    mod = _ada_call(cc, w_ada, b_ada)

    def mods(l):
        lat = [mod[l, :bsz, k * d:(k + 1) * d].reshape(bsz, 1, d) for k in range(N_MOD)]
        con = [mod[l, bsz:bsz + 1, k * d:(k + 1) * d].reshape(1, 1, d) for k in range(N_MOD)]
        return lat, con

    tables = _rope_tables(n)
    w_in_b = w_in.astype(BF16)
    xl = x
    xc = ctx.reshape(1, bsz * nc, d)
    lat, con = mods(0)
    h = _mod_call(xl, norm1_g[0], lat[0], lat[1])
    hc = _mod_call(xc, norm1_g[0], con[0], con[1])

    for l in range(depth):
        last = l == depth - 1
        lam_init = 0.8 - 0.6 * math.exp(-0.3 * l)
        lat, con = mods(l)
        if not last:
            nlat, ncon = mods(l + 1)
            nxt_l = (norm1_g[l + 1], nlat[0], nlat[1])
            nxt_c = (norm1_g[l + 1], ncon[0], ncon[1])
        else:
            nxt_l = (final_norm_g, lat[0], lat[1])
            nxt_c = None

        p = _in_call(h, w_in_b[l], b_gate[l], tables, j0=0, nj=cols.end, rope=True)
        if last:
            pc = _in_call(hc, w_in_b[l], b_gate[l], tables, j0=cols.ck, nj=cols.gate - cols.ck,
                          rope=False)
            ck_c, cv_c = 0, (cols.cv - cols.ck) * head_cb
        else:
            pc = _in_call(hc, w_in_b[l], b_gate[l], tables, j0=0, nj=cols.end, rope=False)
            ck_c, cv_c = ck_l, cv_l
        pc_seq = pc.reshape(bsz, nc, -1)

        wa, wb, wc, wo = (w.astype(BF16) for w in (w_a_out[l], w_b_out[l], w_c_out[l], w_o[l]))

        def channel(xs, hs, g2, nxt, final):
            i = l // 2
            if l % 2 == 0:
                return _ffn_call(xs, hs, ff_w_gate[i].astype(BF16), ff_w_up[i].astype(BF16),
                                 ff_w_down[i].astype(BF16), g2, *nxt, final=final)
            return _moe_call(xs, hs, moe_w_router[i], moe_w_gate[i].astype(BF16),
                             moe_w_up[i].astype(BF16), moe_w_down[i].astype(BF16), g2, *nxt,
                             final=final)

        ya, yb = _local_call(p, mix_pos, a_ln_g[l], a_ln_b[l], a_ws[l], a_bs[l], b_conv[l], seq=n)
        yc = _attn_call(p, cq_l, [(p, ck_l, cv_l), (pc_seq, ck_c, cv_c)],
                        c_lambda[l], c_subln_g[l], lam_init=lam_init)
        x1, h2 = _merge_call(xl, ya, yb, yc, p, gate_cb, wa, wb, wc, wo, lat[2], norm2_g[l],
                             lat[3], lat[4])
        res = channel(x1, h2, lat[5], nxt_l, last)
        if last:
            return res[0]
        xl, h = res

        ya_c, yb_c = _local_call(pc_seq, mix_pos, a_ln_g[l], a_ln_b[l], a_ws[l], a_bs[l],
                                 b_conv[l], seq=nc)
        yc_c = _attn_call(pc_seq, cq_l, [(pc_seq, ck_c, cv_c)], c_lambda[l], c_subln_g[l],
                          lam_init=lam_init)
        flat = lambda a: a.reshape(1, bsz * nc, -1)
        xc1, hc2 = _merge_call(xc, flat(ya_c), flat(yb_c), flat(yc_c), pc, gate_cb, wa, wb, wc, wo,
                               con[2], norm2_g[l], con[3], con[4])
        xc, hc = channel(xc1, hc2, con[5], nxt_c, False)
```

```python
import functools
import math

import jax
import jax.numpy as jnp
from jax import lax
from jax.experimental import pallas as pl
from jax.experimental.pallas import tpu as pltpu

F32 = jnp.float32
BF16 = jnp.bfloat16

EPS = 1e-6
GRID_W = 64
N_MOD = 6
N_BRANCH = 3
CHUNK = 128
A_GROUPS = 8
A_GROUP_DIM = 64
A_WIDTH = A_GROUPS * A_GROUP_DIM
B_WIDTH = 512
CONV_W = 3
C_HEADS = 8
C_HEAD_DIM = 64
C_V_DIM = 2 * C_HEAD_DIM
ROPE_BASE = 10000.0
TOP_K = 2
LOG2E = 1.4426950408889634

LANES = 128
BF16_SUBLANES = 16
V7X_VMEM_LIMIT = 56 * 1024 * 1024

COL_BLOCK = 512


def _cparams(sem):
    return pltpu.CompilerParams(dimension_semantics=sem, vmem_limit_bytes=V7X_VMEM_LIMIT)


def _resident(shape, index_map):
    return pl.BlockSpec(shape, index_map, pipeline_mode=pl.Buffered(1))


def _sigmoid(v):
    return 0.5 * (1.0 + jnp.tanh(0.5 * v))


def _silu(v):
    return v * _sigmoid(v)


def _modulate(x, g, sh, sc):
    ms = jnp.mean(x * x, axis=-1, keepdims=True)
    y = x * lax.rsqrt(ms + EPS)
    return (y * g) * (1.0 + sc) + sh


def _rmsnorm(x, g):
    ms = jnp.mean(x * x, axis=-1, keepdims=True)
    return x * lax.rsqrt(ms + EPS) * g


def _ada_kernel(c_ref, w_ref, b_ref, o_ref):
    s = _silu(c_ref[...])
    o_ref[0] = jnp.dot(s, w_ref[0], preferred_element_type=F32,
                       precision=lax.Precision.HIGHEST) + b_ref[0]


def _ada_call(cc, w_ada, b_ada):
    depth, d, cols = w_ada.shape
    rows = cc.shape[0]
    tn = 1536
    return pl.pallas_call(
        _ada_kernel,
        grid=(depth, cols // tn),
        in_specs=[
            pl.BlockSpec((rows, d), lambda l, j: (0, 0)),
            pl.BlockSpec((1, d, tn), lambda l, j: (l, 0, j)),
            pl.BlockSpec((1, 1, tn), lambda l, j: (l, 0, j)),
        ],
        out_specs=pl.BlockSpec((1, rows, tn), lambda l, j: (l, 0, j)),
        out_shape=jax.ShapeDtypeStruct((depth, rows, cols), F32),
        compiler_params=_cparams(("parallel", "parallel")),
        name="ada_proj",
    )(cc, w_ada, b_ada.reshape(depth, 1, cols))


def _mod_kernel(x_ref, g_ref, sh_ref, sc_ref, o_ref):
    o_ref[0] = _modulate(x_ref[0], g_ref[...], sh_ref[0], sc_ref[0]).astype(BF16)


def _mod_call(x, g, sh, sc):
    nb, n, d = x.shape
    tm = min(1024, n)
    vec = pl.BlockSpec((1, 1, d), lambda b, i: (b, 0, 0))
    return pl.pallas_call(
        _mod_kernel,
        grid=(nb, n // tm),
        in_specs=[pl.BlockSpec((1, tm, d), lambda b, i: (b, i, 0)),
                  pl.BlockSpec((1, d), lambda b, i: (0, 0)), vec, vec],
        out_specs=pl.BlockSpec((1, tm, d), lambda b, i: (b, i, 0)),
        out_shape=jax.ShapeDtypeStruct((nb, n, d), BF16),
        compiler_params=_cparams(("parallel", "parallel")),
        name="modulate",
    )(x, g.reshape(1, d), sh, sc)


class _Cols:
    def __init__(self, d):
        self.au = 0
        self.av = self.au + A_WIDTH // COL_BLOCK
        self.bb = self.av + A_WIDTH // COL_BLOCK
        self.bc = self.bb + B_WIDTH // COL_BLOCK
        self.bh = self.bc + B_WIDTH // COL_BLOCK
        self.cq = self.bh + B_WIDTH // COL_BLOCK
        qk = C_HEADS * 2 * C_HEAD_DIM // COL_BLOCK
        self.ck = self.cq + qk
        self.cv = self.ck + qk
        self.gate = self.cv + C_HEADS * C_V_DIM // COL_BLOCK
        self.end = self.gate + N_BRANCH * d // COL_BLOCK

    def out(self, jw):
        return (jw - self.gate) % self.end


def _in_kernel(h_ref, w_ref, bg_ref, cos_ref, sa_ref, sb_ref, o_ref, *, cols, j0, rope, qscale,
               sub):
    j = pl.program_id(2) + j0
    tm = h_ref.shape[1]

    def run(epilogue):
        def mm(r):
            return jnp.dot(h_ref[0, r * sub:(r + 1) * sub, :], w_ref[...],
                           preferred_element_type=F32)
        nxt = mm(0)
        for r in range(tm // sub):
            acc = nxt
            if r + 1 < tm // sub:
                nxt = mm(r + 1)
            epilogue(acc, slice(r * sub, (r + 1) * sub))

    @pl.when(j < cols.bb)
    def _():
        def gelu(acc, rows):
            o_ref[0, rows, :] = jax.nn.gelu(acc).astype(BF16)
        run(gelu)

    @pl.when(((j >= cols.bb) & (j < cols.cq)) | ((j >= cols.cv) & (j < cols.gate)))
    def _():
        def plain(acc, rows):
            o_ref[0, rows, :] = acc.astype(BF16)
        run(plain)

    @pl.when((j >= cols.cq) & (j < cols.cv))
    def _():
        scale = jnp.where(j < cols.ck, qscale, 1.0).astype(F32)

        def rotary(acc, rows):
            if not rope:
                o_ref[0, rows, :] = (acc * scale).astype(BF16)
                return
            cos, sa, sb = cos_ref[rows, :] * scale, sa_ref[rows, :] * scale, sb_ref[rows, :] * scale
            for s in range(COL_BLOCK // LANES):
                t = acc[:, s * LANES:(s + 1) * LANES]
                r = (t * cos + pltpu.roll(t, LANES - 16, 1) * sa + pltpu.roll(t, 16, 1) * sb)
                o_ref[0, rows, s * LANES:(s + 1) * LANES] = r.astype(BF16)
        run(rotary)

    @pl.when(j >= cols.gate)
    def _():
        def gate(acc, rows):
            o_ref[0, rows, :] = _sigmoid(acc + bg_ref[...]).astype(BF16)
        run(gate)


def _in_call(h, w, b_gate, tables, *, j0, nj, rope):
    nb, n, d = h.shape
    cols = _Cols(d)
    tm = min(4096, n)
    full = (j0 == 0) and (nj == cols.end)
    if rope:
        cos, sa, sb = tables
    else:
        cos = sa = sb = jnp.zeros((tm, LANES), F32)
    tab = pl.BlockSpec((tm, LANES), lambda b, i, j: (i if rope else 0, 0))

    def out_idx(b, i, j):
        if full:
            return (b, i, jnp.where(j >= cols.gate, j - cols.gate, j + cols.end - cols.gate))
        return (b, i, j)
    kern = functools.partial(_in_kernel, cols=cols, j0=j0, rope=rope,
                             qscale=C_HEAD_DIM ** -0.5 * LOG2E, sub=min(512, tm))
    return pl.pallas_call(
        kern,
        grid=(nb, n // tm, nj),
        in_specs=[
            pl.BlockSpec((1, tm, d), lambda b, i, j: (b, i, 0)),
            pl.BlockSpec((d, COL_BLOCK), lambda b, i, j: (0, j + j0)),
            pl.BlockSpec((1, COL_BLOCK), lambda b, i, j: (0, jnp.maximum(j + j0 - cols.gate, 0))),
            tab, tab, tab,
        ],
        out_specs=pl.BlockSpec((1, tm, COL_BLOCK), out_idx),
        out_shape=jax.ShapeDtypeStruct((nb, n, nj * COL_BLOCK), BF16),
        compiler_params=_cparams(("parallel", "parallel", "arbitrary")),
        name="in_proj",
    )(h, w, b_gate.reshape(1, -1), cos, sa, sb)


def _rope_tables(n):
    rows = n // GRID_W
    r = jnp.repeat(jnp.arange(rows, dtype=F32), GRID_W)
    col = jnp.tile(jnp.arange(GRID_W, dtype=F32), rows)
    quarter = C_HEAD_DIM // 4
    inv = ROPE_BASE ** (-jnp.arange(quarter, dtype=F32) / quarter)
    ar = r[:, None] * inv
    ac = col[:, None] * inv
    ang = jnp.concatenate([ar, ar, ac, ac], axis=-1)
    ang = jnp.tile(ang, (1, LANES // C_HEAD_DIM))
    cos, sin = jnp.cos(ang), jnp.sin(ang)
    first_of_pair = (jnp.arange(LANES) // quarter) % 2 == 0
    sa = jnp.where(first_of_pair, -sin, 0.0)
    sb = jnp.where(first_of_pair, 0.0, sin)
    return cos, sa, sb


def _local_kernel(u_ref, v_ref, bg_ref, cg_ref, hh_ref, cgp_ref, hhp_ref, cgn_ref, hhn_ref,
                  lng_ref, lnb_ref, ws_ref, bias_ref, cw_ref, ya_ref, yb_ref, *, tm, seq):
    i = pl.program_id(1)
    lane = lax.broadcasted_iota(jnp.int32, (CHUNK, LANES), 1)
    lo = lane < A_GROUP_DIM
    for c in range(tm // CHUNK):
        rows = pl.ds(c * CHUNK, CHUNK)
        v = v_ref[0, rows, :].astype(F32)
        mu = jnp.mean(v, axis=-1, keepdims=True)
        var = jnp.mean(jnp.square(v - mu), axis=-1, keepdims=True)
        vn = ((v - mu) * lax.rsqrt(var + EPS) * lng_ref[...] + lnb_ref[...]).astype(BF16)
        for k in range(A_WIDTH // LANES):
            blk = vn[:, k * LANES:(k + 1) * LANES]
            zero = jnp.zeros_like(blk)
            mixed = (jnp.dot(ws_ref[2 * k], jnp.where(lo, blk, zero), preferred_element_type=F32)
                     + jnp.dot(ws_ref[2 * k + 1], jnp.where(lo, zero, blk), preferred_element_type=F32)
                     + bias_ref[:, k * LANES:(k + 1) * LANES])
            u = u_ref[0, rows, k * LANES:(k + 1) * LANES].astype(F32)
            ya_ref[0, rows, k * LANES:(k + 1) * LANES] = (u * mixed).astype(BF16)

    z = cg_ref[0].astype(F32) * hh_ref[0].astype(F32)
    last = BF16_SUBLANES - 1
    z_prev = cgp_ref[0, last:last + 1, :].astype(F32) * hhp_ref[0, last:last + 1, :].astype(F32)
    z_next = cgn_ref[0, 0:1, :].astype(F32) * hhn_ref[0, 0:1, :].astype(F32)
    row = lax.broadcasted_iota(jnp.int32, (tm, 1), 0)
    pos = (i * tm) % seq + row
    zm1 = jnp.where(row == 0, z_prev, pltpu.roll(z, 1, 0))
    zm1 = jnp.where(pos == 0, 0.0, zm1)
    zp1 = jnp.where(row == tm - 1, z_next, pltpu.roll(z, tm - 1, 0))
    zp1 = jnp.where(pos == seq - 1, 0.0, zp1)
    conv = cw_ref[0:1, :] * zm1 + cw_ref[1:2, :] * z + cw_ref[2:3, :] * zp1
    yb_ref[0] = (bg_ref[0].astype(F32) * conv).astype(BF16)


def _local_call(p, pos, ln_g, ln_b, ws, bs, conv_w, *, seq):
    nb, n, _ = p.shape
    au, av, bb, bc, bh = pos
    tm = min(512, seq)
    hb = tm // BF16_SUBLANES
    nhb = n // BF16_SUBLANES

    def main(cb):
        return pl.BlockSpec((1, tm, COL_BLOCK), lambda b, i: (b, i, cb))

    def prev(cb):
        return pl.BlockSpec((1, BF16_SUBLANES, COL_BLOCK),
                            lambda b, i: (b, jnp.maximum(i * hb - 1, 0), cb))

    def nxt(cb):
        return pl.BlockSpec((1, BF16_SUBLANES, COL_BLOCK),
                            lambda b, i: (b, jnp.minimum((i + 1) * hb, nhb - 1), cb))

    bias = jnp.repeat(bs.T, A_GROUP_DIM, axis=1)
    out = jax.ShapeDtypeStruct((nb, n, A_WIDTH), BF16)
    kern = functools.partial(_local_kernel, tm=tm, seq=seq)
    return pl.pallas_call(
        kern,
        grid=(nb, n // tm),
        in_specs=[main(au), main(av), main(bb), main(bc), main(bh),
                  prev(bc), prev(bh), nxt(bc), nxt(bh),
                  pl.BlockSpec((1, A_WIDTH), lambda b, i: (0, 0)),
                  pl.BlockSpec((1, A_WIDTH), lambda b, i: (0, 0)),
                  pl.BlockSpec((A_GROUPS, CHUNK, CHUNK), lambda b, i: (0, 0, 0)),
                  pl.BlockSpec((CHUNK, A_WIDTH), lambda b, i: (0, 0)),
                  pl.BlockSpec((CONV_W, B_WIDTH), lambda b, i: (0, 0))],
        out_specs=[pl.BlockSpec((1, tm, A_WIDTH), lambda b, i: (b, i, 0)),
                   pl.BlockSpec((1, tm, B_WIDTH), lambda b, i: (b, i, 0))],
        out_shape=[out, out],
        compiler_params=_cparams(("parallel", "parallel")),
        name="local_mixers",
    )(p, p, p, p, p, p, p, p, p, ln_g.reshape(1, -1), ln_b.reshape(1, -1), ws.astype(BF16),
      bias, conv_w)


def _attn_kernel(*refs, tq, ck, sizes, lam_init):
    nsrc = len(sizes)
    cl_ref, g_ref, q_ref = refs[:3]
    kv_refs = refs[3:3 + 2 * nsrc]
    o_ref = refs[3 + 2 * nsrc]
    vt_refs = refs[4 + 2 * nsrc:4 + 3 * nsrc]
    kmax_ref = refs[4 + 3 * nsrc]
    half = lax.broadcasted_iota(jnp.int32, (1, LANES), 1) < C_HEAD_DIM

    @pl.when(pl.program_id(2) == 0)
    def _():
        per_map = jnp.where(lax.broadcasted_iota(jnp.int32, (LANES, LANES), 0) // C_HEAD_DIM
                            == lax.broadcasted_iota(jnp.int32, (LANES, LANES), 1), 1.0, 0.0)
        kmax2 = jnp.zeros((1, LANES), F32)
        for src in range(nsrc):
            vt_refs[src][...] = kv_refs[2 * src + 1][0].astype(F32).T.astype(BF16)
            k = kv_refs[2 * src][0]
            n2 = jnp.dot(k * k, per_map.astype(BF16), preferred_element_type=F32)
            kmax2 = jnp.maximum(kmax2, jnp.max(n2, axis=0, keepdims=True))
        kmax_ref[0:1, :] = jnp.sqrt(kmax2) * (1.0 + 2.0 ** -7)

    q = q_ref[0]
    zero = jnp.zeros_like(q)
    qs = [jnp.where(half, q, zero), jnp.where(half, zero, q)]
    chunks = [(kv_refs[2 * src], vt_refs[src], c0, min(ck, nk))
              for src, nk in enumerate(sizes) for c0 in range(0, nk, min(ck, nk))]

    def scores(chunk, qm):
        k_ref, _, c0, c = chunk
        return lax.dot_general(k_ref[0, c0:c0 + c, :], qm, (((1,), (1,)), ((), ())),
                               preferred_element_type=F32)

    def finish(accs, ls):
        cl = cl_ref[...]
        lam = (jnp.exp(jnp.sum(cl[0:1] * cl[1:2], axis=-1, keepdims=True))
               - jnp.exp(jnp.sum(cl[2:3] * cl[3:4], axis=-1, keepdims=True)) + lam_init)
        o = accs[0] / ls[0] - lam * (accs[1] / ls[1])
        ms = jnp.mean(o * o, axis=0, keepdims=True)
        y = o * lax.rsqrt(ms + EPS) * (g_ref[...] * (1.0 - lam_init))
        o_ref[0] = y.T.astype(BF16)

    qsq = jnp.square(q.astype(F32).T)
    refs_r = [jnp.sqrt(jnp.sum(qsq[j * C_HEAD_DIM:(j + 1) * C_HEAD_DIM], axis=0, keepdims=True))
              * kmax_ref[0:1, j:j + 1] for j in range(2)]
    ls = [jnp.zeros((1, tq), F32) for _ in qs]
    accs = [jnp.zeros((C_V_DIM, tq), F32) for _ in qs]
    s_next = [scores(chunks[0], qm) for qm in qs]
    for t, (_, vt_ref, c0, c) in enumerate(chunks):
        s_cur = s_next
        if t + 1 < len(chunks):
            s_next = [scores(chunks[t + 1], qm) for qm in qs]
        vt = vt_ref[:, c0:c0 + c]
        for j, s in enumerate(s_cur):
            p = jnp.exp2(s - refs_r[j])
            ls[j] = ls[j] + jnp.sum(p, axis=0, keepdims=True)
            accs[j] = accs[j] + jnp.dot(vt, p.astype(BF16), preferred_element_type=F32)
    finish(accs, ls)

    healthy = jnp.min(jnp.minimum(ls[0], ls[1])) >= 2.0 ** -60

    @pl.when(jnp.logical_not(healthy))
    def _():
        ms = [jnp.full((1, tq), -jnp.inf, F32) for _ in qs]
        ls = [jnp.zeros((1, tq), F32) for _ in qs]
        accs = [jnp.zeros((C_V_DIM, tq), F32) for _ in qs]
        for chunk in chunks:
            _, vt_ref, c0, c = chunk
            vt = vt_ref[:, c0:c0 + c]
            for j, qm in enumerate(qs):
                s = scores(chunk, qm)
                m_new = jnp.maximum(ms[j], jnp.max(s, axis=0, keepdims=True))
                alpha = jnp.exp2(ms[j] - m_new)
                p = jnp.exp2(s - m_new)
                ls[j] = alpha * ls[j] + jnp.sum(p, axis=0, keepdims=True)
                accs[j] = alpha * accs[j] + jnp.dot(vt, p.astype(BF16),
                                                    preferred_element_type=F32)
                ms[j] = m_new
        finish(accs, ls)


def _attn_call(q_arr, q_cb, sources, c_lambda, subln_g, *, lam_init):
    nb, n, _ = q_arr.shape
    tq = min(512, n)
    sizes = tuple(a.shape[1] for a, _, _ in sources)
    in_specs = [
        pl.BlockSpec((4, C_HEAD_DIM), lambda b, h, i: (0, 0)),
        pl.BlockSpec((C_V_DIM, 1), lambda b, h, i: (0, 0)),
        pl.BlockSpec((1, tq, LANES), lambda b, h, i: (b, i, q_cb + h)),
    ]
    args = [c_lambda, subln_g.reshape(-1, 1), q_arr]
    for arr, kcb, vcb in sources:
        nk = arr.shape[1]
        in_specs.append(pl.BlockSpec((1, nk, LANES), lambda b, h, i, kcb=kcb: (b, 0, kcb + h)))
        in_specs.append(pl.BlockSpec((1, nk, LANES), lambda b, h, i, vcb=vcb: (b, 0, vcb + h)))
        args += [arr, arr]
    kern = functools.partial(_attn_kernel, tq=tq, ck=2048, sizes=sizes, lam_init=lam_init)
    return pl.pallas_call(
        kern,
        grid=(nb, C_HEADS, n // tq),
        in_specs=in_specs,
        out_specs=pl.BlockSpec((1, tq, C_V_DIM), lambda b, h, i: (b, i, h)),
        out_shape=jax.ShapeDtypeStruct((nb, n, C_HEADS * C_V_DIM), BF16),
        scratch_shapes=[pltpu.VMEM((C_V_DIM, nk), BF16) for nk in sizes]
        + [pltpu.VMEM((8, LANES), F32)],
        compiler_params=_cparams(("parallel", "parallel", "arbitrary")),
        name="diff_attention",
    )(*args)


def _merge_kernel(x_ref, ya_ref, yb_ref, yc_ref, ga_ref, gb_ref, gc_ref, wa_ref, wb_ref, wc_ref,
                  wo_ref, g1_ref, n2_ref, sh2_ref, sc2_ref, xo_ref, ho_ref):
    a = jnp.dot(ya_ref[0], wa_ref[...], preferred_element_type=F32)
    b = jnp.dot(yb_ref[0], wb_ref[...], preferred_element_type=F32)
    c = jnp.dot(yc_ref[0], wc_ref[...], preferred_element_type=F32)
    y = (ga_ref[0].astype(F32) * a + gb_ref[0].astype(F32) * b + gc_ref[0].astype(F32) * c)
    m = jnp.dot(y.astype(BF16), wo_ref[...], preferred_element_type=F32)
    xn = x_ref[0] + g1_ref[0] * m
    xo_ref[0] = xn
    ho_ref[0] = _modulate(xn, n2_ref[...], sh2_ref[0], sc2_ref[0]).astype(BF16)


def _merge_call(x, ya, yb, yc, p, gate_cb, wa, wb, wc, wo, g1, n2g, sh2, sc2):
    nb, n, d = x.shape
    tm = min(512, n)

    def tok(w, cb=0):
        return pl.BlockSpec((1, tm, w), lambda b, i: (b, i, cb))

    vec = pl.BlockSpec((1, 1, d), lambda b, i: (b, 0, 0))
    return pl.pallas_call(
        _merge_kernel,
        grid=(nb, n // tm),
        in_specs=[tok(d), tok(A_WIDTH), tok(B_WIDTH), tok(d),
                  tok(d, gate_cb), tok(d, gate_cb + 1), tok(d, gate_cb + 2),
                  _resident(wa.shape, lambda b, i: (0, 0)), _resident(wb.shape, lambda b, i: (0, 0)),
                  _resident(wc.shape, lambda b, i: (0, 0)), _resident(wo.shape, lambda b, i: (0, 0)),
                  vec, pl.BlockSpec((1, d), lambda b, i: (0, 0)), vec, vec],
        out_specs=[tok(d), tok(d)],
        out_shape=[jax.ShapeDtypeStruct((nb, n, d), F32), jax.ShapeDtypeStruct((nb, n, d), BF16)],
        compiler_params=_cparams(("parallel", "parallel")),
        name="merge_branches",
    )(x, ya, yb, yc, p, p, p, wa, wb, wc, wo, g1, n2g.reshape(1, d), sh2, sc2)


def _ff_chunks(width, chunk=512):
    out, c0 = [], 0
    while c0 < width:
        cw = min(chunk, width - c0)
        out.append((c0, cw))
        c0 += cw
    return out


def _finish(xn, final, ng_ref, sh_ref, sc_ref, out_refs):
    if final:
        out_refs[0][0] = _rmsnorm(xn, ng_ref[...])
    else:
        out_refs[0][0] = xn
        out_refs[1][0] = _modulate(xn, ng_ref[...], sh_ref[0], sc_ref[0]).astype(BF16)


def _ffn_kernel(x_ref, h_ref, wg_ref, wu_ref, wd_ref, g2_ref, ng_ref, sh_ref, sc_ref, *out_refs,
                final):
    h = h_ref[0]
    acc = None
    for c0, cw in _ff_chunks(wg_ref.shape[1]):
        a = jnp.dot(h, wg_ref[:, c0:c0 + cw], preferred_element_type=F32)
        b = jnp.dot(h, wu_ref[:, c0:c0 + cw], preferred_element_type=F32)
        t = (_silu(a) * b).astype(BF16)
        part = jnp.dot(t, wd_ref[c0:c0 + cw, :], preferred_element_type=F32)
        acc = part if acc is None else acc + part
    xn = x_ref[0] + g2_ref[0] * acc
    _finish(xn, final, ng_ref, sh_ref, sc_ref, out_refs)


def _epilogue_specs(nb, n, d, tm, final, idx):
    tok = pl.BlockSpec((1, tm, d), idx)
    if final:
        return [tok], [jax.ShapeDtypeStruct((nb, n, d), F32)]
    return [tok, tok], [jax.ShapeDtypeStruct((nb, n, d), F32), jax.ShapeDtypeStruct((nb, n, d), BF16)]


def _ffn_call(x, h, wg, wu, wd, g2, ng, sh, sc, *, final):
    nb, n, d = x.shape
    tm = min(512, n)
    idx = lambda b, i: (b, i, 0)
    vec = pl.BlockSpec((1, 1, d), lambda b, i: (b, 0, 0))
    out_specs, out_shape = _epilogue_specs(nb, n, d, tm, final, idx)
    return pl.pallas_call(
        functools.partial(_ffn_kernel, final=final),
        grid=(nb, n // tm),
        in_specs=[pl.BlockSpec((1, tm, d), idx), pl.BlockSpec((1, tm, d), idx),
                  _resident(wg.shape, lambda b, i: (0, 0)), _resident(wu.shape, lambda b, i: (0, 0)),
                  _resident(wd.shape, lambda b, i: (0, 0)),
                  vec, pl.BlockSpec((1, d), lambda b, i: (0, 0)), vec, vec],
        out_specs=out_specs,
        out_shape=out_shape,
        compiler_params=_cparams(("parallel", "parallel")),
        name="dense_swiglu",
    )(x, h, wg, wu, wd, g2, ng.reshape(1, d), sh, sc)


MOE_TILE = 2048
MOE_WIN = 512
MOE_SLAB = 256
MOE_ROWS = 512
MOE_FF = 512
CNT_ROWS = 8


def _route_kernel(h_ref, wr_ref, comb_ref, rank_ref, rankt_ref, cnt_ref, *, n_experts, win):
    h = h_ref[0]
    tm = h.shape[0]
    lane = lax.broadcasted_iota(jnp.int32, (tm, LANES), 1)
    logits = jnp.dot(h, wr_ref[...], preferred_element_type=F32)
    lg = jnp.where(lane < n_experts, logits, -jnp.inf)
    m1 = jnp.max(lg, axis=-1, keepdims=True)
    i1 = jnp.min(jnp.where(lg == m1, lane, LANES), axis=-1, keepdims=True)
    lg2 = jnp.where(lane == i1, -jnp.inf, lg)
    m2 = jnp.max(lg2, axis=-1, keepdims=True)
    i2 = jnp.min(jnp.where(lg2 == m2, lane, LANES), axis=-1, keepdims=True)
    e2 = jnp.exp(m2 - m1)
    w1 = 1.0 / (1.0 + e2)
    comb_ref[0] = jnp.where(lane == i1, w1, 0.0) + jnp.where(lane == i2, e2 * w1, 0.0)
    sel = (lane == i1) | (lane == i2)
    tri = jnp.where(lax.broadcasted_iota(jnp.int32, (win, win), 0)
                    > lax.broadcasted_iota(jnp.int32, (win, win), 1), 1.0, 0.0).astype(BF16)
    base = jnp.zeros((1, LANES), F32)
    bases = []
    for w in range(tm // win):
        rows = slice(w * win, (w + 1) * win)
        sw = jnp.where(sel[rows], 1.0, 0.0)
        excl = jnp.dot(tri, sw.astype(BF16), preferred_element_type=F32)
        rank_ref[0, rows, :] = jnp.where(sel[rows], excl + base, -1.0)
        bases.append(base)
        base = base + jnp.sum(sw, axis=0, keepdims=True)
    bases.append(base)
    bases += [jnp.zeros((1, LANES), F32)] * (CNT_ROWS - len(bases))
    cnt_ref[0] = jnp.concatenate(bases, axis=0).astype(jnp.int32)
    rankt_ref[0] = rank_ref[0].T[:CNT_ROWS]


def _route_call(h, w_router, tm):
    nb, n, d = h.shape
    n_experts = w_router.shape[1]
    assert n_experts <= CNT_ROWS and tm // MOE_WIN + 1 <= CNT_ROWS
    nt = n // tm
    wr = jnp.zeros((d, LANES), BF16).at[:, :n_experts].set(w_router.astype(BF16))
    tok = pl.BlockSpec((1, tm, LANES), lambda b, i: (b, i, 0))
    per_tile = lambda r, c: pl.BlockSpec((1, r, c), lambda b, i: (b * nt + i, 0, 0))
    return pl.pallas_call(
        functools.partial(_route_kernel, n_experts=n_experts, win=min(MOE_WIN, tm)),
        grid=(nb, nt),
        in_specs=[pl.BlockSpec((1, tm, d), lambda b, i: (b, i, 0)),
                  pl.BlockSpec((d, LANES), lambda b, i: (0, 0))],
        out_specs=[tok, tok, per_tile(CNT_ROWS, tm), per_tile(CNT_ROWS, LANES)],
        out_shape=[jax.ShapeDtypeStruct((nb, n, LANES), F32), jax.ShapeDtypeStruct((nb, n, LANES), F32),
                   jax.ShapeDtypeStruct((nb * nt, CNT_ROWS, tm), F32),
                   jax.ShapeDtypeStruct((nb * nt, CNT_ROWS, LANES), jnp.int32)],
        compiler_params=_cparams(("parallel", "parallel")),
        name="moe_route",
    )(h, wr)


def _moe_kernel(cnt_ref, x_ref, h_ref, comb_ref, rank_ref, rankt_ref, wg_ref, wu_ref, wd_ref, g2_ref,
                ng_ref, sh_ref, sc_ref, *rest, final, n_experts, win, slab, rblk):
    out_refs, (hc_ref, yc_ref) = rest[:-2], rest[-2:]
    acc_ref = out_refs[0].at[0]
    b, i, e, f = (pl.program_id(k) for k in range(4))
    tm, d = h_ref.shape[1], h_ref.shape[2]
    nw = tm // win
    tile = b * pl.num_programs(1) + i
    cbase = tile * (CNT_ROWS * n_experts)

    def count(w):
        return cnt_ref[cbase + w * n_experts + e]

    def slabs(w):
        row0 = (count(w) // BF16_SUBLANES) * BF16_SUBLANES
        return row0, (count(w + 1) - row0 + slab - 1) // slab

    @pl.when((e == 0) & (f == 0))
    def _():
        acc_ref[...] = jnp.zeros_like(acc_ref)

    total = count(nw)

    @pl.when(f == 0)
    def _():
        def zero(s, carry):
            rows = pl.ds(pl.multiple_of(s * slab, slab), slab)
            hc_ref[rows, :] = jnp.zeros((slab, d), BF16)
            yc_ref[rows, :] = jnp.zeros((slab, d), F32)
            return carry

        lax.fori_loop(0, jnp.minimum((total + rblk + 2 * slab) // slab, hc_ref.shape[0] // slab),
                      zero, 0)
        for w in range(nw):
            row0, nsl = slabs(w)
            rt = rankt_ref[0, 0, :, w * win:(w + 1) * win]
            hw = h_ref[0, w * win:(w + 1) * win, :]

            def body(s, carry, row0=row0, rt=rt, hw=hw):
                r0 = pl.multiple_of(row0 + s * slab, BF16_SUBLANES)
                rid = (r0 + lax.broadcasted_iota(jnp.int32, (slab, win), 0)).astype(F32)
                onehot = jnp.where(rt == rid, 1.0, 0.0).astype(BF16)
                rows = pl.ds(r0, slab)
                got = jnp.dot(onehot, hw, preferred_element_type=F32)
                hc_ref[rows, :] = (hc_ref[rows, :].astype(F32) + got).astype(BF16)
                return carry

            lax.fori_loop(0, nsl, body, 0)

    def ffn(r0, size):
        rows = pl.ds(pl.multiple_of(r0, rblk // 4), size)
        hb = hc_ref[rows, :]
        a = jnp.dot(hb, wg_ref[0, 0], preferred_element_type=F32)
        u = jnp.dot(hb, wu_ref[0, 0], preferred_element_type=F32)
        t = (_silu(a) * u).astype(BF16)
        yc_ref[rows, :] += jnp.dot(t, wd_ref[0], preferred_element_type=F32)

    quarters = (total + rblk // 4 - 1) // (rblk // 4)
    nfull = (quarters + 1) // 4
    rest = jnp.maximum(quarters - 4 * nfull, 0)

    def full(r, carry):
        ffn(r * rblk, rblk)
        return carry

    lax.fori_loop(0, nfull, full, 0)

    @pl.when(rest == 2)
    def _():
        ffn(nfull * rblk, rblk // 2)

    @pl.when(rest == 1)
    def _():
        ffn(nfull * rblk, rblk // 4)

    @pl.when(f == pl.num_programs(3) - 1)
    def _():
        for w in range(nw):
            row0, nsl = slabs(w)
            trows = slice(w * win, (w + 1) * win)
            mine = lax.broadcasted_iota(jnp.int32, (win, LANES), 1) == e
            rcol = jnp.sum(jnp.where(mine, rank_ref[0, trows, :], 0.0), axis=-1, keepdims=True)
            wcol = jnp.sum(jnp.where(mine, comb_ref[0, trows, :], 0.0), axis=-1, keepdims=True)

            def body(s, carry, row0=row0, rcol=rcol, wcol=wcol, trows=trows):
                r0 = pl.multiple_of(row0 + s * slab, BF16_SUBLANES)
                cid = (r0 + lax.broadcasted_iota(jnp.int32, (win, slab), 1)).astype(F32)
                onehot = jnp.where(rcol == cid, 1.0, 0.0).astype(BF16)
                yb = yc_ref[pl.ds(r0, slab), :].astype(BF16)
                acc_ref[trows, :] += wcol * jnp.dot(onehot, yb, preferred_element_type=F32)
                return carry

            lax.fori_loop(0, nsl, body, 0)

    @pl.when((e == pl.num_programs(2) - 1) & (f == pl.num_programs(3) - 1))
    def _():
        xn = x_ref[0] + g2_ref[0] * acc_ref[...]
        _finish(xn, final, ng_ref, sh_ref, sc_ref, out_refs)


def _moe_call(x, h, w_router, wg, wu, wd, g2, ng, sh, sc, *, final):
    nb, n, d = x.shape
    n_experts, _, dff = wg.shape
    tm = min(MOE_TILE, n)
    win = min(MOE_WIN, tm)
    nt = n // tm
    comb, rank, rankt, cnt = _route_call(h, w_router, tm)
    cnt = cnt[:, :, :n_experts].reshape(-1)
    blocked = lambda w: w.reshape(n_experts, d, dff // MOE_FF, MOE_FF).transpose(0, 2, 1, 3)
    wg, wu = blocked(wg), blocked(wu)
    rankt = rankt.reshape(nb * nt, CNT_ROWS, 1, tm)
    cap_ffn = -(-tm // MOE_ROWS) * MOE_ROWS
    cap = -(-max(cap_ffn, tm + MOE_SLAB) // MOE_SLAB) * MOE_SLAB

    idx = lambda b, i, e, f, c: (b, i, 0)
    vec = pl.BlockSpec((1, 1, d), lambda b, i, e, f, c: (b, 0, 0))
    out_specs, out_shape = _epilogue_specs(nb, n, d, tm, final, idx)
    grid_spec = pltpu.PrefetchScalarGridSpec(
        num_scalar_prefetch=1,
        grid=(nb, nt, n_experts, dff // MOE_FF),
        in_specs=[_resident((1, tm, d), idx), _resident((1, tm, d), idx),
                  _resident((1, tm, LANES), idx), _resident((1, tm, LANES), idx),
                  pl.BlockSpec((1, 1, 1, tm), lambda b, i, e, f, c: (b * nt + i, e, 0, 0)),
                  pl.BlockSpec((1, 1, d, MOE_FF), lambda b, i, e, f, c: (e, f, 0, 0)),
                  pl.BlockSpec((1, 1, d, MOE_FF), lambda b, i, e, f, c: (e, f, 0, 0)),
                  pl.BlockSpec((1, MOE_FF, d), lambda b, i, e, f, c: (e, f, 0)),
                  vec, pl.BlockSpec((1, d), lambda b, i, e, f, c: (0, 0)), vec, vec],
        out_specs=out_specs,
        scratch_shapes=[pltpu.VMEM((cap, d), BF16), pltpu.VMEM((cap, d), F32)],
    )
    return pl.pallas_call(
        functools.partial(_moe_kernel, final=final, n_experts=n_experts, win=win, slab=MOE_SLAB,
                          rblk=MOE_ROWS),
        grid_spec=grid_spec,
        out_shape=out_shape,
        compiler_params=_cparams(("parallel", "parallel", "arbitrary", "arbitrary")),
        name="moe_swiglu",
    )(cnt, x, h, comb, rank, rankt, wg, wu, wd, g2, ng.reshape(1, d), sh, sc)


def kernel(x, c, ctx, c_ctx, w_ada, b_ada, norm1_g, norm2_g, w_in, b_gate, a_ln_g, a_ln_b, a_ws,
           a_bs, b_conv, c_lambda, c_subln_g, w_a_out, w_b_out, w_c_out, w_o, ff_w_gate, ff_w_up,
           ff_w_down, moe_w_router, moe_w_gate, moe_w_up, moe_w_down, final_norm_g):
    bsz, n, d = x.shape
    nc = ctx.shape[1]
    depth = w_ada.shape[0]
    cols = _Cols(d)
    head_cb = COL_BLOCK // LANES
    gate_cb = cols.out(cols.gate) * COL_BLOCK // d
    mix_pos = tuple(cols.out(j) for j in (cols.au, cols.av, cols.bb, cols.bc, cols.bh))
    cq_l, ck_l, cv_l = (cols.out(j) * head_cb for j in (cols.cq, cols.ck, cols.cv))

    pad = (-(bsz + 1)) % 8
    cc = jnp.concatenate([c, c_ctx[None], jnp.zeros((pad, d), F32)], axis=0)
    mod = _ada_call(cc, w_ada, b_ada)

    def mods(l):
        lat = [mod[l, :bsz, k * d:(k + 1) * d].reshape(bsz, 1, d) for k in range(N_MOD)]
        con = [mod[l, bsz:bsz + 1, k * d:(k + 1) * d].reshape(1, 1, d) for k in range(N_MOD)]
        return lat, con

    tables = _rope_tables(n)
    w_in_b = w_in.astype(BF16)
    xl = x
    xc = ctx.reshape(1, bsz * nc, d)
    lat, con = mods(0)
    h = _mod_call(xl, norm1_g[0], lat[0], lat[1])
    hc = _mod_call(xc, norm1_g[0], con[0], con[1])

    for l in range(depth):
        last = l == depth - 1
        lam_init = 0.8 - 0.6 * math.exp(-0.3 * l)
        lat, con = mods(l)
        if not last:
            nlat, ncon = mods(l + 1)
            nxt_l = (norm1_g[l + 1], nlat[0], nlat[1])
            nxt_c = (norm1_g[l + 1], ncon[0], ncon[1])
        else:
            nxt_l = (final_norm_g, lat[0], lat[1])
            nxt_c = None

        p = _in_call(h, w_in_b[l], b_gate[l], tables, j0=0, nj=cols.end, rope=True)
        if last:
            pc = _in_call(hc, w_in_b[l], b_gate[l], tables, j0=cols.ck, nj=cols.gate - cols.ck,
                          rope=False)
            ck_c, cv_c = 0, (cols.cv - cols.ck) * head_cb
        else:
            pc = _in_call(hc, w_in_b[l], b_gate[l], tables, j0=0, nj=cols.end, rope=False)
            ck_c, cv_c = ck_l, cv_l
        pc_seq = pc.reshape(bsz, nc, -1)

        wa, wb, wc, wo = (w.astype(BF16) for w in (w_a_out[l], w_b_out[l], w_c_out[l], w_o[l]))

        def channel(xs, hs, g2, nxt, final):
            i = l // 2
            if l % 2 == 0:
                return _ffn_call(xs, hs, ff_w_gate[i].astype(BF16), ff_w_up[i].astype(BF16),
                                 ff_w_down[i].astype(BF16), g2, *nxt, final=final)
            return _moe_call(xs, hs, moe_w_router[i], moe_w_gate[i].astype(BF16),
                             moe_w_up[i].astype(BF16), moe_w_down[i].astype(BF16), g2, *nxt,
                             final=final)

        ya, yb = _local_call(p, mix_pos, a_ln_g[l], a_ln_b[l], a_ws[l], a_bs[l], b_conv[l], seq=n)
        yc = _attn_call(p, cq_l, [(p, ck_l, cv_l), (pc_seq, ck_c, cv_c)],
                        c_lambda[l], c_subln_g[l], lam_init=lam_init)
        x1, h2 = _merge_call(xl, ya, yb, yc, p, gate_cb, wa, wb, wc, wo, lat[2], norm2_g[l],
                             lat[3], lat[4])
        res = channel(x1, h2, lat[5], nxt_l, last)
        if last:
            return res[0]
        xl, h = res

        ya_c, yb_c = _local_call(pc_seq, mix_pos, a_ln_g[l], a_ln_b[l], a_ws[l], a_bs[l],
                                 b_conv[l], seq=nc)
        yc_c = _attn_call(pc_seq, cq_l, [(pc_seq, ck_c, cv_c)], c_lambda[l], c_subln_g[l],
                          lam_init=lam_init)
        flat = lambda a: a.reshape(1, bsz * nc, -1)
        xc1, hc2 = _merge_call(xc, flat(ya_c), flat(yb_c), flat(yc_c), pc, gate_cb, wa, wb, wc, wo,
                               con[2], norm2_g[l], con[3], con[4])
        xc, hc = channel(xc1, hc2, con[5], nxt_c, False)
```

```python
import functools
import math

import jax
import jax.numpy as jnp
from jax import lax
from jax.experimental import pallas as pl
from jax.experimental.pallas import tpu as pltpu

F32 = jnp.float32
BF16 = jnp.bfloat16

EPS = 1e-6
GRID_W = 64
N_MOD = 6
N_BRANCH = 3
CHUNK = 128
A_GROUPS = 8
A_GROUP_DIM = 64
A_WIDTH = A_GROUPS * A_GROUP_DIM
B_WIDTH = 512
CONV_W = 3
C_HEADS = 8
C_HEAD_DIM = 64
C_V_DIM = 2 * C_HEAD_DIM
ROPE_BASE = 10000.0
TOP_K = 2
LOG2E = 1.4426950408889634

LANES = 128
BF16_SUBLANES = 16
V7X_VMEM_LIMIT = 56 * 1024 * 1024

COL_BLOCK = 512


def _cparams(sem):
    return pltpu.CompilerParams(dimension_semantics=sem, vmem_limit_bytes=V7X_VMEM_LIMIT)


def _resident(shape, index_map):
    return pl.BlockSpec(shape, index_map, pipeline_mode=pl.Buffered(1))


def _sigmoid(v):
    return 0.5 * (1.0 + jnp.tanh(0.5 * v))


def _silu(v):
    return v * _sigmoid(v)


def _modulate(x, g, sh, sc):
    ms = jnp.mean(x * x, axis=-1, keepdims=True)
    y = x * lax.rsqrt(ms + EPS)
    return (y * g) * (1.0 + sc) + sh


def _rmsnorm(x, g):
    ms = jnp.mean(x * x, axis=-1, keepdims=True)
    return x * lax.rsqrt(ms + EPS) * g


def _ada_kernel(c_ref, w_ref, b_ref, o_ref):
    s = _silu(c_ref[...])
    o_ref[0] = jnp.dot(s, w_ref[0], preferred_element_type=F32,
                       precision=lax.Precision.HIGHEST) + b_ref[0]


def _ada_call(cc, w_ada, b_ada):
    depth, d, cols = w_ada.shape
    rows = cc.shape[0]
    tn = 1536
    return pl.pallas_call(
        _ada_kernel,
        grid=(depth, cols // tn),
        in_specs=[
            pl.BlockSpec((rows, d), lambda l, j: (0, 0)),
            pl.BlockSpec((1, d, tn), lambda l, j: (l, 0, j)),
            pl.BlockSpec((1, 1, tn), lambda l, j: (l, 0, j)),
        ],
        out_specs=pl.BlockSpec((1, rows, tn), lambda l, j: (l, 0, j)),
        out_shape=jax.ShapeDtypeStruct((depth, rows, cols), F32),
        compiler_params=_cparams(("parallel", "parallel")),
        name="ada_proj",
    )(cc, w_ada, b_ada.reshape(depth, 1, cols))


def _mod_kernel(x_ref, g_ref, sh_ref, sc_ref, o_ref):
    o_ref[0] = _modulate(x_ref[0], g_ref[...], sh_ref[0], sc_ref[0]).astype(BF16)


def _mod_call(x, g, sh, sc):
    nb, n, d = x.shape
    tm = min(1024, n)
    vec = pl.BlockSpec((1, 1, d), lambda b, i: (b, 0, 0))
    return pl.pallas_call(
        _mod_kernel,
        grid=(nb, n // tm),
        in_specs=[pl.BlockSpec((1, tm, d), lambda b, i: (b, i, 0)),
                  pl.BlockSpec((1, d), lambda b, i: (0, 0)), vec, vec],
        out_specs=pl.BlockSpec((1, tm, d), lambda b, i: (b, i, 0)),
        out_shape=jax.ShapeDtypeStruct((nb, n, d), BF16),
        compiler_params=_cparams(("parallel", "parallel")),
        name="modulate",
    )(x, g.reshape(1, d), sh, sc)


class _Cols:
    def __init__(self, d):
        self.au = 0
        self.av = self.au + A_WIDTH // COL_BLOCK
        self.bb = self.av + A_WIDTH // COL_BLOCK
        self.bc = self.bb + B_WIDTH // COL_BLOCK
        self.bh = self.bc + B_WIDTH // COL_BLOCK
        self.cq = self.bh + B_WIDTH // COL_BLOCK
        qk = C_HEADS * 2 * C_HEAD_DIM // COL_BLOCK
        self.ck = self.cq + qk
        self.cv = self.ck + qk
        self.gate = self.cv + C_HEADS * C_V_DIM // COL_BLOCK
        self.end = self.gate + N_BRANCH * d // COL_BLOCK

    def out(self, jw):
        return (jw - self.gate) % self.end


def _in_kernel(h_ref, w_ref, bg_ref, cos_ref, sa_ref, sb_ref, o_ref, *, cols, j0, rope, qscale,
               sub):
    j = pl.program_id(2) + j0
    tm = h_ref.shape[1]

    def run(epilogue):
        def mm(r):
            return jnp.dot(h_ref[0, r * sub:(r + 1) * sub, :], w_ref[...],
                           preferred_element_type=F32)
        nxt = mm(0)
        for r in range(tm // sub):
            acc = nxt
            if r + 1 < tm // sub:
                nxt = mm(r + 1)
            epilogue(acc, slice(r * sub, (r + 1) * sub))

    @pl.when(j < cols.bb)
    def _():
        def gelu(acc, rows):
            o_ref[0, rows, :] = jax.nn.gelu(acc).astype(BF16)
        run(gelu)

    @pl.when(((j >= cols.bb) & (j < cols.cq)) | ((j >= cols.cv) & (j < cols.gate)))
    def _():
        def plain(acc, rows):
            o_ref[0, rows, :] = acc.astype(BF16)
        run(plain)

    @pl.when((j >= cols.cq) & (j < cols.cv))
    def _():
        scale = jnp.where(j < cols.ck, qscale, 1.0).astype(F32)

        def rotary(acc, rows):
            if not rope:
                o_ref[0, rows, :] = (acc * scale).astype(BF16)
                return
            cos, sa, sb = cos_ref[rows, :] * scale, sa_ref[rows, :] * scale, sb_ref[rows, :] * scale
            for s in range(COL_BLOCK // LANES):
                t = acc[:, s * LANES:(s + 1) * LANES]
                r = (t * cos + pltpu.roll(t, LANES - 16, 1) * sa + pltpu.roll(t, 16, 1) * sb)
                o_ref[0, rows, s * LANES:(s + 1) * LANES] = r.astype(BF16)
        run(rotary)

    @pl.when(j >= cols.gate)
    def _():
        def gate(acc, rows):
            o_ref[0, rows, :] = _sigmoid(acc + bg_ref[...]).astype(BF16)
        run(gate)


def _in_call(h, w, b_gate, tables, *, j0, nj, rope):
    nb, n, d = h.shape
    cols = _Cols(d)
    tm = min(4096, n)
    full = (j0 == 0) and (nj == cols.end)
    if rope:
        cos, sa, sb = tables
    else:
        cos = sa = sb = jnp.zeros((tm, LANES), F32)
    tab = pl.BlockSpec((tm, LANES), lambda b, i, j: (i if rope else 0, 0))

    def out_idx(b, i, j):
        if full:
            return (b, i, jnp.where(j >= cols.gate, j - cols.gate, j + cols.end - cols.gate))
        return (b, i, j)
    kern = functools.partial(_in_kernel, cols=cols, j0=j0, rope=rope,
                             qscale=C_HEAD_DIM ** -0.5 * LOG2E, sub=min(512, tm))
    return pl.pallas_call(
        kern,
        grid=(nb, n // tm, nj),
        in_specs=[
            pl.BlockSpec((1, tm, d), lambda b, i, j: (b, i, 0)),
            pl.BlockSpec((d, COL_BLOCK), lambda b, i, j: (0, j + j0)),
            pl.BlockSpec((1, COL_BLOCK), lambda b, i, j: (0, jnp.maximum(j + j0 - cols.gate, 0))),
            tab, tab, tab,
        ],
        out_specs=pl.BlockSpec((1, tm, COL_BLOCK), out_idx),
        out_shape=jax.ShapeDtypeStruct((nb, n, nj * COL_BLOCK), BF16),
        compiler_params=_cparams(("parallel", "parallel", "arbitrary")),
        name="in_proj",
    )(h, w, b_gate.reshape(1, -1), cos, sa, sb)


def _rope_tables(n):
    rows = n // GRID_W
    r = jnp.repeat(jnp.arange(rows, dtype=F32), GRID_W)
    col = jnp.tile(jnp.arange(GRID_W, dtype=F32), rows)
    quarter = C_HEAD_DIM // 4
    inv = ROPE_BASE ** (-jnp.arange(quarter, dtype=F32) / quarter)
    ar = r[:, None] * inv
    ac = col[:, None] * inv
    ang = jnp.concatenate([ar, ar, ac, ac], axis=-1)
    ang = jnp.tile(ang, (1, LANES // C_HEAD_DIM))
    cos, sin = jnp.cos(ang), jnp.sin(ang)
    first_of_pair = (jnp.arange(LANES) // quarter) % 2 == 0
    sa = jnp.where(first_of_pair, -sin, 0.0)
    sb = jnp.where(first_of_pair, 0.0, sin)
    return cos, sa, sb


def _local_kernel(u_ref, v_ref, bg_ref, cg_ref, hh_ref, cgp_ref, hhp_ref, cgn_ref, hhn_ref,
                  lng_ref, lnb_ref, ws_ref, bias_ref, cw_ref, ya_ref, yb_ref, *, tm, seq):
    i = pl.program_id(1)
    lane = lax.broadcasted_iota(jnp.int32, (CHUNK, LANES), 1)
    lo = lane < A_GROUP_DIM
    for c in range(tm // CHUNK):
        rows = pl.ds(c * CHUNK, CHUNK)
        v = v_ref[0, rows, :].astype(F32)
        mu = jnp.mean(v, axis=-1, keepdims=True)
        var = jnp.mean(jnp.square(v - mu), axis=-1, keepdims=True)
        vn = ((v - mu) * lax.rsqrt(var + EPS) * lng_ref[...] + lnb_ref[...]).astype(BF16)
        for k in range(A_WIDTH // LANES):
            blk = vn[:, k * LANES:(k + 1) * LANES]
            zero = jnp.zeros_like(blk)
            mixed = (jnp.dot(ws_ref[2 * k], jnp.where(lo, blk, zero), preferred_element_type=F32)
                     + jnp.dot(ws_ref[2 * k + 1], jnp.where(lo, zero, blk), preferred_element_type=F32)
                     + bias_ref[:, k * LANES:(k + 1) * LANES])
            u = u_ref[0, rows, k * LANES:(k + 1) * LANES].astype(F32)
            ya_ref[0, rows, k * LANES:(k + 1) * LANES] = (u * mixed).astype(BF16)

    z = cg_ref[0].astype(F32) * hh_ref[0].astype(F32)
    last = BF16_SUBLANES - 1
    z_prev = cgp_ref[0, last:last + 1, :].astype(F32) * hhp_ref[0, last:last + 1, :].astype(F32)
    z_next = cgn_ref[0, 0:1, :].astype(F32) * hhn_ref[0, 0:1, :].astype(F32)
    row = lax.broadcasted_iota(jnp.int32, (tm, 1), 0)
    pos = (i * tm) % seq + row
    zm1 = jnp.where(row == 0, z_prev, pltpu.roll(z, 1, 0))
    zm1 = jnp.where(pos == 0, 0.0, zm1)
    zp1 = jnp.where(row == tm - 1, z_next, pltpu.roll(z, tm - 1, 0))
    zp1 = jnp.where(pos == seq - 1, 0.0, zp1)
    conv = cw_ref[0:1, :] * zm1 + cw_ref[1:2, :] * z + cw_ref[2:3, :] * zp1
    yb_ref[0] = (bg_ref[0].astype(F32) * conv).astype(BF16)


def _local_call(p, pos, ln_g, ln_b, ws, bs, conv_w, *, seq):
    nb, n, _ = p.shape
    au, av, bb, bc, bh = pos
    tm = min(512, seq)
    hb = tm // BF16_SUBLANES
    nhb = n // BF16_SUBLANES

    def main(cb):
        return pl.BlockSpec((1, tm, COL_BLOCK), lambda b, i: (b, i, cb))

    def prev(cb):
        return pl.BlockSpec((1, BF16_SUBLANES, COL_BLOCK),
                            lambda b, i: (b, jnp.maximum(i * hb - 1, 0), cb))

    def nxt(cb):
        return pl.BlockSpec((1, BF16_SUBLANES, COL_BLOCK),
                            lambda b, i: (b, jnp.minimum((i + 1) * hb, nhb - 1), cb))

    bias = jnp.repeat(bs.T, A_GROUP_DIM, axis=1)
    out = jax.ShapeDtypeStruct((nb, n, A_WIDTH), BF16)
    kern = functools.partial(_local_kernel, tm=tm, seq=seq)
    return pl.pallas_call(
        kern,
        grid=(nb, n // tm),
        in_specs=[main(au), main(av), main(bb), main(bc), main(bh),
                  prev(bc), prev(bh), nxt(bc), nxt(bh),
                  pl.BlockSpec((1, A_WIDTH), lambda b, i: (0, 0)),
                  pl.BlockSpec((1, A_WIDTH), lambda b, i: (0, 0)),
                  pl.BlockSpec((A_GROUPS, CHUNK, CHUNK), lambda b, i: (0, 0, 0)),
                  pl.BlockSpec((CHUNK, A_WIDTH), lambda b, i: (0, 0)),
                  pl.BlockSpec((CONV_W, B_WIDTH), lambda b, i: (0, 0))],
        out_specs=[pl.BlockSpec((1, tm, A_WIDTH), lambda b, i: (b, i, 0)),
                   pl.BlockSpec((1, tm, B_WIDTH), lambda b, i: (b, i, 0))],
        out_shape=[out, out],
        compiler_params=_cparams(("parallel", "parallel")),
        name="local_mixers",
    )(p, p, p, p, p, p, p, p, p, ln_g.reshape(1, -1), ln_b.reshape(1, -1), ws.astype(BF16),
      bias, conv_w)


def _attn_kernel(*refs, tq, ck, sizes, lam_init):
    nsrc = len(sizes)
    cl_ref, g_ref, q_ref = refs[:3]
    kv_refs = refs[3:3 + 2 * nsrc]
    o_ref = refs[3 + 2 * nsrc]
    vt_refs = refs[4 + 2 * nsrc:4 + 3 * nsrc]
    kmax_ref = refs[4 + 3 * nsrc]
    half = lax.broadcasted_iota(jnp.int32, (1, LANES), 1) < C_HEAD_DIM

    @pl.when(pl.program_id(2) == 0)
    def _():
        per_map = jnp.where(lax.broadcasted_iota(jnp.int32, (LANES, LANES), 0) // C_HEAD_DIM
                            == lax.broadcasted_iota(jnp.int32, (LANES, LANES), 1), 1.0, 0.0)
        kmax2 = jnp.zeros((1, LANES), F32)
        for src in range(nsrc):
            vt_refs[src][...] = kv_refs[2 * src + 1][0].astype(F32).T.astype(BF16)
            k = kv_refs[2 * src][0]
            n2 = jnp.dot(k * k, per_map.astype(BF16), preferred_element_type=F32)
            kmax2 = jnp.maximum(kmax2, jnp.max(n2, axis=0, keepdims=True))
        kmax_ref[0:1, :] = jnp.sqrt(kmax2) * (1.0 + 2.0 ** -7)

    q = q_ref[0]
    zero = jnp.zeros_like(q)
    qs = [jnp.where(half, q, zero), jnp.where(half, zero, q)]
    chunks = [(kv_refs[2 * src], vt_refs[src], c0, min(ck, nk))
              for src, nk in enumerate(sizes) for c0 in range(0, nk, min(ck, nk))]

    def scores(chunk, qm):
        k_ref, _, c0, c = chunk
        return lax.dot_general(k_ref[0, c0:c0 + c, :], qm, (((1,), (1,)), ((), ())),
                               preferred_element_type=F32)

    def finish(accs, ls):
        cl = cl_ref[...]
        lam = (jnp.exp(jnp.sum(cl[0:1] * cl[1:2], axis=-1, keepdims=True))
               - jnp.exp(jnp.sum(cl[2:3] * cl[3:4], axis=-1, keepdims=True)) + lam_init)
        o = accs[0] / ls[0] - lam * (accs[1] / ls[1])
        ms = jnp.mean(o * o, axis=0, keepdims=True)
        y = o * lax.rsqrt(ms + EPS) * (g_ref[...] * (1.0 - lam_init))
        o_ref[0] = y.T.astype(BF16)

    qsq = jnp.square(q.astype(F32).T)
    refs_r = [jnp.sqrt(jnp.sum(qsq[j * C_HEAD_DIM:(j + 1) * C_HEAD_DIM], axis=0, keepdims=True))
              * kmax_ref[0:1, j:j + 1] for j in range(2)]
    ls = [jnp.zeros((1, tq), F32) for _ in qs]
    accs = [jnp.zeros((C_V_DIM, tq), F32) for _ in qs]
    s_next = [scores(chunks[0], qm) for qm in qs]
    for t, (_, vt_ref, c0, c) in enumerate(chunks):
        s_cur = s_next
        if t + 1 < len(chunks):
            s_next = [scores(chunks[t + 1], qm) for qm in qs]
        vt = vt_ref[:, c0:c0 + c]
        for j, s in enumerate(s_cur):
            p = jnp.exp2(s - refs_r[j])
            ls[j] = ls[j] + jnp.sum(p, axis=0, keepdims=True)
            accs[j] = accs[j] + jnp.dot(vt, p.astype(BF16), preferred_element_type=F32)
    finish(accs, ls)

    healthy = jnp.min(jnp.minimum(ls[0], ls[1])) >= 2.0 ** -60

    @pl.when(jnp.logical_not(healthy))
    def _():
        ms = [jnp.full((1, tq), -jnp.inf, F32) for _ in qs]
        ls = [jnp.zeros((1, tq), F32) for _ in qs]
        accs = [jnp.zeros((C_V_DIM, tq), F32) for _ in qs]
        for chunk in chunks:
            _, vt_ref, c0, c = chunk
            vt = vt_ref[:, c0:c0 + c]
            for j, qm in enumerate(qs):
                s = scores(chunk, qm)
                m_new = jnp.maximum(ms[j], jnp.max(s, axis=0, keepdims=True))
                alpha = jnp.exp2(ms[j] - m_new)
                p = jnp.exp2(s - m_new)
                ls[j] = alpha * ls[j] + jnp.sum(p, axis=0, keepdims=True)
                accs[j] = alpha * accs[j] + jnp.dot(vt, p.astype(BF16),
                                                    preferred_element_type=F32)
                ms[j] = m_new
        finish(accs, ls)


def _attn_call(q_arr, q_cb, sources, c_lambda, subln_g, *, lam_init):
    nb, n, _ = q_arr.shape
    tq = min(512, n)
    sizes = tuple(a.shape[1] for a, _, _ in sources)
    in_specs = [
        pl.BlockSpec((4, C_HEAD_DIM), lambda b, h, i: (0, 0)),
        pl.BlockSpec((C_V_DIM, 1), lambda b, h, i: (0, 0)),
        pl.BlockSpec((1, tq, LANES), lambda b, h, i: (b, i, q_cb + h)),
    ]
    args = [c_lambda, subln_g.reshape(-1, 1), q_arr]
    for arr, kcb, vcb in sources:
        nk = arr.shape[1]
        in_specs.append(pl.BlockSpec((1, nk, LANES), lambda b, h, i, kcb=kcb: (b, 0, kcb + h)))
        in_specs.append(pl.BlockSpec((1, nk, LANES), lambda b, h, i, vcb=vcb: (b, 0, vcb + h)))
        args += [arr, arr]
    kern = functools.partial(_attn_kernel, tq=tq, ck=2048, sizes=sizes, lam_init=lam_init)
    return pl.pallas_call(
        kern,
        grid=(nb, C_HEADS, n // tq),
        in_specs=in_specs,
        out_specs=pl.BlockSpec((1, tq, C_V_DIM), lambda b, h, i: (b, i, h)),
        out_shape=jax.ShapeDtypeStruct((nb, n, C_HEADS * C_V_DIM), BF16),
        scratch_shapes=[pltpu.VMEM((C_V_DIM, nk), BF16) for nk in sizes]
        + [pltpu.VMEM((8, LANES), F32)],
        compiler_params=_cparams(("parallel", "parallel", "arbitrary")),
        name="diff_attention",
    )(*args)


def _merge_kernel(x_ref, ya_ref, yb_ref, yc_ref, ga_ref, gb_ref, gc_ref, wa_ref, wb_ref, wc_ref,
                  wo_ref, g1_ref, n2_ref, sh2_ref, sc2_ref, xo_ref, ho_ref):
    a = jnp.dot(ya_ref[0], wa_ref[...], preferred_element_type=F32)
    b = jnp.dot(yb_ref[0], wb_ref[...], preferred_element_type=F32)
    c = jnp.dot(yc_ref[0], wc_ref[...], preferred_element_type=F32)
    y = (ga_ref[0].astype(F32) * a + gb_ref[0].astype(F32) * b + gc_ref[0].astype(F32) * c)
    m = jnp.dot(y.astype(BF16), wo_ref[...], preferred_element_type=F32)
    xn = x_ref[0] + g1_ref[0] * m
    xo_ref[0] = xn
    ho_ref[0] = _modulate(xn, n2_ref[...], sh2_ref[0], sc2_ref[0]).astype(BF16)


def _merge_call(x, ya, yb, yc, p, gate_cb, wa, wb, wc, wo, g1, n2g, sh2, sc2):
    nb, n, d = x.shape
    tm = min(512, n)

    def tok(w, cb=0):
        return pl.BlockSpec((1, tm, w), lambda b, i: (b, i, cb))

    vec = pl.BlockSpec((1, 1, d), lambda b, i: (b, 0, 0))
    return pl.pallas_call(
        _merge_kernel,
        grid=(nb, n // tm),
        in_specs=[tok(d), tok(A_WIDTH), tok(B_WIDTH), tok(d),
                  tok(d, gate_cb), tok(d, gate_cb + 1), tok(d, gate_cb + 2),
                  _resident(wa.shape, lambda b, i: (0, 0)), _resident(wb.shape, lambda b, i: (0, 0)),
                  _resident(wc.shape, lambda b, i: (0, 0)), _resident(wo.shape, lambda b, i: (0, 0)),
                  vec, pl.BlockSpec((1, d), lambda b, i: (0, 0)), vec, vec],
        out_specs=[tok(d), tok(d)],
        out_shape=[jax.ShapeDtypeStruct((nb, n, d), F32), jax.ShapeDtypeStruct((nb, n, d), BF16)],
        compiler_params=_cparams(("parallel", "parallel")),
        name="merge_branches",
    )(x, ya, yb, yc, p, p, p, wa, wb, wc, wo, g1, n2g.reshape(1, d), sh2, sc2)


def _ff_chunks(width, chunk=512):
    out, c0 = [], 0
    while c0 < width:
        cw = min(chunk, width - c0)
        out.append((c0, cw))
        c0 += cw
    return out


def _finish(xn, final, ng_ref, sh_ref, sc_ref, out_refs):
    if final:
        out_refs[0][0] = _rmsnorm(xn, ng_ref[...])
    else:
        out_refs[0][0] = xn
        out_refs[1][0] = _modulate(xn, ng_ref[...], sh_ref[0], sc_ref[0]).astype(BF16)


def _ffn_kernel(x_ref, h_ref, wg_ref, wu_ref, wd_ref, g2_ref, ng_ref, sh_ref, sc_ref, *out_refs,
                final):
    h = h_ref[0]
    acc = None
    for c0, cw in _ff_chunks(wg_ref.shape[1]):
        a = jnp.dot(h, wg_ref[:, c0:c0 + cw], preferred_element_type=F32)
        b = jnp.dot(h, wu_ref[:, c0:c0 + cw], preferred_element_type=F32)
        t = (_silu(a) * b).astype(BF16)
        part = jnp.dot(t, wd_ref[c0:c0 + cw, :], preferred_element_type=F32)
        acc = part if acc is None else acc + part
    xn = x_ref[0] + g2_ref[0] * acc
    _finish(xn, final, ng_ref, sh_ref, sc_ref, out_refs)


def _epilogue_specs(nb, n, d, tm, final, idx):
    tok = pl.BlockSpec((1, tm, d), idx)
    if final:
        return [tok], [jax.ShapeDtypeStruct((nb, n, d), F32)]
    return [tok, tok], [jax.ShapeDtypeStruct((nb, n, d), F32), jax.ShapeDtypeStruct((nb, n, d), BF16)]


def _ffn_call(x, h, wg, wu, wd, g2, ng, sh, sc, *, final):
    nb, n, d = x.shape
    tm = min(512, n)
    idx = lambda b, i: (b, i, 0)
    vec = pl.BlockSpec((1, 1, d), lambda b, i: (b, 0, 0))
    out_specs, out_shape = _epilogue_specs(nb, n, d, tm, final, idx)
    return pl.pallas_call(
        functools.partial(_ffn_kernel, final=final),
        grid=(nb, n // tm),
        in_specs=[pl.BlockSpec((1, tm, d), idx), pl.BlockSpec((1, tm, d), idx),
                  _resident(wg.shape, lambda b, i: (0, 0)), _resident(wu.shape, lambda b, i: (0, 0)),
                  _resident(wd.shape, lambda b, i: (0, 0)),
                  vec, pl.BlockSpec((1, d), lambda b, i: (0, 0)), vec, vec],
        out_specs=out_specs,
        out_shape=out_shape,
        compiler_params=_cparams(("parallel", "parallel")),
        name="dense_swiglu",
    )(x, h, wg, wu, wd, g2, ng.reshape(1, d), sh, sc)


MOE_TILE = 2048
MOE_WIN = 512
MOE_SLAB = 256
MOE_ROWS = 512
MOE_FF = 512
CNT_ROWS = 8


def _route_kernel(h_ref, wr_ref, comb_ref, rank_ref, rankt_ref, cnt_ref, *, n_experts, win):
    h = h_ref[0]
    tm = h.shape[0]
    lane = lax.broadcasted_iota(jnp.int32, (tm, LANES), 1)
    logits = jnp.dot(h, wr_ref[...], preferred_element_type=F32)
    lg = jnp.where(lane < n_experts, logits, -jnp.inf)
    m1 = jnp.max(lg, axis=-1, keepdims=True)
    i1 = jnp.min(jnp.where(lg == m1, lane, LANES), axis=-1, keepdims=True)
    lg2 = jnp.where(lane == i1, -jnp.inf, lg)
    m2 = jnp.max(lg2, axis=-1, keepdims=True)
    i2 = jnp.min(jnp.where(lg2 == m2, lane, LANES), axis=-1, keepdims=True)
    e2 = jnp.exp(m2 - m1)
    w1 = 1.0 / (1.0 + e2)
    comb_ref[0] = jnp.where(lane == i1, w1, 0.0) + jnp.where(lane == i2, e2 * w1, 0.0)
    sel = (lane == i1) | (lane == i2)
    tri = jnp.where(lax.broadcasted_iota(jnp.int32, (win, win), 0)
                    > lax.broadcasted_iota(jnp.int32, (win, win), 1), 1.0, 0.0).astype(BF16)
    base = jnp.zeros((1, LANES), F32)
    bases = []
    for w in range(tm // win):
        rows = slice(w * win, (w + 1) * win)
        sw = jnp.where(sel[rows], 1.0, 0.0)
        excl = jnp.dot(tri, sw.astype(BF16), preferred_element_type=F32)
        rank_ref[0, rows, :] = jnp.where(sel[rows], excl + base, -1.0)
        bases.append(base)
        base = base + jnp.sum(sw, axis=0, keepdims=True)
    bases.append(base)
    bases += [jnp.zeros((1, LANES), F32)] * (CNT_ROWS - len(bases))
    cnt_ref[0] = jnp.concatenate(bases, axis=0).astype(jnp.int32)
    rankt_ref[0] = rank_ref[0].T[:CNT_ROWS]


def _route_call(h, w_router, tm):
    nb, n, d = h.shape
    n_experts = w_router.shape[1]
    assert n_experts <= CNT_ROWS and tm // MOE_WIN + 1 <= CNT_ROWS
    nt = n // tm
    wr = jnp.zeros((d, LANES), BF16).at[:, :n_experts].set(w_router.astype(BF16))
    tok = pl.BlockSpec((1, tm, LANES), lambda b, i: (b, i, 0))
    per_tile = lambda r, c: pl.BlockSpec((1, r, c), lambda b, i: (b * nt + i, 0, 0))
    return pl.pallas_call(
        functools.partial(_route_kernel, n_experts=n_experts, win=min(MOE_WIN, tm)),
        grid=(nb, nt),
        in_specs=[pl.BlockSpec((1, tm, d), lambda b, i: (b, i, 0)),
                  pl.BlockSpec((d, LANES), lambda b, i: (0, 0))],
        out_specs=[tok, tok, per_tile(CNT_ROWS, tm), per_tile(CNT_ROWS, LANES)],
        out_shape=[jax.ShapeDtypeStruct((nb, n, LANES), F32), jax.ShapeDtypeStruct((nb, n, LANES), F32),
                   jax.ShapeDtypeStruct((nb * nt, CNT_ROWS, tm), F32),
                   jax.ShapeDtypeStruct((nb * nt, CNT_ROWS, LANES), jnp.int32)],
        compiler_params=_cparams(("parallel", "parallel")),
        name="moe_route",
    )(h, wr)


def _moe_kernel(cnt_ref, x_ref, h_ref, comb_ref, rank_ref, rankt_ref, wg_ref, wu_ref, wd_ref, g2_ref,
                ng_ref, sh_ref, sc_ref, *rest, final, n_experts, win, slab, rblk):
    out_refs, (hc_ref, yc_ref) = rest[:-2], rest[-2:]
    acc_ref = out_refs[0].at[0]
    b, i, e, f = (pl.program_id(k) for k in range(4))
    tm, d = h_ref.shape[1], h_ref.shape[2]
    nw = tm // win
    tile = b * pl.num_programs(1) + i
    cbase = tile * (CNT_ROWS * n_experts)

    def count(w):
        return cnt_ref[cbase + w * n_experts + e]

    def slabs(w):
        row0 = (count(w) // BF16_SUBLANES) * BF16_SUBLANES
        return row0, (count(w + 1) - row0 + slab - 1) // slab

    @pl.when((e == 0) & (f == 0))
    def _():
        acc_ref[...] = jnp.zeros_like(acc_ref)

    total = count(nw)

    @pl.when(f == 0)
    def _():
        def zero(s, carry):
            rows = pl.ds(pl.multiple_of(s * slab, slab), slab)
            hc_ref[rows, :] = jnp.zeros((slab, d), BF16)
            yc_ref[rows, :] = jnp.zeros((slab, d), F32)
            return carry

        lax.fori_loop(0, jnp.minimum((total + rblk + 2 * slab) // slab, hc_ref.shape[0] // slab),
                      zero, 0)
        def gather(w):
            row0, nsl = slabs(w)
            rt = rankt_ref[0, 0, :, w * win:(w + 1) * win]
            hw = h_ref[0, w * win:(w + 1) * win, :]

            def body(s, carry):
                r0 = pl.multiple_of(row0 + s * slab, BF16_SUBLANES)
                rid = (r0 + lax.broadcasted_iota(jnp.int32, (slab, win), 0)).astype(F32)
                onehot = jnp.where(rt == rid, 1.0, 0.0).astype(BF16)
                rows = pl.ds(r0, slab)
                got = jnp.dot(onehot, hw, preferred_element_type=F32)
                hc_ref[rows, :] = (hc_ref[rows, :].astype(F32) + got).astype(BF16)
                return carry

            return body, nsl

        for w in range(nw):
            gather(w)[0](0, 0)
        for w in range(nw):
            body, nsl = gather(w)
            lax.fori_loop(1, nsl, body, 0)

    def ffn(r0, size):
        rows = pl.ds(pl.multiple_of(r0, rblk // 4), size)
        hb = hc_ref[rows, :]
        a = jnp.dot(hb, wg_ref[0], preferred_element_type=F32)
        u = jnp.dot(hb, wu_ref[0], preferred_element_type=F32)
        t = (_silu(a) * u).astype(BF16)
        yc_ref[rows, :] += jnp.dot(t, wd_ref[0], preferred_element_type=F32)

    quarters = (total + rblk // 4 - 1) // (rblk // 4)
    nfull = (quarters + 1) // 4
    rest = jnp.maximum(quarters - 4 * nfull, 0)

    def full(r, carry):
        ffn(r * rblk, rblk)
        return carry

    lax.fori_loop(0, nfull, full, 0)

    @pl.when(rest == 2)
    def _():
        ffn(nfull * rblk, rblk // 2)

    @pl.when(rest == 1)
    def _():
        ffn(nfull * rblk, rblk // 4)

    @pl.when(f == pl.num_programs(3) - 1)
    def _():
        def spread(w):
            row0, nsl = slabs(w)
            trows = slice(w * win, (w + 1) * win)
            mine = lax.broadcasted_iota(jnp.int32, (win, LANES), 1) == e
            rcol = jnp.sum(jnp.where(mine, rank_ref[0, trows, :], 0.0), axis=-1, keepdims=True)
            wcol = jnp.sum(jnp.where(mine, comb_ref[0, trows, :], 0.0), axis=-1, keepdims=True)

            def body(s, carry):
                r0 = pl.multiple_of(row0 + s * slab, BF16_SUBLANES)
                cid = (r0 + lax.broadcasted_iota(jnp.int32, (win, slab), 1)).astype(F32)
                onehot = jnp.where(rcol == cid, 1.0, 0.0).astype(BF16)
                yb = yc_ref[pl.ds(r0, slab), :].astype(BF16)
                acc_ref[trows, :] += wcol * jnp.dot(onehot, yb, preferred_element_type=F32)
                return carry

            return body, nsl

        bodies = [spread(w) for w in range(nw)]
        for body, _ in bodies:
            body(0, 0)
        for body, nsl in bodies:
            lax.fori_loop(1, nsl, body, 0)

    @pl.when((e == pl.num_programs(2) - 1) & (f == pl.num_programs(3) - 1))
    def _():
        xn = x_ref[0] + g2_ref[0] * acc_ref[...]
        _finish(xn, final, ng_ref, sh_ref, sc_ref, out_refs)


def _moe_call(x, h, w_router, wg, wu, wd, g2, ng, sh, sc, *, final):
    nb, n, d = x.shape
    n_experts, _, dff = wg.shape
    tm = min(MOE_TILE, n)
    win = min(MOE_WIN, tm)
    nt = n // tm
    comb, rank, rankt, cnt = _route_call(h, w_router, tm)
    cnt = cnt[:, :, :n_experts].reshape(-1)
    rankt = rankt.reshape(nb * nt, CNT_ROWS, 1, tm)
    cap_ffn = -(-tm // MOE_ROWS) * MOE_ROWS
    cap = -(-max(cap_ffn, tm + MOE_SLAB) // MOE_SLAB) * MOE_SLAB

    idx = lambda b, i, e, f, c: (b, i, 0)
    vec = pl.BlockSpec((1, 1, d), lambda b, i, e, f, c: (b, 0, 0))
    out_specs, out_shape = _epilogue_specs(nb, n, d, tm, final, idx)
    grid_spec = pltpu.PrefetchScalarGridSpec(
        num_scalar_prefetch=1,
        grid=(nb, nt, n_experts, dff // MOE_FF),
        in_specs=[_resident((1, tm, d), idx), _resident((1, tm, d), idx),
                  _resident((1, tm, LANES), idx), _resident((1, tm, LANES), idx),
                  pl.BlockSpec((1, 1, 1, tm), lambda b, i, e, f, c: (b * nt + i, e, 0, 0)),
                  pl.BlockSpec((1, d, MOE_FF), lambda b, i, e, f, c: (e, 0, f)),
                  pl.BlockSpec((1, d, MOE_FF), lambda b, i, e, f, c: (e, 0, f)),
                  pl.BlockSpec((1, MOE_FF, d), lambda b, i, e, f, c: (e, f, 0)),
                  vec, pl.BlockSpec((1, d), lambda b, i, e, f, c: (0, 0)), vec, vec],
        out_specs=out_specs,
        scratch_shapes=[pltpu.VMEM((cap, d), BF16), pltpu.VMEM((cap, d), F32)],
    )
    return pl.pallas_call(
        functools.partial(_moe_kernel, final=final, n_experts=n_experts, win=win, slab=MOE_SLAB,
                          rblk=MOE_ROWS),
        grid_spec=grid_spec,
        out_shape=out_shape,
        compiler_params=_cparams(("parallel", "parallel", "arbitrary", "arbitrary")),
        name="moe_swiglu",
    )(cnt, x, h, comb, rank, rankt, wg, wu, wd, g2, ng.reshape(1, d), sh, sc)


def kernel(x, c, ctx, c_ctx, w_ada, b_ada, norm1_g, norm2_g, w_in, b_gate, a_ln_g, a_ln_b, a_ws,
           a_bs, b_conv, c_lambda, c_subln_g, w_a_out, w_b_out, w_c_out, w_o, ff_w_gate, ff_w_up,
           ff_w_down, moe_w_router, moe_w_gate, moe_w_up, moe_w_down, final_norm_g):
    bsz, n, d = x.shape
    nc = ctx.shape[1]
    depth = w_ada.shape[0]
    cols = _Cols(d)
    head_cb = COL_BLOCK // LANES
    gate_cb = cols.out(cols.gate) * COL_BLOCK // d
    mix_pos = tuple(cols.out(j) for j in (cols.au, cols.av, cols.bb, cols.bc, cols.bh))
    cq_l, ck_l, cv_l = (cols.out(j) * head_cb for j in (cols.cq, cols.ck, cols.cv))

    pad = (-(bsz + 1)) % 8
    cc = jnp.concatenate([c, c_ctx[None], jnp.zeros((pad, d), F32)], axis=0)
    mod = _ada_call(cc, w_ada, b_ada)

    def mods(l):
        lat = [mod[l, :bsz, k * d:(k + 1) * d].reshape(bsz, 1, d) for k in range(N_MOD)]
        con = [mod[l, bsz:bsz + 1, k * d:(k + 1) * d].reshape(1, 1, d) for k in range(N_MOD)]
        return lat, con

    tables = _rope_tables(n)
    w_in_b = w_in.astype(BF16)
    xl = x
    xc = ctx.reshape(1, bsz * nc, d)
    lat, con = mods(0)
    h = _mod_call(xl, norm1_g[0], lat[0], lat[1])
    hc = _mod_call(xc, norm1_g[0], con[0], con[1])

    for l in range(depth):
        last = l == depth - 1
        lam_init = 0.8 - 0.6 * math.exp(-0.3 * l)
        lat, con = mods(l)
        if not last:
            nlat, ncon = mods(l + 1)
            nxt_l = (norm1_g[l + 1], nlat[0], nlat[1])
            nxt_c = (norm1_g[l + 1], ncon[0], ncon[1])
        else:
            nxt_l = (final_norm_g, lat[0], lat[1])
            nxt_c = None

        p = _in_call(h, w_in_b[l], b_gate[l], tables, j0=0, nj=cols.end, rope=True)
        if last:
            pc = _in_call(hc, w_in_b[l], b_gate[l], tables, j0=cols.ck, nj=cols.gate - cols.ck,
                          rope=False)
            ck_c, cv_c = 0, (cols.cv - cols.ck) * head_cb
        else:
            pc = _in_call(hc, w_in_b[l], b_gate[l], tables, j0=0, nj=cols.end, rope=False)
            ck_c, cv_c = ck_l, cv_l
        pc_seq = pc.reshape(bsz, nc, -1)

        wa, wb, wc, wo = (w.astype(BF16) for w in (w_a_out[l], w_b_out[l], w_c_out[l], w_o[l]))

        def channel(xs, hs, g2, nxt, final):
            i = l // 2
            if l % 2 == 0:
                return _ffn_call(xs, hs, ff_w_gate[i].astype(BF16), ff_w_up[i].astype(BF16),
                                 ff_w_down[i].astype(BF16), g2, *nxt, final=final)
            return _moe_call(xs, hs, moe_w_router[i], moe_w_gate[i].astype(BF16),
                             moe_w_up[i].astype(BF16), moe_w_down[i].astype(BF16), g2, *nxt,
                             final=final)

        ya, yb = _local_call(p, mix_pos, a_ln_g[l], a_ln_b[l], a_ws[l], a_bs[l], b_conv[l], seq=n)
        yc = _attn_call(p, cq_l, [(p, ck_l, cv_l), (pc_seq, ck_c, cv_c)],
                        c_lambda[l], c_subln_g[l], lam_init=lam_init)
        x1, h2 = _merge_call(xl, ya, yb, yc, p, gate_cb, wa, wb, wc, wo, lat[2], norm2_g[l],
                             lat[3], lat[4])
        res = channel(x1, h2, lat[5], nxt_l, last)
        if last:
            return res[0]
        xl, h = res

        ya_c, yb_c = _local_call(pc_seq, mix_pos, a_ln_g[l], a_ln_b[l], a_ws[l], a_bs[l],
                                 b_conv[l], seq=nc)
        yc_c = _attn_call(pc_seq, cq_l, [(pc_seq, ck_c, cv_c)], c_lambda[l], c_subln_g[l],
                          lam_init=lam_init)
        flat = lambda a: a.reshape(1, bsz * nc, -1)
        xc1, hc2 = _merge_call(xc, flat(ya_c), flat(yb_c), flat(yc_c), pc, gate_cb, wa, wb, wc, wo,
                               con[2], norm2_g[l], con[3], con[4])
        xc, hc = channel(xc1, hc2, con[5], nxt_c, False)
```

```python
import functools
import math

import jax
import jax.numpy as jnp
from jax import lax
from jax.experimental import pallas as pl
from jax.experimental.pallas import tpu as pltpu

F32 = jnp.float32
BF16 = jnp.bfloat16

EPS = 1e-6
GRID_W = 64
N_MOD = 6
N_BRANCH = 3
CHUNK = 128
A_GROUPS = 8
A_GROUP_DIM = 64
A_WIDTH = A_GROUPS * A_GROUP_DIM
B_WIDTH = 512
CONV_W = 3
C_HEADS = 8
C_HEAD_DIM = 64
C_V_DIM = 2 * C_HEAD_DIM
ROPE_BASE = 10000.0
TOP_K = 2
LOG2E = 1.4426950408889634

LANES = 128
BF16_SUBLANES = 16
V7X_VMEM_LIMIT = 56 * 1024 * 1024

COL_BLOCK = 512


def _cparams(sem):
    return pltpu.CompilerParams(dimension_semantics=sem, vmem_limit_bytes=V7X_VMEM_LIMIT)


def _resident(shape, index_map):
    return pl.BlockSpec(shape, index_map, pipeline_mode=pl.Buffered(1))


def _sigmoid(v):
    return 0.5 * (1.0 + jnp.tanh(0.5 * v))


def _silu(v):
    return v * _sigmoid(v)


def _modulate(x, g, sh, sc):
    ms = jnp.mean(x * x, axis=-1, keepdims=True)
    y = x * lax.rsqrt(ms + EPS)
    return (y * g) * (1.0 + sc) + sh


def _rmsnorm(x, g):
    ms = jnp.mean(x * x, axis=-1, keepdims=True)
    return x * lax.rsqrt(ms + EPS) * g


def _ada_kernel(c_ref, w_ref, b_ref, o_ref):
    s = _silu(c_ref[...])
    o_ref[0] = jnp.dot(s, w_ref[0], preferred_element_type=F32,
                       precision=lax.Precision.HIGHEST) + b_ref[0]


def _ada_call(cc, w_ada, b_ada):
    depth, d, cols = w_ada.shape
    rows = cc.shape[0]
    tn = 1536
    return pl.pallas_call(
        _ada_kernel,
        grid=(depth, cols // tn),
        in_specs=[
            pl.BlockSpec((rows, d), lambda l, j: (0, 0)),
            pl.BlockSpec((1, d, tn), lambda l, j: (l, 0, j)),
            pl.BlockSpec((1, 1, tn), lambda l, j: (l, 0, j)),
        ],
        out_specs=pl.BlockSpec((1, rows, tn), lambda l, j: (l, 0, j)),
        out_shape=jax.ShapeDtypeStruct((depth, rows, cols), F32),
        compiler_params=_cparams(("parallel", "parallel")),
        name="ada_proj",
    )(cc, w_ada, b_ada.reshape(depth, 1, cols))


def _mod_kernel(x_ref, g_ref, sh_ref, sc_ref, o_ref):
    o_ref[0] = _modulate(x_ref[0], g_ref[...], sh_ref[0], sc_ref[0]).astype(BF16)


def _mod_call(x, g, sh, sc):
    nb, n, d = x.shape
    tm = min(1024, n)
    vec = pl.BlockSpec((1, 1, d), lambda b, i: (b, 0, 0))
    return pl.pallas_call(
        _mod_kernel,
        grid=(nb, n // tm),
        in_specs=[pl.BlockSpec((1, tm, d), lambda b, i: (b, i, 0)),
                  pl.BlockSpec((1, d), lambda b, i: (0, 0)), vec, vec],
        out_specs=pl.BlockSpec((1, tm, d), lambda b, i: (b, i, 0)),
        out_shape=jax.ShapeDtypeStruct((nb, n, d), BF16),
        compiler_params=_cparams(("parallel", "parallel")),
        name="modulate",
    )(x, g.reshape(1, d), sh, sc)


class _Cols:
    def __init__(self, d):
        self.au = 0
        self.av = self.au + A_WIDTH // COL_BLOCK
        self.bb = self.av + A_WIDTH // COL_BLOCK
        self.bc = self.bb + B_WIDTH // COL_BLOCK
        self.bh = self.bc + B_WIDTH // COL_BLOCK
        self.cq = self.bh + B_WIDTH // COL_BLOCK
        qk = C_HEADS * 2 * C_HEAD_DIM // COL_BLOCK
        self.ck = self.cq + qk
        self.cv = self.ck + qk
        self.gate = self.cv + C_HEADS * C_V_DIM // COL_BLOCK
        self.end = self.gate + N_BRANCH * d // COL_BLOCK

    def out(self, jw):
        return (jw - self.gate) % self.end


def _in_kernel(h_ref, w_ref, bg_ref, cos_ref, sa_ref, sb_ref, o_ref, *, cols, j0, rope, qscale,
               sub):
    j = pl.program_id(2) + j0
    tm = h_ref.shape[1]

    def run(epilogue):
        def mm(r):
            return jnp.dot(h_ref[0, r * sub:(r + 1) * sub, :], w_ref[...],
                           preferred_element_type=F32)
        nxt = mm(0)
        for r in range(tm // sub):
            acc = nxt
            if r + 1 < tm // sub:
                nxt = mm(r + 1)
            epilogue(acc, slice(r * sub, (r + 1) * sub))

    @pl.when(j < cols.bb)
    def _():
        def gelu(acc, rows):
            o_ref[0, rows, :] = jax.nn.gelu(acc).astype(BF16)
        run(gelu)

    @pl.when(((j >= cols.bb) & (j < cols.cq)) | ((j >= cols.cv) & (j < cols.gate)))
    def _():
        def plain(acc, rows):
            o_ref[0, rows, :] = acc.astype(BF16)
        run(plain)

    @pl.when((j >= cols.cq) & (j < cols.cv))
    def _():
        scale = jnp.where(j < cols.ck, qscale, 1.0).astype(F32)

        def rotary(acc, rows):
            if not rope:
                o_ref[0, rows, :] = (acc * scale).astype(BF16)
                return
            cos, sa, sb = cos_ref[rows, :] * scale, sa_ref[rows, :] * scale, sb_ref[rows, :] * scale
            for s in range(COL_BLOCK // LANES):
                t = acc[:, s * LANES:(s + 1) * LANES]
                r = (t * cos + pltpu.roll(t, LANES - 16, 1) * sa + pltpu.roll(t, 16, 1) * sb)
                o_ref[0, rows, s * LANES:(s + 1) * LANES] = r.astype(BF16)
        run(rotary)

    @pl.when(j >= cols.gate)
    def _():
        def gate(acc, rows):
            o_ref[0, rows, :] = _sigmoid(acc + bg_ref[...]).astype(BF16)
        run(gate)


def _in_call(h, w, b_gate, tables, *, j0, nj, rope):
    nb, n, d = h.shape
    cols = _Cols(d)
    tm = min(4096, n)
    full = (j0 == 0) and (nj == cols.end)
    if rope:
        cos, sa, sb = tables
    else:
        cos = sa = sb = jnp.zeros((tm, LANES), F32)
    tab = pl.BlockSpec((tm, LANES), lambda b, i, j: (i if rope else 0, 0))

    def out_idx(b, i, j):
        if full:
            return (b, i, jnp.where(j >= cols.gate, j - cols.gate, j + cols.end - cols.gate))
        return (b, i, j)
    kern = functools.partial(_in_kernel, cols=cols, j0=j0, rope=rope,
                             qscale=C_HEAD_DIM ** -0.5 * LOG2E, sub=min(512, tm))
    return pl.pallas_call(
        kern,
        grid=(nb, n // tm, nj),
        in_specs=[
            pl.BlockSpec((1, tm, d), lambda b, i, j: (b, i, 0)),
            pl.BlockSpec((d, COL_BLOCK), lambda b, i, j: (0, j + j0)),
            pl.BlockSpec((1, COL_BLOCK), lambda b, i, j: (0, jnp.maximum(j + j0 - cols.gate, 0))),
            tab, tab, tab,
        ],
        out_specs=pl.BlockSpec((1, tm, COL_BLOCK), out_idx),
        out_shape=jax.ShapeDtypeStruct((nb, n, nj * COL_BLOCK), BF16),
        compiler_params=_cparams(("parallel", "parallel", "arbitrary")),
        name="in_proj",
    )(h, w, b_gate.reshape(1, -1), cos, sa, sb)


def _rope_tables(n):
    rows = n // GRID_W
    r = jnp.repeat(jnp.arange(rows, dtype=F32), GRID_W)
    col = jnp.tile(jnp.arange(GRID_W, dtype=F32), rows)
    quarter = C_HEAD_DIM // 4
    inv = ROPE_BASE ** (-jnp.arange(quarter, dtype=F32) / quarter)
    ar = r[:, None] * inv
    ac = col[:, None] * inv
    ang = jnp.concatenate([ar, ar, ac, ac], axis=-1)
    ang = jnp.tile(ang, (1, LANES // C_HEAD_DIM))
    cos, sin = jnp.cos(ang), jnp.sin(ang)
    first_of_pair = (jnp.arange(LANES) // quarter) % 2 == 0
    sa = jnp.where(first_of_pair, -sin, 0.0)
    sb = jnp.where(first_of_pair, 0.0, sin)
    return cos, sa, sb


def _local_mixers(u_ref, v_ref, bg_ref, cg_ref, hh_ref, cgp_ref, hhp_ref, cgn_ref, hhn_ref,
                  lng_ref, lnb_ref, ws_ref, bias_ref, cw_ref, ya_ref, yb_ref, *, tm, seq):
    i = pl.program_id(1)
    lane = lax.broadcasted_iota(jnp.int32, (CHUNK, LANES), 1)
    lo = lane < A_GROUP_DIM
    for c in range(tm // CHUNK):
        rows = pl.ds(c * CHUNK, CHUNK)
        v = v_ref[0, rows, :].astype(F32)
        mu = jnp.mean(v, axis=-1, keepdims=True)
        var = jnp.mean(jnp.square(v - mu), axis=-1, keepdims=True)
        vn = ((v - mu) * lax.rsqrt(var + EPS) * lng_ref[...] + lnb_ref[...]).astype(BF16)
        for k in range(A_WIDTH // LANES):
            blk = vn[:, k * LANES:(k + 1) * LANES]
            zero = jnp.zeros_like(blk)
            mixed = (jnp.dot(ws_ref[2 * k], jnp.where(lo, blk, zero), preferred_element_type=F32)
                     + jnp.dot(ws_ref[2 * k + 1], jnp.where(lo, zero, blk), preferred_element_type=F32)
                     + bias_ref[:, k * LANES:(k + 1) * LANES])
            u = u_ref[0, rows, k * LANES:(k + 1) * LANES].astype(F32)
            ya_ref[rows, k * LANES:(k + 1) * LANES] = (u * mixed).astype(BF16)

    z = cg_ref[0].astype(F32) * hh_ref[0].astype(F32)
    last = BF16_SUBLANES - 1
    z_prev = cgp_ref[0, last:last + 1, :].astype(F32) * hhp_ref[0, last:last + 1, :].astype(F32)
    z_next = cgn_ref[0, 0:1, :].astype(F32) * hhn_ref[0, 0:1, :].astype(F32)
    row = lax.broadcasted_iota(jnp.int32, (tm, 1), 0)
    pos = lax.rem(i * tm + row, seq)
    zm1 = jnp.where(row == 0, z_prev, pltpu.roll(z, 1, 0))
    zm1 = jnp.where(pos == 0, 0.0, zm1)
    zp1 = jnp.where(row == tm - 1, z_next, pltpu.roll(z, tm - 1, 0))
    zp1 = jnp.where(pos == seq - 1, 0.0, zp1)
    conv = cw_ref[0:1, :] * zm1 + cw_ref[1:2, :] * z + cw_ref[2:3, :] * zp1
    yb_ref[...] = (bg_ref[0].astype(F32) * conv).astype(BF16)


def _attn_kernel(*refs, tq, ck, sizes, lam_init):
    nsrc = len(sizes)
    cl_ref, g_ref, q_ref = refs[:3]
    kv_refs = refs[3:3 + 2 * nsrc]
    o_ref = refs[3 + 2 * nsrc]
    vt_refs = refs[4 + 2 * nsrc:4 + 3 * nsrc]
    kmax_ref = refs[4 + 3 * nsrc]
    half = lax.broadcasted_iota(jnp.int32, (1, LANES), 1) < C_HEAD_DIM

    @pl.when(pl.program_id(2) == 0)
    def _():
        per_map = jnp.where(lax.broadcasted_iota(jnp.int32, (LANES, LANES), 0) // C_HEAD_DIM
                            == lax.broadcasted_iota(jnp.int32, (LANES, LANES), 1), 1.0, 0.0)
        kmax2 = jnp.zeros((1, LANES), F32)
        for src in range(nsrc):
            vt_refs[src][...] = kv_refs[2 * src + 1][0].astype(F32).T.astype(BF16)
            k = kv_refs[2 * src][0]
            n2 = jnp.dot(k * k, per_map.astype(BF16), preferred_element_type=F32)
            kmax2 = jnp.maximum(kmax2, jnp.max(n2, axis=0, keepdims=True))
        kmax_ref[0:1, :] = jnp.sqrt(kmax2) * (1.0 + 2.0 ** -7)

    q = q_ref[0]
    zero = jnp.zeros_like(q)
    qs = [jnp.where(half, q, zero), jnp.where(half, zero, q)]
    chunks = [(kv_refs[2 * src], vt_refs[src], c0, min(ck, nk))
              for src, nk in enumerate(sizes) for c0 in range(0, nk, min(ck, nk))]

    def scores(chunk, qm):
        k_ref, _, c0, c = chunk
        return lax.dot_general(k_ref[0, c0:c0 + c, :], qm, (((1,), (1,)), ((), ())),
                               preferred_element_type=F32)

    def finish(accs, ls):
        cl = cl_ref[...]
        lam = (jnp.exp(jnp.sum(cl[0:1] * cl[1:2], axis=-1, keepdims=True))
               - jnp.exp(jnp.sum(cl[2:3] * cl[3:4], axis=-1, keepdims=True)) + lam_init)
        o = accs[0] / ls[0] - lam * (accs[1] / ls[1])
        ms = jnp.mean(o * o, axis=0, keepdims=True)
        y = o * lax.rsqrt(ms + EPS) * (g_ref[...] * (1.0 - lam_init))
        o_ref[0] = y.T.astype(BF16)

    qsq = jnp.square(q.astype(F32).T)
    refs_r = [jnp.sqrt(jnp.sum(qsq[j * C_HEAD_DIM:(j + 1) * C_HEAD_DIM], axis=0, keepdims=True))
              * kmax_ref[0:1, j:j + 1] for j in range(2)]
    ls = [jnp.zeros((1, tq), F32) for _ in qs]
    accs = [jnp.zeros((C_V_DIM, tq), F32) for _ in qs]
    s_next = [scores(chunks[0], qm) for qm in qs]
    for t, (_, vt_ref, c0, c) in enumerate(chunks):
        s_cur = s_next
        if t + 1 < len(chunks):
            s_next = [scores(chunks[t + 1], qm) for qm in qs]
        vt = vt_ref[:, c0:c0 + c]
        for j, s in enumerate(s_cur):
            p = jnp.exp2(s - refs_r[j])
            ls[j] = ls[j] + jnp.sum(p, axis=0, keepdims=True)
            accs[j] = accs[j] + jnp.dot(vt, p.astype(BF16), preferred_element_type=F32)
    finish(accs, ls)

    healthy = jnp.min(jnp.minimum(ls[0], ls[1])) >= 2.0 ** -60

    @pl.when(jnp.logical_not(healthy))
    def _():
        ms = [jnp.full((1, tq), -jnp.inf, F32) for _ in qs]
        ls = [jnp.zeros((1, tq), F32) for _ in qs]
        accs = [jnp.zeros((C_V_DIM, tq), F32) for _ in qs]
        for chunk in chunks:
            _, vt_ref, c0, c = chunk
            vt = vt_ref[:, c0:c0 + c]
            for j, qm in enumerate(qs):
                s = scores(chunk, qm)
                m_new = jnp.maximum(ms[j], jnp.max(s, axis=0, keepdims=True))
                alpha = jnp.exp2(ms[j] - m_new)
                p = jnp.exp2(s - m_new)
                ls[j] = alpha * ls[j] + jnp.sum(p, axis=0, keepdims=True)
                accs[j] = alpha * accs[j] + jnp.dot(vt, p.astype(BF16),
                                                    preferred_element_type=F32)
                ms[j] = m_new
        finish(accs, ls)


def _attn_call(q_arr, q_cb, sources, c_lambda, subln_g, *, lam_init):
    nb, n, _ = q_arr.shape
    tq = min(512, n)
    sizes = tuple(a.shape[1] for a, _, _ in sources)
    in_specs = [
        pl.BlockSpec((4, C_HEAD_DIM), lambda b, h, i: (0, 0)),
        pl.BlockSpec((C_V_DIM, 1), lambda b, h, i: (0, 0)),
        pl.BlockSpec((1, tq, LANES), lambda b, h, i: (b, i, q_cb + h)),
    ]
    args = [c_lambda, subln_g.reshape(-1, 1), q_arr]
    for arr, kcb, vcb in sources:
        nk = arr.shape[1]
        in_specs.append(pl.BlockSpec((1, nk, LANES), lambda b, h, i, kcb=kcb: (b, 0, kcb + h)))
        in_specs.append(pl.BlockSpec((1, nk, LANES), lambda b, h, i, vcb=vcb: (b, 0, vcb + h)))
        args += [arr, arr]
    kern = functools.partial(_attn_kernel, tq=tq, ck=2048, sizes=sizes, lam_init=lam_init)
    return pl.pallas_call(
        kern,
        grid=(nb, C_HEADS, n // tq),
        in_specs=in_specs,
        out_specs=pl.BlockSpec((1, tq, C_V_DIM), lambda b, h, i: (b, i, h)),
        out_shape=jax.ShapeDtypeStruct((nb, n, C_HEADS * C_V_DIM), BF16),
        scratch_shapes=[pltpu.VMEM((C_V_DIM, nk), BF16) for nk in sizes]
        + [pltpu.VMEM((8, LANES), F32)],
        compiler_params=_cparams(("parallel", "parallel", "arbitrary")),
        name="diff_attention",
    )(*args)


def _merge_kernel(*refs, tm, seq):
    mixer_refs = refs[:14]
    (x_ref, yc_ref, ga_ref, gb_ref, gc_ref, wa_ref, wb_ref, wc_ref, wo_ref, g1_ref, n2_ref,
     sh2_ref, sc2_ref, xo_ref, ho_ref, ya_ref, yb_ref) = refs[14:]
    c = jnp.dot(yc_ref[0], wc_ref[...], preferred_element_type=F32)
    _local_mixers(*mixer_refs, ya_ref, yb_ref, tm=tm, seq=seq)
    a = jnp.dot(ya_ref[...], wa_ref[...], preferred_element_type=F32)
    b = jnp.dot(yb_ref[...], wb_ref[...], preferred_element_type=F32)
    y = (ga_ref[0].astype(F32) * a + gb_ref[0].astype(F32) * b + gc_ref[0].astype(F32) * c)
    m = jnp.dot(y.astype(BF16), wo_ref[...], preferred_element_type=F32)
    xn = x_ref[0] + g1_ref[0] * m
    xo_ref[0] = xn
    ho_ref[0] = _modulate(xn, n2_ref[...], sh2_ref[0], sc2_ref[0]).astype(BF16)


def _merge_call(x, p, pos, yc, gate_cb, mixer_w, wa, wb, wc, wo, g1, n2g, sh2, sc2, *, seq):
    nb, n, d = x.shape
    au, av, bb, bc, bh = pos
    ln_g, ln_b, ws, bs, conv_w = mixer_w
    tm = min(512, n)
    hb = tm // BF16_SUBLANES
    nhb = n // BF16_SUBLANES

    def tok(w, cb=0):
        return pl.BlockSpec((1, tm, w), lambda b, i: (b, i, cb))

    def prev(cb):
        return pl.BlockSpec((1, BF16_SUBLANES, COL_BLOCK),
                            lambda b, i: (b, jnp.maximum(i * hb - 1, 0), cb))

    def nxt(cb):
        return pl.BlockSpec((1, BF16_SUBLANES, COL_BLOCK),
                            lambda b, i: (b, jnp.minimum((i + 1) * hb, nhb - 1), cb))

    const = lambda shape: pl.BlockSpec(shape, lambda b, i: (0,) * len(shape))
    bias = jnp.repeat(bs.T, A_GROUP_DIM, axis=1)
    vec = pl.BlockSpec((1, 1, d), lambda b, i: (b, 0, 0))
    return pl.pallas_call(
        functools.partial(_merge_kernel, tm=tm, seq=seq),
        grid=(nb, n // tm),
        in_specs=[tok(COL_BLOCK, au), tok(COL_BLOCK, av), tok(COL_BLOCK, bb), tok(COL_BLOCK, bc),
                  tok(COL_BLOCK, bh), prev(bc), prev(bh), nxt(bc), nxt(bh),
                  const((1, A_WIDTH)), const((1, A_WIDTH)), const((A_GROUPS, CHUNK, CHUNK)),
                  const((CHUNK, A_WIDTH)), const((CONV_W, B_WIDTH)),
                  tok(d), tok(d), tok(d, gate_cb), tok(d, gate_cb + 1), tok(d, gate_cb + 2),
                  _resident(wa.shape, lambda b, i: (0, 0)), _resident(wb.shape, lambda b, i: (0, 0)),
                  _resident(wc.shape, lambda b, i: (0, 0)), _resident(wo.shape, lambda b, i: (0, 0)),
                  vec, const((1, d)), vec, vec],
        out_specs=[tok(d), tok(d)],
        out_shape=[jax.ShapeDtypeStruct((nb, n, d), F32), jax.ShapeDtypeStruct((nb, n, d), BF16)],
        scratch_shapes=[pltpu.VMEM((tm, A_WIDTH), BF16), pltpu.VMEM((tm, B_WIDTH), BF16)],
        compiler_params=_cparams(("parallel", "parallel")),
        name="merge_branches",
    )(p, p, p, p, p, p, p, p, p, ln_g.reshape(1, -1), ln_b.reshape(1, -1), ws.astype(BF16), bias,
      conv_w, x, yc, p, p, p, wa, wb, wc, wo, g1, n2g.reshape(1, d), sh2, sc2)


def _ff_chunks(width, chunk=512):
    out, c0 = [], 0
    while c0 < width:
        cw = min(chunk, width - c0)
        out.append((c0, cw))
        c0 += cw
    return out


def _finish(xn, final, ng_ref, sh_ref, sc_ref, out_refs):
    if final:
        out_refs[0][0] = _rmsnorm(xn, ng_ref[...])
    else:
        out_refs[0][0] = xn
        out_refs[1][0] = _modulate(xn, ng_ref[...], sh_ref[0], sc_ref[0]).astype(BF16)


def _ffn_kernel(x_ref, h_ref, wg_ref, wu_ref, wd_ref, g2_ref, ng_ref, sh_ref, sc_ref, *out_refs,
                final):
    h = h_ref[0]
    acc = None
    for c0, cw in _ff_chunks(wg_ref.shape[1]):
        a = jnp.dot(h, wg_ref[:, c0:c0 + cw], preferred_element_type=F32)
        b = jnp.dot(h, wu_ref[:, c0:c0 + cw], preferred_element_type=F32)
        t = (_silu(a) * b).astype(BF16)
        part = jnp.dot(t, wd_ref[c0:c0 + cw, :], preferred_element_type=F32)
        acc = part if acc is None else acc + part
    xn = x_ref[0] + g2_ref[0] * acc
    _finish(xn, final, ng_ref, sh_ref, sc_ref, out_refs)


def _epilogue_specs(nb, n, d, tm, final, idx):
    tok = pl.BlockSpec((1, tm, d), idx)
    if final:
        return [tok], [jax.ShapeDtypeStruct((nb, n, d), F32)]
    return [tok, tok], [jax.ShapeDtypeStruct((nb, n, d), F32), jax.ShapeDtypeStruct((nb, n, d), BF16)]


def _ffn_call(x, h, wg, wu, wd, g2, ng, sh, sc, *, final):
    nb, n, d = x.shape
    tm = min(512, n)
    idx = lambda b, i: (b, i, 0)
    vec = pl.BlockSpec((1, 1, d), lambda b, i: (b, 0, 0))
    out_specs, out_shape = _epilogue_specs(nb, n, d, tm, final, idx)
    return pl.pallas_call(
        functools.partial(_ffn_kernel, final=final),
        grid=(nb, n // tm),
        in_specs=[pl.BlockSpec((1, tm, d), idx), pl.BlockSpec((1, tm, d), idx),
                  _resident(wg.shape, lambda b, i: (0, 0)), _resident(wu.shape, lambda b, i: (0, 0)),
                  _resident(wd.shape, lambda b, i: (0, 0)),
                  vec, pl.BlockSpec((1, d), lambda b, i: (0, 0)), vec, vec],
        out_specs=out_specs,
        out_shape=out_shape,
        compiler_params=_cparams(("parallel", "parallel")),
        name="dense_swiglu",
    )(x, h, wg, wu, wd, g2, ng.reshape(1, d), sh, sc)


MOE_TILE = 2048
MOE_WIN = 512
MOE_SLAB = 256
MOE_ROWS = 512
MOE_FF = 512
CNT_ROWS = 8


def _route_kernel(h_ref, wr_ref, comb_ref, rank_ref, rankt_ref, cnt_ref, *, n_experts, win):
    h = h_ref[0]
    tm = h.shape[0]
    lane = lax.broadcasted_iota(jnp.int32, (tm, LANES), 1)
    logits = jnp.dot(h, wr_ref[...], preferred_element_type=F32)
    lg = jnp.where(lane < n_experts, logits, -jnp.inf)
    m1 = jnp.max(lg, axis=-1, keepdims=True)
    i1 = jnp.min(jnp.where(lg == m1, lane, LANES), axis=-1, keepdims=True)
    lg2 = jnp.where(lane == i1, -jnp.inf, lg)
    m2 = jnp.max(lg2, axis=-1, keepdims=True)
    i2 = jnp.min(jnp.where(lg2 == m2, lane, LANES), axis=-1, keepdims=True)
    e2 = jnp.exp(m2 - m1)
    w1 = 1.0 / (1.0 + e2)
    comb_ref[0] = jnp.where(lane == i1, w1, 0.0) + jnp.where(lane == i2, e2 * w1, 0.0)
    sel = (lane == i1) | (lane == i2)
    tri = jnp.where(lax.broadcasted_iota(jnp.int32, (win, win), 0)
                    > lax.broadcasted_iota(jnp.int32, (win, win), 1), 1.0, 0.0).astype(BF16)
    base = jnp.zeros((1, LANES), F32)
    bases = []
    for w in range(tm // win):
        rows = slice(w * win, (w + 1) * win)
        sw = jnp.where(sel[rows], 1.0, 0.0)
        excl = jnp.dot(tri, sw.astype(BF16), preferred_element_type=F32)
        rank_ref[0, rows, :] = jnp.where(sel[rows], excl + base, -1.0)
        bases.append(base)
        base = base + jnp.sum(sw, axis=0, keepdims=True)
    bases.append(base)
    bases += [jnp.zeros((1, LANES), F32)] * (CNT_ROWS - len(bases))
    cnt_ref[0] = jnp.concatenate(bases, axis=0).astype(jnp.int32)
    rankt_ref[0] = rank_ref[0].T[:CNT_ROWS]


def _route_call(h, w_router, tm):
    nb, n, d = h.shape
    n_experts = w_router.shape[1]
    assert n_experts <= CNT_ROWS and tm // MOE_WIN + 1 <= CNT_ROWS
    nt = n // tm
    wr = jnp.zeros((d, LANES), BF16).at[:, :n_experts].set(w_router.astype(BF16))
    tok = pl.BlockSpec((1, tm, LANES), lambda b, i: (b, i, 0))
    per_tile = lambda r, c: pl.BlockSpec((1, r, c), lambda b, i: (b * nt + i, 0, 0))
    return pl.pallas_call(
        functools.partial(_route_kernel, n_experts=n_experts, win=min(MOE_WIN, tm)),
        grid=(nb, nt),
        in_specs=[pl.BlockSpec((1, tm, d), lambda b, i: (b, i, 0)),
                  pl.BlockSpec((d, LANES), lambda b, i: (0, 0))],
        out_specs=[tok, tok, per_tile(CNT_ROWS, tm), per_tile(CNT_ROWS, LANES)],
        out_shape=[jax.ShapeDtypeStruct((nb, n, LANES), F32), jax.ShapeDtypeStruct((nb, n, LANES), F32),
                   jax.ShapeDtypeStruct((nb * nt, CNT_ROWS, tm), F32),
                   jax.ShapeDtypeStruct((nb * nt, CNT_ROWS, LANES), jnp.int32)],
        compiler_params=_cparams(("parallel", "parallel")),
        name="moe_route",
    )(h, wr)


def _moe_kernel(cnt_ref, x_ref, h_ref, comb_ref, rank_ref, rankt_ref, wg_ref, wu_ref, wd_ref, g2_ref,
                ng_ref, sh_ref, sc_ref, *rest, final, n_experts, win, slab, rblk):
    out_refs, (hc_ref, yc_ref) = rest[:-2], rest[-2:]
    acc_ref = out_refs[0].at[0]
    b, i, e, f = (pl.program_id(k) for k in range(4))
    tm, d = h_ref.shape[1], h_ref.shape[2]
    nw = tm // win
    tile = b * pl.num_programs(1) + i
    cbase = tile * (CNT_ROWS * n_experts)

    def count(w):
        return cnt_ref[cbase + w * n_experts + e]

    def slabs(w):
        row0 = (count(w) // BF16_SUBLANES) * BF16_SUBLANES
        return row0, (count(w + 1) - row0 + slab - 1) // slab

    @pl.when((e == 0) & (f == 0))
    def _():
        acc_ref[...] = jnp.zeros_like(acc_ref)

    total = count(nw)

    @pl.when(f == 0)
    def _():
        def zero(s, carry):
            rows = pl.ds(pl.multiple_of(s * slab, slab), slab)
            hc_ref[rows, :] = jnp.zeros((slab, d), BF16)
            yc_ref[rows, :] = jnp.zeros((slab, d), F32)
            return carry

        lax.fori_loop(0, jnp.minimum((total + rblk + 2 * slab) // slab, hc_ref.shape[0] // slab),
                      zero, 0)
        def gather(w):
            row0, nsl = slabs(w)
            rt = rankt_ref[0, 0, :, w * win:(w + 1) * win]
            hw = h_ref[0, w * win:(w + 1) * win, :]

            def body(s, carry):
                r0 = pl.multiple_of(row0 + s * slab, BF16_SUBLANES)
                rid = (r0 + lax.broadcasted_iota(jnp.int32, (slab, win), 0)).astype(F32)
                onehot = jnp.where(rt == rid, 1.0, 0.0).astype(BF16)
                rows = pl.ds(r0, slab)
                got = jnp.dot(onehot, hw, preferred_element_type=F32)
                hc_ref[rows, :] = (hc_ref[rows, :].astype(F32) + got).astype(BF16)
                return carry

            return body, nsl

        for w in range(nw):
            gather(w)[0](0, 0)
        for w in range(nw):
            body, nsl = gather(w)
            lax.fori_loop(1, nsl, body, 0)

    def ffn(r0, size):
        rows = pl.ds(pl.multiple_of(r0, rblk // 4), size)
        hb = hc_ref[rows, :]
        a = jnp.dot(hb, wg_ref[0], preferred_element_type=F32)
        u = jnp.dot(hb, wu_ref[0], preferred_element_type=F32)
        t = (_silu(a) * u).astype(BF16)
        yc_ref[rows, :] += jnp.dot(t, wd_ref[0], preferred_element_type=F32)

    quarters = (total + rblk // 4 - 1) // (rblk // 4)
    nfull = (quarters + 1) // 4
    rest = jnp.maximum(quarters - 4 * nfull, 0)

    def full(r, carry):
        ffn(r * rblk, rblk)
        return carry

    lax.fori_loop(0, nfull, full, 0)

    @pl.when(rest == 2)
    def _():
        ffn(nfull * rblk, rblk // 2)

    @pl.when(rest == 1)
    def _():
        ffn(nfull * rblk, rblk // 4)

    @pl.when(f == pl.num_programs(3) - 1)
    def _():
        def spread(w):
            row0, nsl = slabs(w)
            trows = slice(w * win, (w + 1) * win)
            mine = lax.broadcasted_iota(jnp.int32, (win, LANES), 1) == e
            rcol = jnp.sum(jnp.where(mine, rank_ref[0, trows, :], 0.0), axis=-1, keepdims=True)
            wcol = jnp.sum(jnp.where(mine, comb_ref[0, trows, :], 0.0), axis=-1, keepdims=True)

            def body(s, carry):
                r0 = pl.multiple_of(row0 + s * slab, BF16_SUBLANES)
                cid = (r0 + lax.broadcasted_iota(jnp.int32, (win, slab), 1)).astype(F32)
                onehot = jnp.where(rcol == cid, 1.0, 0.0).astype(BF16)
                yb = yc_ref[pl.ds(r0, slab), :].astype(BF16)
                acc_ref[trows, :] += wcol * jnp.dot(onehot, yb, preferred_element_type=F32)
                return carry

            return body, nsl

        bodies = [spread(w) for w in range(nw)]
        for body, _ in bodies:
            body(0, 0)
        for body, nsl in bodies:
            lax.fori_loop(1, nsl, body, 0)

    @pl.when((e == pl.num_programs(2) - 1) & (f == pl.num_programs(3) - 1))
    def _():
        xn = x_ref[0] + g2_ref[0] * acc_ref[...]
        _finish(xn, final, ng_ref, sh_ref, sc_ref, out_refs)


def _moe_call(x, h, w_router, wg, wu, wd, g2, ng, sh, sc, *, final):
    nb, n, d = x.shape
    n_experts, _, dff = wg.shape
    tm = min(MOE_TILE, n)
    win = min(MOE_WIN, tm)
    nt = n // tm
    comb, rank, rankt, cnt = _route_call(h, w_router, tm)
    cnt = cnt[:, :, :n_experts].reshape(-1)
    rankt = rankt.reshape(nb * nt, CNT_ROWS, 1, tm)
    cap_ffn = -(-tm // MOE_ROWS) * MOE_ROWS
    cap = -(-max(cap_ffn, tm + MOE_SLAB) // MOE_SLAB) * MOE_SLAB

    idx = lambda b, i, e, f, c: (b, i, 0)
    vec = pl.BlockSpec((1, 1, d), lambda b, i, e, f, c: (b, 0, 0))
    out_specs, out_shape = _epilogue_specs(nb, n, d, tm, final, idx)
    grid_spec = pltpu.PrefetchScalarGridSpec(
        num_scalar_prefetch=1,
        grid=(nb, nt, n_experts, dff // MOE_FF),
        in_specs=[_resident((1, tm, d), idx), _resident((1, tm, d), idx),
                  _resident((1, tm, LANES), idx), _resident((1, tm, LANES), idx),
                  pl.BlockSpec((1, 1, 1, tm), lambda b, i, e, f, c: (b * nt + i, e, 0, 0)),
                  pl.BlockSpec((1, d, MOE_FF), lambda b, i, e, f, c: (e, 0, f)),
                  pl.BlockSpec((1, d, MOE_FF), lambda b, i, e, f, c: (e, 0, f)),
                  pl.BlockSpec((1, MOE_FF, d), lambda b, i, e, f, c: (e, f, 0)),
                  vec, pl.BlockSpec((1, d), lambda b, i, e, f, c: (0, 0)), vec, vec],
        out_specs=out_specs,
        scratch_shapes=[pltpu.VMEM((cap, d), BF16), pltpu.VMEM((cap, d), F32)],
    )
    return pl.pallas_call(
        functools.partial(_moe_kernel, final=final, n_experts=n_experts, win=win, slab=MOE_SLAB,
                          rblk=MOE_ROWS),
        grid_spec=grid_spec,
        out_shape=out_shape,
        compiler_params=_cparams(("parallel", "parallel", "arbitrary", "arbitrary")),
        name="moe_swiglu",
    )(cnt, x, h, comb, rank, rankt, wg, wu, wd, g2, ng.reshape(1, d), sh, sc)


def kernel(x, c, ctx, c_ctx, w_ada, b_ada, norm1_g, norm2_g, w_in, b_gate, a_ln_g, a_ln_b, a_ws,
           a_bs, b_conv, c_lambda, c_subln_g, w_a_out, w_b_out, w_c_out, w_o, ff_w_gate, ff_w_up,
           ff_w_down, moe_w_router, moe_w_gate, moe_w_up, moe_w_down, final_norm_g):
    bsz, n, d = x.shape
    nc = ctx.shape[1]
    depth = w_ada.shape[0]
    cols = _Cols(d)
    head_cb = COL_BLOCK // LANES
    gate_cb = cols.out(cols.gate) * COL_BLOCK // d
    mix_pos = tuple(cols.out(j) for j in (cols.au, cols.av, cols.bb, cols.bc, cols.bh))
    cq_l, ck_l, cv_l = (cols.out(j) * head_cb for j in (cols.cq, cols.ck, cols.cv))

    pad = (-(bsz + 1)) % 8
    cc = jnp.concatenate([c, c_ctx[None], jnp.zeros((pad, d), F32)], axis=0)
    mod = _ada_call(cc, w_ada, b_ada)

    def mods(l):
        lat = [mod[l, :bsz, k * d:(k + 1) * d].reshape(bsz, 1, d) for k in range(N_MOD)]
        con = [mod[l, bsz:bsz + 1, k * d:(k + 1) * d].reshape(1, 1, d) for k in range(N_MOD)]
        return lat, con

    tables = _rope_tables(n)
    w_in_b = w_in.astype(BF16)
    xl = x
    xc = ctx.reshape(1, bsz * nc, d)
    lat, con = mods(0)
    h = _mod_call(xl, norm1_g[0], lat[0], lat[1])
    hc = _mod_call(xc, norm1_g[0], con[0], con[1])

    for l in range(depth):
        last = l == depth - 1
        lam_init = 0.8 - 0.6 * math.exp(-0.3 * l)
        lat, con = mods(l)
        if not last:
            nlat, ncon = mods(l + 1)
            nxt_l = (norm1_g[l + 1], nlat[0], nlat[1])
            nxt_c = (norm1_g[l + 1], ncon[0], ncon[1])
        else:
            nxt_l = (final_norm_g, lat[0], lat[1])
            nxt_c = None

        p = _in_call(h, w_in_b[l], b_gate[l], tables, j0=0, nj=cols.end, rope=True)
        if last:
            pc = _in_call(hc, w_in_b[l], b_gate[l], tables, j0=cols.ck, nj=cols.gate - cols.ck,
                          rope=False)
            ck_c, cv_c = 0, (cols.cv - cols.ck) * head_cb
        else:
            pc = _in_call(hc, w_in_b[l], b_gate[l], tables, j0=0, nj=cols.end, rope=False)
            ck_c, cv_c = ck_l, cv_l
        pc_seq = pc.reshape(bsz, nc, -1)

        wa, wb, wc, wo = (w.astype(BF16) for w in (w_a_out[l], w_b_out[l], w_c_out[l], w_o[l]))

        def channel(xs, hs, g2, nxt, final):
            i = l // 2
            if l % 2 == 0:
                return _ffn_call(xs, hs, ff_w_gate[i].astype(BF16), ff_w_up[i].astype(BF16),
                                 ff_w_down[i].astype(BF16), g2, *nxt, final=final)
            return _moe_call(xs, hs, moe_w_router[i], moe_w_gate[i].astype(BF16),
                             moe_w_up[i].astype(BF16), moe_w_down[i].astype(BF16), g2, *nxt,
                             final=final)

        mixer_w = (a_ln_g[l], a_ln_b[l], a_ws[l], a_bs[l], b_conv[l])
        yc = _attn_call(p, cq_l, [(p, ck_l, cv_l), (pc_seq, ck_c, cv_c)],
                        c_lambda[l], c_subln_g[l], lam_init=lam_init)
        x1, h2 = _merge_call(xl, p, mix_pos, yc, gate_cb, mixer_w, wa, wb, wc, wo, lat[2],
                             norm2_g[l], lat[3], lat[4], seq=n)
        res = channel(x1, h2, lat[5], nxt_l, last)
        if last:
            return res[0]
        xl, h = res

        yc_c = _attn_call(pc_seq, cq_l, [(pc_seq, ck_c, cv_c)], c_lambda[l], c_subln_g[l],
                          lam_init=lam_init)
        xc1, hc2 = _merge_call(xc, pc, mix_pos, yc_c.reshape(1, bsz * nc, -1), gate_cb, mixer_w,
                               wa, wb, wc, wo, con[2], norm2_g[l], con[3], con[4], seq=nc)
        xc, hc = channel(xc1, hc2, con[5], nxt_c, False)
```

```python
import functools
import math

import jax
import jax.numpy as jnp
from jax import lax
from jax.experimental import pallas as pl
from jax.experimental.pallas import tpu as pltpu

F32 = jnp.float32
BF16 = jnp.bfloat16

EPS = 1e-6
GRID_W = 64
N_MOD = 6
N_BRANCH = 3
CHUNK = 128
A_GROUPS = 8
A_GROUP_DIM = 64
A_WIDTH = A_GROUPS * A_GROUP_DIM
B_WIDTH = 512
CONV_W = 3
C_HEADS = 8
C_HEAD_DIM = 64
C_V_DIM = 2 * C_HEAD_DIM
ROPE_BASE = 10000.0
TOP_K = 2
LOG2E = 1.4426950408889634

LANES = 128
BF16_SUBLANES = 16
V7X_VMEM_LIMIT = 56 * 1024 * 1024

COL_BLOCK = 512


def _cparams(sem):
    return pltpu.CompilerParams(dimension_semantics=sem, vmem_limit_bytes=V7X_VMEM_LIMIT)


def _resident(shape, index_map):
    return pl.BlockSpec(shape, index_map, pipeline_mode=pl.Buffered(1))


def _sigmoid(v):
    return 0.5 * (1.0 + jnp.tanh(0.5 * v))


def _silu(v):
    return v * _sigmoid(v)


def _modulate(x, g, sh, sc):
    ms = jnp.mean(x * x, axis=-1, keepdims=True)
    y = x * lax.rsqrt(ms + EPS)
    return (y * g) * (1.0 + sc) + sh


def _rmsnorm(x, g):
    ms = jnp.mean(x * x, axis=-1, keepdims=True)
    return x * lax.rsqrt(ms + EPS) * g


def _ada_kernel(c_ref, w_ref, b_ref, o_ref):
    s = _silu(c_ref[...])
    o_ref[0] = jnp.dot(s, w_ref[0], preferred_element_type=F32,
                       precision=lax.Precision.HIGHEST) + b_ref[0]


def _ada_call(cc, w_ada, b_ada):
    depth, d, cols = w_ada.shape
    rows = cc.shape[0]
    tn = 1536
    return pl.pallas_call(
        _ada_kernel,
        grid=(depth, cols // tn),
        in_specs=[
            pl.BlockSpec((rows, d), lambda l, j: (0, 0)),
            pl.BlockSpec((1, d, tn), lambda l, j: (l, 0, j)),
            pl.BlockSpec((1, 1, tn), lambda l, j: (l, 0, j)),
        ],
        out_specs=pl.BlockSpec((1, rows, tn), lambda l, j: (l, 0, j)),
        out_shape=jax.ShapeDtypeStruct((depth, rows, cols), F32),
        compiler_params=_cparams(("parallel", "parallel")),
        name="ada_proj",
    )(cc, w_ada, b_ada.reshape(depth, 1, cols))


def _mod_kernel(x_ref, g_ref, sh_ref, sc_ref, o_ref):
    o_ref[0] = _modulate(x_ref[0], g_ref[...], sh_ref[0], sc_ref[0]).astype(BF16)


def _mod_call(x, g, sh, sc):
    nb, n, d = x.shape
    tm = min(1024, n)
    vec = pl.BlockSpec((1, 1, d), lambda b, i: (b, 0, 0))
    return pl.pallas_call(
        _mod_kernel,
        grid=(nb, n // tm),
        in_specs=[pl.BlockSpec((1, tm, d), lambda b, i: (b, i, 0)),
                  pl.BlockSpec((1, d), lambda b, i: (0, 0)), vec, vec],
        out_specs=pl.BlockSpec((1, tm, d), lambda b, i: (b, i, 0)),
        out_shape=jax.ShapeDtypeStruct((nb, n, d), BF16),
        compiler_params=_cparams(("parallel", "parallel")),
        name="modulate",
    )(x, g.reshape(1, d), sh, sc)


class _Cols:
    def __init__(self, d):
        self.au = 0
        self.av = self.au + A_WIDTH // COL_BLOCK
        self.bb = self.av + A_WIDTH // COL_BLOCK
        self.bc = self.bb + B_WIDTH // COL_BLOCK
        self.bh = self.bc + B_WIDTH // COL_BLOCK
        self.cq = self.bh + B_WIDTH // COL_BLOCK
        qk = C_HEADS * 2 * C_HEAD_DIM // COL_BLOCK
        self.ck = self.cq + qk
        self.cv = self.ck + qk
        self.gate = self.cv + C_HEADS * C_V_DIM // COL_BLOCK
        self.end = self.gate + N_BRANCH * d // COL_BLOCK

    def out(self, jw):
        return (jw - self.gate) % self.end


def _in_kernel(h_ref, w_ref, bg_ref, cos_ref, sa_ref, sb_ref, o_ref, *, cols, j0, rope, qscale,
               sub):
    j = pl.program_id(2) + j0
    tm = h_ref.shape[1]

    def run(epilogue):
        def mm(r):
            return jnp.dot(h_ref[0, r * sub:(r + 1) * sub, :], w_ref[...],
                           preferred_element_type=F32)
        nxt = mm(0)
        for r in range(tm // sub):
            acc = nxt
            if r + 1 < tm // sub:
                nxt = mm(r + 1)
            epilogue(acc, slice(r * sub, (r + 1) * sub))

    @pl.when(j < cols.bb)
    def _():
        def gelu(acc, rows):
            o_ref[0, rows, :] = jax.nn.gelu(acc).astype(BF16)
        run(gelu)

    @pl.when(((j >= cols.bb) & (j < cols.cq)) | ((j >= cols.cv) & (j < cols.gate)))
    def _():
        def plain(acc, rows):
            o_ref[0, rows, :] = acc.astype(BF16)
        run(plain)

    @pl.when((j >= cols.cq) & (j < cols.cv))
    def _():
        scale = jnp.where(j < cols.ck, qscale, 1.0).astype(F32)

        def rotary(acc, rows):
            if not rope:
                o_ref[0, rows, :] = (acc * scale).astype(BF16)
                return
            cos, sa, sb = cos_ref[rows, :] * scale, sa_ref[rows, :] * scale, sb_ref[rows, :] * scale
            for s in range(COL_BLOCK // LANES):
                t = acc[:, s * LANES:(s + 1) * LANES]
                r = (t * cos + pltpu.roll(t, LANES - 16, 1) * sa + pltpu.roll(t, 16, 1) * sb)
                o_ref[0, rows, s * LANES:(s + 1) * LANES] = r.astype(BF16)
        run(rotary)

    @pl.when(j >= cols.gate)
    def _():
        def gate(acc, rows):
            o_ref[0, rows, :] = _sigmoid(acc + bg_ref[...]).astype(BF16)
        run(gate)


def _in_call(h, w, b_gate, tables, *, j0, nj, rope):
    nb, n, d = h.shape
    cols = _Cols(d)
    tm = min(4096, n)
    full = (j0 == 0) and (nj == cols.end)
    if rope:
        cos, sa, sb = tables
    else:
        cos = sa = sb = jnp.zeros((tm, LANES), F32)
    tab = pl.BlockSpec((tm, LANES), lambda b, i, j: (i if rope else 0, 0))

    def out_idx(b, i, j):
        if full:
            return (b, i, jnp.where(j >= cols.gate, j - cols.gate, j + cols.end - cols.gate))
        return (b, i, j)
    kern = functools.partial(_in_kernel, cols=cols, j0=j0, rope=rope,
                             qscale=C_HEAD_DIM ** -0.5 * LOG2E, sub=min(512, tm))
    return pl.pallas_call(
        kern,
        grid=(nb, n // tm, nj),
        in_specs=[
            pl.BlockSpec((1, tm, d), lambda b, i, j: (b, i, 0)),
            pl.BlockSpec((d, COL_BLOCK), lambda b, i, j: (0, j + j0)),
            pl.BlockSpec((1, COL_BLOCK), lambda b, i, j: (0, jnp.maximum(j + j0 - cols.gate, 0))),
            tab, tab, tab,
        ],
        out_specs=pl.BlockSpec((1, tm, COL_BLOCK), out_idx),
        out_shape=jax.ShapeDtypeStruct((nb, n, nj * COL_BLOCK), BF16),
        compiler_params=_cparams(("parallel", "parallel", "arbitrary")),
        name="in_proj",
    )(h, w, b_gate.reshape(1, -1), cos, sa, sb)


def _rope_tables(n):
    rows = n // GRID_W
    r = jnp.repeat(jnp.arange(rows, dtype=F32), GRID_W)
    col = jnp.tile(jnp.arange(GRID_W, dtype=F32), rows)
    quarter = C_HEAD_DIM // 4
    inv = ROPE_BASE ** (-jnp.arange(quarter, dtype=F32) / quarter)
    ar = r[:, None] * inv
    ac = col[:, None] * inv
    ang = jnp.concatenate([ar, ar, ac, ac], axis=-1)
    ang = jnp.tile(ang, (1, LANES // C_HEAD_DIM))
    cos, sin = jnp.cos(ang), jnp.sin(ang)
    first_of_pair = (jnp.arange(LANES) // quarter) % 2 == 0
    sa = jnp.where(first_of_pair, -sin, 0.0)
    sb = jnp.where(first_of_pair, 0.0, sin)
    return cos, sa, sb


def _local_mixers(u_ref, v_ref, bg_ref, cg_ref, hh_ref, cgp_ref, hhp_ref, cgn_ref, hhn_ref,
                  lng_ref, lnb_ref, ws_ref, bias_ref, cw_ref, ya_ref, yb_ref, *, tm, seq):
    i = pl.program_id(1)
    lane = lax.broadcasted_iota(jnp.int32, (CHUNK, LANES), 1)
    lo = lane < A_GROUP_DIM
    for c in range(tm // CHUNK):
        rows = pl.ds(c * CHUNK, CHUNK)
        v = v_ref[0, rows, :].astype(F32)
        mu = jnp.mean(v, axis=-1, keepdims=True)
        var = jnp.mean(jnp.square(v - mu), axis=-1, keepdims=True)
        vn = ((v - mu) * lax.rsqrt(var + EPS) * lng_ref[...] + lnb_ref[...]).astype(BF16)
        for k in range(A_WIDTH // LANES):
            blk = vn[:, k * LANES:(k + 1) * LANES]
            zero = jnp.zeros_like(blk)
            mixed = (jnp.dot(ws_ref[2 * k], jnp.where(lo, blk, zero), preferred_element_type=F32)
                     + jnp.dot(ws_ref[2 * k + 1], jnp.where(lo, zero, blk), preferred_element_type=F32)
                     + bias_ref[:, k * LANES:(k + 1) * LANES])
            u = u_ref[0, rows, k * LANES:(k + 1) * LANES].astype(F32)
            ya_ref[rows, k * LANES:(k + 1) * LANES] = (u * mixed).astype(BF16)

    z = cg_ref[0].astype(F32) * hh_ref[0].astype(F32)
    last = BF16_SUBLANES - 1
    z_prev = cgp_ref[0, last:last + 1, :].astype(F32) * hhp_ref[0, last:last + 1, :].astype(F32)
    z_next = cgn_ref[0, 0:1, :].astype(F32) * hhn_ref[0, 0:1, :].astype(F32)
    row = lax.broadcasted_iota(jnp.int32, (tm, 1), 0)
    pos = lax.rem(i * tm + row, seq)
    zm1 = jnp.where(row == 0, z_prev, pltpu.roll(z, 1, 0))
    zm1 = jnp.where(pos == 0, 0.0, zm1)
    zp1 = jnp.where(row == tm - 1, z_next, pltpu.roll(z, tm - 1, 0))
    zp1 = jnp.where(pos == seq - 1, 0.0, zp1)
    conv = cw_ref[0:1, :] * zm1 + cw_ref[1:2, :] * z + cw_ref[2:3, :] * zp1
    yb_ref[...] = (bg_ref[0].astype(F32) * conv).astype(BF16)


def _attn_kernel(*refs, tq, ck, sizes, lam_init):
    nsrc = len(sizes)
    cl_ref, g_ref, q_ref = refs[:3]
    kv_refs = refs[3:3 + 2 * nsrc]
    o_ref = refs[3 + 2 * nsrc]
    vt_refs = refs[4 + 2 * nsrc:4 + 3 * nsrc]
    kmax_ref = refs[4 + 3 * nsrc]
    half = lax.broadcasted_iota(jnp.int32, (1, LANES), 1) < C_HEAD_DIM

    @pl.when(pl.program_id(2) == 0)
    def _():
        per_map = jnp.where(lax.broadcasted_iota(jnp.int32, (LANES, LANES), 0) // C_HEAD_DIM
                            == lax.broadcasted_iota(jnp.int32, (LANES, LANES), 1), 1.0, 0.0)
        kmax2 = jnp.zeros((1, LANES), F32)
        for src in range(nsrc):
            vt_refs[src][...] = kv_refs[2 * src + 1][0].astype(F32).T.astype(BF16)
            k = kv_refs[2 * src][0]
            n2 = jnp.dot(k * k, per_map.astype(BF16), preferred_element_type=F32)
            kmax2 = jnp.maximum(kmax2, jnp.max(n2, axis=0, keepdims=True))
        kmax_ref[0:1, :] = jnp.sqrt(kmax2) * (1.0 + 2.0 ** -7)

    q = q_ref[0]
    zero = jnp.zeros_like(q)
    qs = [jnp.where(half, q, zero), jnp.where(half, zero, q)]
    chunks = [(kv_refs[2 * src], vt_refs[src], c0, min(ck, nk))
              for src, nk in enumerate(sizes) for c0 in range(0, nk, min(ck, nk))]

    def scores(chunk, qm):
        k_ref, _, c0, c = chunk
        return lax.dot_general(k_ref[0, c0:c0 + c, :], qm, (((1,), (1,)), ((), ())),
                               preferred_element_type=F32)

    def finish(accs, ls):
        cl = cl_ref[...]
        lam = (jnp.exp(jnp.sum(cl[0:1] * cl[1:2], axis=-1, keepdims=True))
               - jnp.exp(jnp.sum(cl[2:3] * cl[3:4], axis=-1, keepdims=True)) + lam_init)
        o = accs[0] / ls[0] - lam * (accs[1] / ls[1])
        ms = jnp.mean(o * o, axis=0, keepdims=True)
        y = o * lax.rsqrt(ms + EPS) * (g_ref[...] * (1.0 - lam_init))
        o_ref[0] = y.T.astype(BF16)

    qsq = jnp.square(q.astype(F32).T)
    refs_r = [jnp.sqrt(jnp.sum(qsq[j * C_HEAD_DIM:(j + 1) * C_HEAD_DIM], axis=0, keepdims=True))
              * kmax_ref[0:1, j:j + 1] for j in range(2)]
    ls = [jnp.zeros((1, tq), F32) for _ in qs]
    accs = [jnp.zeros((C_V_DIM, tq), F32) for _ in qs]
    s_next = [scores(chunks[0], qm) for qm in qs]
    for t, (_, vt_ref, c0, c) in enumerate(chunks):
        s_cur = s_next
        if t + 1 < len(chunks):
            s_next = [scores(chunks[t + 1], qm) for qm in qs]
        vt = vt_ref[:, c0:c0 + c]
        for j, s in enumerate(s_cur):
            p = jnp.exp2(s - refs_r[j])
            ls[j] = ls[j] + jnp.sum(p, axis=0, keepdims=True)
            accs[j] = accs[j] + jnp.dot(vt, p.astype(BF16), preferred_element_type=F32)
    finish(accs, ls)

    healthy = jnp.min(jnp.minimum(ls[0], ls[1])) >= 2.0 ** -60

    @pl.when(jnp.logical_not(healthy))
    def _():
        ms = [jnp.full((1, tq), -jnp.inf, F32) for _ in qs]
        ls = [jnp.zeros((1, tq), F32) for _ in qs]
        accs = [jnp.zeros((C_V_DIM, tq), F32) for _ in qs]
        for chunk in chunks:
            _, vt_ref, c0, c = chunk
            vt = vt_ref[:, c0:c0 + c]
            for j, qm in enumerate(qs):
                s = scores(chunk, qm)
                m_new = jnp.maximum(ms[j], jnp.max(s, axis=0, keepdims=True))
                alpha = jnp.exp2(ms[j] - m_new)
                p = jnp.exp2(s - m_new)
                ls[j] = alpha * ls[j] + jnp.sum(p, axis=0, keepdims=True)
                accs[j] = alpha * accs[j] + jnp.dot(vt, p.astype(BF16),
                                                    preferred_element_type=F32)
                ms[j] = m_new
        finish(accs, ls)


def _attn_call(q_arr, q_cb, sources, c_lambda, subln_g, *, lam_init):
    nb, n, _ = q_arr.shape
    tq = min(512, n)
    sizes = tuple(a.shape[1] for a, _, _ in sources)
    in_specs = [
        pl.BlockSpec((4, C_HEAD_DIM), lambda b, h, i: (0, 0)),
        pl.BlockSpec((C_V_DIM, 1), lambda b, h, i: (0, 0)),
        pl.BlockSpec((1, tq, LANES), lambda b, h, i: (b, i, q_cb + h)),
    ]
    args = [c_lambda, subln_g.reshape(-1, 1), q_arr]
    for arr, kcb, vcb in sources:
        nk = arr.shape[1]
        in_specs.append(pl.BlockSpec((1, nk, LANES), lambda b, h, i, kcb=kcb: (b, 0, kcb + h)))
        in_specs.append(pl.BlockSpec((1, nk, LANES), lambda b, h, i, vcb=vcb: (b, 0, vcb + h)))
        args += [arr, arr]
    kern = functools.partial(_attn_kernel, tq=tq, ck=2048, sizes=sizes, lam_init=lam_init)
    return pl.pallas_call(
        kern,
        grid=(nb, C_HEADS, n // tq),
        in_specs=in_specs,
        out_specs=pl.BlockSpec((1, tq, C_V_DIM), lambda b, h, i: (b, i, h)),
        out_shape=jax.ShapeDtypeStruct((nb, n, C_HEADS * C_V_DIM), BF16),
        scratch_shapes=[pltpu.VMEM((C_V_DIM, nk), BF16) for nk in sizes]
        + [pltpu.VMEM((8, LANES), F32)],
        compiler_params=_cparams(("parallel", "parallel", "arbitrary")),
        name="diff_attention",
    )(*args)


def _merge_kernel(*refs, tm, seq):
    mixer_refs = refs[:14]
    (x_ref, yc_ref, ga_ref, gb_ref, gc_ref, wa_ref, wb_ref, wc_ref, wo_ref, g1_ref, n2_ref,
     sh2_ref, sc2_ref, xo_ref, ho_ref, ya_ref, yb_ref) = refs[14:]
    c = jnp.dot(yc_ref[0], wc_ref[...], preferred_element_type=F32)
    _local_mixers(*mixer_refs, ya_ref, yb_ref, tm=tm, seq=seq)
    a = jnp.dot(ya_ref[...], wa_ref[...], preferred_element_type=F32)
    b = jnp.dot(yb_ref[...], wb_ref[...], preferred_element_type=F32)
    y = (ga_ref[0].astype(F32) * a + gb_ref[0].astype(F32) * b + gc_ref[0].astype(F32) * c)
    m = jnp.dot(y.astype(BF16), wo_ref[...], preferred_element_type=F32)
    xn = x_ref[0] + g1_ref[0] * m
    xo_ref[0] = xn
    ho_ref[0] = _modulate(xn, n2_ref[...], sh2_ref[0], sc2_ref[0]).astype(BF16)


def _merge_call(x, p, pos, yc, gate_cb, mixer_w, wa, wb, wc, wo, g1, n2g, sh2, sc2, *, seq):
    nb, n, d = x.shape
    au, av, bb, bc, bh = pos
    ln_g, ln_b, ws, bs, conv_w = mixer_w
    tm = min(512, n)
    hb = tm // BF16_SUBLANES
    nhb = n // BF16_SUBLANES

    def tok(w, cb=0):
        return pl.BlockSpec((1, tm, w), lambda b, i: (b, i, cb))

    def prev(cb):
        return pl.BlockSpec((1, BF16_SUBLANES, COL_BLOCK),
                            lambda b, i: (b, jnp.maximum(i * hb - 1, 0), cb))

    def nxt(cb):
        return pl.BlockSpec((1, BF16_SUBLANES, COL_BLOCK),
                            lambda b, i: (b, jnp.minimum((i + 1) * hb, nhb - 1), cb))

    const = lambda shape: pl.BlockSpec(shape, lambda b, i: (0,) * len(shape))
    bias = jnp.repeat(bs.T, A_GROUP_DIM, axis=1)
    vec = pl.BlockSpec((1, 1, d), lambda b, i: (b, 0, 0))
    return pl.pallas_call(
        functools.partial(_merge_kernel, tm=tm, seq=seq),
        grid=(nb, n // tm),
        in_specs=[tok(COL_BLOCK, au), tok(COL_BLOCK, av), tok(COL_BLOCK, bb), tok(COL_BLOCK, bc),
                  tok(COL_BLOCK, bh), prev(bc), prev(bh), nxt(bc), nxt(bh),
                  const((1, A_WIDTH)), const((1, A_WIDTH)), const((A_GROUPS, CHUNK, CHUNK)),
                  const((CHUNK, A_WIDTH)), const((CONV_W, B_WIDTH)),
                  tok(d), tok(d), tok(d, gate_cb), tok(d, gate_cb + 1), tok(d, gate_cb + 2),
                  _resident(wa.shape, lambda b, i: (0, 0)), _resident(wb.shape, lambda b, i: (0, 0)),
                  _resident(wc.shape, lambda b, i: (0, 0)), _resident(wo.shape, lambda b, i: (0, 0)),
                  vec, const((1, d)), vec, vec],
        out_specs=[tok(d), tok(d)],
        out_shape=[jax.ShapeDtypeStruct((nb, n, d), F32), jax.ShapeDtypeStruct((nb, n, d), BF16)],
        scratch_shapes=[pltpu.VMEM((tm, A_WIDTH), BF16), pltpu.VMEM((tm, B_WIDTH), BF16)],
        compiler_params=_cparams(("parallel", "parallel")),
        name="merge_branches",
    )(p, p, p, p, p, p, p, p, p, ln_g.reshape(1, -1), ln_b.reshape(1, -1), ws.astype(BF16), bias,
      conv_w, x, yc, p, p, p, wa, wb, wc, wo, g1, n2g.reshape(1, d), sh2, sc2)


def _ff_chunks(width, chunk=512):
    out, c0 = [], 0
    while c0 < width:
        cw = min(chunk, width - c0)
        out.append((c0, cw))
        c0 += cw
    return out


def _finish(xn, final, ng_ref, sh_ref, sc_ref, out_refs):
    if final:
        out_refs[0][0] = _rmsnorm(xn, ng_ref[...])
    else:
        out_refs[0][0] = xn
        out_refs[1][0] = _modulate(xn, ng_ref[...], sh_ref[0], sc_ref[0]).astype(BF16)


def _ffn_kernel(x_ref, h_ref, wg_ref, wu_ref, wd_ref, g2_ref, ng_ref, sh_ref, sc_ref, *out_refs,
                final):
    h = h_ref[0]
    acc = None
    for c0, cw in _ff_chunks(wg_ref.shape[1]):
        a = jnp.dot(h, wg_ref[:, c0:c0 + cw], preferred_element_type=F32)
        b = jnp.dot(h, wu_ref[:, c0:c0 + cw], preferred_element_type=F32)
        t = (_silu(a) * b).astype(BF16)
        part = jnp.dot(t, wd_ref[c0:c0 + cw, :], preferred_element_type=F32)
        acc = part if acc is None else acc + part
    xn = x_ref[0] + g2_ref[0] * acc
    _finish(xn, final, ng_ref, sh_ref, sc_ref, out_refs)


def _epilogue_specs(nb, n, d, tm, final, idx):
    tok = pl.BlockSpec((1, tm, d), idx)
    if final:
        return [tok], [jax.ShapeDtypeStruct((nb, n, d), F32)]
    return [tok, tok], [jax.ShapeDtypeStruct((nb, n, d), F32), jax.ShapeDtypeStruct((nb, n, d), BF16)]


def _ffn_call(x, h, wg, wu, wd, g2, ng, sh, sc, *, final):
    nb, n, d = x.shape
    tm = min(512, n)
    idx = lambda b, i: (b, i, 0)
    vec = pl.BlockSpec((1, 1, d), lambda b, i: (b, 0, 0))
    out_specs, out_shape = _epilogue_specs(nb, n, d, tm, final, idx)
    return pl.pallas_call(
        functools.partial(_ffn_kernel, final=final),
        grid=(nb, n // tm),
        in_specs=[pl.BlockSpec((1, tm, d), idx), pl.BlockSpec((1, tm, d), idx),
                  _resident(wg.shape, lambda b, i: (0, 0)), _resident(wu.shape, lambda b, i: (0, 0)),
                  _resident(wd.shape, lambda b, i: (0, 0)),
                  vec, pl.BlockSpec((1, d), lambda b, i: (0, 0)), vec, vec],
        out_specs=out_specs,
        out_shape=out_shape,
        compiler_params=_cparams(("parallel", "parallel")),
        name="dense_swiglu",
    )(x, h, wg, wu, wd, g2, ng.reshape(1, d), sh, sc)


MOE_TILE = 2048
MOE_WIN = 512
MOE_SLAB = 256
MOE_ROWS = 512
MOE_FF = 512
CNT_ROWS = 8


def _route_kernel(h_ref, wr_ref, comb_ref, rank_ref, rankt_ref, cnt_ref, *, n_experts, win):
    h = h_ref[0]
    tm = h.shape[0]
    lane = lax.broadcasted_iota(jnp.int32, (tm, LANES), 1)
    logits = jnp.dot(h, wr_ref[...], preferred_element_type=F32)
    lg = jnp.where(lane < n_experts, logits, -jnp.inf)
    m1 = jnp.max(lg, axis=-1, keepdims=True)
    i1 = jnp.min(jnp.where(lg == m1, lane, LANES), axis=-1, keepdims=True)
    lg2 = jnp.where(lane == i1, -jnp.inf, lg)
    m2 = jnp.max(lg2, axis=-1, keepdims=True)
    i2 = jnp.min(jnp.where(lg2 == m2, lane, LANES), axis=-1, keepdims=True)
    e2 = jnp.exp(m2 - m1)
    w1 = 1.0 / (1.0 + e2)
    comb_ref[0] = jnp.where(lane == i1, w1, 0.0) + jnp.where(lane == i2, e2 * w1, 0.0)
    sel = (lane == i1) | (lane == i2)
    tri = jnp.where(lax.broadcasted_iota(jnp.int32, (win, win), 0)
                    > lax.broadcasted_iota(jnp.int32, (win, win), 1), 1.0, 0.0).astype(BF16)
    base = jnp.zeros((1, LANES), F32)
    bases = []
    for w in range(tm // win):
        rows = slice(w * win, (w + 1) * win)
        sw = jnp.where(sel[rows], 1.0, 0.0)
        excl = jnp.dot(tri, sw.astype(BF16), preferred_element_type=F32)
        rank_ref[0, rows, :] = jnp.where(sel[rows], excl + base, -1.0)
        bases.append(base)
        base = base + jnp.sum(sw, axis=0, keepdims=True)
    bases.append(base)
    bases += [jnp.zeros((1, LANES), F32)] * (CNT_ROWS - len(bases))
    cnt_ref[0] = jnp.concatenate(bases, axis=0).astype(jnp.int32)
    rankt_ref[0] = rank_ref[0].T[:CNT_ROWS]


def _route_call(h, w_router, tm):
    nb, n, d = h.shape
    n_experts = w_router.shape[1]
    assert n_experts <= CNT_ROWS and tm // MOE_WIN + 1 <= CNT_ROWS
    nt = n // tm
    wr = jnp.zeros((d, LANES), BF16).at[:, :n_experts].set(w_router.astype(BF16))
    tok = pl.BlockSpec((1, tm, LANES), lambda b, i: (b, i, 0))
    per_tile = lambda r, c: pl.BlockSpec((1, r, c), lambda b, i: (b * nt + i, 0, 0))
    return pl.pallas_call(
        functools.partial(_route_kernel, n_experts=n_experts, win=min(MOE_WIN, tm)),
        grid=(nb, nt),
        in_specs=[pl.BlockSpec((1, tm, d), lambda b, i: (b, i, 0)),
                  pl.BlockSpec((d, LANES), lambda b, i: (0, 0))],
        out_specs=[tok, tok, per_tile(CNT_ROWS, tm), per_tile(CNT_ROWS, LANES)],
        out_shape=[jax.ShapeDtypeStruct((nb, n, LANES), F32), jax.ShapeDtypeStruct((nb, n, LANES), F32),
                   jax.ShapeDtypeStruct((nb * nt, CNT_ROWS, tm), F32),
                   jax.ShapeDtypeStruct((nb * nt, CNT_ROWS, LANES), jnp.int32)],
        compiler_params=_cparams(("parallel", "parallel")),
        name="moe_route",
    )(h, wr)


def _moe_kernel(cnt_ref, x_ref, h_ref, comb_ref, rank_ref, rankt_ref, wg_hbm, wu_hbm, wd_hbm, g2_ref,
                ng_ref, sh_ref, sc_ref, *rest, final, n_experts, win, slab, rblk, ff):
    out_refs, (hc_ref, yc_ref, wg_buf, wu_buf, wd_buf, sem) = rest[:-6], rest[-6:]
    acc_ref = out_refs[0].at[0]
    b, i, e = (pl.program_id(k) for k in range(3))
    tm, d = h_ref.shape[1], h_ref.shape[2]
    nw = tm // win
    nf = wg_hbm.shape[2] // ff
    tile = b * pl.num_programs(1) + i
    cbase = tile * (CNT_ROWS * n_experts)
    step = tile * n_experts + e
    last_step = pl.num_programs(0) * pl.num_programs(1) * n_experts - 1

    def weight_copies(expert, f, slot):
        cols = pl.ds(pl.multiple_of(f * ff, ff), ff)
        return (pltpu.make_async_copy(wg_hbm.at[expert, :, cols], wg_buf.at[slot], sem.at[slot, 0]),
                pltpu.make_async_copy(wu_hbm.at[expert, :, cols], wu_buf.at[slot], sem.at[slot, 1]),
                pltpu.make_async_copy(wd_hbm.at[expert, cols, :], wd_buf.at[slot], sem.at[slot, 2]))

    def fetch(expert, f, slot):
        for copy in weight_copies(expert, f, slot):
            copy.start()

    @pl.when(step == 0)
    def _():
        fetch(e, 0, 0)

    def count(w):
        return cnt_ref[cbase + w * n_experts + e]

    def slabs(w):
        row0 = (count(w) // BF16_SUBLANES) * BF16_SUBLANES
        return row0, (count(w + 1) - row0 + slab - 1) // slab

    @pl.when(e == 0)
    def _():
        acc_ref[...] = jnp.zeros_like(acc_ref)

    total = count(nw)

    def zero(s, carry):
        rows = pl.ds(pl.multiple_of(s * slab, slab), slab)
        hc_ref[rows, :] = jnp.zeros((slab, d), BF16)
        yc_ref[rows, :] = jnp.zeros((slab, d), F32)
        return carry

    lax.fori_loop(0, jnp.minimum((total + rblk + 2 * slab) // slab, hc_ref.shape[0] // slab),
                  zero, 0)

    def gather(w):
        row0, nsl = slabs(w)
        rt = rankt_ref[0, 0, :, w * win:(w + 1) * win]
        hw = h_ref[0, w * win:(w + 1) * win, :]

        def body(s, carry):
            r0 = pl.multiple_of(row0 + s * slab, BF16_SUBLANES)
            rid = (r0 + lax.broadcasted_iota(jnp.int32, (slab, win), 0)).astype(F32)
            onehot = jnp.where(rt == rid, 1.0, 0.0).astype(BF16)
            rows = pl.ds(r0, slab)
            got = jnp.dot(onehot, hw, preferred_element_type=F32)
            hc_ref[rows, :] = (hc_ref[rows, :].astype(F32) + got).astype(BF16)
            return carry

        return body, nsl

    for w in range(nw):
        gather(w)[0](0, 0)
    for w in range(nw):
        body, nsl = gather(w)
        lax.fori_loop(1, nsl, body, 0)

    quarters = (total + rblk // 4 - 1) // (rblk // 4)
    nfull = (quarters + 1) // 4
    rest = jnp.maximum(quarters - 4 * nfull, 0)

    def ff_block(f, carry):
        slot = lax.rem(step * nf + f, 2)

        @pl.when(f + 1 < nf)
        def _():
            fetch(e, f + 1, 1 - slot)

        @pl.when((f + 1 == nf) & (step < last_step))
        def _():
            fetch(lax.rem(e + 1, n_experts), 0, 1 - slot)

        for copy in weight_copies(e, f, slot):
            copy.wait()

        def ffn(r0, size):
            rows = pl.ds(pl.multiple_of(r0, rblk // 4), size)
            hb = hc_ref[rows, :]
            a = jnp.dot(hb, wg_buf[slot], preferred_element_type=F32)
            u = jnp.dot(hb, wu_buf[slot], preferred_element_type=F32)
            t = (_silu(a) * u).astype(BF16)
            yc_ref[rows, :] += jnp.dot(t, wd_buf[slot], preferred_element_type=F32)

        def full(r, c):
            ffn(r * rblk, rblk)
            return c

        lax.fori_loop(0, nfull, full, 0)

        @pl.when(rest == 2)
        def _():
            ffn(nfull * rblk, rblk // 2)

        @pl.when(rest == 1)
        def _():
            ffn(nfull * rblk, rblk // 4)

        return carry

    lax.fori_loop(0, nf, ff_block, 0)

    def spread(w):
        row0, nsl = slabs(w)
        trows = slice(w * win, (w + 1) * win)
        mine = lax.broadcasted_iota(jnp.int32, (win, LANES), 1) == e
        rcol = jnp.sum(jnp.where(mine, rank_ref[0, trows, :], 0.0), axis=-1, keepdims=True)
        wcol = jnp.sum(jnp.where(mine, comb_ref[0, trows, :], 0.0), axis=-1, keepdims=True)

        def body(s, carry):
            r0 = pl.multiple_of(row0 + s * slab, BF16_SUBLANES)
            cid = (r0 + lax.broadcasted_iota(jnp.int32, (win, slab), 1)).astype(F32)
            onehot = jnp.where(rcol == cid, 1.0, 0.0).astype(BF16)
            yb = yc_ref[pl.ds(r0, slab), :].astype(BF16)
            acc_ref[trows, :] += wcol * jnp.dot(onehot, yb, preferred_element_type=F32)
            return carry

        return body, nsl

    bodies = [spread(w) for w in range(nw)]
    for body, _ in bodies:
        body(0, 0)
    for body, nsl in bodies:
        lax.fori_loop(1, nsl, body, 0)

    @pl.when(e == n_experts - 1)
    def _():
        xn = x_ref[0] + g2_ref[0] * acc_ref[...]
        _finish(xn, final, ng_ref, sh_ref, sc_ref, out_refs)


def _moe_call(x, h, w_router, wg, wu, wd, g2, ng, sh, sc, *, final):
    nb, n, d = x.shape
    n_experts, _, dff = wg.shape
    tm = min(MOE_TILE, n)
    win = min(MOE_WIN, tm)
    nt = n // tm
    comb, rank, rankt, cnt = _route_call(h, w_router, tm)
    cnt = cnt[:, :, :n_experts].reshape(-1)
    rankt = rankt.reshape(nb * nt, CNT_ROWS, 1, tm)
    cap_ffn = -(-tm // MOE_ROWS) * MOE_ROWS
    cap = -(-max(cap_ffn, tm + MOE_SLAB) // MOE_SLAB) * MOE_SLAB

    idx = lambda b, i, e, c: (b, i, 0)
    vec = pl.BlockSpec((1, 1, d), lambda b, i, e, c: (b, 0, 0))
    hbm = pl.BlockSpec(memory_space=pl.ANY)
    out_specs, out_shape = _epilogue_specs(nb, n, d, tm, final, idx)
    grid_spec = pltpu.PrefetchScalarGridSpec(
        num_scalar_prefetch=1,
        grid=(nb, nt, n_experts),
        in_specs=[_resident((1, tm, d), idx), _resident((1, tm, d), idx),
                  _resident((1, tm, LANES), idx), _resident((1, tm, LANES), idx),
                  pl.BlockSpec((1, 1, 1, tm), lambda b, i, e, c: (b * nt + i, e, 0, 0)),
                  hbm, hbm, hbm,
                  vec, pl.BlockSpec((1, d), lambda b, i, e, c: (0, 0)), vec, vec],
        out_specs=out_specs,
        scratch_shapes=[pltpu.VMEM((cap, d), BF16), pltpu.VMEM((cap, d), F32),
                        pltpu.VMEM((2, d, MOE_FF), BF16), pltpu.VMEM((2, d, MOE_FF), BF16),
                        pltpu.VMEM((2, MOE_FF, d), BF16), pltpu.SemaphoreType.DMA((2, 3))],
    )
    return pl.pallas_call(
        functools.partial(_moe_kernel, final=final, n_experts=n_experts, win=win, slab=MOE_SLAB,
                          rblk=MOE_ROWS, ff=MOE_FF),
        grid_spec=grid_spec,
        out_shape=out_shape,
        compiler_params=_cparams(("arbitrary", "arbitrary", "arbitrary")),
        name="moe_swiglu",
    )(cnt, x, h, comb, rank, rankt, wg, wu, wd, g2, ng.reshape(1, d), sh, sc)


def kernel(x, c, ctx, c_ctx, w_ada, b_ada, norm1_g, norm2_g, w_in, b_gate, a_ln_g, a_ln_b, a_ws,
           a_bs, b_conv, c_lambda, c_subln_g, w_a_out, w_b_out, w_c_out, w_o, ff_w_gate, ff_w_up,
           ff_w_down, moe_w_router, moe_w_gate, moe_w_up, moe_w_down, final_norm_g):
    bsz, n, d = x.shape
    nc = ctx.shape[1]
    depth = w_ada.shape[0]
    cols = _Cols(d)
    head_cb = COL_BLOCK // LANES
    gate_cb = cols.out(cols.gate) * COL_BLOCK // d
    mix_pos = tuple(cols.out(j) for j in (cols.au, cols.av, cols.bb, cols.bc, cols.bh))
    cq_l, ck_l, cv_l = (cols.out(j) * head_cb for j in (cols.cq, cols.ck, cols.cv))

    pad = (-(bsz + 1)) % 8
    cc = jnp.concatenate([c, c_ctx[None], jnp.zeros((pad, d), F32)], axis=0)
    mod = _ada_call(cc, w_ada, b_ada)

    def mods(l):
        lat = [mod[l, :bsz, k * d:(k + 1) * d].reshape(bsz, 1, d) for k in range(N_MOD)]
        con = [mod[l, bsz:bsz + 1, k * d:(k + 1) * d].reshape(1, 1, d) for k in range(N_MOD)]
        return lat, con

    tables = _rope_tables(n)
    w_in_b = w_in.astype(BF16)
    xl = x
    xc = ctx.reshape(1, bsz * nc, d)
    lat, con = mods(0)
    h = _mod_call(xl, norm1_g[0], lat[0], lat[1])
    hc = _mod_call(xc, norm1_g[0], con[0], con[1])

    for l in range(depth):
        last = l == depth - 1
        lam_init = 0.8 - 0.6 * math.exp(-0.3 * l)
        lat, con = mods(l)
        if not last:
            nlat, ncon = mods(l + 1)
            nxt_l = (norm1_g[l + 1], nlat[0], nlat[1])
            nxt_c = (norm1_g[l + 1], ncon[0], ncon[1])
        else:
            nxt_l = (final_norm_g, lat[0], lat[1])
            nxt_c = None

        p = _in_call(h, w_in_b[l], b_gate[l], tables, j0=0, nj=cols.end, rope=True)
        if last:
            pc = _in_call(hc, w_in_b[l], b_gate[l], tables, j0=cols.ck, nj=cols.gate - cols.ck,
                          rope=False)
            ck_c, cv_c = 0, (cols.cv - cols.ck) * head_cb
        else:
            pc = _in_call(hc, w_in_b[l], b_gate[l], tables, j0=0, nj=cols.end, rope=False)
            ck_c, cv_c = ck_l, cv_l
        pc_seq = pc.reshape(bsz, nc, -1)

        wa, wb, wc, wo = (w.astype(BF16) for w in (w_a_out[l], w_b_out[l], w_c_out[l], w_o[l]))

        def channel(xs, hs, g2, nxt, final):
            i = l // 2
            if l % 2 == 0:
                return _ffn_call(xs, hs, ff_w_gate[i].astype(BF16), ff_w_up[i].astype(BF16),
                                 ff_w_down[i].astype(BF16), g2, *nxt, final=final)
            return _moe_call(xs, hs, moe_w_router[i], moe_w_gate[i].astype(BF16),
                             moe_w_up[i].astype(BF16), moe_w_down[i].astype(BF16), g2, *nxt,
                             final=final)

        mixer_w = (a_ln_g[l], a_ln_b[l], a_ws[l], a_bs[l], b_conv[l])
        yc = _attn_call(p, cq_l, [(p, ck_l, cv_l), (pc_seq, ck_c, cv_c)],
                        c_lambda[l], c_subln_g[l], lam_init=lam_init)
        x1, h2 = _merge_call(xl, p, mix_pos, yc, gate_cb, mixer_w, wa, wb, wc, wo, lat[2],
                             norm2_g[l], lat[3], lat[4], seq=n)
        res = channel(x1, h2, lat[5], nxt_l, last)
        if last:
            return res[0]
        xl, h = res

        yc_c = _attn_call(pc_seq, cq_l, [(pc_seq, ck_c, cv_c)], c_lambda[l], c_subln_g[l],
                          lam_init=lam_init)
        xc1, hc2 = _merge_call(xc, pc, mix_pos, yc_c.reshape(1, bsz * nc, -1), gate_cb, mixer_w,
                               wa, wb, wc, wo, con[2], norm2_g[l], con[3], con[4], seq=nc)
        xc, hc = channel(xc1, hc2, con[5], nxt_c, False)
```

```python
import functools
import math

import jax
import jax.numpy as jnp
from jax import lax
from jax.experimental import pallas as pl
from jax.experimental.pallas import tpu as pltpu

F32 = jnp.float32
BF16 = jnp.bfloat16

EPS = 1e-6
GRID_W = 64
N_MOD = 6
N_BRANCH = 3
CHUNK = 128
A_GROUPS = 8
A_GROUP_DIM = 64
A_WIDTH = A_GROUPS * A_GROUP_DIM
B_WIDTH = 512
CONV_W = 3
C_HEADS = 8
C_HEAD_DIM = 64
C_V_DIM = 2 * C_HEAD_DIM
ROPE_BASE = 10000.0
ROPE_PAIR = C_HEAD_DIM // 4
LOG2E = 1.4426950408889634

LANES = 128
BF16_SUBLANES = 16
V7X_VMEM_LIMIT = 56 * 1024 * 1024

COL_BLOCK = 512
ADA_COLS = 1536
MOD_TILE = 1024
IN_TILE, IN_SUB = 4096, 512
ATTN_Q_TILE, ATTN_KEYS = 512, 2048
MERGE_TILE = 512
FFN_TILE, FFN_CHUNK = 512, 512


def _cparams(sem):
    return pltpu.CompilerParams(dimension_semantics=sem, vmem_limit_bytes=V7X_VMEM_LIMIT)


def _resident(shape, index_map):
    return pl.BlockSpec(shape, index_map, pipeline_mode=pl.Buffered(1))


def _sigmoid(v):
    return 0.5 * (1.0 + jnp.tanh(0.5 * v))


def _silu(v):
    return v * _sigmoid(v)


def _modulate(x, g, sh, sc):
    ms = jnp.mean(x * x, axis=-1, keepdims=True)
    y = x * lax.rsqrt(ms + EPS)
    return (y * g) * (1.0 + sc) + sh


def _rmsnorm(x, g):
    ms = jnp.mean(x * x, axis=-1, keepdims=True)
    return x * lax.rsqrt(ms + EPS) * g


def _ada_kernel(c_ref, w_ref, b_ref, o_ref):
    s = _silu(c_ref[...])
    o_ref[0] = jnp.dot(s, w_ref[0], preferred_element_type=F32,
                       precision=lax.Precision.HIGHEST) + b_ref[0]


def _ada_call(cc, w_ada, b_ada):
    depth, d, cols = w_ada.shape
    rows = cc.shape[0]
    tn = ADA_COLS
    return pl.pallas_call(
        _ada_kernel,
        grid=(depth, cols // tn),
        in_specs=[
            pl.BlockSpec((rows, d), lambda l, j: (0, 0)),
            pl.BlockSpec((1, d, tn), lambda l, j: (l, 0, j)),
            pl.BlockSpec((1, 1, tn), lambda l, j: (l, 0, j)),
        ],
        out_specs=pl.BlockSpec((1, rows, tn), lambda l, j: (l, 0, j)),
        out_shape=jax.ShapeDtypeStruct((depth, rows, cols), F32),
        compiler_params=_cparams(("parallel", "parallel")),
        name="ada_proj",
    )(cc, w_ada, b_ada.reshape(depth, 1, cols))


def _mod_kernel(x_ref, g_ref, sh_ref, sc_ref, o_ref):
    o_ref[0] = _modulate(x_ref[0], g_ref[...], sh_ref[0], sc_ref[0]).astype(BF16)


def _mod_call(x, g, sh, sc):
    nb, n, d = x.shape
    tm = min(MOD_TILE, n)
    vec = pl.BlockSpec((1, 1, d), lambda b, i: (b, 0, 0))
    return pl.pallas_call(
        _mod_kernel,
        grid=(nb, n // tm),
        in_specs=[pl.BlockSpec((1, tm, d), lambda b, i: (b, i, 0)),
                  pl.BlockSpec((1, d), lambda b, i: (0, 0)), vec, vec],
        out_specs=pl.BlockSpec((1, tm, d), lambda b, i: (b, i, 0)),
        out_shape=jax.ShapeDtypeStruct((nb, n, d), BF16),
        compiler_params=_cparams(("parallel", "parallel")),
        name="modulate",
    )(x, g.reshape(1, d), sh, sc)


class _Cols:
    def __init__(self, d):
        self.au = 0
        self.av = self.au + A_WIDTH // COL_BLOCK
        self.bb = self.av + A_WIDTH // COL_BLOCK
        self.bc = self.bb + B_WIDTH // COL_BLOCK
        self.bh = self.bc + B_WIDTH // COL_BLOCK
        self.cq = self.bh + B_WIDTH // COL_BLOCK
        qk = C_HEADS * 2 * C_HEAD_DIM // COL_BLOCK
        self.ck = self.cq + qk
        self.cv = self.ck + qk
        self.gate = self.cv + C_HEADS * C_V_DIM // COL_BLOCK
        self.end = self.gate + N_BRANCH * d // COL_BLOCK

    def out(self, jw):
        return (jw - self.gate) % self.end


def _in_kernel(h_ref, w_ref, bg_ref, cos_ref, sa_ref, sb_ref, o_ref, *, cols, j0, rope, qscale,
               sub):
    j = pl.program_id(2) + j0
    tm = h_ref.shape[1]

    def run(epilogue):
        def mm(r):
            return jnp.dot(h_ref[0, r * sub:(r + 1) * sub, :], w_ref[...],
                           preferred_element_type=F32)
        nxt = mm(0)
        for r in range(tm // sub):
            acc = nxt
            if r + 1 < tm // sub:
                nxt = mm(r + 1)
            epilogue(acc, slice(r * sub, (r + 1) * sub))

    @pl.when(j < cols.bb)
    def _():
        def gelu(acc, rows):
            o_ref[0, rows, :] = jax.nn.gelu(acc).astype(BF16)
        run(gelu)

    @pl.when(((j >= cols.bb) & (j < cols.cq)) | ((j >= cols.cv) & (j < cols.gate)))
    def _():
        def plain(acc, rows):
            o_ref[0, rows, :] = acc.astype(BF16)
        run(plain)

    @pl.when((j >= cols.cq) & (j < cols.cv))
    def _():
        scale = jnp.where(j < cols.ck, qscale, 1.0).astype(F32)

        def rotary(acc, rows):
            if not rope:
                o_ref[0, rows, :] = (acc * scale).astype(BF16)
                return
            cos, sa, sb = cos_ref[rows, :] * scale, sa_ref[rows, :] * scale, sb_ref[rows, :] * scale
            for s in range(COL_BLOCK // LANES):
                t = acc[:, s * LANES:(s + 1) * LANES]
                r = (t * cos + pltpu.roll(t, LANES - ROPE_PAIR, 1) * sa
                     + pltpu.roll(t, ROPE_PAIR, 1) * sb)
                o_ref[0, rows, s * LANES:(s + 1) * LANES] = r.astype(BF16)
        run(rotary)

    @pl.when(j >= cols.gate)
    def _():
        def gate(acc, rows):
            o_ref[0, rows, :] = _sigmoid(acc + bg_ref[...]).astype(BF16)
        run(gate)


def _in_call(h, w, b_gate, tables, *, j0, nj, rope):
    nb, n, d = h.shape
    cols = _Cols(d)
    tm = min(IN_TILE, n)
    full = (j0 == 0) and (nj == cols.end)
    if rope:
        cos, sa, sb = tables
    else:
        cos = sa = sb = jnp.zeros((tm, LANES), F32)
    tab = pl.BlockSpec((tm, LANES), lambda b, i, j: (i if rope else 0, 0))

    def out_idx(b, i, j):
        if full:
            return (b, i, jnp.where(j >= cols.gate, j - cols.gate, j + cols.end - cols.gate))
        return (b, i, j)
    kern = functools.partial(_in_kernel, cols=cols, j0=j0, rope=rope,
                             qscale=C_HEAD_DIM ** -0.5 * LOG2E, sub=min(IN_SUB, tm))
    return pl.pallas_call(
        kern,
        grid=(nb, n // tm, nj),
        in_specs=[
            pl.BlockSpec((1, tm, d), lambda b, i, j: (b, i, 0)),
            pl.BlockSpec((d, COL_BLOCK), lambda b, i, j: (0, j + j0)),
            pl.BlockSpec((1, COL_BLOCK), lambda b, i, j: (0, jnp.maximum(j + j0 - cols.gate, 0))),
            tab, tab, tab,
        ],
        out_specs=pl.BlockSpec((1, tm, COL_BLOCK), out_idx),
        out_shape=jax.ShapeDtypeStruct((nb, n, nj * COL_BLOCK), BF16),
        compiler_params=_cparams(("parallel", "parallel", "arbitrary")),
        name="in_proj",
    )(h, w, b_gate.reshape(1, -1), cos, sa, sb)


def _rope_tables(n):
    rows = n // GRID_W
    r = jnp.repeat(jnp.arange(rows, dtype=F32), GRID_W)
    col = jnp.tile(jnp.arange(GRID_W, dtype=F32), rows)
    quarter = C_HEAD_DIM // 4
    inv = ROPE_BASE ** (-jnp.arange(quarter, dtype=F32) / quarter)
    ar = r[:, None] * inv
    ac = col[:, None] * inv
    ang = jnp.concatenate([ar, ar, ac, ac], axis=-1)
    ang = jnp.tile(ang, (1, LANES // C_HEAD_DIM))
    cos, sin = jnp.cos(ang), jnp.sin(ang)
    first_of_pair = (jnp.arange(LANES) // quarter) % 2 == 0
    sa = jnp.where(first_of_pair, -sin, 0.0)
    sb = jnp.where(first_of_pair, 0.0, sin)
    return cos, sa, sb


def _local_mixers(u_ref, v_ref, bg_ref, cg_ref, hh_ref, cgp_ref, hhp_ref, cgn_ref, hhn_ref,
                  lng_ref, lnb_ref, ws_ref, bias_ref, cw_ref, ya_ref, yb_ref, *, tm, seq):
    i = pl.program_id(1)
    lane = lax.broadcasted_iota(jnp.int32, (CHUNK, LANES), 1)
    lo = lane < A_GROUP_DIM
    for c in range(tm // CHUNK):
        rows = pl.ds(c * CHUNK, CHUNK)
        v = v_ref[0, rows, :].astype(F32)
        mu = jnp.mean(v, axis=-1, keepdims=True)
        var = jnp.mean(jnp.square(v - mu), axis=-1, keepdims=True)
        vn = ((v - mu) * lax.rsqrt(var + EPS) * lng_ref[...] + lnb_ref[...]).astype(BF16)
        for k in range(A_WIDTH // LANES):
            blk = vn[:, k * LANES:(k + 1) * LANES]
            zero = jnp.zeros_like(blk)
            mixed = (jnp.dot(ws_ref[2 * k], jnp.where(lo, blk, zero), preferred_element_type=F32)
                     + jnp.dot(ws_ref[2 * k + 1], jnp.where(lo, zero, blk), preferred_element_type=F32)
                     + bias_ref[:, k * LANES:(k + 1) * LANES])
            u = u_ref[0, rows, k * LANES:(k + 1) * LANES].astype(F32)
            ya_ref[rows, k * LANES:(k + 1) * LANES] = (u * mixed).astype(BF16)

    z = cg_ref[0].astype(F32) * hh_ref[0].astype(F32)
    last = BF16_SUBLANES - 1
    z_prev = cgp_ref[0, last:last + 1, :].astype(F32) * hhp_ref[0, last:last + 1, :].astype(F32)
    z_next = cgn_ref[0, 0:1, :].astype(F32) * hhn_ref[0, 0:1, :].astype(F32)
    row = lax.broadcasted_iota(jnp.int32, (tm, 1), 0)
    pos = lax.rem(i * tm + row, seq)
    zm1 = jnp.where(row == 0, z_prev, pltpu.roll(z, 1, 0))
    zm1 = jnp.where(pos == 0, 0.0, zm1)
    zp1 = jnp.where(row == tm - 1, z_next, pltpu.roll(z, tm - 1, 0))
    zp1 = jnp.where(pos == seq - 1, 0.0, zp1)
    conv = cw_ref[0:1, :] * zm1 + cw_ref[1:2, :] * z + cw_ref[2:3, :] * zp1
    yb_ref[...] = (bg_ref[0].astype(F32) * conv).astype(BF16)


def _attn_kernel(*refs, tq, ck, sizes, lam_init):
    nsrc = len(sizes)
    cl_ref, g_ref, q_ref = refs[:3]
    kv_refs = refs[3:3 + 2 * nsrc]
    o_ref = refs[3 + 2 * nsrc]
    vt_refs = refs[4 + 2 * nsrc:4 + 3 * nsrc]
    kmax_ref = refs[4 + 3 * nsrc]
    half = lax.broadcasted_iota(jnp.int32, (1, LANES), 1) < C_HEAD_DIM

    @pl.when(pl.program_id(2) == 0)
    def _():
        per_map = jnp.where(lax.broadcasted_iota(jnp.int32, (LANES, LANES), 0) // C_HEAD_DIM
                            == lax.broadcasted_iota(jnp.int32, (LANES, LANES), 1), 1.0, 0.0)
        kmax2 = jnp.zeros((1, LANES), F32)
        for src in range(nsrc):
            vt_refs[src][...] = kv_refs[2 * src + 1][0].astype(F32).T.astype(BF16)
            k = kv_refs[2 * src][0]
            n2 = jnp.dot(k * k, per_map.astype(BF16), preferred_element_type=F32)
            kmax2 = jnp.maximum(kmax2, jnp.max(n2, axis=0, keepdims=True))
        kmax_ref[0:1, :] = jnp.sqrt(kmax2) * (1.0 + 2.0 ** -7)

    q = q_ref[0]
    zero = jnp.zeros_like(q)
    qs = [jnp.where(half, q, zero), jnp.where(half, zero, q)]
    chunks = [(kv_refs[2 * src], vt_refs[src], c0, min(ck, nk))
              for src, nk in enumerate(sizes) for c0 in range(0, nk, min(ck, nk))]

    def scores(chunk, qm):
        k_ref, _, c0, c = chunk
        return lax.dot_general(k_ref[0, c0:c0 + c, :], qm, (((1,), (1,)), ((), ())),
                               preferred_element_type=F32)

    def finish(accs, ls):
        cl = cl_ref[...]
        lam = (jnp.exp(jnp.sum(cl[0:1] * cl[1:2], axis=-1, keepdims=True))
               - jnp.exp(jnp.sum(cl[2:3] * cl[3:4], axis=-1, keepdims=True)) + lam_init)
        o = accs[0] / ls[0] - lam * (accs[1] / ls[1])
        ms = jnp.mean(o * o, axis=0, keepdims=True)
        y = o * lax.rsqrt(ms + EPS) * (g_ref[...] * (1.0 - lam_init))
        o_ref[0] = y.T.astype(BF16)

    qsq = jnp.square(q.astype(F32).T)
    refs_r = [jnp.sqrt(jnp.sum(qsq[j * C_HEAD_DIM:(j + 1) * C_HEAD_DIM], axis=0, keepdims=True))
              * kmax_ref[0:1, j:j + 1] for j in range(2)]
    ls = [jnp.zeros((1, tq), F32) for _ in qs]
    accs = [jnp.zeros((C_V_DIM, tq), F32) for _ in qs]
    s_next = [scores(chunks[0], qm) for qm in qs]
    for t, (_, vt_ref, c0, c) in enumerate(chunks):
        s_cur = s_next
        if t + 1 < len(chunks):
            s_next = [scores(chunks[t + 1], qm) for qm in qs]
        vt = vt_ref[:, c0:c0 + c]
        for j, s in enumerate(s_cur):
            p = jnp.exp2(s - refs_r[j])
            ls[j] = ls[j] + jnp.sum(p, axis=0, keepdims=True)
            accs[j] = accs[j] + jnp.dot(vt, p.astype(BF16), preferred_element_type=F32)
    finish(accs, ls)

    healthy = jnp.min(jnp.minimum(ls[0], ls[1])) >= 2.0 ** -60

    @pl.when(jnp.logical_not(healthy))
    def _():
        ms = [jnp.full((1, tq), -jnp.inf, F32) for _ in qs]
        ls = [jnp.zeros((1, tq), F32) for _ in qs]
        accs = [jnp.zeros((C_V_DIM, tq), F32) for _ in qs]
        for chunk in chunks:
            _, vt_ref, c0, c = chunk
            vt = vt_ref[:, c0:c0 + c]
            for j, qm in enumerate(qs):
                s = scores(chunk, qm)
                m_new = jnp.maximum(ms[j], jnp.max(s, axis=0, keepdims=True))
                alpha = jnp.exp2(ms[j] - m_new)
                p = jnp.exp2(s - m_new)
                ls[j] = alpha * ls[j] + jnp.sum(p, axis=0, keepdims=True)
                accs[j] = alpha * accs[j] + jnp.dot(vt, p.astype(BF16),
                                                    preferred_element_type=F32)
                ms[j] = m_new
        finish(accs, ls)


def _attn_call(q_arr, q_cb, sources, c_lambda, subln_g, *, lam_init):
    nb, n, _ = q_arr.shape
    tq = min(ATTN_Q_TILE, n)
    sizes = tuple(a.shape[1] for a, _, _ in sources)
    in_specs = [
        pl.BlockSpec((4, C_HEAD_DIM), lambda b, h, i: (0, 0)),
        pl.BlockSpec((C_V_DIM, 1), lambda b, h, i: (0, 0)),
        pl.BlockSpec((1, tq, LANES), lambda b, h, i: (b, i, q_cb + h)),
    ]
    args = [c_lambda, subln_g.reshape(-1, 1), q_arr]
    for arr, kcb, vcb in sources:
        nk = arr.shape[1]
        in_specs.append(pl.BlockSpec((1, nk, LANES), lambda b, h, i, kcb=kcb: (b, 0, kcb + h)))
        in_specs.append(pl.BlockSpec((1, nk, LANES), lambda b, h, i, vcb=vcb: (b, 0, vcb + h)))
        args += [arr, arr]
    kern = functools.partial(_attn_kernel, tq=tq, ck=ATTN_KEYS, sizes=sizes, lam_init=lam_init)
    return pl.pallas_call(
        kern,
        grid=(nb, C_HEADS, n // tq),
        in_specs=in_specs,
        out_specs=pl.BlockSpec((1, tq, C_V_DIM), lambda b, h, i: (b, i, h)),
        out_shape=jax.ShapeDtypeStruct((nb, n, C_HEADS * C_V_DIM), BF16),
        scratch_shapes=[pltpu.VMEM((C_V_DIM, nk), BF16) for nk in sizes]
        + [pltpu.VMEM((8, LANES), F32)],
        compiler_params=_cparams(("parallel", "parallel", "arbitrary")),
        name="diff_attention",
    )(*args)


def _merge_kernel(*refs, tm, seq):
    mixer_refs = refs[:14]
    (x_ref, yc_ref, ga_ref, gb_ref, gc_ref, wa_ref, wb_ref, wc_ref, wo_ref, g1_ref, n2_ref,
     sh2_ref, sc2_ref, xo_ref, ho_ref, ya_ref, yb_ref) = refs[14:]
    c = jnp.dot(yc_ref[0], wc_ref[...], preferred_element_type=F32)
    _local_mixers(*mixer_refs, ya_ref, yb_ref, tm=tm, seq=seq)
    a = jnp.dot(ya_ref[...], wa_ref[...], preferred_element_type=F32)
    b = jnp.dot(yb_ref[...], wb_ref[...], preferred_element_type=F32)
    y = (ga_ref[0].astype(F32) * a + gb_ref[0].astype(F32) * b + gc_ref[0].astype(F32) * c)
    m = jnp.dot(y.astype(BF16), wo_ref[...], preferred_element_type=F32)
    xn = x_ref[0] + g1_ref[0] * m
    xo_ref[0] = xn
    ho_ref[0] = _modulate(xn, n2_ref[...], sh2_ref[0], sc2_ref[0]).astype(BF16)


def _merge_call(x, p, pos, yc, gate_cb, mixer_w, wa, wb, wc, wo, g1, n2g, sh2, sc2, *, seq):
    nb, n, d = x.shape
    au, av, bb, bc, bh = pos
    ln_g, ln_b, ws, bs, conv_w = mixer_w
    tm = min(MERGE_TILE, n)
    hb = tm // BF16_SUBLANES
    nhb = n // BF16_SUBLANES

    def tok(w, cb=0):
        return pl.BlockSpec((1, tm, w), lambda b, i: (b, i, cb))

    def prev(cb):
        return pl.BlockSpec((1, BF16_SUBLANES, COL_BLOCK),
                            lambda b, i: (b, jnp.maximum(i * hb - 1, 0), cb))

    def nxt(cb):
        return pl.BlockSpec((1, BF16_SUBLANES, COL_BLOCK),
                            lambda b, i: (b, jnp.minimum((i + 1) * hb, nhb - 1), cb))

    const = lambda shape: pl.BlockSpec(shape, lambda b, i: (0,) * len(shape))
    bias = jnp.repeat(bs.T, A_GROUP_DIM, axis=1)
    vec = pl.BlockSpec((1, 1, d), lambda b, i: (b, 0, 0))
    return pl.pallas_call(
        functools.partial(_merge_kernel, tm=tm, seq=seq),
        grid=(nb, n // tm),
        in_specs=[tok(COL_BLOCK, au), tok(COL_BLOCK, av), tok(COL_BLOCK, bb), tok(COL_BLOCK, bc),
                  tok(COL_BLOCK, bh), prev(bc), prev(bh), nxt(bc), nxt(bh),
                  const((1, A_WIDTH)), const((1, A_WIDTH)), const((A_GROUPS, CHUNK, CHUNK)),
                  const((CHUNK, A_WIDTH)), const((CONV_W, B_WIDTH)),
                  tok(d), tok(d), tok(d, gate_cb), tok(d, gate_cb + 1), tok(d, gate_cb + 2),
                  _resident(wa.shape, lambda b, i: (0, 0)), _resident(wb.shape, lambda b, i: (0, 0)),
                  _resident(wc.shape, lambda b, i: (0, 0)), _resident(wo.shape, lambda b, i: (0, 0)),
                  vec, const((1, d)), vec, vec],
        out_specs=[tok(d), tok(d)],
        out_shape=[jax.ShapeDtypeStruct((nb, n, d), F32), jax.ShapeDtypeStruct((nb, n, d), BF16)],
        scratch_shapes=[pltpu.VMEM((tm, A_WIDTH), BF16), pltpu.VMEM((tm, B_WIDTH), BF16)],
        compiler_params=_cparams(("parallel", "parallel")),
        name="merge_branches",
    )(p, p, p, p, p, p, p, p, p, ln_g.reshape(1, -1), ln_b.reshape(1, -1), ws.astype(BF16), bias,
      conv_w, x, yc, p, p, p, wa, wb, wc, wo, g1, n2g.reshape(1, d), sh2, sc2)


def _ff_chunks(width, chunk=FFN_CHUNK):
    out, c0 = [], 0
    while c0 < width:
        cw = min(chunk, width - c0)
        out.append((c0, cw))
        c0 += cw
    return out


def _finish(xn, final, ng_ref, sh_ref, sc_ref, out_refs):
    if final:
        out_refs[0][0] = _rmsnorm(xn, ng_ref[...])
    else:
        out_refs[0][0] = xn
        out_refs[1][0] = _modulate(xn, ng_ref[...], sh_ref[0], sc_ref[0]).astype(BF16)


def _ffn_kernel(x_ref, h_ref, wg_ref, wu_ref, wd_ref, g2_ref, ng_ref, sh_ref, sc_ref, *out_refs,
                final):
    h = h_ref[0]
    acc = None
    for c0, cw in _ff_chunks(wg_ref.shape[1]):
        a = jnp.dot(h, wg_ref[:, c0:c0 + cw], preferred_element_type=F32)
        b = jnp.dot(h, wu_ref[:, c0:c0 + cw], preferred_element_type=F32)
        t = (_silu(a) * b).astype(BF16)
        part = jnp.dot(t, wd_ref[c0:c0 + cw, :], preferred_element_type=F32)
        acc = part if acc is None else acc + part
    xn = x_ref[0] + g2_ref[0] * acc
    _finish(xn, final, ng_ref, sh_ref, sc_ref, out_refs)


def _epilogue_specs(nb, n, d, tm, final, idx):
    tok = pl.BlockSpec((1, tm, d), idx)
    if final:
        return [tok], [jax.ShapeDtypeStruct((nb, n, d), F32)]
    return [tok, tok], [jax.ShapeDtypeStruct((nb, n, d), F32), jax.ShapeDtypeStruct((nb, n, d), BF16)]


def _ffn_call(x, h, wg, wu, wd, g2, ng, sh, sc, *, final):
    nb, n, d = x.shape
    tm = min(FFN_TILE, n)
    idx = lambda b, i: (b, i, 0)
    vec = pl.BlockSpec((1, 1, d), lambda b, i: (b, 0, 0))
    out_specs, out_shape = _epilogue_specs(nb, n, d, tm, final, idx)
    return pl.pallas_call(
        functools.partial(_ffn_kernel, final=final),
        grid=(nb, n // tm),
        in_specs=[pl.BlockSpec((1, tm, d), idx), pl.BlockSpec((1, tm, d), idx),
                  _resident(wg.shape, lambda b, i: (0, 0)), _resident(wu.shape, lambda b, i: (0, 0)),
                  _resident(wd.shape, lambda b, i: (0, 0)),
                  vec, pl.BlockSpec((1, d), lambda b, i: (0, 0)), vec, vec],
        out_specs=out_specs,
        out_shape=out_shape,
        compiler_params=_cparams(("parallel", "parallel")),
        name="dense_swiglu",
    )(x, h, wg, wu, wd, g2, ng.reshape(1, d), sh, sc)


MOE_TILE = 2048
MOE_WIN = 256
MOE_SLAB = 128
MOE_ROWS = 512
MOE_FF = 512
CNT_ROWS = 16


def _route_kernel(h_ref, wr_ref, comb_ref, rank_ref, rankt_ref, cnt_ref, *, n_experts, win):
    h = h_ref[0]
    tm = h.shape[0]
    lane = lax.broadcasted_iota(jnp.int32, (tm, LANES), 1)
    logits = jnp.dot(h, wr_ref[...], preferred_element_type=F32)
    lg = jnp.where(lane < n_experts, logits, -jnp.inf)
    m1 = jnp.max(lg, axis=-1, keepdims=True)
    i1 = jnp.min(jnp.where(lg == m1, lane, LANES), axis=-1, keepdims=True)
    lg2 = jnp.where(lane == i1, -jnp.inf, lg)
    m2 = jnp.max(lg2, axis=-1, keepdims=True)
    i2 = jnp.min(jnp.where(lg2 == m2, lane, LANES), axis=-1, keepdims=True)
    e2 = jnp.exp(m2 - m1)
    w1 = 1.0 / (1.0 + e2)
    comb_ref[0] = jnp.where(lane == i1, w1, 0.0) + jnp.where(lane == i2, e2 * w1, 0.0)
    sel = (lane == i1) | (lane == i2)
    tri = jnp.where(lax.broadcasted_iota(jnp.int32, (win, win), 0)
                    > lax.broadcasted_iota(jnp.int32, (win, win), 1), 1.0, 0.0).astype(BF16)
    base = jnp.zeros((1, LANES), F32)
    bases = []
    for w in range(tm // win):
        rows = slice(w * win, (w + 1) * win)
        sw = jnp.where(sel[rows], 1.0, 0.0)
        excl = jnp.dot(tri, sw.astype(BF16), preferred_element_type=F32)
        rank_ref[0, rows, :] = jnp.where(sel[rows], excl + base, -1.0)
        bases.append(base)
        base = base + jnp.sum(sw, axis=0, keepdims=True)
    bases.append(base)
    bases += [jnp.zeros((1, LANES), F32)] * (CNT_ROWS - len(bases))
    cnt_ref[0] = jnp.concatenate(bases, axis=0).astype(jnp.int32)
    rankt_ref[0] = rank_ref[0].T[:CNT_ROWS]


def _route_call(h, w_router, tm):
    nb, n, d = h.shape
    n_experts = w_router.shape[1]
    assert n_experts <= CNT_ROWS and tm // MOE_WIN + 1 <= CNT_ROWS
    nt = n // tm
    wr = jnp.zeros((d, LANES), BF16).at[:, :n_experts].set(w_router.astype(BF16))
    tok = pl.BlockSpec((1, tm, LANES), lambda b, i: (b, i, 0))
    per_tile = lambda r, c: pl.BlockSpec((1, r, c), lambda b, i: (b * nt + i, 0, 0))
    return pl.pallas_call(
        functools.partial(_route_kernel, n_experts=n_experts, win=min(MOE_WIN, tm)),
        grid=(nb, nt),
        in_specs=[pl.BlockSpec((1, tm, d), lambda b, i: (b, i, 0)),
                  pl.BlockSpec((d, LANES), lambda b, i: (0, 0))],
        out_specs=[tok, tok, per_tile(CNT_ROWS, tm), per_tile(CNT_ROWS, LANES)],
        out_shape=[jax.ShapeDtypeStruct((nb, n, LANES), F32), jax.ShapeDtypeStruct((nb, n, LANES), F32),
                   jax.ShapeDtypeStruct((nb * nt, CNT_ROWS, tm), F32),
                   jax.ShapeDtypeStruct((nb * nt, CNT_ROWS, LANES), jnp.int32)],
        compiler_params=_cparams(("parallel", "parallel")),
        name="moe_route",
    )(h, wr)


def _moe_kernel(cnt_ref, x_ref, h_ref, comb_ref, rank_ref, rankt_ref, wg_hbm, wu_hbm, wd_hbm, g2_ref,
                ng_ref, sh_ref, sc_ref, *rest, final, n_experts, win, slab, rblk, ff):
    out_refs, (hc_ref, yc_ref, wg_buf, wu_buf, wd_buf, sem) = rest[:-6], rest[-6:]
    acc_ref = out_refs[0].at[0]
    b, i, e = (pl.program_id(k) for k in range(3))
    tm, d = h_ref.shape[1], h_ref.shape[2]
    nw = tm // win
    nf = wg_hbm.shape[2] // ff
    tile = b * pl.num_programs(1) + i
    cbase = tile * (CNT_ROWS * n_experts)
    step = tile * n_experts + e
    last_step = pl.num_programs(0) * pl.num_programs(1) * n_experts - 1

    def weight_copies(expert, f, slot):
        cols = pl.ds(pl.multiple_of(f * ff, ff), ff)
        return (pltpu.make_async_copy(wg_hbm.at[expert, :, cols], wg_buf.at[slot], sem.at[slot, 0]),
                pltpu.make_async_copy(wu_hbm.at[expert, :, cols], wu_buf.at[slot], sem.at[slot, 1]),
                pltpu.make_async_copy(wd_hbm.at[expert, cols, :], wd_buf.at[slot], sem.at[slot, 2]))

    def fetch(expert, f, slot):
        for copy in weight_copies(expert, f, slot):
            copy.start()

    @pl.when(step == 0)
    def _():
        fetch(e, 0, 0)

    def count(w):
        return cnt_ref[cbase + w * n_experts + e]

    def slabs(w):
        row0 = (count(w) // BF16_SUBLANES) * BF16_SUBLANES
        return row0, (count(w + 1) - row0 + slab - 1) // slab

    @pl.when(e == 0)
    def _():
        acc_ref[...] = jnp.zeros_like(acc_ref)

    total = count(nw)

    def zero(s, carry):
        rows = pl.ds(pl.multiple_of(s * slab, slab), slab)
        hc_ref[rows, :] = jnp.zeros((slab, d), BF16)
        yc_ref[rows, :] = jnp.zeros((slab, d), F32)
        return carry

    lax.fori_loop(0, jnp.minimum((total + rblk + 2 * slab) // slab, hc_ref.shape[0] // slab),
                  zero, 0)

    def gather(w):
        row0, nsl = slabs(w)
        rt = rankt_ref[0, 0, :, w * win:(w + 1) * win]
        hw = h_ref[0, w * win:(w + 1) * win, :]

        def body(s, carry):
            r0 = pl.multiple_of(row0 + s * slab, BF16_SUBLANES)
            rid = (r0 + lax.broadcasted_iota(jnp.int32, (slab, win), 0)).astype(F32)
            onehot = jnp.where(rt == rid, 1.0, 0.0).astype(BF16)
            rows = pl.ds(r0, slab)
            got = jnp.dot(onehot, hw, preferred_element_type=F32)
            hc_ref[rows, :] = (hc_ref[rows, :].astype(F32) + got).astype(BF16)
            return carry

        return body, nsl

    for w in range(nw):
        gather(w)[0](0, 0)
    for w in range(nw):
        body, nsl = gather(w)
        lax.fori_loop(1, nsl, body, 0)

    quarters = (total + rblk // 4 - 1) // (rblk // 4)
    nfull = (quarters + 1) // 4
    rest = jnp.maximum(quarters - 4 * nfull, 0)

    def ff_block(f, carry):
        slot = lax.rem(step * nf + f, 2)

        @pl.when(f + 1 < nf)
        def _():
            fetch(e, f + 1, 1 - slot)

        @pl.when((f + 1 == nf) & (step < last_step))
        def _():
            fetch(lax.rem(e + 1, n_experts), 0, 1 - slot)

        for copy in weight_copies(e, f, slot):
            copy.wait()

        def ffn(r0, size):
            rows = pl.ds(pl.multiple_of(r0, rblk // 4), size)
            hb = hc_ref[rows, :]
            a = jnp.dot(hb, wg_buf[slot], preferred_element_type=F32)
            u = jnp.dot(hb, wu_buf[slot], preferred_element_type=F32)
            t = (_silu(a) * u).astype(BF16)
            yc_ref[rows, :] += jnp.dot(t, wd_buf[slot], preferred_element_type=F32)

        def full(r, c):
            ffn(r * rblk, rblk)
            return c

        lax.fori_loop(0, nfull, full, 0)

        @pl.when(rest == 2)
        def _():
            ffn(nfull * rblk, rblk // 2)

        @pl.when(rest == 1)
        def _():
            ffn(nfull * rblk, rblk // 4)

        return carry

    lax.fori_loop(0, nf, ff_block, 0)

    def spread(w):
        row0, nsl = slabs(w)
        trows = slice(w * win, (w + 1) * win)
        mine = lax.broadcasted_iota(jnp.int32, (win, LANES), 1) == e
        rcol = jnp.sum(jnp.where(mine, rank_ref[0, trows, :], 0.0), axis=-1, keepdims=True)
        wcol = jnp.sum(jnp.where(mine, comb_ref[0, trows, :], 0.0), axis=-1, keepdims=True)

        def body(s, carry):
            r0 = pl.multiple_of(row0 + s * slab, BF16_SUBLANES)
            cid = (r0 + lax.broadcasted_iota(jnp.int32, (win, slab), 1)).astype(F32)
            onehot = jnp.where(rcol == cid, 1.0, 0.0).astype(BF16)
            yb = yc_ref[pl.ds(r0, slab), :].astype(BF16)
            acc_ref[trows, :] += wcol * jnp.dot(onehot, yb, preferred_element_type=F32)
            return carry

        return body, nsl

    bodies = [spread(w) for w in range(nw)]
    for body, _ in bodies:
        body(0, 0)
    for body, nsl in bodies:
        lax.fori_loop(1, nsl, body, 0)

    @pl.when(e == n_experts - 1)
    def _():
        xn = x_ref[0] + g2_ref[0] * acc_ref[...]
        _finish(xn, final, ng_ref, sh_ref, sc_ref, out_refs)


def _moe_call(x, h, w_router, wg, wu, wd, g2, ng, sh, sc, *, final):
    nb, n, d = x.shape
    n_experts, _, dff = wg.shape
    tm = min(MOE_TILE, n)
    win = min(MOE_WIN, tm)
    nt = n // tm
    comb, rank, rankt, cnt = _route_call(h, w_router, tm)
    cnt = cnt[:, :, :n_experts].reshape(-1)
    rankt = rankt.reshape(nb * nt, CNT_ROWS, 1, tm)
    cap_ffn = -(-tm // MOE_ROWS) * MOE_ROWS
    cap = -(-max(cap_ffn, tm + MOE_SLAB) // MOE_SLAB) * MOE_SLAB

    idx = lambda b, i, e, c: (b, i, 0)
    vec = pl.BlockSpec((1, 1, d), lambda b, i, e, c: (b, 0, 0))
    hbm = pl.BlockSpec(memory_space=pl.ANY)
    out_specs, out_shape = _epilogue_specs(nb, n, d, tm, final, idx)
    grid_spec = pltpu.PrefetchScalarGridSpec(
        num_scalar_prefetch=1,
        grid=(nb, nt, n_experts),
        in_specs=[_resident((1, tm, d), idx), _resident((1, tm, d), idx),
                  _resident((1, tm, LANES), idx), _resident((1, tm, LANES), idx),
                  pl.BlockSpec((1, 1, 1, tm), lambda b, i, e, c: (b * nt + i, e, 0, 0)),
                  hbm, hbm, hbm,
                  vec, pl.BlockSpec((1, d), lambda b, i, e, c: (0, 0)), vec, vec],
        out_specs=out_specs,
        scratch_shapes=[pltpu.VMEM((cap, d), BF16), pltpu.VMEM((cap, d), F32),
                        pltpu.VMEM((2, d, MOE_FF), BF16), pltpu.VMEM((2, d, MOE_FF), BF16),
                        pltpu.VMEM((2, MOE_FF, d), BF16), pltpu.SemaphoreType.DMA((2, 3))],
    )
    return pl.pallas_call(
        functools.partial(_moe_kernel, final=final, n_experts=n_experts, win=win, slab=MOE_SLAB,
                          rblk=MOE_ROWS, ff=MOE_FF),
        grid_spec=grid_spec,
        out_shape=out_shape,
        compiler_params=_cparams(("arbitrary", "arbitrary", "arbitrary")),
        name="moe_swiglu",
    )(cnt, x, h, comb, rank, rankt, wg, wu, wd, g2, ng.reshape(1, d), sh, sc)


def kernel(x, c, ctx, c_ctx, w_ada, b_ada, norm1_g, norm2_g, w_in, b_gate, a_ln_g, a_ln_b, a_ws,
           a_bs, b_conv, c_lambda, c_subln_g, w_a_out, w_b_out, w_c_out, w_o, ff_w_gate, ff_w_up,
           ff_w_down, moe_w_router, moe_w_gate, moe_w_up, moe_w_down, final_norm_g):
    bsz, n, d = x.shape
    nc = ctx.shape[1]
    depth = w_ada.shape[0]
    cols = _Cols(d)
    head_cb = COL_BLOCK // LANES
    gate_cb = cols.out(cols.gate) * COL_BLOCK // d
    mix_pos = tuple(cols.out(j) for j in (cols.au, cols.av, cols.bb, cols.bc, cols.bh))
    cq_l, ck_l, cv_l = (cols.out(j) * head_cb for j in (cols.cq, cols.ck, cols.cv))

    pad = (-(bsz + 1)) % 8
    cc = jnp.concatenate([c, c_ctx[None], jnp.zeros((pad, d), F32)], axis=0)
    mod = _ada_call(cc, w_ada, b_ada)

    def mods(l):
        lat = [mod[l, :bsz, k * d:(k + 1) * d].reshape(bsz, 1, d) for k in range(N_MOD)]
        con = [mod[l, bsz:bsz + 1, k * d:(k + 1) * d].reshape(1, 1, d) for k in range(N_MOD)]
        return lat, con

    tables = _rope_tables(n)
    w_in_b = w_in.astype(BF16)
    xl = x
    xc = ctx.reshape(1, bsz * nc, d)
    lat, con = mods(0)
    h = _mod_call(xl, norm1_g[0], lat[0], lat[1])
    hc = _mod_call(xc, norm1_g[0], con[0], con[1])

    for l in range(depth):
        last = l == depth - 1
        lam_init = 0.8 - 0.6 * math.exp(-0.3 * l)
        lat, con = mods(l)
        if not last:
            nlat, ncon = mods(l + 1)
            nxt_l = (norm1_g[l + 1], nlat[0], nlat[1])
            nxt_c = (norm1_g[l + 1], ncon[0], ncon[1])
        else:
            nxt_l = (final_norm_g, lat[0], lat[1])
            nxt_c = None

        p = _in_call(h, w_in_b[l], b_gate[l], tables, j0=0, nj=cols.end, rope=True)
        if last:
            pc = _in_call(hc, w_in_b[l], b_gate[l], tables, j0=cols.ck, nj=cols.gate - cols.ck,
                          rope=False)
            ck_c, cv_c = 0, (cols.cv - cols.ck) * head_cb
        else:
            pc = _in_call(hc, w_in_b[l], b_gate[l], tables, j0=0, nj=cols.end, rope=False)
            ck_c, cv_c = ck_l, cv_l
        pc_seq = pc.reshape(bsz, nc, -1)

        wa, wb, wc, wo = (w.astype(BF16) for w in (w_a_out[l], w_b_out[l], w_c_out[l], w_o[l]))

        def channel(xs, hs, g2, nxt, final):
            i = l // 2
            if l % 2 == 0:
                return _ffn_call(xs, hs, ff_w_gate[i].astype(BF16), ff_w_up[i].astype(BF16),
                                 ff_w_down[i].astype(BF16), g2, *nxt, final=final)
            return _moe_call(xs, hs, moe_w_router[i], moe_w_gate[i].astype(BF16),
                             moe_w_up[i].astype(BF16), moe_w_down[i].astype(BF16), g2, *nxt,
                             final=final)

        mixer_w = (a_ln_g[l], a_ln_b[l], a_ws[l], a_bs[l], b_conv[l])
        yc = _attn_call(p, cq_l, [(p, ck_l, cv_l), (pc_seq, ck_c, cv_c)],
                        c_lambda[l], c_subln_g[l], lam_init=lam_init)
        x1, h2 = _merge_call(xl, p, mix_pos, yc, gate_cb, mixer_w, wa, wb, wc, wo, lat[2],
                             norm2_g[l], lat[3], lat[4], seq=n)
        res = channel(x1, h2, lat[5], nxt_l, last)
        if last:
            return res[0]
        xl, h = res

        yc_c = _attn_call(pc_seq, cq_l, [(pc_seq, ck_c, cv_c)], c_lambda[l], c_subln_g[l],
                          lam_init=lam_init)
        xc1, hc2 = _merge_call(xc, pc, mix_pos, yc_c.reshape(1, bsz * nc, -1), gate_cb, mixer_w,
                               wa, wb, wc, wo, con[2], norm2_g[l], con[3], con[4], seq=nc)
        xc, hc = channel(xc1, hc2, con[5], nxt_c, False)
```

```python
import functools
import math

import jax
import jax.numpy as jnp
from jax import lax
from jax.experimental import pallas as pl
from jax.experimental.pallas import tpu as pltpu

F32 = jnp.float32
BF16 = jnp.bfloat16

EPS = 1e-6
GRID_W = 64
N_MOD = 6
N_BRANCH = 3
CHUNK = 128
A_GROUPS = 8
A_GROUP_DIM = 64
A_WIDTH = A_GROUPS * A_GROUP_DIM
B_WIDTH = 512
CONV_W = 3
C_HEADS = 8
C_HEAD_DIM = 64
C_V_DIM = 2 * C_HEAD_DIM
ROPE_BASE = 10000.0
ROPE_PAIR = C_HEAD_DIM // 4
LOG2E = 1.4426950408889634

LANES = 128
BF16_SUBLANES = 16
V7X_VMEM_LIMIT = 56 * 1024 * 1024

COL_BLOCK = 512
ADA_COLS = 1536
MOD_TILE = 1024
IN_TILE, IN_SUB = 4096, 512
ATTN_Q_TILE, ATTN_KEYS = 512, 2048
MERGE_TILE = 512
FFN_TILE, FFN_CHUNK = 512, 512


def _cparams(sem):
    return pltpu.CompilerParams(dimension_semantics=sem, vmem_limit_bytes=V7X_VMEM_LIMIT)


def _resident(shape, index_map):
    return pl.BlockSpec(shape, index_map, pipeline_mode=pl.Buffered(1))


def _sigmoid(v):
    return 0.5 * (1.0 + jnp.tanh(0.5 * v))


def _silu(v):
    return v * _sigmoid(v)


def _modulate(x, g, sh, sc):
    ms = jnp.mean(x * x, axis=-1, keepdims=True)
    y = x * lax.rsqrt(ms + EPS)
    return (y * g) * (1.0 + sc) + sh


def _rmsnorm(x, g):
    ms = jnp.mean(x * x, axis=-1, keepdims=True)
    return x * lax.rsqrt(ms + EPS) * g


def _ada_kernel(c_ref, w_ref, b_ref, o_ref):
    s = _silu(c_ref[...])
    o_ref[0] = jnp.dot(s, w_ref[0], preferred_element_type=F32,
                       precision=lax.Precision.HIGHEST) + b_ref[0]


def _ada_call(cc, w_ada, b_ada):
    depth, d, cols = w_ada.shape
    rows = cc.shape[0]
    tn = ADA_COLS
    return pl.pallas_call(
        _ada_kernel,
        grid=(depth, cols // tn),
        in_specs=[
            pl.BlockSpec((rows, d), lambda l, j: (0, 0)),
            pl.BlockSpec((1, d, tn), lambda l, j: (l, 0, j)),
            pl.BlockSpec((1, 1, tn), lambda l, j: (l, 0, j)),
        ],
        out_specs=pl.BlockSpec((1, rows, tn), lambda l, j: (l, 0, j)),
        out_shape=jax.ShapeDtypeStruct((depth, rows, cols), F32),
        compiler_params=_cparams(("parallel", "parallel")),
        name="ada_proj",
    )(cc, w_ada, b_ada.reshape(depth, 1, cols))


def _mod_kernel(x_ref, g_ref, sh_ref, sc_ref, o_ref):
    o_ref[0] = _modulate(x_ref[0], g_ref[...], sh_ref[0], sc_ref[0]).astype(BF16)


def _mod_call(x, g, sh, sc):
    nb, n, d = x.shape
    tm = min(MOD_TILE, n)
    vec = pl.BlockSpec((1, 1, d), lambda b, i: (b, 0, 0))
    return pl.pallas_call(
        _mod_kernel,
        grid=(nb, n // tm),
        in_specs=[pl.BlockSpec((1, tm, d), lambda b, i: (b, i, 0)),
                  pl.BlockSpec((1, d), lambda b, i: (0, 0)), vec, vec],
        out_specs=pl.BlockSpec((1, tm, d), lambda b, i: (b, i, 0)),
        out_shape=jax.ShapeDtypeStruct((nb, n, d), BF16),
        compiler_params=_cparams(("parallel", "parallel")),
        name="modulate",
    )(x, g.reshape(1, d), sh, sc)


class _Cols:
    def __init__(self, d):
        self.au = 0
        self.av = self.au + A_WIDTH // COL_BLOCK
        self.bb = self.av + A_WIDTH // COL_BLOCK
        self.bc = self.bb + B_WIDTH // COL_BLOCK
        self.bh = self.bc + B_WIDTH // COL_BLOCK
        self.cq = self.bh + B_WIDTH // COL_BLOCK
        qk = C_HEADS * 2 * C_HEAD_DIM // COL_BLOCK
        self.ck = self.cq + qk
        self.cv = self.ck + qk
        self.gate = self.cv + C_HEADS * C_V_DIM // COL_BLOCK
        self.end = self.gate + N_BRANCH * d // COL_BLOCK

    def out(self, jw):
        return (jw - self.gate) % self.end


def _in_kernel(h_ref, w_ref, bg_ref, cos_ref, sa_ref, sb_ref, o_ref, *, cols, j0, rope, qscale,
               sub):
    j = pl.program_id(2) + j0
    tm = h_ref.shape[1]

    def run(epilogue):
        def mm(r):
            return jnp.dot(h_ref[0, r * sub:(r + 1) * sub, :], w_ref[...],
                           preferred_element_type=F32)
        nxt = mm(0)
        for r in range(tm // sub):
            acc = nxt
            if r + 1 < tm // sub:
                nxt = mm(r + 1)
            epilogue(acc, slice(r * sub, (r + 1) * sub))

    @pl.when(j < cols.bb)
    def _():
        def gelu(acc, rows):
            o_ref[0, rows, :] = jax.nn.gelu(acc).astype(BF16)
        run(gelu)

    @pl.when(((j >= cols.bb) & (j < cols.cq)) | ((j >= cols.cv) & (j < cols.gate)))
    def _():
        def plain(acc, rows):
            o_ref[0, rows, :] = acc.astype(BF16)
        run(plain)

    @pl.when((j >= cols.cq) & (j < cols.cv))
    def _():
        scale = jnp.where(j < cols.ck, qscale, 1.0).astype(F32)

        def rotary(acc, rows):
            if not rope:
                o_ref[0, rows, :] = (acc * scale).astype(BF16)
                return
            cos, sa, sb = cos_ref[rows, :] * scale, sa_ref[rows, :] * scale, sb_ref[rows, :] * scale
            for s in range(COL_BLOCK // LANES):
                t = acc[:, s * LANES:(s + 1) * LANES]
                r = (t * cos + pltpu.roll(t, LANES - ROPE_PAIR, 1) * sa
                     + pltpu.roll(t, ROPE_PAIR, 1) * sb)
                o_ref[0, rows, s * LANES:(s + 1) * LANES] = r.astype(BF16)
        run(rotary)

    @pl.when(j >= cols.gate)
    def _():
        def gate(acc, rows):
            o_ref[0, rows, :] = _sigmoid(acc + bg_ref[...]).astype(BF16)
        run(gate)


def _in_call(h, w, b_gate, tables, *, j0, nj, rope):
    nb, n, d = h.shape
    cols = _Cols(d)
    tm = min(IN_TILE, n)
    full = (j0 == 0) and (nj == cols.end)
    if rope:
        cos, sa, sb = tables
    else:
        cos = sa = sb = jnp.zeros((tm, LANES), F32)
    tab = pl.BlockSpec((tm, LANES), lambda b, i, j: (i if rope else 0, 0))

    def out_idx(b, i, j):
        if full:
            return (b, i, jnp.where(j >= cols.gate, j - cols.gate, j + cols.end - cols.gate))
        return (b, i, j)
    kern = functools.partial(_in_kernel, cols=cols, j0=j0, rope=rope,
                             qscale=C_HEAD_DIM ** -0.5 * LOG2E, sub=min(IN_SUB, tm))
    return pl.pallas_call(
        kern,
        grid=(nb, n // tm, nj),
        in_specs=[
            pl.BlockSpec((1, tm, d), lambda b, i, j: (b, i, 0)),
            pl.BlockSpec((d, COL_BLOCK), lambda b, i, j: (0, j + j0)),
            pl.BlockSpec((1, COL_BLOCK), lambda b, i, j: (0, jnp.maximum(j + j0 - cols.gate, 0))),
            tab, tab, tab,
        ],
        out_specs=pl.BlockSpec((1, tm, COL_BLOCK), out_idx),
        out_shape=jax.ShapeDtypeStruct((nb, n, nj * COL_BLOCK), BF16),
        compiler_params=_cparams(("parallel", "parallel", "arbitrary")),
        name="in_proj",
    )(h, w, b_gate.reshape(1, -1), cos, sa, sb)


def _rope_tables(n):
    rows = n // GRID_W
    r = jnp.repeat(jnp.arange(rows, dtype=F32), GRID_W)
    col = jnp.tile(jnp.arange(GRID_W, dtype=F32), rows)
    quarter = C_HEAD_DIM // 4
    inv = ROPE_BASE ** (-jnp.arange(quarter, dtype=F32) / quarter)
    ar = r[:, None] * inv
    ac = col[:, None] * inv
    ang = jnp.concatenate([ar, ar, ac, ac], axis=-1)
    ang = jnp.tile(ang, (1, LANES // C_HEAD_DIM))
    cos, sin = jnp.cos(ang), jnp.sin(ang)
    first_of_pair = (jnp.arange(LANES) // quarter) % 2 == 0
    sa = jnp.where(first_of_pair, -sin, 0.0)
    sb = jnp.where(first_of_pair, 0.0, sin)
    return cos, sa, sb


def _local_mixers(u_ref, v_ref, bg_ref, cg_ref, hh_ref, cgp_ref, hhp_ref, cgn_ref, hhn_ref,
                  lng_ref, lnb_ref, ws_ref, bias_ref, cw_ref, ya_ref, yb_ref, *, tm, seq):
    i = pl.program_id(1)
    lane = lax.broadcasted_iota(jnp.int32, (CHUNK, LANES), 1)
    lo = lane < A_GROUP_DIM
    for c in range(tm // CHUNK):
        rows = pl.ds(c * CHUNK, CHUNK)
        v = v_ref[0, rows, :].astype(F32)
        mu = jnp.mean(v, axis=-1, keepdims=True)
        var = jnp.mean(jnp.square(v - mu), axis=-1, keepdims=True)
        vn = ((v - mu) * lax.rsqrt(var + EPS) * lng_ref[...] + lnb_ref[...]).astype(BF16)
        for k in range(A_WIDTH // LANES):
            blk = vn[:, k * LANES:(k + 1) * LANES]
            zero = jnp.zeros_like(blk)
            mixed = (jnp.dot(ws_ref[2 * k], jnp.where(lo, blk, zero), preferred_element_type=F32)
                     + jnp.dot(ws_ref[2 * k + 1], jnp.where(lo, zero, blk), preferred_element_type=F32)
                     + bias_ref[:, k * LANES:(k + 1) * LANES])
            u = u_ref[0, rows, k * LANES:(k + 1) * LANES].astype(F32)
            ya_ref[rows, k * LANES:(k + 1) * LANES] = (u * mixed).astype(BF16)

    z = cg_ref[0].astype(F32) * hh_ref[0].astype(F32)
    last = BF16_SUBLANES - 1
    z_prev = cgp_ref[0, last:last + 1, :].astype(F32) * hhp_ref[0, last:last + 1, :].astype(F32)
    z_next = cgn_ref[0, 0:1, :].astype(F32) * hhn_ref[0, 0:1, :].astype(F32)
    row = lax.broadcasted_iota(jnp.int32, (tm, 1), 0)
    pos = lax.rem(i * tm + row, seq)
    zm1 = jnp.where(row == 0, z_prev, pltpu.roll(z, 1, 0))
    zm1 = jnp.where(pos == 0, 0.0, zm1)
    zp1 = jnp.where(row == tm - 1, z_next, pltpu.roll(z, tm - 1, 0))
    zp1 = jnp.where(pos == seq - 1, 0.0, zp1)
    conv = cw_ref[0:1, :] * zm1 + cw_ref[1:2, :] * z + cw_ref[2:3, :] * zp1
    yb_ref[...] = (bg_ref[0].astype(F32) * conv).astype(BF16)


def _attn_kernel(*refs, tq, ck, sizes, lam_init):
    nsrc = len(sizes)
    cl_ref, g_ref, q_ref = refs[:3]
    kv_refs = refs[3:3 + 2 * nsrc]
    o_ref = refs[3 + 2 * nsrc]
    vt_refs = refs[4 + 2 * nsrc:4 + 3 * nsrc]
    kmax_ref = refs[4 + 3 * nsrc]
    half = lax.broadcasted_iota(jnp.int32, (1, LANES), 1) < C_HEAD_DIM

    @pl.when(pl.program_id(2) == 0)
    def _():
        per_map = jnp.where(lax.broadcasted_iota(jnp.int32, (LANES, LANES), 0) // C_HEAD_DIM
                            == lax.broadcasted_iota(jnp.int32, (LANES, LANES), 1), 1.0, 0.0)
        kmax2 = jnp.zeros((1, LANES), F32)
        for src in range(nsrc):
            vt_refs[src][...] = kv_refs[2 * src + 1][0].astype(F32).T.astype(BF16)
            k = kv_refs[2 * src][0]
            n2 = jnp.dot(k * k, per_map.astype(BF16), preferred_element_type=F32)
            kmax2 = jnp.maximum(kmax2, jnp.max(n2, axis=0, keepdims=True))
        kmax_ref[0:1, :] = jnp.sqrt(kmax2) * (1.0 + 2.0 ** -7)

    q = q_ref[0]
    zero = jnp.zeros_like(q)
    qs = [jnp.where(half, q, zero), jnp.where(half, zero, q)]
    chunks = [(kv_refs[2 * src], vt_refs[src], c0, min(ck, nk))
              for src, nk in enumerate(sizes) for c0 in range(0, nk, min(ck, nk))]

    def scores(chunk, qm):
        k_ref, _, c0, c = chunk
        return lax.dot_general(k_ref[0, c0:c0 + c, :], qm, (((1,), (1,)), ((), ())),
                               preferred_element_type=F32)

    def finish(accs, ls):
        cl = cl_ref[...]
        lam = (jnp.exp(jnp.sum(cl[0:1] * cl[1:2], axis=-1, keepdims=True))
               - jnp.exp(jnp.sum(cl[2:3] * cl[3:4], axis=-1, keepdims=True)) + lam_init)
        o = accs[0] / ls[0] - lam * (accs[1] / ls[1])
        ms = jnp.mean(o * o, axis=0, keepdims=True)
        y = o * lax.rsqrt(ms + EPS) * (g_ref[...] * (1.0 - lam_init))
        o_ref[0] = y.T.astype(BF16)

    qsq = jnp.square(q.astype(F32).T)
    refs_r = [jnp.sqrt(jnp.sum(qsq[j * C_HEAD_DIM:(j + 1) * C_HEAD_DIM], axis=0, keepdims=True))
              * kmax_ref[0:1, j:j + 1] for j in range(2)]
    ls = [jnp.zeros((1, tq), F32) for _ in qs]
    accs = [jnp.zeros((C_V_DIM, tq), F32) for _ in qs]
    s_next = [scores(chunks[0], qm) for qm in qs]
    for t, (_, vt_ref, c0, c) in enumerate(chunks):
        s_cur = s_next
        if t + 1 < len(chunks):
            s_next = [scores(chunks[t + 1], qm) for qm in qs]
        vt = vt_ref[:, c0:c0 + c]
        for j, s in enumerate(s_cur):
            p = jnp.exp2(s - refs_r[j])
            ls[j] = ls[j] + jnp.sum(p, axis=0, keepdims=True)
            accs[j] = accs[j] + jnp.dot(vt, p.astype(BF16), preferred_element_type=F32)
    finish(accs, ls)

    healthy = jnp.min(jnp.minimum(ls[0], ls[1])) >= 2.0 ** -60

    @pl.when(jnp.logical_not(healthy))
    def _():
        ms = [jnp.full((1, tq), -jnp.inf, F32) for _ in qs]
        ls = [jnp.zeros((1, tq), F32) for _ in qs]
        accs = [jnp.zeros((C_V_DIM, tq), F32) for _ in qs]
        for chunk in chunks:
            _, vt_ref, c0, c = chunk
            vt = vt_ref[:, c0:c0 + c]
            for j, qm in enumerate(qs):
                s = scores(chunk, qm)
                m_new = jnp.maximum(ms[j], jnp.max(s, axis=0, keepdims=True))
                alpha = jnp.exp2(ms[j] - m_new)
                p = jnp.exp2(s - m_new)
                ls[j] = alpha * ls[j] + jnp.sum(p, axis=0, keepdims=True)
                accs[j] = alpha * accs[j] + jnp.dot(vt, p.astype(BF16),
                                                    preferred_element_type=F32)
                ms[j] = m_new
        finish(accs, ls)


def _attn_call(q_arr, q_cb, sources, c_lambda, subln_g, *, lam_init):
    nb, n, _ = q_arr.shape
    tq = min(ATTN_Q_TILE, n)
    sizes = tuple(a.shape[1] for a, _, _ in sources)
    in_specs = [
        pl.BlockSpec((4, C_HEAD_DIM), lambda b, h, i: (0, 0)),
        pl.BlockSpec((C_V_DIM, 1), lambda b, h, i: (0, 0)),
        pl.BlockSpec((1, tq, LANES), lambda b, h, i: (b, i, q_cb + h)),
    ]
    args = [c_lambda, subln_g.reshape(-1, 1), q_arr]
    for arr, kcb, vcb in sources:
        nk = arr.shape[1]
        in_specs.append(pl.BlockSpec((1, nk, LANES), lambda b, h, i, kcb=kcb: (b, 0, kcb + h)))
        in_specs.append(pl.BlockSpec((1, nk, LANES), lambda b, h, i, vcb=vcb: (b, 0, vcb + h)))
        args += [arr, arr]
    kern = functools.partial(_attn_kernel, tq=tq, ck=ATTN_KEYS, sizes=sizes, lam_init=lam_init)
    return pl.pallas_call(
        kern,
        grid=(nb, C_HEADS, n // tq),
        in_specs=in_specs,
        out_specs=pl.BlockSpec((1, tq, C_V_DIM), lambda b, h, i: (b, i, h)),
        out_shape=jax.ShapeDtypeStruct((nb, n, C_HEADS * C_V_DIM), BF16),
        scratch_shapes=[pltpu.VMEM((C_V_DIM, nk), BF16) for nk in sizes]
        + [pltpu.VMEM((8, LANES), F32)],
        compiler_params=_cparams(("parallel", "parallel", "arbitrary")),
        name="diff_attention",
    )(*args)


def _merge_kernel(*refs, tm, seq):
    mixer_refs = refs[:14]
    (x_ref, yc_ref, ga_ref, gb_ref, gc_ref, wa_ref, wb_ref, wc_ref, wo_ref, g1_ref, n2_ref,
     sh2_ref, sc2_ref, xo_ref, ho_ref, ya_ref, yb_ref) = refs[14:]
    c = jnp.dot(yc_ref[0], wc_ref[...], preferred_element_type=F32)
    _local_mixers(*mixer_refs, ya_ref, yb_ref, tm=tm, seq=seq)
    a = jnp.dot(ya_ref[...], wa_ref[...], preferred_element_type=F32)
    b = jnp.dot(yb_ref[...], wb_ref[...], preferred_element_type=F32)
    y = (ga_ref[0].astype(F32) * a + gb_ref[0].astype(F32) * b + gc_ref[0].astype(F32) * c)
    m = jnp.dot(y.astype(BF16), wo_ref[...], preferred_element_type=F32)
    xn = x_ref[0] + g1_ref[0] * m
    xo_ref[0] = xn
    ho_ref[0] = _modulate(xn, n2_ref[...], sh2_ref[0], sc2_ref[0]).astype(BF16)


def _merge_call(x, p, pos, yc, gate_cb, mixer_w, wa, wb, wc, wo, g1, n2g, sh2, sc2, *, seq):
    nb, n, d = x.shape
    au, av, bb, bc, bh = pos
    ln_g, ln_b, ws, bs, conv_w = mixer_w
    tm = min(MERGE_TILE, n)
    hb = tm // BF16_SUBLANES
    nhb = n // BF16_SUBLANES

    def tok(w, cb=0):
        return pl.BlockSpec((1, tm, w), lambda b, i: (b, i, cb))

    def prev(cb):
        return pl.BlockSpec((1, BF16_SUBLANES, COL_BLOCK),
                            lambda b, i: (b, jnp.maximum(i * hb - 1, 0), cb))

    def nxt(cb):
        return pl.BlockSpec((1, BF16_SUBLANES, COL_BLOCK),
                            lambda b, i: (b, jnp.minimum((i + 1) * hb, nhb - 1), cb))

    const = lambda shape: pl.BlockSpec(shape, lambda b, i: (0,) * len(shape))
    bias = jnp.repeat(bs.T, A_GROUP_DIM, axis=1)
    vec = pl.BlockSpec((1, 1, d), lambda b, i: (b, 0, 0))
    return pl.pallas_call(
        functools.partial(_merge_kernel, tm=tm, seq=seq),
        grid=(nb, n // tm),
        in_specs=[tok(COL_BLOCK, au), tok(COL_BLOCK, av), tok(COL_BLOCK, bb), tok(COL_BLOCK, bc),
                  tok(COL_BLOCK, bh), prev(bc), prev(bh), nxt(bc), nxt(bh),
                  const((1, A_WIDTH)), const((1, A_WIDTH)), const((A_GROUPS, CHUNK, CHUNK)),
                  const((CHUNK, A_WIDTH)), const((CONV_W, B_WIDTH)),
                  tok(d), tok(d), tok(d, gate_cb), tok(d, gate_cb + 1), tok(d, gate_cb + 2),
                  _resident(wa.shape, lambda b, i: (0, 0)), _resident(wb.shape, lambda b, i: (0, 0)),
                  _resident(wc.shape, lambda b, i: (0, 0)), _resident(wo.shape, lambda b, i: (0, 0)),
                  vec, const((1, d)), vec, vec],
        out_specs=[tok(d), tok(d)],
        out_shape=[jax.ShapeDtypeStruct((nb, n, d), F32), jax.ShapeDtypeStruct((nb, n, d), BF16)],
        scratch_shapes=[pltpu.VMEM((tm, A_WIDTH), BF16), pltpu.VMEM((tm, B_WIDTH), BF16)],
        compiler_params=_cparams(("parallel", "parallel")),
        name="merge_branches",
    )(p, p, p, p, p, p, p, p, p, ln_g.reshape(1, -1), ln_b.reshape(1, -1), ws.astype(BF16), bias,
      conv_w, x, yc, p, p, p, wa, wb, wc, wo, g1, n2g.reshape(1, d), sh2, sc2)


def _ff_chunks(width, chunk=FFN_CHUNK):
    out, c0 = [], 0
    while c0 < width:
        cw = min(chunk, width - c0)
        out.append((c0, cw))
        c0 += cw
    return out


def _finish(xn, final, ng_ref, sh_ref, sc_ref, out_refs):
    if final:
        out_refs[0][0] = _rmsnorm(xn, ng_ref[...])
    else:
        out_refs[0][0] = xn
        out_refs[1][0] = _modulate(xn, ng_ref[...], sh_ref[0], sc_ref[0]).astype(BF16)


def _ffn_kernel(x_ref, h_ref, wg_ref, wu_ref, wd_ref, g2_ref, ng_ref, sh_ref, sc_ref, *out_refs,
                final):
    h = h_ref[0]
    acc = None
    for c0, cw in _ff_chunks(wg_ref.shape[1]):
        a = jnp.dot(h, wg_ref[:, c0:c0 + cw], preferred_element_type=F32)
        b = jnp.dot(h, wu_ref[:, c0:c0 + cw], preferred_element_type=F32)
        t = (_silu(a) * b).astype(BF16)
        part = jnp.dot(t, wd_ref[c0:c0 + cw, :], preferred_element_type=F32)
        acc = part if acc is None else acc + part
    xn = x_ref[0] + g2_ref[0] * acc
    _finish(xn, final, ng_ref, sh_ref, sc_ref, out_refs)


def _epilogue_specs(nb, n, d, tm, final, idx):
    tok = pl.BlockSpec((1, tm, d), idx)
    if final:
        return [tok], [jax.ShapeDtypeStruct((nb, n, d), F32)]
    return [tok, tok], [jax.ShapeDtypeStruct((nb, n, d), F32), jax.ShapeDtypeStruct((nb, n, d), BF16)]


def _ffn_call(x, h, wg, wu, wd, g2, ng, sh, sc, *, final):
    nb, n, d = x.shape
    tm = min(FFN_TILE, n)
    idx = lambda b, i: (b, i, 0)
    vec = pl.BlockSpec((1, 1, d), lambda b, i: (b, 0, 0))
    out_specs, out_shape = _epilogue_specs(nb, n, d, tm, final, idx)
    return pl.pallas_call(
        functools.partial(_ffn_kernel, final=final),
        grid=(nb, n // tm),
        in_specs=[pl.BlockSpec((1, tm, d), idx), pl.BlockSpec((1, tm, d), idx),
                  _resident(wg.shape, lambda b, i: (0, 0)), _resident(wu.shape, lambda b, i: (0, 0)),
                  _resident(wd.shape, lambda b, i: (0, 0)),
                  vec, pl.BlockSpec((1, d), lambda b, i: (0, 0)), vec, vec],
        out_specs=out_specs,
        out_shape=out_shape,
        compiler_params=_cparams(("parallel", "parallel")),
        name="dense_swiglu",
    )(x, h, wg, wu, wd, g2, ng.reshape(1, d), sh, sc)


MOE_TILE = 2048
MOE_WIN = 256
MOE_SLAB = 128
MOE_ROWS = 1024
MOE_UNIT = 128
MOE_FF = 512
CNT_ROWS = 16


def _route_kernel(h_ref, wr_ref, comb_ref, rank_ref, rankt_ref, cnt_ref, *, n_experts, win):
    h = h_ref[0]
    tm = h.shape[0]
    lane = lax.broadcasted_iota(jnp.int32, (tm, LANES), 1)
    logits = jnp.dot(h, wr_ref[...], preferred_element_type=F32)
    lg = jnp.where(lane < n_experts, logits, -jnp.inf)
    m1 = jnp.max(lg, axis=-1, keepdims=True)
    i1 = jnp.min(jnp.where(lg == m1, lane, LANES), axis=-1, keepdims=True)
    lg2 = jnp.where(lane == i1, -jnp.inf, lg)
    m2 = jnp.max(lg2, axis=-1, keepdims=True)
    i2 = jnp.min(jnp.where(lg2 == m2, lane, LANES), axis=-1, keepdims=True)
    e2 = jnp.exp(m2 - m1)
    w1 = 1.0 / (1.0 + e2)
    comb_ref[0] = jnp.where(lane == i1, w1, 0.0) + jnp.where(lane == i2, e2 * w1, 0.0)
    sel = (lane == i1) | (lane == i2)
    tri = jnp.where(lax.broadcasted_iota(jnp.int32, (win, win), 0)
                    > lax.broadcasted_iota(jnp.int32, (win, win), 1), 1.0, 0.0).astype(BF16)
    base = jnp.zeros((1, LANES), F32)
    bases = []
    for w in range(tm // win):
        rows = slice(w * win, (w + 1) * win)
        sw = jnp.where(sel[rows], 1.0, 0.0)
        excl = jnp.dot(tri, sw.astype(BF16), preferred_element_type=F32)
        rank_ref[0, rows, :] = jnp.where(sel[rows], excl + base, -1.0)
        bases.append(base)
        base = base + jnp.sum(sw, axis=0, keepdims=True)
    bases.append(base)
    bases += [jnp.zeros((1, LANES), F32)] * (CNT_ROWS - len(bases))
    cnt_ref[0] = jnp.concatenate(bases, axis=0).astype(jnp.int32)
    rankt_ref[0] = rank_ref[0].T[:CNT_ROWS]


def _route_call(h, w_router, tm):
    nb, n, d = h.shape
    n_experts = w_router.shape[1]
    assert n_experts <= CNT_ROWS and tm // MOE_WIN + 1 <= CNT_ROWS
    nt = n // tm
    wr = jnp.zeros((d, LANES), BF16).at[:, :n_experts].set(w_router.astype(BF16))
    tok = pl.BlockSpec((1, tm, LANES), lambda b, i: (b, i, 0))
    per_tile = lambda r, c: pl.BlockSpec((1, r, c), lambda b, i: (b * nt + i, 0, 0))
    return pl.pallas_call(
        functools.partial(_route_kernel, n_experts=n_experts, win=min(MOE_WIN, tm)),
        grid=(nb, nt),
        in_specs=[pl.BlockSpec((1, tm, d), lambda b, i: (b, i, 0)),
                  pl.BlockSpec((d, LANES), lambda b, i: (0, 0))],
        out_specs=[tok, tok, per_tile(CNT_ROWS, tm), per_tile(CNT_ROWS, LANES)],
        out_shape=[jax.ShapeDtypeStruct((nb, n, LANES), F32), jax.ShapeDtypeStruct((nb, n, LANES), F32),
                   jax.ShapeDtypeStruct((nb * nt, CNT_ROWS, tm), F32),
                   jax.ShapeDtypeStruct((nb * nt, CNT_ROWS, LANES), jnp.int32)],
        compiler_params=_cparams(("parallel", "parallel")),
        name="moe_route",
    )(h, wr)


def _moe_kernel(cnt_ref, x_ref, h_ref, comb_ref, rank_ref, rankt_ref, wg_hbm, wu_hbm, wd_hbm, g2_ref,
                ng_ref, sh_ref, sc_ref, *rest, final, n_experts, win, slab, rblk, unit, ff):
    out_refs, (hc_ref, yc_ref, wg_buf, wu_buf, wd_buf, sem) = rest[:-6], rest[-6:]
    acc_ref = out_refs[0].at[0]
    b, i, e = (pl.program_id(k) for k in range(3))
    tm, d = h_ref.shape[1], h_ref.shape[2]
    nw = tm // win
    nf = wg_hbm.shape[2] // ff
    tile = b * pl.num_programs(1) + i
    cbase = tile * (CNT_ROWS * n_experts)
    step = tile * n_experts + e
    last_step = pl.num_programs(0) * pl.num_programs(1) * n_experts - 1

    def weight_copies(expert, f, slot):
        cols = pl.ds(pl.multiple_of(f * ff, ff), ff)
        return (pltpu.make_async_copy(wg_hbm.at[expert, :, cols], wg_buf.at[slot], sem.at[slot, 0]),
                pltpu.make_async_copy(wu_hbm.at[expert, :, cols], wu_buf.at[slot], sem.at[slot, 1]),
                pltpu.make_async_copy(wd_hbm.at[expert, cols, :], wd_buf.at[slot], sem.at[slot, 2]))

    def fetch(expert, f, slot):
        for copy in weight_copies(expert, f, slot):
            copy.start()

    @pl.when(step == 0)
    def _():
        fetch(e, 0, 0)

    def count(w):
        return cnt_ref[cbase + w * n_experts + e]

    def slabs(w):
        row0 = (count(w) // BF16_SUBLANES) * BF16_SUBLANES
        return row0, (count(w + 1) - row0 + slab - 1) // slab

    @pl.when(e == 0)
    def _():
        acc_ref[...] = jnp.zeros_like(acc_ref)

    total = count(nw)

    def zero(s, carry):
        rows = pl.ds(pl.multiple_of(s * slab, slab), slab)
        hc_ref[rows, :] = jnp.zeros((slab, d), BF16)
        yc_ref[rows, :] = jnp.zeros((slab, d), F32)
        return carry

    lax.fori_loop(0, jnp.minimum((total + unit + 2 * slab) // slab, hc_ref.shape[0] // slab),
                  zero, 0)

    def gather(w):
        row0, nsl = slabs(w)
        rt = rankt_ref[0, 0, :, w * win:(w + 1) * win]
        hw = h_ref[0, w * win:(w + 1) * win, :]

        def body(s, carry):
            r0 = pl.multiple_of(row0 + s * slab, BF16_SUBLANES)
            rid = (r0 + lax.broadcasted_iota(jnp.int32, (slab, win), 0)).astype(F32)
            onehot = jnp.where(rt == rid, 1.0, 0.0).astype(BF16)
            rows = pl.ds(r0, slab)
            got = jnp.dot(onehot, hw, preferred_element_type=F32)
            hc_ref[rows, :] = (hc_ref[rows, :].astype(F32) + got).astype(BF16)
            return carry

        return body, nsl

    for w in range(nw):
        gather(w)[0](0, 0)
    for w in range(nw):
        body, nsl = gather(w)
        lax.fori_loop(1, nsl, body, 0)

    units = (total + unit - 1) // unit
    per_block = rblk // unit
    nfull = units // per_block
    rest = units - per_block * nfull

    def ff_block(f, carry):
        slot = lax.rem(step * nf + f, 2)

        @pl.when(f + 1 < nf)
        def _():
            fetch(e, f + 1, 1 - slot)

        @pl.when((f + 1 == nf) & (step < last_step))
        def _():
            fetch(lax.rem(e + 1, n_experts), 0, 1 - slot)

        for copy in weight_copies(e, f, slot):
            copy.wait()

        def ffn(r0, size):
            rows = pl.ds(pl.multiple_of(r0, unit), size)
            hb = hc_ref[rows, :]
            a = jnp.dot(hb, wg_buf[slot], preferred_element_type=F32)
            u = jnp.dot(hb, wu_buf[slot], preferred_element_type=F32)
            t = (_silu(a) * u).astype(BF16)
            yc_ref[rows, :] += jnp.dot(t, wd_buf[slot], preferred_element_type=F32)

        def full(r, c):
            ffn(r * rblk, rblk)
            return c

        lax.fori_loop(0, nfull, full, 0)
        for k in range(1, per_block):
            @pl.when(rest == k)
            def _(k=k):
                ffn(nfull * rblk, k * unit)

        return carry

    lax.fori_loop(0, nf, ff_block, 0)

    def spread(w):
        row0, nsl = slabs(w)
        trows = slice(w * win, (w + 1) * win)
        mine = lax.broadcasted_iota(jnp.int32, (win, LANES), 1) == e
        rcol = jnp.sum(jnp.where(mine, rank_ref[0, trows, :], 0.0), axis=-1, keepdims=True)
        wcol = jnp.sum(jnp.where(mine, comb_ref[0, trows, :], 0.0), axis=-1, keepdims=True)

        def body(s, carry):
            r0 = pl.multiple_of(row0 + s * slab, BF16_SUBLANES)
            cid = (r0 + lax.broadcasted_iota(jnp.int32, (win, slab), 1)).astype(F32)
            onehot = jnp.where(rcol == cid, 1.0, 0.0).astype(BF16)
            yb = yc_ref[pl.ds(r0, slab), :].astype(BF16)
            acc_ref[trows, :] += wcol * jnp.dot(onehot, yb, preferred_element_type=F32)
            return carry

        return body, nsl

    bodies = [spread(w) for w in range(nw)]
    for body, _ in bodies:
        body(0, 0)
    for body, nsl in bodies:
        lax.fori_loop(1, nsl, body, 0)

    @pl.when(e == n_experts - 1)
    def _():
        xn = x_ref[0] + g2_ref[0] * acc_ref[...]
        _finish(xn, final, ng_ref, sh_ref, sc_ref, out_refs)


def _moe_call(x, h, w_router, wg, wu, wd, g2, ng, sh, sc, *, final):
    nb, n, d = x.shape
    n_experts, _, dff = wg.shape
    tm = min(MOE_TILE, n)
    win = min(MOE_WIN, tm)
    nt = n // tm
    comb, rank, rankt, cnt = _route_call(h, w_router, tm)
    cnt = cnt[:, :, :n_experts].reshape(-1)
    rankt = rankt.reshape(nb * nt, CNT_ROWS, 1, tm)
    cap_ffn = -(-tm // MOE_UNIT) * MOE_UNIT
    cap = -(-max(cap_ffn, tm + MOE_SLAB) // MOE_SLAB) * MOE_SLAB

    idx = lambda b, i, e, c: (b, i, 0)
    vec = pl.BlockSpec((1, 1, d), lambda b, i, e, c: (b, 0, 0))
    hbm = pl.BlockSpec(memory_space=pl.ANY)
    out_specs, out_shape = _epilogue_specs(nb, n, d, tm, final, idx)
    grid_spec = pltpu.PrefetchScalarGridSpec(
        num_scalar_prefetch=1,
        grid=(nb, nt, n_experts),
        in_specs=[_resident((1, tm, d), idx), _resident((1, tm, d), idx),
                  _resident((1, tm, LANES), idx), _resident((1, tm, LANES), idx),
                  pl.BlockSpec((1, 1, 1, tm), lambda b, i, e, c: (b * nt + i, e, 0, 0)),
                  hbm, hbm, hbm,
                  vec, pl.BlockSpec((1, d), lambda b, i, e, c: (0, 0)), vec, vec],
        out_specs=out_specs,
        scratch_shapes=[pltpu.VMEM((cap, d), BF16), pltpu.VMEM((cap, d), F32),
                        pltpu.VMEM((2, d, MOE_FF), BF16), pltpu.VMEM((2, d, MOE_FF), BF16),
                        pltpu.VMEM((2, MOE_FF, d), BF16), pltpu.SemaphoreType.DMA((2, 3))],
    )
    return pl.pallas_call(
        functools.partial(_moe_kernel, final=final, n_experts=n_experts, win=win, slab=MOE_SLAB,
                          rblk=MOE_ROWS, unit=MOE_UNIT, ff=MOE_FF),
        grid_spec=grid_spec,
        out_shape=out_shape,
        compiler_params=_cparams(("arbitrary", "arbitrary", "arbitrary")),
        name="moe_swiglu",
    )(cnt, x, h, comb, rank, rankt, wg, wu, wd, g2, ng.reshape(1, d), sh, sc)


def kernel(x, c, ctx, c_ctx, w_ada, b_ada, norm1_g, norm2_g, w_in, b_gate, a_ln_g, a_ln_b, a_ws,
           a_bs, b_conv, c_lambda, c_subln_g, w_a_out, w_b_out, w_c_out, w_o, ff_w_gate, ff_w_up,
           ff_w_down, moe_w_router, moe_w_gate, moe_w_up, moe_w_down, final_norm_g):
    bsz, n, d = x.shape
    nc = ctx.shape[1]
    depth = w_ada.shape[0]
    cols = _Cols(d)
    head_cb = COL_BLOCK // LANES
    gate_cb = cols.out(cols.gate) * COL_BLOCK // d
    mix_pos = tuple(cols.out(j) for j in (cols.au, cols.av, cols.bb, cols.bc, cols.bh))
    cq_l, ck_l, cv_l = (cols.out(j) * head_cb for j in (cols.cq, cols.ck, cols.cv))

    pad = (-(bsz + 1)) % 8
    cc = jnp.concatenate([c, c_ctx[None], jnp.zeros((pad, d), F32)], axis=0)
    mod = _ada_call(cc, w_ada, b_ada)

    def mods(l):
        lat = [mod[l, :bsz, k * d:(k + 1) * d].reshape(bsz, 1, d) for k in range(N_MOD)]
        con = [mod[l, bsz:bsz + 1, k * d:(k + 1) * d].reshape(1, 1, d) for k in range(N_MOD)]
        return lat, con

    tables = _rope_tables(n)
    w_in_b = w_in.astype(BF16)
    xl = x
    xc = ctx.reshape(1, bsz * nc, d)
    lat, con = mods(0)
    h = _mod_call(xl, norm1_g[0], lat[0], lat[1])
    hc = _mod_call(xc, norm1_g[0], con[0], con[1])

    for l in range(depth):
        last = l == depth - 1
        lam_init = 0.8 - 0.6 * math.exp(-0.3 * l)
        lat, con = mods(l)
        if not last:
            nlat, ncon = mods(l + 1)
            nxt_l = (norm1_g[l + 1], nlat[0], nlat[1])
            nxt_c = (norm1_g[l + 1], ncon[0], ncon[1])
        else:
            nxt_l = (final_norm_g, lat[0], lat[1])
            nxt_c = None

        p = _in_call(h, w_in_b[l], b_gate[l], tables, j0=0, nj=cols.end, rope=True)
        if last:
            pc = _in_call(hc, w_in_b[l], b_gate[l], tables, j0=cols.ck, nj=cols.gate - cols.ck,
                          rope=False)
            ck_c, cv_c = 0, (cols.cv - cols.ck) * head_cb
        else:
            pc = _in_call(hc, w_in_b[l], b_gate[l], tables, j0=0, nj=cols.end, rope=False)
            ck_c, cv_c = ck_l, cv_l
        pc_seq = pc.reshape(bsz, nc, -1)

        wa, wb, wc, wo = (w.astype(BF16) for w in (w_a_out[l], w_b_out[l], w_c_out[l], w_o[l]))

        def channel(xs, hs, g2, nxt, final):
            i = l // 2
            if l % 2 == 0:
                return _ffn_call(xs, hs, ff_w_gate[i].astype(BF16), ff_w_up[i].astype(BF16),
                                 ff_w_down[i].astype(BF16), g2, *nxt, final=final)
            return _moe_call(xs, hs, moe_w_router[i], moe_w_gate[i].astype(BF16),
                             moe_w_up[i].astype(BF16), moe_w_down[i].astype(BF16), g2, *nxt,
                             final=final)

        mixer_w = (a_ln_g[l], a_ln_b[l], a_ws[l], a_bs[l], b_conv[l])
        yc = _attn_call(p, cq_l, [(p, ck_l, cv_l), (pc_seq, ck_c, cv_c)],
                        c_lambda[l], c_subln_g[l], lam_init=lam_init)
        x1, h2 = _merge_call(xl, p, mix_pos, yc, gate_cb, mixer_w, wa, wb, wc, wo, lat[2],
                             norm2_g[l], lat[3], lat[4], seq=n)
        res = channel(x1, h2, lat[5], nxt_l, last)
        if last:
            return res[0]
        xl, h = res

        yc_c = _attn_call(pc_seq, cq_l, [(pc_seq, ck_c, cv_c)], c_lambda[l], c_subln_g[l],
                          lam_init=lam_init)
        xc1, hc2 = _merge_call(xc, pc, mix_pos, yc_c.reshape(1, bsz * nc, -1), gate_cb, mixer_w,
                               wa, wb, wc, wo, con[2], norm2_g[l], con[3], con[4], seq=nc)
        xc, hc = channel(xc1, hc2, con[5], nxt_c, False)
```

```python
import functools
import math

import jax
import jax.numpy as jnp
from jax import lax
from jax.experimental import pallas as pl
from jax.experimental.pallas import tpu as pltpu

F32 = jnp.float32
BF16 = jnp.bfloat16

EPS = 1e-6
GRID_W = 64
N_MOD = 6
N_BRANCH = 3
CHUNK = 128
A_GROUPS = 8
A_GROUP_DIM = 64
A_WIDTH = A_GROUPS * A_GROUP_DIM
B_WIDTH = 512
CONV_W = 3
C_HEADS = 8
C_HEAD_DIM = 64
C_V_DIM = 2 * C_HEAD_DIM
ROPE_BASE = 10000.0
ROPE_PAIR = C_HEAD_DIM // 4
LOG2E = 1.4426950408889634

LANES = 128
BF16_SUBLANES = 16
V7X_VMEM_LIMIT = 56 * 1024 * 1024

COL_BLOCK = 512
ADA_COLS = 1536
MOD_TILE = 1024
IN_TILE, IN_SUB = 4096, 512
ATTN_Q_TILE, ATTN_KEYS = 512, 2048
MERGE_TILE = 512
FFN_TILE, FFN_CHUNK = 512, 512


def _cparams(sem):
    return pltpu.CompilerParams(dimension_semantics=sem, vmem_limit_bytes=V7X_VMEM_LIMIT)


def _resident(shape, index_map):
    return pl.BlockSpec(shape, index_map, pipeline_mode=pl.Buffered(1))


def _sigmoid(v):
    return 0.5 * (1.0 + jnp.tanh(0.5 * v))


def _silu(v):
    return v * _sigmoid(v)


def _modulate(x, g, sh, sc):
    ms = jnp.mean(x * x, axis=-1, keepdims=True)
    y = x * lax.rsqrt(ms + EPS)
    return (y * g) * (1.0 + sc) + sh


def _rmsnorm(x, g):
    ms = jnp.mean(x * x, axis=-1, keepdims=True)
    return x * lax.rsqrt(ms + EPS) * g


def _ada_kernel(c_ref, w_ref, b_ref, o_ref):
    s = _silu(c_ref[...])
    o_ref[0] = jnp.dot(s, w_ref[0], preferred_element_type=F32,
                       precision=lax.Precision.HIGHEST) + b_ref[0]


def _ada_call(cc, w_ada, b_ada):
    depth, d, cols = w_ada.shape
    rows = cc.shape[0]
    tn = ADA_COLS
    return pl.pallas_call(
        _ada_kernel,
        grid=(depth, cols // tn),
        in_specs=[
            pl.BlockSpec((rows, d), lambda l, j: (0, 0)),
            pl.BlockSpec((1, d, tn), lambda l, j: (l, 0, j)),
            pl.BlockSpec((1, 1, tn), lambda l, j: (l, 0, j)),
        ],
        out_specs=pl.BlockSpec((1, rows, tn), lambda l, j: (l, 0, j)),
        out_shape=jax.ShapeDtypeStruct((depth, rows, cols), F32),
        compiler_params=_cparams(("parallel", "parallel")),
        name="ada_proj",
    )(cc, w_ada, b_ada.reshape(depth, 1, cols))


def _mod_kernel(x_ref, g_ref, sh_ref, sc_ref, o_ref):
    o_ref[0] = _modulate(x_ref[0], g_ref[...], sh_ref[0], sc_ref[0]).astype(BF16)


def _mod_call(x, g, sh, sc):
    nb, n, d = x.shape
    tm = min(MOD_TILE, n)
    vec = pl.BlockSpec((1, 1, d), lambda b, i: (b, 0, 0))
    return pl.pallas_call(
        _mod_kernel,
        grid=(nb, n // tm),
        in_specs=[pl.BlockSpec((1, tm, d), lambda b, i: (b, i, 0)),
                  pl.BlockSpec((1, d), lambda b, i: (0, 0)), vec, vec],
        out_specs=pl.BlockSpec((1, tm, d), lambda b, i: (b, i, 0)),
        out_shape=jax.ShapeDtypeStruct((nb, n, d), BF16),
        compiler_params=_cparams(("parallel", "parallel")),
        name="modulate",
    )(x, g.reshape(1, d), sh, sc)


class _Cols:
    def __init__(self, d):
        self.au = 0
        self.av = self.au + A_WIDTH // COL_BLOCK
        self.bb = self.av + A_WIDTH // COL_BLOCK
        self.bc = self.bb + B_WIDTH // COL_BLOCK
        self.bh = self.bc + B_WIDTH // COL_BLOCK
        self.cq = self.bh + B_WIDTH // COL_BLOCK
        qk = C_HEADS * 2 * C_HEAD_DIM // COL_BLOCK
        self.ck = self.cq + qk
        self.cv = self.ck + qk
        self.gate = self.cv + C_HEADS * C_V_DIM // COL_BLOCK
        self.end = self.gate + N_BRANCH * d // COL_BLOCK

    def out(self, jw):
        return (jw - self.gate) % self.end


def _in_kernel(h_ref, w_ref, bg_ref, cos_ref, sa_ref, sb_ref, o_ref, *, cols, j0, rope, qscale,
               sub):
    j = pl.program_id(2) + j0
    tm = h_ref.shape[1]

    def run(epilogue):
        def mm(r):
            return jnp.dot(h_ref[0, r * sub:(r + 1) * sub, :], w_ref[...],
                           preferred_element_type=F32)
        nxt = mm(0)
        for r in range(tm // sub):
            acc = nxt
            if r + 1 < tm // sub:
                nxt = mm(r + 1)
            epilogue(acc, slice(r * sub, (r + 1) * sub))

    @pl.when(j < cols.bb)
    def _():
        def gelu(acc, rows):
            o_ref[0, rows, :] = jax.nn.gelu(acc).astype(BF16)
        run(gelu)

    @pl.when(((j >= cols.bb) & (j < cols.cq)) | ((j >= cols.cv) & (j < cols.gate)))
    def _():
        def plain(acc, rows):
            o_ref[0, rows, :] = acc.astype(BF16)
        run(plain)

    @pl.when((j >= cols.cq) & (j < cols.cv))
    def _():
        scale = jnp.where(j < cols.ck, qscale, 1.0).astype(F32)

        def rotary(acc, rows):
            if not rope:
                o_ref[0, rows, :] = (acc * scale).astype(BF16)
                return
            cos, sa, sb = cos_ref[rows, :] * scale, sa_ref[rows, :] * scale, sb_ref[rows, :] * scale
            for s in range(COL_BLOCK // LANES):
                t = acc[:, s * LANES:(s + 1) * LANES]
                r = (t * cos + pltpu.roll(t, LANES - ROPE_PAIR, 1) * sa
                     + pltpu.roll(t, ROPE_PAIR, 1) * sb)
                o_ref[0, rows, s * LANES:(s + 1) * LANES] = r.astype(BF16)
        run(rotary)

    @pl.when(j >= cols.gate)
    def _():
        def gate(acc, rows):
            o_ref[0, rows, :] = _sigmoid(acc + bg_ref[...]).astype(BF16)
        run(gate)


def _in_call(h, w, b_gate, tables, *, j0, nj, rope):
    nb, n, d = h.shape
    cols = _Cols(d)
    tm = min(IN_TILE, n)
    full = (j0 == 0) and (nj == cols.end)
    if rope:
        cos, sa, sb = tables
    else:
        cos = sa = sb = jnp.zeros((tm, LANES), F32)
    tab = pl.BlockSpec((tm, LANES), lambda b, i, j: (i if rope else 0, 0))

    def out_idx(b, i, j):
        if full:
            return (b, i, jnp.where(j >= cols.gate, j - cols.gate, j + cols.end - cols.gate))
        return (b, i, j)
    kern = functools.partial(_in_kernel, cols=cols, j0=j0, rope=rope,
                             qscale=C_HEAD_DIM ** -0.5 * LOG2E, sub=min(IN_SUB, tm))
    return pl.pallas_call(
        kern,
        grid=(nb, n // tm, nj),
        in_specs=[
            pl.BlockSpec((1, tm, d), lambda b, i, j: (b, i, 0)),
            pl.BlockSpec((d, COL_BLOCK), lambda b, i, j: (0, j + j0)),
            pl.BlockSpec((1, COL_BLOCK), lambda b, i, j: (0, jnp.maximum(j + j0 - cols.gate, 0))),
            tab, tab, tab,
        ],
        out_specs=pl.BlockSpec((1, tm, COL_BLOCK), out_idx),
        out_shape=jax.ShapeDtypeStruct((nb, n, nj * COL_BLOCK), BF16),
        compiler_params=_cparams(("parallel", "parallel", "arbitrary")),
        name="in_proj",
    )(h, w, b_gate.reshape(1, -1), cos, sa, sb)


def _rope_tables(n):
    rows = n // GRID_W
    r = jnp.repeat(jnp.arange(rows, dtype=F32), GRID_W)
    col = jnp.tile(jnp.arange(GRID_W, dtype=F32), rows)
    quarter = C_HEAD_DIM // 4
    inv = ROPE_BASE ** (-jnp.arange(quarter, dtype=F32) / quarter)
    ar = r[:, None] * inv
    ac = col[:, None] * inv
    ang = jnp.concatenate([ar, ar, ac, ac], axis=-1)
    ang = jnp.tile(ang, (1, LANES // C_HEAD_DIM))
    cos, sin = jnp.cos(ang), jnp.sin(ang)
    first_of_pair = (jnp.arange(LANES) // quarter) % 2 == 0
    sa = jnp.where(first_of_pair, -sin, 0.0)
    sb = jnp.where(first_of_pair, 0.0, sin)
    return cos, sa, sb


def _local_mixers(u_ref, v_ref, bg_ref, cg_ref, hh_ref, cgp_ref, hhp_ref, cgn_ref, hhn_ref,
                  lng_ref, lnb_ref, ws_ref, bias_ref, cw_ref, ya_ref, yb_ref, *, tm, seq):
    i = pl.program_id(1)
    lane = lax.broadcasted_iota(jnp.int32, (CHUNK, LANES), 1)
    lo = lane < A_GROUP_DIM
    for c in range(tm // CHUNK):
        rows = pl.ds(c * CHUNK, CHUNK)
        v = v_ref[0, rows, :].astype(F32)
        mu = jnp.mean(v, axis=-1, keepdims=True)
        var = jnp.mean(jnp.square(v - mu), axis=-1, keepdims=True)
        vn = ((v - mu) * lax.rsqrt(var + EPS) * lng_ref[...] + lnb_ref[...]).astype(BF16)
        for k in range(A_WIDTH // LANES):
            blk = vn[:, k * LANES:(k + 1) * LANES]
            zero = jnp.zeros_like(blk)
            mixed = (jnp.dot(ws_ref[2 * k], jnp.where(lo, blk, zero), preferred_element_type=F32)
                     + jnp.dot(ws_ref[2 * k + 1], jnp.where(lo, zero, blk), preferred_element_type=F32)
                     + bias_ref[:, k * LANES:(k + 1) * LANES])
            u = u_ref[0, rows, k * LANES:(k + 1) * LANES].astype(F32)
            ya_ref[rows, k * LANES:(k + 1) * LANES] = (u * mixed).astype(BF16)

    z = cg_ref[0].astype(F32) * hh_ref[0].astype(F32)
    last = BF16_SUBLANES - 1
    z_prev = cgp_ref[0, last:last + 1, :].astype(F32) * hhp_ref[0, last:last + 1, :].astype(F32)
    z_next = cgn_ref[0, 0:1, :].astype(F32) * hhn_ref[0, 0:1, :].astype(F32)
    row = lax.broadcasted_iota(jnp.int32, (tm, 1), 0)
    pos = lax.rem(i * tm + row, seq)
    zm1 = jnp.where(row == 0, z_prev, pltpu.roll(z, 1, 0))
    zm1 = jnp.where(pos == 0, 0.0, zm1)
    zp1 = jnp.where(row == tm - 1, z_next, pltpu.roll(z, tm - 1, 0))
    zp1 = jnp.where(pos == seq - 1, 0.0, zp1)
    conv = cw_ref[0:1, :] * zm1 + cw_ref[1:2, :] * z + cw_ref[2:3, :] * zp1
    yb_ref[...] = (bg_ref[0].astype(F32) * conv).astype(BF16)


def _attn_kernel(*refs, tq, ck, sizes, lam_init):
    nsrc = len(sizes)
    cl_ref, g_ref, q_ref = refs[:3]
    kv_refs = refs[3:3 + 2 * nsrc]
    o_ref = refs[3 + 2 * nsrc]
    vt_refs = refs[4 + 2 * nsrc:4 + 3 * nsrc]
    kmax_ref = refs[4 + 3 * nsrc]
    half = lax.broadcasted_iota(jnp.int32, (1, LANES), 1) < C_HEAD_DIM

    @pl.when(pl.program_id(2) == 0)
    def _():
        per_map = jnp.where(lax.broadcasted_iota(jnp.int32, (LANES, LANES), 0) // C_HEAD_DIM
                            == lax.broadcasted_iota(jnp.int32, (LANES, LANES), 1), 1.0, 0.0)
        kmax2 = jnp.zeros((1, LANES), F32)
        for src in range(nsrc):
            vt_refs[src][...] = kv_refs[2 * src + 1][0].astype(F32).T.astype(BF16)
            k = kv_refs[2 * src][0]
            n2 = jnp.dot(k * k, per_map.astype(BF16), preferred_element_type=F32)
            kmax2 = jnp.maximum(kmax2, jnp.max(n2, axis=0, keepdims=True))
        kmax_ref[0:1, :] = jnp.sqrt(kmax2) * (1.0 + 2.0 ** -7)

    q = q_ref[0]
    zero = jnp.zeros_like(q)
    qs = [jnp.where(half, q, zero), jnp.where(half, zero, q)]
    chunks = [(kv_refs[2 * src], vt_refs[src], c0, min(ck, nk))
              for src, nk in enumerate(sizes) for c0 in range(0, nk, min(ck, nk))]

    def scores(chunk, qm):
        k_ref, _, c0, c = chunk
        return lax.dot_general(k_ref[0, c0:c0 + c, :], qm, (((1,), (1,)), ((), ())),
                               preferred_element_type=F32)

    def finish(accs, ls):
        cl = cl_ref[...]
        lam = (jnp.exp(jnp.sum(cl[0:1] * cl[1:2], axis=-1, keepdims=True))
               - jnp.exp(jnp.sum(cl[2:3] * cl[3:4], axis=-1, keepdims=True)) + lam_init)
        o = accs[0] / ls[0] - lam * (accs[1] / ls[1])
        ms = jnp.mean(o * o, axis=0, keepdims=True)
        y = o * lax.rsqrt(ms + EPS) * (g_ref[...] * (1.0 - lam_init))
        o_ref[0] = y.T.astype(BF16)

    qsq = jnp.square(q.astype(F32).T)
    refs_r = [jnp.sqrt(jnp.sum(qsq[j * C_HEAD_DIM:(j + 1) * C_HEAD_DIM], axis=0, keepdims=True))
              * kmax_ref[0:1, j:j + 1] for j in range(2)]
    ls = [jnp.zeros((1, tq), F32) for _ in qs]
    accs = [jnp.zeros((C_V_DIM, tq), F32) for _ in qs]
    s_next = [scores(chunks[0], qm) for qm in qs]
    for t, (_, vt_ref, c0, c) in enumerate(chunks):
        s_cur = s_next
        if t + 1 < len(chunks):
            s_next = [scores(chunks[t + 1], qm) for qm in qs]
        vt = vt_ref[:, c0:c0 + c]
        for j, s in enumerate(s_cur):
            p = jnp.exp2(s - refs_r[j])
            ls[j] = ls[j] + jnp.sum(p, axis=0, keepdims=True)
            accs[j] = accs[j] + jnp.dot(vt, p.astype(BF16), preferred_element_type=F32)
    finish(accs, ls)

    healthy = jnp.min(jnp.minimum(ls[0], ls[1])) >= 2.0 ** -60

    @pl.when(jnp.logical_not(healthy))
    def _():
        ms = [jnp.full((1, tq), -jnp.inf, F32) for _ in qs]
        ls = [jnp.zeros((1, tq), F32) for _ in qs]
        accs = [jnp.zeros((C_V_DIM, tq), F32) for _ in qs]
        for chunk in chunks:
            _, vt_ref, c0, c = chunk
            vt = vt_ref[:, c0:c0 + c]
            for j, qm in enumerate(qs):
                s = scores(chunk, qm)
                m_new = jnp.maximum(ms[j], jnp.max(s, axis=0, keepdims=True))
                alpha = jnp.exp2(ms[j] - m_new)
                p = jnp.exp2(s - m_new)
                ls[j] = alpha * ls[j] + jnp.sum(p, axis=0, keepdims=True)
                accs[j] = alpha * accs[j] + jnp.dot(vt, p.astype(BF16),
                                                    preferred_element_type=F32)
                ms[j] = m_new
        finish(accs, ls)


def _attn_call(q_arr, q_cb, sources, c_lambda, subln_g, *, lam_init):
    nb, n, _ = q_arr.shape
    tq = min(ATTN_Q_TILE, n)
    sizes = tuple(a.shape[1] for a, _, _ in sources)
    in_specs = [
        pl.BlockSpec((4, C_HEAD_DIM), lambda b, h, i: (0, 0)),
        pl.BlockSpec((C_V_DIM, 1), lambda b, h, i: (0, 0)),
        pl.BlockSpec((1, tq, LANES), lambda b, h, i: (b, i, q_cb + h)),
    ]
    args = [c_lambda, subln_g.reshape(-1, 1), q_arr]
    for arr, kcb, vcb in sources:
        nk = arr.shape[1]
        in_specs.append(pl.BlockSpec((1, nk, LANES), lambda b, h, i, kcb=kcb: (b, 0, kcb + h)))
        in_specs.append(pl.BlockSpec((1, nk, LANES), lambda b, h, i, vcb=vcb: (b, 0, vcb + h)))
        args += [arr, arr]
    kern = functools.partial(_attn_kernel, tq=tq, ck=ATTN_KEYS, sizes=sizes, lam_init=lam_init)
    return pl.pallas_call(
        kern,
        grid=(nb, C_HEADS, n // tq),
        in_specs=in_specs,
        out_specs=pl.BlockSpec((1, tq, C_V_DIM), lambda b, h, i: (b, i, h)),
        out_shape=jax.ShapeDtypeStruct((nb, n, C_HEADS * C_V_DIM), BF16),
        scratch_shapes=[pltpu.VMEM((C_V_DIM, nk), BF16) for nk in sizes]
        + [pltpu.VMEM((8, LANES), F32)],
        compiler_params=_cparams(("parallel", "parallel", "arbitrary")),
        name="diff_attention",
    )(*args)


def _merge_kernel(*refs, tm, seq):
    mixer_refs = refs[:14]
    (x_ref, yc_ref, ga_ref, gb_ref, gc_ref, wa_ref, wb_ref, wc_ref, wo_ref, g1_ref, n2_ref,
     sh2_ref, sc2_ref, xo_ref, ho_ref, ya_ref, yb_ref) = refs[14:]
    c = jnp.dot(yc_ref[0], wc_ref[...], preferred_element_type=F32)
    _local_mixers(*mixer_refs, ya_ref, yb_ref, tm=tm, seq=seq)
    a = jnp.dot(ya_ref[...], wa_ref[...], preferred_element_type=F32)
    b = jnp.dot(yb_ref[...], wb_ref[...], preferred_element_type=F32)
    y = (ga_ref[0].astype(F32) * a + gb_ref[0].astype(F32) * b + gc_ref[0].astype(F32) * c)
    m = jnp.dot(y.astype(BF16), wo_ref[...], preferred_element_type=F32)
    xn = x_ref[0] + g1_ref[0] * m
    xo_ref[0] = xn
    ho_ref[0] = _modulate(xn, n2_ref[...], sh2_ref[0], sc2_ref[0]).astype(BF16)


def _merge_call(x, p, pos, yc, gate_cb, mixer_w, wa, wb, wc, wo, g1, n2g, sh2, sc2, *, seq):
    nb, n, d = x.shape
    au, av, bb, bc, bh = pos
    ln_g, ln_b, ws, bs, conv_w = mixer_w
    tm = min(MERGE_TILE, n)
    hb = tm // BF16_SUBLANES
    nhb = n // BF16_SUBLANES

    def tok(w, cb=0):
        return pl.BlockSpec((1, tm, w), lambda b, i: (b, i, cb))

    def prev(cb):
        return pl.BlockSpec((1, BF16_SUBLANES, COL_BLOCK),
                            lambda b, i: (b, jnp.maximum(i * hb - 1, 0), cb))

    def nxt(cb):
        return pl.BlockSpec((1, BF16_SUBLANES, COL_BLOCK),
                            lambda b, i: (b, jnp.minimum((i + 1) * hb, nhb - 1), cb))

    const = lambda shape: pl.BlockSpec(shape, lambda b, i: (0,) * len(shape))
    bias = jnp.repeat(bs.T, A_GROUP_DIM, axis=1)
    vec = pl.BlockSpec((1, 1, d), lambda b, i: (b, 0, 0))
    return pl.pallas_call(
        functools.partial(_merge_kernel, tm=tm, seq=seq),
        grid=(nb, n // tm),
        in_specs=[tok(COL_BLOCK, au), tok(COL_BLOCK, av), tok(COL_BLOCK, bb), tok(COL_BLOCK, bc),
                  tok(COL_BLOCK, bh), prev(bc), prev(bh), nxt(bc), nxt(bh),
                  const((1, A_WIDTH)), const((1, A_WIDTH)), const((A_GROUPS, CHUNK, CHUNK)),
                  const((CHUNK, A_WIDTH)), const((CONV_W, B_WIDTH)),
                  tok(d), tok(d), tok(d, gate_cb), tok(d, gate_cb + 1), tok(d, gate_cb + 2),
                  _resident(wa.shape, lambda b, i: (0, 0)), _resident(wb.shape, lambda b, i: (0, 0)),
                  _resident(wc.shape, lambda b, i: (0, 0)), _resident(wo.shape, lambda b, i: (0, 0)),
                  vec, const((1, d)), vec, vec],
        out_specs=[tok(d), tok(d)],
        out_shape=[jax.ShapeDtypeStruct((nb, n, d), F32), jax.ShapeDtypeStruct((nb, n, d), BF16)],
        scratch_shapes=[pltpu.VMEM((tm, A_WIDTH), BF16), pltpu.VMEM((tm, B_WIDTH), BF16)],
        compiler_params=_cparams(("parallel", "parallel")),
        name="merge_branches",
    )(p, p, p, p, p, p, p, p, p, ln_g.reshape(1, -1), ln_b.reshape(1, -1), ws.astype(BF16), bias,
      conv_w, x, yc, p, p, p, wa, wb, wc, wo, g1, n2g.reshape(1, d), sh2, sc2)


def _ff_chunks(width, chunk=FFN_CHUNK):
    out, c0 = [], 0
    while c0 < width:
        cw = min(chunk, width - c0)
        out.append((c0, cw))
        c0 += cw
    return out


def _finish(xn, final, ng_ref, sh_ref, sc_ref, out_refs):
    if final:
        out_refs[0][0] = _rmsnorm(xn, ng_ref[...])
    else:
        out_refs[0][0] = xn
        out_refs[1][0] = _modulate(xn, ng_ref[...], sh_ref[0], sc_ref[0]).astype(BF16)


def _ffn_kernel(x_ref, h_ref, wg_ref, wu_ref, wd_ref, g2_ref, ng_ref, sh_ref, sc_ref, *out_refs,
                final):
    h = h_ref[0]
    acc = None
    for c0, cw in _ff_chunks(wg_ref.shape[1]):
        a = jnp.dot(h, wg_ref[:, c0:c0 + cw], preferred_element_type=F32)
        b = jnp.dot(h, wu_ref[:, c0:c0 + cw], preferred_element_type=F32)
        t = (_silu(a) * b).astype(BF16)
        part = jnp.dot(t, wd_ref[c0:c0 + cw, :], preferred_element_type=F32)
        acc = part if acc is None else acc + part
    xn = x_ref[0] + g2_ref[0] * acc
    _finish(xn, final, ng_ref, sh_ref, sc_ref, out_refs)


def _epilogue_specs(nb, n, d, tm, final, idx):
    tok = pl.BlockSpec((1, tm, d), idx)
    if final:
        return [tok], [jax.ShapeDtypeStruct((nb, n, d), F32)]
    return [tok, tok], [jax.ShapeDtypeStruct((nb, n, d), F32), jax.ShapeDtypeStruct((nb, n, d), BF16)]


def _ffn_call(x, h, wg, wu, wd, g2, ng, sh, sc, *, final):
    nb, n, d = x.shape
    tm = min(FFN_TILE, n)
    idx = lambda b, i: (b, i, 0)
    vec = pl.BlockSpec((1, 1, d), lambda b, i: (b, 0, 0))
    out_specs, out_shape = _epilogue_specs(nb, n, d, tm, final, idx)
    return pl.pallas_call(
        functools.partial(_ffn_kernel, final=final),
        grid=(nb, n // tm),
        in_specs=[pl.BlockSpec((1, tm, d), idx), pl.BlockSpec((1, tm, d), idx),
                  _resident(wg.shape, lambda b, i: (0, 0)), _resident(wu.shape, lambda b, i: (0, 0)),
                  _resident(wd.shape, lambda b, i: (0, 0)),
                  vec, pl.BlockSpec((1, d), lambda b, i: (0, 0)), vec, vec],
        out_specs=out_specs,
        out_shape=out_shape,
        compiler_params=_cparams(("parallel", "parallel")),
        name="dense_swiglu",
    )(x, h, wg, wu, wd, g2, ng.reshape(1, d), sh, sc)


MOE_TILE = 2048
MOE_WIN = 256
MOE_SLAB = 128
MOE_ROWS = 1024
MOE_UNIT = 64
MOE_FF = 512
CNT_ROWS = 16


def _route_kernel(h_ref, wr_ref, comb_ref, rank_ref, rankt_ref, cnt_ref, *, n_experts, win):
    h = h_ref[0]
    tm = h.shape[0]
    lane = lax.broadcasted_iota(jnp.int32, (tm, LANES), 1)
    logits = jnp.dot(h, wr_ref[...], preferred_element_type=F32)
    lg = jnp.where(lane < n_experts, logits, -jnp.inf)
    m1 = jnp.max(lg, axis=-1, keepdims=True)
    i1 = jnp.min(jnp.where(lg == m1, lane, LANES), axis=-1, keepdims=True)
    lg2 = jnp.where(lane == i1, -jnp.inf, lg)
    m2 = jnp.max(lg2, axis=-1, keepdims=True)
    i2 = jnp.min(jnp.where(lg2 == m2, lane, LANES), axis=-1, keepdims=True)
    e2 = jnp.exp(m2 - m1)
    w1 = 1.0 / (1.0 + e2)
    comb_ref[0] = jnp.where(lane == i1, w1, 0.0) + jnp.where(lane == i2, e2 * w1, 0.0)
    sel = (lane == i1) | (lane == i2)
    tri = jnp.where(lax.broadcasted_iota(jnp.int32, (win, win), 0)
                    > lax.broadcasted_iota(jnp.int32, (win, win), 1), 1.0, 0.0).astype(BF16)
    base = jnp.zeros((1, LANES), F32)
    bases = []
    for w in range(tm // win):
        rows = slice(w * win, (w + 1) * win)
        sw = jnp.where(sel[rows], 1.0, 0.0)
        excl = jnp.dot(tri, sw.astype(BF16), preferred_element_type=F32)
        rank_ref[0, rows, :] = jnp.where(sel[rows], excl + base, -1.0)
        bases.append(base)
        base = base + jnp.sum(sw, axis=0, keepdims=True)
    bases.append(base)
    bases += [jnp.zeros((1, LANES), F32)] * (CNT_ROWS - len(bases))
    cnt_ref[0] = jnp.concatenate(bases, axis=0).astype(jnp.int32)
    rankt_ref[0] = rank_ref[0].T[:CNT_ROWS]


def _route_call(h, w_router, tm):
    nb, n, d = h.shape
    n_experts = w_router.shape[1]
    assert n_experts <= CNT_ROWS and tm // MOE_WIN + 1 <= CNT_ROWS
    nt = n // tm
    wr = jnp.zeros((d, LANES), BF16).at[:, :n_experts].set(w_router.astype(BF16))
    tok = pl.BlockSpec((1, tm, LANES), lambda b, i: (b, i, 0))
    per_tile = lambda r, c: pl.BlockSpec((1, r, c), lambda b, i: (b * nt + i, 0, 0))
    return pl.pallas_call(
        functools.partial(_route_kernel, n_experts=n_experts, win=min(MOE_WIN, tm)),
        grid=(nb, nt),
        in_specs=[pl.BlockSpec((1, tm, d), lambda b, i: (b, i, 0)),
                  pl.BlockSpec((d, LANES), lambda b, i: (0, 0))],
        out_specs=[tok, tok, per_tile(CNT_ROWS, tm), per_tile(CNT_ROWS, LANES)],
        out_shape=[jax.ShapeDtypeStruct((nb, n, LANES), F32), jax.ShapeDtypeStruct((nb, n, LANES), F32),
                   jax.ShapeDtypeStruct((nb * nt, CNT_ROWS, tm), F32),
                   jax.ShapeDtypeStruct((nb * nt, CNT_ROWS, LANES), jnp.int32)],
        compiler_params=_cparams(("parallel", "parallel")),
        name="moe_route",
    )(h, wr)


def _moe_kernel(cnt_ref, x_ref, h_ref, comb_ref, rank_ref, rankt_ref, wg_hbm, wu_hbm, wd_hbm, g2_ref,
                ng_ref, sh_ref, sc_ref, *rest, final, n_experts, win, slab, rblk, unit, ff):
    out_refs, (hc_ref, yc_ref, wg_buf, wu_buf, wd_buf, sem) = rest[:-6], rest[-6:]
    acc_ref = out_refs[0].at[0]
    b, i, e = (pl.program_id(k) for k in range(3))
    tm, d = h_ref.shape[1], h_ref.shape[2]
    nw = tm // win
    nf = wg_hbm.shape[2] // ff
    tile = b * pl.num_programs(1) + i
    cbase = tile * (CNT_ROWS * n_experts)
    step = tile * n_experts + e
    last_step = pl.num_programs(0) * pl.num_programs(1) * n_experts - 1

    def weight_copies(expert, f, slot):
        cols = pl.ds(pl.multiple_of(f * ff, ff), ff)
        return (pltpu.make_async_copy(wg_hbm.at[expert, :, cols], wg_buf.at[slot], sem.at[slot, 0]),
                pltpu.make_async_copy(wu_hbm.at[expert, :, cols], wu_buf.at[slot], sem.at[slot, 1]),
                pltpu.make_async_copy(wd_hbm.at[expert, cols, :], wd_buf.at[slot], sem.at[slot, 2]))

    def fetch(expert, f, slot):
        for copy in weight_copies(expert, f, slot):
            copy.start()

    @pl.when(step == 0)
    def _():
        fetch(e, 0, 0)

    def count(w):
        return cnt_ref[cbase + w * n_experts + e]

    def slabs(w):
        row0 = (count(w) // BF16_SUBLANES) * BF16_SUBLANES
        return row0, (count(w + 1) - row0 + slab - 1) // slab

    @pl.when(e == 0)
    def _():
        acc_ref[...] = jnp.zeros_like(acc_ref)

    total = count(nw)

    def zero(s, carry):
        rows = pl.ds(pl.multiple_of(s * slab, slab), slab)
        hc_ref[rows, :] = jnp.zeros((slab, d), BF16)
        yc_ref[rows, :] = jnp.zeros((slab, d), F32)
        return carry

    lax.fori_loop(0, jnp.minimum((total + unit + 2 * slab) // slab, hc_ref.shape[0] // slab),
                  zero, 0)

    def gather(w):
        row0, nsl = slabs(w)
        rt = rankt_ref[0, 0, :, w * win:(w + 1) * win]
        hw = h_ref[0, w * win:(w + 1) * win, :]

        def body(s, carry):
            r0 = pl.multiple_of(row0 + s * slab, BF16_SUBLANES)
            rid = (r0 + lax.broadcasted_iota(jnp.int32, (slab, win), 0)).astype(F32)
            onehot = jnp.where(rt == rid, 1.0, 0.0).astype(BF16)
            rows = pl.ds(r0, slab)
            got = jnp.dot(onehot, hw, preferred_element_type=F32)
            hc_ref[rows, :] = (hc_ref[rows, :].astype(F32) + got).astype(BF16)
            return carry

        return body, nsl

    for w in range(nw):
        gather(w)[0](0, 0)
    for w in range(nw):
        body, nsl = gather(w)
        lax.fori_loop(1, nsl, body, 0)

    units = (total + unit - 1) // unit
    per_block = rblk // unit
    nfull = units // per_block
    rest = units - per_block * nfull

    def ff_block(f, carry):
        slot = lax.rem(step * nf + f, 2)

        @pl.when(f + 1 < nf)
        def _():
            fetch(e, f + 1, 1 - slot)

        @pl.when((f + 1 == nf) & (step < last_step))
        def _():
            fetch(lax.rem(e + 1, n_experts), 0, 1 - slot)

        for copy in weight_copies(e, f, slot):
            copy.wait()

        def ffn(r0, size):
            rows = pl.ds(pl.multiple_of(r0, unit), size)
            hb = hc_ref[rows, :]
            a = jnp.dot(hb, wg_buf[slot], preferred_element_type=F32)
            u = jnp.dot(hb, wu_buf[slot], preferred_element_type=F32)
            t = (_silu(a) * u).astype(BF16)
            yc_ref[rows, :] += jnp.dot(t, wd_buf[slot], preferred_element_type=F32)

        def full(r, c):
            ffn(r * rblk, rblk)
            return c

        lax.fori_loop(0, nfull, full, 0)
        for k in range(1, per_block):
            @pl.when(rest == k)
            def _(k=k):
                ffn(nfull * rblk, k * unit)

        return carry

    lax.fori_loop(0, nf, ff_block, 0)

    def spread(w):
        row0, nsl = slabs(w)
        trows = slice(w * win, (w + 1) * win)
        mine = lax.broadcasted_iota(jnp.int32, (win, LANES), 1) == e
        rcol = jnp.sum(jnp.where(mine, rank_ref[0, trows, :], 0.0), axis=-1, keepdims=True)
        wcol = jnp.sum(jnp.where(mine, comb_ref[0, trows, :], 0.0), axis=-1, keepdims=True)

        def body(s, carry):
            r0 = pl.multiple_of(row0 + s * slab, BF16_SUBLANES)
            cid = (r0 + lax.broadcasted_iota(jnp.int32, (win, slab), 1)).astype(F32)
            onehot = jnp.where(rcol == cid, 1.0, 0.0).astype(BF16)
            yb = yc_ref[pl.ds(r0, slab), :].astype(BF16)
            acc_ref[trows, :] += wcol * jnp.dot(onehot, yb, preferred_element_type=F32)
            return carry

        return body, nsl

    bodies = [spread(w) for w in range(nw)]
    for body, _ in bodies:
        body(0, 0)
    for body, nsl in bodies:
        lax.fori_loop(1, nsl, body, 0)

    @pl.when(e == n_experts - 1)
    def _():
        xn = x_ref[0] + g2_ref[0] * acc_ref[...]
        _finish(xn, final, ng_ref, sh_ref, sc_ref, out_refs)


def _moe_call(x, h, w_router, wg, wu, wd, g2, ng, sh, sc, *, final):
    nb, n, d = x.shape
    n_experts, _, dff = wg.shape
    tm = min(MOE_TILE, n)
    win = min(MOE_WIN, tm)
    nt = n // tm
    comb, rank, rankt, cnt = _route_call(h, w_router, tm)
    cnt = cnt[:, :, :n_experts].reshape(-1)
    rankt = rankt.reshape(nb * nt, CNT_ROWS, 1, tm)
    cap_ffn = -(-tm // MOE_UNIT) * MOE_UNIT
    cap = -(-max(cap_ffn, tm + MOE_SLAB) // MOE_SLAB) * MOE_SLAB

    idx = lambda b, i, e, c: (b, i, 0)
    vec = pl.BlockSpec((1, 1, d), lambda b, i, e, c: (b, 0, 0))
    hbm = pl.BlockSpec(memory_space=pl.ANY)
    out_specs, out_shape = _epilogue_specs(nb, n, d, tm, final, idx)
    grid_spec = pltpu.PrefetchScalarGridSpec(
        num_scalar_prefetch=1,
        grid=(nb, nt, n_experts),
        in_specs=[_resident((1, tm, d), idx), _resident((1, tm, d), idx),
                  _resident((1, tm, LANES), idx), _resident((1, tm, LANES), idx),
                  pl.BlockSpec((1, 1, 1, tm), lambda b, i, e, c: (b * nt + i, e, 0, 0)),
                  hbm, hbm, hbm,
                  vec, pl.BlockSpec((1, d), lambda b, i, e, c: (0, 0)), vec, vec],
        out_specs=out_specs,
        scratch_shapes=[pltpu.VMEM((cap, d), BF16), pltpu.VMEM((cap, d), F32),
                        pltpu.VMEM((2, d, MOE_FF), BF16), pltpu.VMEM((2, d, MOE_FF), BF16),
                        pltpu.VMEM((2, MOE_FF, d), BF16), pltpu.SemaphoreType.DMA((2, 3))],
    )
    return pl.pallas_call(
        functools.partial(_moe_kernel, final=final, n_experts=n_experts, win=win, slab=MOE_SLAB,
                          rblk=MOE_ROWS, unit=MOE_UNIT, ff=MOE_FF),
        grid_spec=grid_spec,
        out_shape=out_shape,
        compiler_params=_cparams(("arbitrary", "arbitrary", "arbitrary")),
        name="moe_swiglu",
    )(cnt, x, h, comb, rank, rankt, wg, wu, wd, g2, ng.reshape(1, d), sh, sc)


def kernel(x, c, ctx, c_ctx, w_ada, b_ada, norm1_g, norm2_g, w_in, b_gate, a_ln_g, a_ln_b, a_ws,
           a_bs, b_conv, c_lambda, c_subln_g, w_a_out, w_b_out, w_c_out, w_o, ff_w_gate, ff_w_up,
           ff_w_down, moe_w_router, moe_w_gate, moe_w_up, moe_w_down, final_norm_g):
    bsz, n, d = x.shape
    nc = ctx.shape[1]
    depth = w_ada.shape[0]
    cols = _Cols(d)
    head_cb = COL_BLOCK // LANES
    gate_cb = cols.out(cols.gate) * COL_BLOCK // d
    mix_pos = tuple(cols.out(j) for j in (cols.au, cols.av, cols.bb, cols.bc, cols.bh))
    cq_l, ck_l, cv_l = (cols.out(j) * head_cb for j in (cols.cq, cols.ck, cols.cv))

    pad = (-(bsz + 1)) % 8
    cc = jnp.concatenate([c, c_ctx[None], jnp.zeros((pad, d), F32)], axis=0)
    mod = _ada_call(cc, w_ada, b_ada)

    def mods(l):
        lat = [mod[l, :bsz, k * d:(k + 1) * d].reshape(bsz, 1, d) for k in range(N_MOD)]
        con = [mod[l, bsz:bsz + 1, k * d:(k + 1) * d].reshape(1, 1, d) for k in range(N_MOD)]
        return lat, con

    tables = _rope_tables(n)
    w_in_b = w_in.astype(BF16)
    xl = x
    xc = ctx.reshape(1, bsz * nc, d)
    lat, con = mods(0)
    h = _mod_call(xl, norm1_g[0], lat[0], lat[1])
    hc = _mod_call(xc, norm1_g[0], con[0], con[1])

    for l in range(depth):
        last = l == depth - 1
        lam_init = 0.8 - 0.6 * math.exp(-0.3 * l)
        lat, con = mods(l)
        if not last:
            nlat, ncon = mods(l + 1)
            nxt_l = (norm1_g[l + 1], nlat[0], nlat[1])
            nxt_c = (norm1_g[l + 1], ncon[0], ncon[1])
        else:
            nxt_l = (final_norm_g, lat[0], lat[1])
            nxt_c = None

        p = _in_call(h, w_in_b[l], b_gate[l], tables, j0=0, nj=cols.end, rope=True)
        if last:
            pc = _in_call(hc, w_in_b[l], b_gate[l], tables, j0=cols.ck, nj=cols.gate - cols.ck,
                          rope=False)
            ck_c, cv_c = 0, (cols.cv - cols.ck) * head_cb
        else:
            pc = _in_call(hc, w_in_b[l], b_gate[l], tables, j0=0, nj=cols.end, rope=False)
            ck_c, cv_c = ck_l, cv_l
        pc_seq = pc.reshape(bsz, nc, -1)

        wa, wb, wc, wo = (w.astype(BF16) for w in (w_a_out[l], w_b_out[l], w_c_out[l], w_o[l]))

        def channel(xs, hs, g2, nxt, final):
            i = l // 2
            if l % 2 == 0:
                return _ffn_call(xs, hs, ff_w_gate[i].astype(BF16), ff_w_up[i].astype(BF16),
                                 ff_w_down[i].astype(BF16), g2, *nxt, final=final)
            return _moe_call(xs, hs, moe_w_router[i], moe_w_gate[i].astype(BF16),
                             moe_w_up[i].astype(BF16), moe_w_down[i].astype(BF16), g2, *nxt,
                             final=final)

        mixer_w = (a_ln_g[l], a_ln_b[l], a_ws[l], a_bs[l], b_conv[l])
        yc = _attn_call(p, cq_l, [(p, ck_l, cv_l), (pc_seq, ck_c, cv_c)],
                        c_lambda[l], c_subln_g[l], lam_init=lam_init)
        x1, h2 = _merge_call(xl, p, mix_pos, yc, gate_cb, mixer_w, wa, wb, wc, wo, lat[2],
                             norm2_g[l], lat[3], lat[4], seq=n)
        res = channel(x1, h2, lat[5], nxt_l, last)
        if last:
            return res[0]
        xl, h = res

        yc_c = _attn_call(pc_seq, cq_l, [(pc_seq, ck_c, cv_c)], c_lambda[l], c_subln_g[l],
                          lam_init=lam_init)
        xc1, hc2 = _merge_call(xc, pc, mix_pos, yc_c.reshape(1, bsz * nc, -1), gate_cb, mixer_w,
                               wa, wb, wc, wo, con[2], norm2_g[l], con[3], con[4], seq=nc)
        xc, hc = channel(xc1, hc2, con[5], nxt_c, False)
```

```python
import functools
import math

import jax
import jax.numpy as jnp
from jax import lax
from jax.experimental import pallas as pl
from jax.experimental.pallas import tpu as pltpu

F32 = jnp.float32
BF16 = jnp.bfloat16

EPS = 1e-6
GRID_W = 64
N_MOD = 6
N_BRANCH = 3
CHUNK = 128
A_GROUPS = 8
A_GROUP_DIM = 64
A_WIDTH = A_GROUPS * A_GROUP_DIM
B_WIDTH = 512
CONV_W = 3
C_HEADS = 8
C_HEAD_DIM = 64
C_V_DIM = 2 * C_HEAD_DIM
ROPE_BASE = 10000.0
ROPE_PAIR = C_HEAD_DIM // 4
LOG2E = 1.4426950408889634

LANES = 128
BF16_SUBLANES = 16
V7X_VMEM_LIMIT = 56 * 1024 * 1024

COL_BLOCK = 512
ADA_COLS = 1536
MOD_TILE = 1024
IN_TILE, IN_SUB = 4096, 512
ATTN_Q_TILE, ATTN_KEYS = 512, 2048
MERGE_TILE = 512
FFN_TILE, FFN_CHUNK = 512, 512


def _cparams(sem, vmem_limit=V7X_VMEM_LIMIT):
    return pltpu.CompilerParams(dimension_semantics=sem, vmem_limit_bytes=vmem_limit)


def _resident(shape, index_map):
    return pl.BlockSpec(shape, index_map, pipeline_mode=pl.Buffered(1))


def _sigmoid(v):
    return 0.5 * (1.0 + jnp.tanh(0.5 * v))


def _silu(v):
    return v * _sigmoid(v)


def _modulate(x, g, sh, sc):
    ms = jnp.mean(x * x, axis=-1, keepdims=True)
    y = x * lax.rsqrt(ms + EPS)
    return (y * g) * (1.0 + sc) + sh


def _rmsnorm(x, g):
    ms = jnp.mean(x * x, axis=-1, keepdims=True)
    return x * lax.rsqrt(ms + EPS) * g


def _ada_kernel(c_ref, w_ref, b_ref, o_ref):
    s = _silu(c_ref[...])
    o_ref[0] = jnp.dot(s, w_ref[0], preferred_element_type=F32,
                       precision=lax.Precision.HIGHEST) + b_ref[0]


def _ada_call(cc, w_ada, b_ada):
    depth, d, cols = w_ada.shape
    rows = cc.shape[0]
    tn = ADA_COLS
    return pl.pallas_call(
        _ada_kernel,
        grid=(depth, cols // tn),
        in_specs=[
            pl.BlockSpec((rows, d), lambda l, j: (0, 0)),
            pl.BlockSpec((1, d, tn), lambda l, j: (l, 0, j)),
            pl.BlockSpec((1, 1, tn), lambda l, j: (l, 0, j)),
        ],
        out_specs=pl.BlockSpec((1, rows, tn), lambda l, j: (l, 0, j)),
        out_shape=jax.ShapeDtypeStruct((depth, rows, cols), F32),
        compiler_params=_cparams(("parallel", "parallel")),
        name="ada_proj",
    )(cc, w_ada, b_ada.reshape(depth, 1, cols))


def _mod_kernel(x_ref, g_ref, sh_ref, sc_ref, o_ref):
    o_ref[0] = _modulate(x_ref[0], g_ref[...], sh_ref[0], sc_ref[0]).astype(BF16)


def _mod_call(x, g, sh, sc):
    nb, n, d = x.shape
    tm = min(MOD_TILE, n)
    vec = pl.BlockSpec((1, 1, d), lambda b, i: (b, 0, 0))
    return pl.pallas_call(
        _mod_kernel,
        grid=(nb, n // tm),
        in_specs=[pl.BlockSpec((1, tm, d), lambda b, i: (b, i, 0)),
                  pl.BlockSpec((1, d), lambda b, i: (0, 0)), vec, vec],
        out_specs=pl.BlockSpec((1, tm, d), lambda b, i: (b, i, 0)),
        out_shape=jax.ShapeDtypeStruct((nb, n, d), BF16),
        compiler_params=_cparams(("parallel", "parallel")),
        name="modulate",
    )(x, g.reshape(1, d), sh, sc)


class _Cols:
    def __init__(self, d):
        self.au = 0
        self.av = self.au + A_WIDTH // COL_BLOCK
        self.bb = self.av + A_WIDTH // COL_BLOCK
        self.bc = self.bb + B_WIDTH // COL_BLOCK
        self.bh = self.bc + B_WIDTH // COL_BLOCK
        self.cq = self.bh + B_WIDTH // COL_BLOCK
        qk = C_HEADS * 2 * C_HEAD_DIM // COL_BLOCK
        self.ck = self.cq + qk
        self.cv = self.ck + qk
        self.gate = self.cv + C_HEADS * C_V_DIM // COL_BLOCK
        self.end = self.gate + N_BRANCH * d // COL_BLOCK

    def out(self, jw):
        return (jw - self.gate) % self.end


def _in_kernel(h_ref, w_ref, bg_ref, cos_ref, sa_ref, sb_ref, o_ref, *, cols, j0, rope, qscale,
               sub):
    j = pl.program_id(2) + j0
    tm = h_ref.shape[1]

    def run(epilogue):
        def mm(r):
            return jnp.dot(h_ref[0, r * sub:(r + 1) * sub, :], w_ref[...],
                           preferred_element_type=F32)
        nxt = mm(0)
        for r in range(tm // sub):
            acc = nxt
            if r + 1 < tm // sub:
                nxt = mm(r + 1)
            epilogue(acc, slice(r * sub, (r + 1) * sub))

    @pl.when(j < cols.bb)
    def _():
        def gelu(acc, rows):
            o_ref[0, rows, :] = jax.nn.gelu(acc).astype(BF16)
        run(gelu)

    @pl.when(((j >= cols.bb) & (j < cols.cq)) | ((j >= cols.cv) & (j < cols.gate)))
    def _():
        def plain(acc, rows):
            o_ref[0, rows, :] = acc.astype(BF16)
        run(plain)

    @pl.when((j >= cols.cq) & (j < cols.cv))
    def _():
        scale = jnp.where(j < cols.ck, qscale, 1.0).astype(F32)

        def rotary(acc, rows):
            if not rope:
                o_ref[0, rows, :] = (acc * scale).astype(BF16)
                return
            cos, sa, sb = cos_ref[rows, :] * scale, sa_ref[rows, :] * scale, sb_ref[rows, :] * scale
            for s in range(COL_BLOCK // LANES):
                t = acc[:, s * LANES:(s + 1) * LANES]
                r = (t * cos + pltpu.roll(t, LANES - ROPE_PAIR, 1) * sa
                     + pltpu.roll(t, ROPE_PAIR, 1) * sb)
                o_ref[0, rows, s * LANES:(s + 1) * LANES] = r.astype(BF16)
        run(rotary)

    @pl.when(j >= cols.gate)
    def _():
        def gate(acc, rows):
            o_ref[0, rows, :] = _sigmoid(acc + bg_ref[...]).astype(BF16)
        run(gate)


def _in_call(h, w, b_gate, tables, *, j0, nj, rope):
    nb, n, d = h.shape
    cols = _Cols(d)
    tm = min(IN_TILE, n)
    full = (j0 == 0) and (nj == cols.end)
    if rope:
        cos, sa, sb = tables
    else:
        cos = sa = sb = jnp.zeros((tm, LANES), F32)
    tab = pl.BlockSpec((tm, LANES), lambda b, i, j: (i if rope else 0, 0))

    def out_idx(b, i, j):
        if full:
            return (b, i, jnp.where(j >= cols.gate, j - cols.gate, j + cols.end - cols.gate))
        return (b, i, j)
    kern = functools.partial(_in_kernel, cols=cols, j0=j0, rope=rope,
                             qscale=C_HEAD_DIM ** -0.5 * LOG2E, sub=min(IN_SUB, tm))
    return pl.pallas_call(
        kern,
        grid=(nb, n // tm, nj),
        in_specs=[
            pl.BlockSpec((1, tm, d), lambda b, i, j: (b, i, 0)),
            pl.BlockSpec((d, COL_BLOCK), lambda b, i, j: (0, j + j0)),
            pl.BlockSpec((1, COL_BLOCK), lambda b, i, j: (0, jnp.maximum(j + j0 - cols.gate, 0))),
            tab, tab, tab,
        ],
        out_specs=pl.BlockSpec((1, tm, COL_BLOCK), out_idx),
        out_shape=jax.ShapeDtypeStruct((nb, n, nj * COL_BLOCK), BF16),
        compiler_params=_cparams(("parallel", "parallel", "arbitrary")),
        name="in_proj",
    )(h, w, b_gate.reshape(1, -1), cos, sa, sb)


def _rope_tables(n):
    rows = n // GRID_W
    r = jnp.repeat(jnp.arange(rows, dtype=F32), GRID_W)
    col = jnp.tile(jnp.arange(GRID_W, dtype=F32), rows)
    quarter = C_HEAD_DIM // 4
    inv = ROPE_BASE ** (-jnp.arange(quarter, dtype=F32) / quarter)
    ar = r[:, None] * inv
    ac = col[:, None] * inv
    ang = jnp.concatenate([ar, ar, ac, ac], axis=-1)
    ang = jnp.tile(ang, (1, LANES // C_HEAD_DIM))
    cos, sin = jnp.cos(ang), jnp.sin(ang)
    first_of_pair = (jnp.arange(LANES) // quarter) % 2 == 0
    sa = jnp.where(first_of_pair, -sin, 0.0)
    sb = jnp.where(first_of_pair, 0.0, sin)
    return cos, sa, sb


def _local_mixers(u_ref, v_ref, bg_ref, cg_ref, hh_ref, cgp_ref, hhp_ref, cgn_ref, hhn_ref,
                  lng_ref, lnb_ref, ws_ref, bias_ref, cw_ref, ya_ref, yb_ref, *, tm, seq):
    i = pl.program_id(1)
    lane = lax.broadcasted_iota(jnp.int32, (CHUNK, LANES), 1)
    lo = lane < A_GROUP_DIM
    for c in range(tm // CHUNK):
        rows = pl.ds(c * CHUNK, CHUNK)
        v = v_ref[0, rows, :].astype(F32)
        mu = jnp.mean(v, axis=-1, keepdims=True)
        var = jnp.mean(jnp.square(v - mu), axis=-1, keepdims=True)
        vn = ((v - mu) * lax.rsqrt(var + EPS) * lng_ref[...] + lnb_ref[...]).astype(BF16)
        for k in range(A_WIDTH // LANES):
            blk = vn[:, k * LANES:(k + 1) * LANES]
            zero = jnp.zeros_like(blk)
            mixed = (jnp.dot(ws_ref[2 * k], jnp.where(lo, blk, zero), preferred_element_type=F32)
                     + jnp.dot(ws_ref[2 * k + 1], jnp.where(lo, zero, blk), preferred_element_type=F32)
                     + bias_ref[:, k * LANES:(k + 1) * LANES])
            u = u_ref[0, rows, k * LANES:(k + 1) * LANES].astype(F32)
            ya_ref[rows, k * LANES:(k + 1) * LANES] = (u * mixed).astype(BF16)

    z = cg_ref[0].astype(F32) * hh_ref[0].astype(F32)
    last = BF16_SUBLANES - 1
    z_prev = cgp_ref[0, last:last + 1, :].astype(F32) * hhp_ref[0, last:last + 1, :].astype(F32)
    z_next = cgn_ref[0, 0:1, :].astype(F32) * hhn_ref[0, 0:1, :].astype(F32)
    row = lax.broadcasted_iota(jnp.int32, (tm, 1), 0)
    pos = lax.rem(i * tm + row, seq)
    zm1 = jnp.where(row == 0, z_prev, pltpu.roll(z, 1, 0))
    zm1 = jnp.where(pos == 0, 0.0, zm1)
    zp1 = jnp.where(row == tm - 1, z_next, pltpu.roll(z, tm - 1, 0))
    zp1 = jnp.where(pos == seq - 1, 0.0, zp1)
    conv = cw_ref[0:1, :] * zm1 + cw_ref[1:2, :] * z + cw_ref[2:3, :] * zp1
    yb_ref[...] = (bg_ref[0].astype(F32) * conv).astype(BF16)


def _attn_kernel(*refs, tq, ck, sizes, lam_init):
    nsrc = len(sizes)
    cl_ref, g_ref, q_ref = refs[:3]
    kv_refs = refs[3:3 + 2 * nsrc]
    o_ref = refs[3 + 2 * nsrc]
    vt_refs = refs[4 + 2 * nsrc:4 + 3 * nsrc]
    kmax_ref = refs[4 + 3 * nsrc]
    half = lax.broadcasted_iota(jnp.int32, (1, LANES), 1) < C_HEAD_DIM

    @pl.when(pl.program_id(2) == 0)
    def _():
        per_map = jnp.where(lax.broadcasted_iota(jnp.int32, (LANES, LANES), 0) // C_HEAD_DIM
                            == lax.broadcasted_iota(jnp.int32, (LANES, LANES), 1), 1.0, 0.0)
        kmax2 = jnp.zeros((1, LANES), F32)
        for src in range(nsrc):
            vt_refs[src][...] = kv_refs[2 * src + 1][0].astype(F32).T.astype(BF16)
            k = kv_refs[2 * src][0]
            n2 = jnp.dot(k * k, per_map.astype(BF16), preferred_element_type=F32)
            kmax2 = jnp.maximum(kmax2, jnp.max(n2, axis=0, keepdims=True))
        kmax_ref[0:1, :] = jnp.sqrt(kmax2) * (1.0 + 2.0 ** -7)

    q = q_ref[0]
    zero = jnp.zeros_like(q)
    qs = [jnp.where(half, q, zero), jnp.where(half, zero, q)]
    chunks = [(kv_refs[2 * src], vt_refs[src], c0, min(ck, nk))
              for src, nk in enumerate(sizes) for c0 in range(0, nk, min(ck, nk))]

    def scores(chunk, qm):
        k_ref, _, c0, c = chunk
        return lax.dot_general(k_ref[0, c0:c0 + c, :], qm, (((1,), (1,)), ((), ())),
                               preferred_element_type=F32)

    def finish(accs, ls):
        cl = cl_ref[...]
        lam = (jnp.exp(jnp.sum(cl[0:1] * cl[1:2], axis=-1, keepdims=True))
               - jnp.exp(jnp.sum(cl[2:3] * cl[3:4], axis=-1, keepdims=True)) + lam_init)
        o = accs[0] / ls[0] - lam * (accs[1] / ls[1])
        ms = jnp.mean(o * o, axis=0, keepdims=True)
        y = o * lax.rsqrt(ms + EPS) * (g_ref[...] * (1.0 - lam_init))
        o_ref[0] = y.T.astype(BF16)

    qsq = jnp.square(q.astype(F32).T)
    refs_r = [jnp.sqrt(jnp.sum(qsq[j * C_HEAD_DIM:(j + 1) * C_HEAD_DIM], axis=0, keepdims=True))
              * kmax_ref[0:1, j:j + 1] for j in range(2)]
    ls = [jnp.zeros((1, tq), F32) for _ in qs]
    accs = [jnp.zeros((C_V_DIM, tq), F32) for _ in qs]
    s_next = [scores(chunks[0], qm) for qm in qs]
    for t, (_, vt_ref, c0, c) in enumerate(chunks):
        s_cur = s_next
        if t + 1 < len(chunks):
            s_next = [scores(chunks[t + 1], qm) for qm in qs]
        vt = vt_ref[:, c0:c0 + c]
        for j, s in enumerate(s_cur):
            p = jnp.exp2(s - refs_r[j])
            ls[j] = ls[j] + jnp.sum(p, axis=0, keepdims=True)
            accs[j] = accs[j] + jnp.dot(vt, p.astype(BF16), preferred_element_type=F32)
    finish(accs, ls)

    healthy = jnp.min(jnp.minimum(ls[0], ls[1])) >= 2.0 ** -60

    @pl.when(jnp.logical_not(healthy))
    def _():
        ms = [jnp.full((1, tq), -jnp.inf, F32) for _ in qs]
        ls = [jnp.zeros((1, tq), F32) for _ in qs]
        accs = [jnp.zeros((C_V_DIM, tq), F32) for _ in qs]
        for chunk in chunks:
            _, vt_ref, c0, c = chunk
            vt = vt_ref[:, c0:c0 + c]
            for j, qm in enumerate(qs):
                s = scores(chunk, qm)
                m_new = jnp.maximum(ms[j], jnp.max(s, axis=0, keepdims=True))
                alpha = jnp.exp2(ms[j] - m_new)
                p = jnp.exp2(s - m_new)
                ls[j] = alpha * ls[j] + jnp.sum(p, axis=0, keepdims=True)
                accs[j] = alpha * accs[j] + jnp.dot(vt, p.astype(BF16),
                                                    preferred_element_type=F32)
                ms[j] = m_new
        finish(accs, ls)


def _attn_call(q_arr, q_cb, sources, c_lambda, subln_g, *, lam_init):
    nb, n, _ = q_arr.shape
    tq = min(ATTN_Q_TILE, n)
    sizes = tuple(a.shape[1] for a, _, _ in sources)
    in_specs = [
        pl.BlockSpec((4, C_HEAD_DIM), lambda b, h, i: (0, 0)),
        pl.BlockSpec((C_V_DIM, 1), lambda b, h, i: (0, 0)),
        pl.BlockSpec((1, tq, LANES), lambda b, h, i: (b, i, q_cb + h)),
    ]
    args = [c_lambda, subln_g.reshape(-1, 1), q_arr]
    for arr, kcb, vcb in sources:
        nk = arr.shape[1]
        in_specs.append(pl.BlockSpec((1, nk, LANES), lambda b, h, i, kcb=kcb: (b, 0, kcb + h)))
        in_specs.append(pl.BlockSpec((1, nk, LANES), lambda b, h, i, vcb=vcb: (b, 0, vcb + h)))
        args += [arr, arr]
    kern = functools.partial(_attn_kernel, tq=tq, ck=ATTN_KEYS, sizes=sizes, lam_init=lam_init)
    return pl.pallas_call(
        kern,
        grid=(nb, C_HEADS, n // tq),
        in_specs=in_specs,
        out_specs=pl.BlockSpec((1, tq, C_V_DIM), lambda b, h, i: (b, i, h)),
        out_shape=jax.ShapeDtypeStruct((nb, n, C_HEADS * C_V_DIM), BF16),
        scratch_shapes=[pltpu.VMEM((C_V_DIM, nk), BF16) for nk in sizes]
        + [pltpu.VMEM((8, LANES), F32)],
        compiler_params=_cparams(("parallel", "parallel", "arbitrary")),
        name="diff_attention",
    )(*args)


def _merge_kernel(*refs, tm, seq):
    mixer_refs = refs[:14]
    (x_ref, yc_ref, ga_ref, gb_ref, gc_ref, wa_ref, wb_ref, wc_ref, wo_ref, g1_ref, n2_ref,
     sh2_ref, sc2_ref, xo_ref, ho_ref, ya_ref, yb_ref) = refs[14:]
    c = jnp.dot(yc_ref[0], wc_ref[...], preferred_element_type=F32)
    _local_mixers(*mixer_refs, ya_ref, yb_ref, tm=tm, seq=seq)
    a = jnp.dot(ya_ref[...], wa_ref[...], preferred_element_type=F32)
    b = jnp.dot(yb_ref[...], wb_ref[...], preferred_element_type=F32)
    y = (ga_ref[0].astype(F32) * a + gb_ref[0].astype(F32) * b + gc_ref[0].astype(F32) * c)
    m = jnp.dot(y.astype(BF16), wo_ref[...], preferred_element_type=F32)
    xn = x_ref[0] + g1_ref[0] * m
    xo_ref[0] = xn
    ho_ref[0] = _modulate(xn, n2_ref[...], sh2_ref[0], sc2_ref[0]).astype(BF16)


def _merge_call(x, p, pos, yc, gate_cb, mixer_w, wa, wb, wc, wo, g1, n2g, sh2, sc2, *, seq):
    nb, n, d = x.shape
    au, av, bb, bc, bh = pos
    ln_g, ln_b, ws, bs, conv_w = mixer_w
    tm = min(MERGE_TILE, n)
    hb = tm // BF16_SUBLANES
    nhb = n // BF16_SUBLANES

    def tok(w, cb=0):
        return pl.BlockSpec((1, tm, w), lambda b, i: (b, i, cb))

    def prev(cb):
        return pl.BlockSpec((1, BF16_SUBLANES, COL_BLOCK),
                            lambda b, i: (b, jnp.maximum(i * hb - 1, 0), cb))

    def nxt(cb):
        return pl.BlockSpec((1, BF16_SUBLANES, COL_BLOCK),
                            lambda b, i: (b, jnp.minimum((i + 1) * hb, nhb - 1), cb))

    const = lambda shape: pl.BlockSpec(shape, lambda b, i: (0,) * len(shape))
    bias = jnp.repeat(bs.T, A_GROUP_DIM, axis=1)
    vec = pl.BlockSpec((1, 1, d), lambda b, i: (b, 0, 0))
    return pl.pallas_call(
        functools.partial(_merge_kernel, tm=tm, seq=seq),
        grid=(nb, n // tm),
        in_specs=[tok(COL_BLOCK, au), tok(COL_BLOCK, av), tok(COL_BLOCK, bb), tok(COL_BLOCK, bc),
                  tok(COL_BLOCK, bh), prev(bc), prev(bh), nxt(bc), nxt(bh),
                  const((1, A_WIDTH)), const((1, A_WIDTH)), const((A_GROUPS, CHUNK, CHUNK)),
                  const((CHUNK, A_WIDTH)), const((CONV_W, B_WIDTH)),
                  tok(d), tok(d), tok(d, gate_cb), tok(d, gate_cb + 1), tok(d, gate_cb + 2),
                  _resident(wa.shape, lambda b, i: (0, 0)), _resident(wb.shape, lambda b, i: (0, 0)),
                  _resident(wc.shape, lambda b, i: (0, 0)), _resident(wo.shape, lambda b, i: (0, 0)),
                  vec, const((1, d)), vec, vec],
        out_specs=[tok(d), tok(d)],
        out_shape=[jax.ShapeDtypeStruct((nb, n, d), F32), jax.ShapeDtypeStruct((nb, n, d), BF16)],
        scratch_shapes=[pltpu.VMEM((tm, A_WIDTH), BF16), pltpu.VMEM((tm, B_WIDTH), BF16)],
        compiler_params=_cparams(("parallel", "parallel")),
        name="merge_branches",
    )(p, p, p, p, p, p, p, p, p, ln_g.reshape(1, -1), ln_b.reshape(1, -1), ws.astype(BF16), bias,
      conv_w, x, yc, p, p, p, wa, wb, wc, wo, g1, n2g.reshape(1, d), sh2, sc2)


def _ff_chunks(width, chunk=FFN_CHUNK):
    out, c0 = [], 0
    while c0 < width:
        cw = min(chunk, width - c0)
        out.append((c0, cw))
        c0 += cw
    return out


def _finish(xn, final, ng_ref, sh_ref, sc_ref, out_refs):
    if final:
        out_refs[0][0] = _rmsnorm(xn, ng_ref[...])
    else:
        out_refs[0][0] = xn
        out_refs[1][0] = _modulate(xn, ng_ref[...], sh_ref[0], sc_ref[0]).astype(BF16)


def _ffn_kernel(x_ref, h_ref, wg_ref, wu_ref, wd_ref, g2_ref, ng_ref, sh_ref, sc_ref, *out_refs,
                final):
    h = h_ref[0]
    acc = None
    for c0, cw in _ff_chunks(wg_ref.shape[1]):
        a = jnp.dot(h, wg_ref[:, c0:c0 + cw], preferred_element_type=F32)
        b = jnp.dot(h, wu_ref[:, c0:c0 + cw], preferred_element_type=F32)
        t = (_silu(a) * b).astype(BF16)
        part = jnp.dot(t, wd_ref[c0:c0 + cw, :], preferred_element_type=F32)
        acc = part if acc is None else acc + part
    xn = x_ref[0] + g2_ref[0] * acc
    _finish(xn, final, ng_ref, sh_ref, sc_ref, out_refs)


def _epilogue_specs(nb, n, d, tm, final, idx):
    tok = pl.BlockSpec((1, tm, d), idx)
    if final:
        return [tok], [jax.ShapeDtypeStruct((nb, n, d), F32)]
    return [tok, tok], [jax.ShapeDtypeStruct((nb, n, d), F32), jax.ShapeDtypeStruct((nb, n, d), BF16)]


def _ffn_call(x, h, wg, wu, wd, g2, ng, sh, sc, *, final):
    nb, n, d = x.shape
    tm = min(FFN_TILE, n)
    idx = lambda b, i: (b, i, 0)
    vec = pl.BlockSpec((1, 1, d), lambda b, i: (b, 0, 0))
    out_specs, out_shape = _epilogue_specs(nb, n, d, tm, final, idx)
    return pl.pallas_call(
        functools.partial(_ffn_kernel, final=final),
        grid=(nb, n // tm),
        in_specs=[pl.BlockSpec((1, tm, d), idx), pl.BlockSpec((1, tm, d), idx),
                  _resident(wg.shape, lambda b, i: (0, 0)), _resident(wu.shape, lambda b, i: (0, 0)),
                  _resident(wd.shape, lambda b, i: (0, 0)),
                  vec, pl.BlockSpec((1, d), lambda b, i: (0, 0)), vec, vec],
        out_specs=out_specs,
        out_shape=out_shape,
        compiler_params=_cparams(("parallel", "parallel")),
        name="dense_swiglu",
    )(x, h, wg, wu, wd, g2, ng.reshape(1, d), sh, sc)


MOE_TILE = 2048
MOE_WIN = 256
MOE_SLAB = 128
MOE_ROWS = 1024
MOE_UNIT = 128
MOE_FF = 512
MOE_SLOTS = 3
MOE_VMEM_LIMIT = 58 * 1024 * 1024
CNT_ROWS = 16


def _route_kernel(h_ref, wr_ref, comb_ref, rank_ref, rankt_ref, cnt_ref, *, n_experts, win):
    h = h_ref[0]
    tm = h.shape[0]
    lane = lax.broadcasted_iota(jnp.int32, (tm, LANES), 1)
    logits = jnp.dot(h, wr_ref[...], preferred_element_type=F32)
    lg = jnp.where(lane < n_experts, logits, -jnp.inf)
    m1 = jnp.max(lg, axis=-1, keepdims=True)
    i1 = jnp.min(jnp.where(lg == m1, lane, LANES), axis=-1, keepdims=True)
    lg2 = jnp.where(lane == i1, -jnp.inf, lg)
    m2 = jnp.max(lg2, axis=-1, keepdims=True)
    i2 = jnp.min(jnp.where(lg2 == m2, lane, LANES), axis=-1, keepdims=True)
    e2 = jnp.exp(m2 - m1)
    w1 = 1.0 / (1.0 + e2)
    comb_ref[0] = jnp.where(lane == i1, w1, 0.0) + jnp.where(lane == i2, e2 * w1, 0.0)
    sel = (lane == i1) | (lane == i2)
    tri = jnp.where(lax.broadcasted_iota(jnp.int32, (win, win), 0)
                    > lax.broadcasted_iota(jnp.int32, (win, win), 1), 1.0, 0.0).astype(BF16)
    base = jnp.zeros((1, LANES), F32)
    bases = []
    for w in range(tm // win):
        rows = slice(w * win, (w + 1) * win)
        sw = jnp.where(sel[rows], 1.0, 0.0)
        excl = jnp.dot(tri, sw.astype(BF16), preferred_element_type=F32)
        rank_ref[0, rows, :] = jnp.where(sel[rows], excl + base, -1.0)
        bases.append(base)
        base = base + jnp.sum(sw, axis=0, keepdims=True)
    bases.append(base)
    bases += [jnp.zeros((1, LANES), F32)] * (CNT_ROWS - len(bases))
    cnt_ref[0] = jnp.concatenate(bases, axis=0).astype(jnp.int32)
    rankt_ref[0] = rank_ref[0].T[:CNT_ROWS]


def _route_call(h, w_router, tm):
    nb, n, d = h.shape
    n_experts = w_router.shape[1]
    assert n_experts <= CNT_ROWS and tm // MOE_WIN + 1 <= CNT_ROWS
    nt = n // tm
    wr = jnp.zeros((d, LANES), BF16).at[:, :n_experts].set(w_router.astype(BF16))
    tok = pl.BlockSpec((1, tm, LANES), lambda b, i: (b, i, 0))
    per_tile = lambda r, c: pl.BlockSpec((1, r, c), lambda b, i: (b * nt + i, 0, 0))
    return pl.pallas_call(
        functools.partial(_route_kernel, n_experts=n_experts, win=min(MOE_WIN, tm)),
        grid=(nb, nt),
        in_specs=[pl.BlockSpec((1, tm, d), lambda b, i: (b, i, 0)),
                  pl.BlockSpec((d, LANES), lambda b, i: (0, 0))],
        out_specs=[tok, tok, per_tile(CNT_ROWS, tm), per_tile(CNT_ROWS, LANES)],
        out_shape=[jax.ShapeDtypeStruct((nb, n, LANES), F32), jax.ShapeDtypeStruct((nb, n, LANES), F32),
                   jax.ShapeDtypeStruct((nb * nt, CNT_ROWS, tm), F32),
                   jax.ShapeDtypeStruct((nb * nt, CNT_ROWS, LANES), jnp.int32)],
        compiler_params=_cparams(("parallel", "parallel")),
        name="moe_route",
    )(h, wr)


def _moe_kernel(cnt_ref, x_ref, h_ref, comb_ref, rank_ref, rankt_ref, wg_hbm, wu_hbm, wd_hbm, g2_ref,
                ng_ref, sh_ref, sc_ref, *rest, final, n_experts, win, slab, rblk, unit, ff):
    out_refs, (hc_ref, yc_ref, wg_buf, wu_buf, wd_buf, sem) = rest[:-6], rest[-6:]
    acc_ref = out_refs[0].at[0]
    b, i, e = (pl.program_id(k) for k in range(3))
    tm, d = h_ref.shape[1], h_ref.shape[2]
    nw = tm // win
    nf = wg_hbm.shape[2] // ff
    tile = b * pl.num_programs(1) + i
    cbase = tile * (CNT_ROWS * n_experts)
    step = tile * n_experts + e
    last_step = pl.num_programs(0) * pl.num_programs(1) * n_experts - 1

    last_block = (last_step + 1) * nf - 1
    def weight_copies(expert, f, slot):
        cols = pl.ds(pl.multiple_of(f * ff, ff), ff)
        return (pltpu.make_async_copy(wg_hbm.at[expert, :, cols], wg_buf.at[slot], sem.at[slot, 0]),
                pltpu.make_async_copy(wu_hbm.at[expert, :, cols], wu_buf.at[slot], sem.at[slot, 1]),
                pltpu.make_async_copy(wd_hbm.at[expert, cols, :], wd_buf.at[slot], sem.at[slot, 2]))

    def fetch(expert, f, slot):
        for copy in weight_copies(expert, f, slot):
            copy.start()

    def fetch_block(g_off):
        over = g_off // nf
        fetch(lax.rem(e + over, n_experts), g_off - over * nf,
              lax.rem(step * nf + g_off, MOE_SLOTS))

    @pl.when(step == 0)
    def _():
        for g_off in range(MOE_SLOTS - 1):
            fetch_block(g_off)

    def count(w):
        return cnt_ref[cbase + w * n_experts + e]

    def slabs(w):
        row0 = (count(w) // BF16_SUBLANES) * BF16_SUBLANES
        return row0, (count(w + 1) - row0 + slab - 1) // slab

    @pl.when(e == 0)
    def _():
        acc_ref[...] = jnp.zeros_like(acc_ref)

    total = count(nw)

    def zero(s, carry):
        rows = pl.ds(pl.multiple_of(s * slab, slab), slab)
        hc_ref[rows, :] = jnp.zeros((slab, d), BF16)
        yc_ref[rows, :] = jnp.zeros((slab, d), F32)
        return carry

    lax.fori_loop(0, jnp.minimum((total + unit + 2 * slab) // slab, hc_ref.shape[0] // slab),
                  zero, 0)

    def gather(w):
        row0, nsl = slabs(w)
        rt = rankt_ref[0, 0, :, w * win:(w + 1) * win]
        hw = h_ref[0, w * win:(w + 1) * win, :]

        def body(s, carry):
            r0 = pl.multiple_of(row0 + s * slab, BF16_SUBLANES)
            rid = (r0 + lax.broadcasted_iota(jnp.int32, (slab, win), 0)).astype(F32)
            onehot = jnp.where(rt == rid, 1.0, 0.0).astype(BF16)
            rows = pl.ds(r0, slab)
            got = jnp.dot(onehot, hw, preferred_element_type=F32)
            hc_ref[rows, :] = (hc_ref[rows, :].astype(F32) + got).astype(BF16)
            return carry

        return body, nsl

    for w in range(nw):
        gather(w)[0](0, 0)
    for w in range(nw):
        body, nsl = gather(w)
        lax.fori_loop(1, nsl, body, 0)

    units = (total + unit - 1) // unit
    per_block = rblk // unit
    nfull = units // per_block
    rest = units - per_block * nfull

    def ff_block(f, carry):
        g = step * nf + f
        slot = lax.rem(g, MOE_SLOTS)
        ahead = f + (MOE_SLOTS - 1)
        wraps = ahead >= nf

        @pl.when(g + (MOE_SLOTS - 1) <= last_block)
        def _():
            fetch(jnp.where(wraps, lax.rem(e + 1, n_experts), e),
                  jnp.where(wraps, ahead - nf, ahead), lax.rem(g + (MOE_SLOTS - 1), MOE_SLOTS))

        for copy in weight_copies(e, f, slot):
            copy.wait()

        def ffn(r0, size):
            rows = pl.ds(pl.multiple_of(r0, unit), size)
            hb = hc_ref[rows, :]
            a = jnp.dot(hb, wg_buf[slot], preferred_element_type=F32)
            u = jnp.dot(hb, wu_buf[slot], preferred_element_type=F32)
            t = (_silu(a) * u).astype(BF16)
            yc_ref[rows, :] += jnp.dot(t, wd_buf[slot], preferred_element_type=F32)

        def full(r, c):
            ffn(r * rblk, rblk)
            return c

        lax.fori_loop(0, nfull, full, 0)
        for k in range(1, per_block):
            @pl.when(rest == k)
            def _(k=k):
                ffn(nfull * rblk, k * unit)

        return carry

    lax.fori_loop(0, nf, ff_block, 0)

    def spread(w):
        row0, nsl = slabs(w)
        trows = slice(w * win, (w + 1) * win)
        mine = lax.broadcasted_iota(jnp.int32, (win, LANES), 1) == e
        rcol = jnp.sum(jnp.where(mine, rank_ref[0, trows, :], 0.0), axis=-1, keepdims=True)
        wcol = jnp.sum(jnp.where(mine, comb_ref[0, trows, :], 0.0), axis=-1, keepdims=True)

        def body(s, carry):
            r0 = pl.multiple_of(row0 + s * slab, BF16_SUBLANES)
            cid = (r0 + lax.broadcasted_iota(jnp.int32, (win, slab), 1)).astype(F32)
            onehot = jnp.where(rcol == cid, 1.0, 0.0).astype(BF16)
            yb = yc_ref[pl.ds(r0, slab), :].astype(BF16)
            acc_ref[trows, :] += wcol * jnp.dot(onehot, yb, preferred_element_type=F32)
            return carry

        return body, nsl

    bodies = [spread(w) for w in range(nw)]
    for body, _ in bodies:
        body(0, 0)
    for body, nsl in bodies:
        lax.fori_loop(1, nsl, body, 0)

    @pl.when(e == n_experts - 1)
    def _():
        xn = x_ref[0] + g2_ref[0] * acc_ref[...]
        _finish(xn, final, ng_ref, sh_ref, sc_ref, out_refs)


def _moe_call(x, h, w_router, wg, wu, wd, g2, ng, sh, sc, *, final):
    nb, n, d = x.shape
    n_experts, _, dff = wg.shape
    tm = min(MOE_TILE, n)
    win = min(MOE_WIN, tm)
    nt = n // tm
    comb, rank, rankt, cnt = _route_call(h, w_router, tm)
    cnt = cnt[:, :, :n_experts].reshape(-1)
    rankt = rankt.reshape(nb * nt, CNT_ROWS, 1, tm)
    cap_ffn = -(-tm // MOE_UNIT) * MOE_UNIT
    cap = -(-max(cap_ffn, tm + MOE_SLAB) // MOE_SLAB) * MOE_SLAB

    idx = lambda b, i, e, c: (b, i, 0)
    vec = pl.BlockSpec((1, 1, d), lambda b, i, e, c: (b, 0, 0))
    hbm = pl.BlockSpec(memory_space=pl.ANY)
    out_specs, out_shape = _epilogue_specs(nb, n, d, tm, final, idx)
    grid_spec = pltpu.PrefetchScalarGridSpec(
        num_scalar_prefetch=1,
        grid=(nb, nt, n_experts),
        in_specs=[_resident((1, tm, d), idx), _resident((1, tm, d), idx),
                  _resident((1, tm, LANES), idx), _resident((1, tm, LANES), idx),
                  pl.BlockSpec((1, 1, 1, tm), lambda b, i, e, c: (b * nt + i, e, 0, 0)),
                  hbm, hbm, hbm,
                  vec, pl.BlockSpec((1, d), lambda b, i, e, c: (0, 0)), vec, vec],
        out_specs=out_specs,
        scratch_shapes=[pltpu.VMEM((cap, d), BF16), pltpu.VMEM((cap, d), F32),
                        pltpu.VMEM((MOE_SLOTS, d, MOE_FF), BF16),
                        pltpu.VMEM((MOE_SLOTS, d, MOE_FF), BF16),
                        pltpu.VMEM((MOE_SLOTS, MOE_FF, d), BF16),
                        pltpu.SemaphoreType.DMA((MOE_SLOTS, 3))],
    )
    return pl.pallas_call(
        functools.partial(_moe_kernel, final=final, n_experts=n_experts, win=win, slab=MOE_SLAB,
                          rblk=MOE_ROWS, unit=MOE_UNIT, ff=MOE_FF),
        grid_spec=grid_spec,
        out_shape=out_shape,
        compiler_params=_cparams(("arbitrary", "arbitrary", "arbitrary"), MOE_VMEM_LIMIT),
        name="moe_swiglu",
    )(cnt, x, h, comb, rank, rankt, wg, wu, wd, g2, ng.reshape(1, d), sh, sc)


def kernel(x, c, ctx, c_ctx, w_ada, b_ada, norm1_g, norm2_g, w_in, b_gate, a_ln_g, a_ln_b, a_ws,
           a_bs, b_conv, c_lambda, c_subln_g, w_a_out, w_b_out, w_c_out, w_o, ff_w_gate, ff_w_up,
           ff_w_down, moe_w_router, moe_w_gate, moe_w_up, moe_w_down, final_norm_g):
    bsz, n, d = x.shape
    nc = ctx.shape[1]
    depth = w_ada.shape[0]
    cols = _Cols(d)
    head_cb = COL_BLOCK // LANES
    gate_cb = cols.out(cols.gate) * COL_BLOCK // d
    mix_pos = tuple(cols.out(j) for j in (cols.au, cols.av, cols.bb, cols.bc, cols.bh))
    cq_l, ck_l, cv_l = (cols.out(j) * head_cb for j in (cols.cq, cols.ck, cols.cv))

    pad = (-(bsz + 1)) % 8
    cc = jnp.concatenate([c, c_ctx[None], jnp.zeros((pad, d), F32)], axis=0)
    mod = _ada_call(cc, w_ada, b_ada)

    def mods(l):
        lat = [mod[l, :bsz, k * d:(k + 1) * d].reshape(bsz, 1, d) for k in range(N_MOD)]
        con = [mod[l, bsz:bsz + 1, k * d:(k + 1) * d].reshape(1, 1, d) for k in range(N_MOD)]
        return lat, con

    tables = _rope_tables(n)
    w_in_b = w_in.astype(BF16)
    xl = x
    xc = ctx.reshape(1, bsz * nc, d)
    lat, con = mods(0)
    h = _mod_call(xl, norm1_g[0], lat[0], lat[1])
    hc = _mod_call(xc, norm1_g[0], con[0], con[1])

    for l in range(depth):
        last = l == depth - 1
        lam_init = 0.8 - 0.6 * math.exp(-0.3 * l)
        lat, con = mods(l)
        if not last:
            nlat, ncon = mods(l + 1)
            nxt_l = (norm1_g[l + 1], nlat[0], nlat[1])
            nxt_c = (norm1_g[l + 1], ncon[0], ncon[1])
        else:
            nxt_l = (final_norm_g, lat[0], lat[1])
            nxt_c = None

        p = _in_call(h, w_in_b[l], b_gate[l], tables, j0=0, nj=cols.end, rope=True)
        if last:
            pc = _in_call(hc, w_in_b[l], b_gate[l], tables, j0=cols.ck, nj=cols.gate - cols.ck,
                          rope=False)
            ck_c, cv_c = 0, (cols.cv - cols.ck) * head_cb
        else:
            pc = _in_call(hc, w_in_b[l], b_gate[l], tables, j0=0, nj=cols.end, rope=False)
            ck_c, cv_c = ck_l, cv_l
        pc_seq = pc.reshape(bsz, nc, -1)

        wa, wb, wc, wo = (w.astype(BF16) for w in (w_a_out[l], w_b_out[l], w_c_out[l], w_o[l]))

        def channel(xs, hs, g2, nxt, final):
            i = l // 2
            if l % 2 == 0:
                return _ffn_call(xs, hs, ff_w_gate[i].astype(BF16), ff_w_up[i].astype(BF16),
                                 ff_w_down[i].astype(BF16), g2, *nxt, final=final)
            return _moe_call(xs, hs, moe_w_router[i], moe_w_gate[i].astype(BF16),
                             moe_w_up[i].astype(BF16), moe_w_down[i].astype(BF16), g2, *nxt,
                             final=final)

        mixer_w = (a_ln_g[l], a_ln_b[l], a_ws[l], a_bs[l], b_conv[l])
        yc = _attn_call(p, cq_l, [(p, ck_l, cv_l), (pc_seq, ck_c, cv_c)],
                        c_lambda[l], c_subln_g[l], lam_init=lam_init)
        x1, h2 = _merge_call(xl, p, mix_pos, yc, gate_cb, mixer_w, wa, wb, wc, wo, lat[2],
                             norm2_g[l], lat[3], lat[4], seq=n)
        res = channel(x1, h2, lat[5], nxt_l, last)
        if last:
            return res[0]
        xl, h = res

        yc_c = _attn_call(pc_seq, cq_l, [(pc_seq, ck_c, cv_c)], c_lambda[l], c_subln_g[l],
                          lam_init=lam_init)
        xc1, hc2 = _merge_call(xc, pc, mix_pos, yc_c.reshape(1, bsz * nc, -1), gate_cb, mixer_w,
                               wa, wb, wc, wo, con[2], norm2_g[l], con[3], con[4], seq=nc)
        xc, hc = channel(xc1, hc2, con[5], nxt_c, False)
```

```python
import functools
import math

import jax
import jax.numpy as jnp
from jax import lax
from jax.experimental import pallas as pl
from jax.experimental.pallas import tpu as pltpu

F32 = jnp.float32
BF16 = jnp.bfloat16

EPS = 1e-6
GRID_W = 64
N_MOD = 6
N_BRANCH = 3
CHUNK = 128
A_GROUPS = 8
A_GROUP_DIM = 64
A_WIDTH = A_GROUPS * A_GROUP_DIM
B_WIDTH = 512
CONV_W = 3
C_HEADS = 8
C_HEAD_DIM = 64
C_V_DIM = 2 * C_HEAD_DIM
ROPE_BASE = 10000.0
ROPE_PAIR = C_HEAD_DIM // 4
LOG2E = 1.4426950408889634

LANES = 128
BF16_SUBLANES = 16
V7X_VMEM_LIMIT = 56 * 1024 * 1024

COL_BLOCK = 512
ADA_COLS = 1536
MOD_TILE = 1024
IN_TILE, IN_SUB = 4096, 512
ATTN_Q_TILE, ATTN_KEYS = 512, 2048
MERGE_TILE = 512
FFN_TILE, FFN_CHUNK = 512, 512


def _cparams(sem, vmem_limit=V7X_VMEM_LIMIT, fuse_inputs=None):
    return pltpu.CompilerParams(dimension_semantics=sem, vmem_limit_bytes=vmem_limit,
                                allow_input_fusion=fuse_inputs)


def _resident(shape, index_map):
    return pl.BlockSpec(shape, index_map, pipeline_mode=pl.Buffered(1))


def _sigmoid(v):
    return 0.5 * (1.0 + jnp.tanh(0.5 * v))


def _silu(v):
    return v * _sigmoid(v)


def _modulate(x, g, sh, sc):
    ms = jnp.mean(x * x, axis=-1, keepdims=True)
    y = x * lax.rsqrt(ms + EPS)
    return (y * g) * (1.0 + sc) + sh


def _rmsnorm(x, g):
    ms = jnp.mean(x * x, axis=-1, keepdims=True)
    return x * lax.rsqrt(ms + EPS) * g


def _ada_kernel(c_ref, w_ref, b_ref, o_ref):
    s = _silu(c_ref[...])
    o_ref[0] = jnp.dot(s, w_ref[0], preferred_element_type=F32,
                       precision=lax.Precision.HIGHEST) + b_ref[0]


def _ada_call(cc, w_ada, b_ada):
    depth, d, cols = w_ada.shape
    rows = cc.shape[0]
    tn = ADA_COLS
    return pl.pallas_call(
        _ada_kernel,
        grid=(depth, cols // tn),
        in_specs=[
            pl.BlockSpec((rows, d), lambda l, j: (0, 0)),
            pl.BlockSpec((1, d, tn), lambda l, j: (l, 0, j)),
            pl.BlockSpec((1, 1, tn), lambda l, j: (l, 0, j)),
        ],
        out_specs=pl.BlockSpec((1, rows, tn), lambda l, j: (l, 0, j)),
        out_shape=jax.ShapeDtypeStruct((depth, rows, cols), F32),
        compiler_params=_cparams(("parallel", "parallel")),
        name="ada_proj",
    )(cc, w_ada, b_ada.reshape(depth, 1, cols))


def _mod_kernel(x_ref, g_ref, sh_ref, sc_ref, o_ref):
    o_ref[0] = _modulate(x_ref[0], g_ref[...], sh_ref[0], sc_ref[0]).astype(BF16)


def _mod_call(x, g, sh, sc):
    nb, n, d = x.shape
    tm = min(MOD_TILE, n)
    vec = pl.BlockSpec((1, 1, d), lambda b, i: (b, 0, 0))
    return pl.pallas_call(
        _mod_kernel,
        grid=(nb, n // tm),
        in_specs=[pl.BlockSpec((1, tm, d), lambda b, i: (b, i, 0)),
                  pl.BlockSpec((1, d), lambda b, i: (0, 0)), vec, vec],
        out_specs=pl.BlockSpec((1, tm, d), lambda b, i: (b, i, 0)),
        out_shape=jax.ShapeDtypeStruct((nb, n, d), BF16),
        compiler_params=_cparams(("parallel", "parallel")),
        name="modulate",
    )(x, g.reshape(1, d), sh, sc)


class _Cols:
    def __init__(self, d):
        self.au = 0
        self.av = self.au + A_WIDTH // COL_BLOCK
        self.bb = self.av + A_WIDTH // COL_BLOCK
        self.bc = self.bb + B_WIDTH // COL_BLOCK
        self.bh = self.bc + B_WIDTH // COL_BLOCK
        self.cq = self.bh + B_WIDTH // COL_BLOCK
        qk = C_HEADS * 2 * C_HEAD_DIM // COL_BLOCK
        self.ck = self.cq + qk
        self.cv = self.ck + qk
        self.gate = self.cv + C_HEADS * C_V_DIM // COL_BLOCK
        self.end = self.gate + N_BRANCH * d // COL_BLOCK

    def out(self, jw):
        return (jw - self.gate) % self.end


def _in_kernel(h_ref, w_ref, bg_ref, cos_ref, sa_ref, sb_ref, o_ref, *, cols, j0, rope, qscale,
               sub):
    j = pl.program_id(2) + j0
    tm = h_ref.shape[1]

    def run(epilogue):
        def mm(r):
            return jnp.dot(h_ref[0, r * sub:(r + 1) * sub, :], w_ref[...],
                           preferred_element_type=F32)
        nxt = mm(0)
        for r in range(tm // sub):
            acc = nxt
            if r + 1 < tm // sub:
                nxt = mm(r + 1)
            epilogue(acc, slice(r * sub, (r + 1) * sub))

    @pl.when(j < cols.bb)
    def _():
        def gelu(acc, rows):
            o_ref[0, rows, :] = jax.nn.gelu(acc).astype(BF16)
        run(gelu)

    @pl.when(((j >= cols.bb) & (j < cols.cq)) | ((j >= cols.cv) & (j < cols.gate)))
    def _():
        def plain(acc, rows):
            o_ref[0, rows, :] = acc.astype(BF16)
        run(plain)

    @pl.when((j >= cols.cq) & (j < cols.cv))
    def _():
        scale = jnp.where(j < cols.ck, qscale, 1.0).astype(F32)

        def rotary(acc, rows):
            if not rope:
                o_ref[0, rows, :] = (acc * scale).astype(BF16)
                return
            cos, sa, sb = cos_ref[rows, :] * scale, sa_ref[rows, :] * scale, sb_ref[rows, :] * scale
            for s in range(COL_BLOCK // LANES):
                t = acc[:, s * LANES:(s + 1) * LANES]
                r = (t * cos + pltpu.roll(t, LANES - ROPE_PAIR, 1) * sa
                     + pltpu.roll(t, ROPE_PAIR, 1) * sb)
                o_ref[0, rows, s * LANES:(s + 1) * LANES] = r.astype(BF16)
        run(rotary)

    @pl.when(j >= cols.gate)
    def _():
        def gate(acc, rows):
            o_ref[0, rows, :] = _sigmoid(acc + bg_ref[...]).astype(BF16)
        run(gate)


def _in_call(h, w, b_gate, tables, *, j0, nj, rope):
    nb, n, d = h.shape
    cols = _Cols(d)
    tm = min(IN_TILE, n)
    full = (j0 == 0) and (nj == cols.end)
    if rope:
        cos, sa, sb = tables
    else:
        cos = sa = sb = jnp.zeros((tm, LANES), F32)
    tab = pl.BlockSpec((tm, LANES), lambda b, i, j: (i if rope else 0, 0))

    def out_idx(b, i, j):
        if full:
            return (b, i, jnp.where(j >= cols.gate, j - cols.gate, j + cols.end - cols.gate))
        return (b, i, j)
    kern = functools.partial(_in_kernel, cols=cols, j0=j0, rope=rope,
                             qscale=C_HEAD_DIM ** -0.5 * LOG2E, sub=min(IN_SUB, tm))
    return pl.pallas_call(
        kern,
        grid=(nb, n // tm, nj),
        in_specs=[
            pl.BlockSpec((1, tm, d), lambda b, i, j: (b, i, 0)),
            pl.BlockSpec((d, COL_BLOCK), lambda b, i, j: (0, j + j0)),
            pl.BlockSpec((1, COL_BLOCK), lambda b, i, j: (0, jnp.maximum(j + j0 - cols.gate, 0))),
            tab, tab, tab,
        ],
        out_specs=pl.BlockSpec((1, tm, COL_BLOCK), out_idx),
        out_shape=jax.ShapeDtypeStruct((nb, n, nj * COL_BLOCK), BF16),
        compiler_params=_cparams(("parallel", "parallel", "arbitrary"),
                                 fuse_inputs=[False, True, False, False, False, False]),
        name="in_proj",
    )(h, w, b_gate.reshape(1, -1), cos, sa, sb)


def _rope_tables(n):
    rows = n // GRID_W
    r = jnp.repeat(jnp.arange(rows, dtype=F32), GRID_W)
    col = jnp.tile(jnp.arange(GRID_W, dtype=F32), rows)
    quarter = C_HEAD_DIM // 4
    inv = ROPE_BASE ** (-jnp.arange(quarter, dtype=F32) / quarter)
    ar = r[:, None] * inv
    ac = col[:, None] * inv
    ang = jnp.concatenate([ar, ar, ac, ac], axis=-1)
    ang = jnp.tile(ang, (1, LANES // C_HEAD_DIM))
    cos, sin = jnp.cos(ang), jnp.sin(ang)
    first_of_pair = (jnp.arange(LANES) // quarter) % 2 == 0
    sa = jnp.where(first_of_pair, -sin, 0.0)
    sb = jnp.where(first_of_pair, 0.0, sin)
    return cos, sa, sb


def _local_mixers(u_ref, v_ref, bg_ref, cg_ref, hh_ref, cgp_ref, hhp_ref, cgn_ref, hhn_ref,
                  lng_ref, lnb_ref, ws_ref, bias_ref, cw_ref, ya_ref, yb_ref, *, tm, seq):
    i = pl.program_id(1)
    lane = lax.broadcasted_iota(jnp.int32, (CHUNK, LANES), 1)
    lo = lane < A_GROUP_DIM
    for c in range(tm // CHUNK):
        rows = pl.ds(c * CHUNK, CHUNK)
        v = v_ref[0, rows, :].astype(F32)
        mu = jnp.mean(v, axis=-1, keepdims=True)
        var = jnp.mean(jnp.square(v - mu), axis=-1, keepdims=True)
        vn = ((v - mu) * lax.rsqrt(var + EPS) * lng_ref[...] + lnb_ref[...]).astype(BF16)
        for k in range(A_WIDTH // LANES):
            blk = vn[:, k * LANES:(k + 1) * LANES]
            zero = jnp.zeros_like(blk)
            mixed = (jnp.dot(ws_ref[2 * k], jnp.where(lo, blk, zero), preferred_element_type=F32)
                     + jnp.dot(ws_ref[2 * k + 1], jnp.where(lo, zero, blk), preferred_element_type=F32)
                     + bias_ref[:, k * LANES:(k + 1) * LANES])
            u = u_ref[0, rows, k * LANES:(k + 1) * LANES].astype(F32)
            ya_ref[rows, k * LANES:(k + 1) * LANES] = (u * mixed).astype(BF16)

    z = cg_ref[0].astype(F32) * hh_ref[0].astype(F32)
    last = BF16_SUBLANES - 1
    z_prev = cgp_ref[0, last:last + 1, :].astype(F32) * hhp_ref[0, last:last + 1, :].astype(F32)
    z_next = cgn_ref[0, 0:1, :].astype(F32) * hhn_ref[0, 0:1, :].astype(F32)
    row = lax.broadcasted_iota(jnp.int32, (tm, 1), 0)
    pos = lax.rem(i * tm + row, seq)
    zm1 = jnp.where(row == 0, z_prev, pltpu.roll(z, 1, 0))
    zm1 = jnp.where(pos == 0, 0.0, zm1)
    zp1 = jnp.where(row == tm - 1, z_next, pltpu.roll(z, tm - 1, 0))
    zp1 = jnp.where(pos == seq - 1, 0.0, zp1)
    conv = cw_ref[0:1, :] * zm1 + cw_ref[1:2, :] * z + cw_ref[2:3, :] * zp1
    yb_ref[...] = (bg_ref[0].astype(F32) * conv).astype(BF16)


def _attn_kernel(*refs, tq, ck, sizes, lam_init):
    nsrc = len(sizes)
    cl_ref, g_ref, q_ref = refs[:3]
    kv_refs = refs[3:3 + 2 * nsrc]
    o_ref = refs[3 + 2 * nsrc]
    vt_refs = refs[4 + 2 * nsrc:4 + 3 * nsrc]
    kmax_ref = refs[4 + 3 * nsrc]
    half = lax.broadcasted_iota(jnp.int32, (1, LANES), 1) < C_HEAD_DIM

    @pl.when(pl.program_id(2) == 0)
    def _():
        per_map = jnp.where(lax.broadcasted_iota(jnp.int32, (LANES, LANES), 0) // C_HEAD_DIM
                            == lax.broadcasted_iota(jnp.int32, (LANES, LANES), 1), 1.0, 0.0)
        kmax2 = jnp.zeros((1, LANES), F32)
        for src in range(nsrc):
            vt_refs[src][...] = kv_refs[2 * src + 1][0].astype(F32).T.astype(BF16)
            k = kv_refs[2 * src][0]
            n2 = jnp.dot(k * k, per_map.astype(BF16), preferred_element_type=F32)
            kmax2 = jnp.maximum(kmax2, jnp.max(n2, axis=0, keepdims=True))
        kmax_ref[0:1, :] = jnp.sqrt(kmax2) * (1.0 + 2.0 ** -7)

    q = q_ref[0]
    zero = jnp.zeros_like(q)
    qs = [jnp.where(half, q, zero), jnp.where(half, zero, q)]
    chunks = [(kv_refs[2 * src], vt_refs[src], c0, min(ck, nk))
              for src, nk in enumerate(sizes) for c0 in range(0, nk, min(ck, nk))]

    def scores(chunk, qm):
        k_ref, _, c0, c = chunk
        return lax.dot_general(k_ref[0, c0:c0 + c, :], qm, (((1,), (1,)), ((), ())),
                               preferred_element_type=F32)

    def finish(accs, ls):
        cl = cl_ref[...]
        lam = (jnp.exp(jnp.sum(cl[0:1] * cl[1:2], axis=-1, keepdims=True))
               - jnp.exp(jnp.sum(cl[2:3] * cl[3:4], axis=-1, keepdims=True)) + lam_init)
        o = accs[0] / ls[0] - lam * (accs[1] / ls[1])
        ms = jnp.mean(o * o, axis=0, keepdims=True)
        y = o * lax.rsqrt(ms + EPS) * (g_ref[...] * (1.0 - lam_init))
        o_ref[0] = y.T.astype(BF16)

    qsq = jnp.square(q.astype(F32).T)
    refs_r = [jnp.sqrt(jnp.sum(qsq[j * C_HEAD_DIM:(j + 1) * C_HEAD_DIM], axis=0, keepdims=True))
              * kmax_ref[0:1, j:j + 1] for j in range(2)]
    ls = [jnp.zeros((1, tq), F32) for _ in qs]
    accs = [jnp.zeros((C_V_DIM, tq), F32) for _ in qs]
    s_next = [scores(chunks[0], qm) for qm in qs]
    for t, (_, vt_ref, c0, c) in enumerate(chunks):
        s_cur = s_next
        if t + 1 < len(chunks):
            s_next = [scores(chunks[t + 1], qm) for qm in qs]
        vt = vt_ref[:, c0:c0 + c]
        for j, s in enumerate(s_cur):
            p = jnp.exp2(s - refs_r[j])
            ls[j] = ls[j] + jnp.sum(p, axis=0, keepdims=True)
            accs[j] = accs[j] + jnp.dot(vt, p.astype(BF16), preferred_element_type=F32)
    finish(accs, ls)

    healthy = jnp.min(jnp.minimum(ls[0], ls[1])) >= 2.0 ** -60

    @pl.when(jnp.logical_not(healthy))
    def _():
        ms = [jnp.full((1, tq), -jnp.inf, F32) for _ in qs]
        ls = [jnp.zeros((1, tq), F32) for _ in qs]
        accs = [jnp.zeros((C_V_DIM, tq), F32) for _ in qs]
        for chunk in chunks:
            _, vt_ref, c0, c = chunk
            vt = vt_ref[:, c0:c0 + c]
            for j, qm in enumerate(qs):
                s = scores(chunk, qm)
                m_new = jnp.maximum(ms[j], jnp.max(s, axis=0, keepdims=True))
                alpha = jnp.exp2(ms[j] - m_new)
                p = jnp.exp2(s - m_new)
                ls[j] = alpha * ls[j] + jnp.sum(p, axis=0, keepdims=True)
                accs[j] = alpha * accs[j] + jnp.dot(vt, p.astype(BF16),
                                                    preferred_element_type=F32)
                ms[j] = m_new
        finish(accs, ls)


def _attn_call(q_arr, q_cb, sources, c_lambda, subln_g, *, lam_init):
    nb, n, _ = q_arr.shape
    tq = min(ATTN_Q_TILE, n)
    sizes = tuple(a.shape[1] for a, _, _ in sources)
    in_specs = [
        pl.BlockSpec((4, C_HEAD_DIM), lambda b, h, i: (0, 0)),
        pl.BlockSpec((C_V_DIM, 1), lambda b, h, i: (0, 0)),
        pl.BlockSpec((1, tq, LANES), lambda b, h, i: (b, i, q_cb + h)),
    ]
    args = [c_lambda, subln_g.reshape(-1, 1), q_arr]
    for arr, kcb, vcb in sources:
        nk = arr.shape[1]
        in_specs.append(pl.BlockSpec((1, nk, LANES), lambda b, h, i, kcb=kcb: (b, 0, kcb + h)))
        in_specs.append(pl.BlockSpec((1, nk, LANES), lambda b, h, i, vcb=vcb: (b, 0, vcb + h)))
        args += [arr, arr]
    kern = functools.partial(_attn_kernel, tq=tq, ck=ATTN_KEYS, sizes=sizes, lam_init=lam_init)
    return pl.pallas_call(
        kern,
        grid=(nb, C_HEADS, n // tq),
        in_specs=in_specs,
        out_specs=pl.BlockSpec((1, tq, C_V_DIM), lambda b, h, i: (b, i, h)),
        out_shape=jax.ShapeDtypeStruct((nb, n, C_HEADS * C_V_DIM), BF16),
        scratch_shapes=[pltpu.VMEM((C_V_DIM, nk), BF16) for nk in sizes]
        + [pltpu.VMEM((8, LANES), F32)],
        compiler_params=_cparams(("parallel", "parallel", "arbitrary")),
        name="diff_attention",
    )(*args)


def _merge_kernel(*refs, tm, seq):
    mixer_refs = refs[:14]
    (x_ref, yc_ref, ga_ref, gb_ref, gc_ref, wa_ref, wb_ref, wc_ref, wo_ref, g1_ref, n2_ref,
     sh2_ref, sc2_ref, xo_ref, ho_ref, ya_ref, yb_ref) = refs[14:]
    c = jnp.dot(yc_ref[0], wc_ref[...], preferred_element_type=F32)
    _local_mixers(*mixer_refs, ya_ref, yb_ref, tm=tm, seq=seq)
    a = jnp.dot(ya_ref[...], wa_ref[...], preferred_element_type=F32)
    b = jnp.dot(yb_ref[...], wb_ref[...], preferred_element_type=F32)
    y = (ga_ref[0].astype(F32) * a + gb_ref[0].astype(F32) * b + gc_ref[0].astype(F32) * c)
    m = jnp.dot(y.astype(BF16), wo_ref[...], preferred_element_type=F32)
    xn = x_ref[0] + g1_ref[0] * m
    xo_ref[0] = xn
    ho_ref[0] = _modulate(xn, n2_ref[...], sh2_ref[0], sc2_ref[0]).astype(BF16)


def _merge_call(x, p, pos, yc, gate_cb, mixer_w, wa, wb, wc, wo, g1, n2g, sh2, sc2, *, seq):
    nb, n, d = x.shape
    au, av, bb, bc, bh = pos
    ln_g, ln_b, ws, bs, conv_w = mixer_w
    tm = min(MERGE_TILE, n)
    hb = tm // BF16_SUBLANES
    nhb = n // BF16_SUBLANES

    def tok(w, cb=0):
        return pl.BlockSpec((1, tm, w), lambda b, i: (b, i, cb))

    def prev(cb):
        return pl.BlockSpec((1, BF16_SUBLANES, COL_BLOCK),
                            lambda b, i: (b, jnp.maximum(i * hb - 1, 0), cb))

    def nxt(cb):
        return pl.BlockSpec((1, BF16_SUBLANES, COL_BLOCK),
                            lambda b, i: (b, jnp.minimum((i + 1) * hb, nhb - 1), cb))

    const = lambda shape: pl.BlockSpec(shape, lambda b, i: (0,) * len(shape))
    bias = jnp.repeat(bs.T, A_GROUP_DIM, axis=1)
    vec = pl.BlockSpec((1, 1, d), lambda b, i: (b, 0, 0))
    return pl.pallas_call(
        functools.partial(_merge_kernel, tm=tm, seq=seq),
        grid=(nb, n // tm),
        in_specs=[tok(COL_BLOCK, au), tok(COL_BLOCK, av), tok(COL_BLOCK, bb), tok(COL_BLOCK, bc),
                  tok(COL_BLOCK, bh), prev(bc), prev(bh), nxt(bc), nxt(bh),
                  const((1, A_WIDTH)), const((1, A_WIDTH)), const((A_GROUPS, CHUNK, CHUNK)),
                  const((CHUNK, A_WIDTH)), const((CONV_W, B_WIDTH)),
                  tok(d), tok(d), tok(d, gate_cb), tok(d, gate_cb + 1), tok(d, gate_cb + 2),
                  _resident(wa.shape, lambda b, i: (0, 0)), _resident(wb.shape, lambda b, i: (0, 0)),
                  _resident(wc.shape, lambda b, i: (0, 0)), _resident(wo.shape, lambda b, i: (0, 0)),
                  vec, const((1, d)), vec, vec],
        out_specs=[tok(d), tok(d)],
        out_shape=[jax.ShapeDtypeStruct((nb, n, d), F32), jax.ShapeDtypeStruct((nb, n, d), BF16)],
        scratch_shapes=[pltpu.VMEM((tm, A_WIDTH), BF16), pltpu.VMEM((tm, B_WIDTH), BF16)],
        compiler_params=_cparams(("parallel", "parallel")),
        name="merge_branches",
    )(p, p, p, p, p, p, p, p, p, ln_g.reshape(1, -1), ln_b.reshape(1, -1), ws.astype(BF16), bias,
      conv_w, x, yc, p, p, p, wa, wb, wc, wo, g1, n2g.reshape(1, d), sh2, sc2)


def _ff_chunks(width, chunk=FFN_CHUNK):
    out, c0 = [], 0
    while c0 < width:
        cw = min(chunk, width - c0)
        out.append((c0, cw))
        c0 += cw
    return out


def _finish(xn, final, ng_ref, sh_ref, sc_ref, out_refs):
    if final:
        out_refs[0][0] = _rmsnorm(xn, ng_ref[...])
    else:
        out_refs[0][0] = xn
        out_refs[1][0] = _modulate(xn, ng_ref[...], sh_ref[0], sc_ref[0]).astype(BF16)


def _ffn_kernel(x_ref, h_ref, wg_ref, wu_ref, wd_ref, g2_ref, ng_ref, sh_ref, sc_ref, *out_refs,
                final):
    h = h_ref[0]
    acc = None
    for c0, cw in _ff_chunks(wg_ref.shape[1]):
        a = jnp.dot(h, wg_ref[:, c0:c0 + cw], preferred_element_type=F32)
        b = jnp.dot(h, wu_ref[:, c0:c0 + cw], preferred_element_type=F32)
        t = (_silu(a) * b).astype(BF16)
        part = jnp.dot(t, wd_ref[c0:c0 + cw, :], preferred_element_type=F32)
        acc = part if acc is None else acc + part
    xn = x_ref[0] + g2_ref[0] * acc
    _finish(xn, final, ng_ref, sh_ref, sc_ref, out_refs)


def _epilogue_specs(nb, n, d, tm, final, idx):
    tok = pl.BlockSpec((1, tm, d), idx)
    if final:
        return [tok], [jax.ShapeDtypeStruct((nb, n, d), F32)]
    return [tok, tok], [jax.ShapeDtypeStruct((nb, n, d), F32), jax.ShapeDtypeStruct((nb, n, d), BF16)]


def _ffn_call(x, h, wg, wu, wd, g2, ng, sh, sc, *, final):
    nb, n, d = x.shape
    tm = min(FFN_TILE, n)
    idx = lambda b, i: (b, i, 0)
    vec = pl.BlockSpec((1, 1, d), lambda b, i: (b, 0, 0))
    out_specs, out_shape = _epilogue_specs(nb, n, d, tm, final, idx)
    return pl.pallas_call(
        functools.partial(_ffn_kernel, final=final),
        grid=(nb, n // tm),
        in_specs=[pl.BlockSpec((1, tm, d), idx), pl.BlockSpec((1, tm, d), idx),
                  _resident(wg.shape, lambda b, i: (0, 0)), _resident(wu.shape, lambda b, i: (0, 0)),
                  _resident(wd.shape, lambda b, i: (0, 0)),
                  vec, pl.BlockSpec((1, d), lambda b, i: (0, 0)), vec, vec],
        out_specs=out_specs,
        out_shape=out_shape,
        compiler_params=_cparams(("parallel", "parallel"),
                                 fuse_inputs=[False, False, True, True, True] + [False] * 4),
        name="dense_swiglu",
    )(x, h, wg, wu, wd, g2, ng.reshape(1, d), sh, sc)


MOE_TILE = 2048
MOE_WIN = 256
MOE_SLAB = 128
MOE_ROWS = 1024
MOE_UNIT = 128
MOE_FF = 512
MOE_SLOTS = 3
MOE_VMEM_LIMIT = 58 * 1024 * 1024
CNT_ROWS = 16


def _route_kernel(h_ref, wr_ref, comb_ref, rank_ref, rankt_ref, cnt_ref, *, n_experts, win):
    h = h_ref[0]
    tm = h.shape[0]
    lane = lax.broadcasted_iota(jnp.int32, (tm, LANES), 1)
    logits = jnp.dot(h, wr_ref[...], preferred_element_type=F32)
    lg = jnp.where(lane < n_experts, logits, -jnp.inf)
    m1 = jnp.max(lg, axis=-1, keepdims=True)
    i1 = jnp.min(jnp.where(lg == m1, lane, LANES), axis=-1, keepdims=True)
    lg2 = jnp.where(lane == i1, -jnp.inf, lg)
    m2 = jnp.max(lg2, axis=-1, keepdims=True)
    i2 = jnp.min(jnp.where(lg2 == m2, lane, LANES), axis=-1, keepdims=True)
    e2 = jnp.exp(m2 - m1)
    w1 = 1.0 / (1.0 + e2)
    comb_ref[0] = jnp.where(lane == i1, w1, 0.0) + jnp.where(lane == i2, e2 * w1, 0.0)
    sel = (lane == i1) | (lane == i2)
    tri = jnp.where(lax.broadcasted_iota(jnp.int32, (win, win), 0)
                    > lax.broadcasted_iota(jnp.int32, (win, win), 1), 1.0, 0.0).astype(BF16)
    base = jnp.zeros((1, LANES), F32)
    bases = []
    for w in range(tm // win):
        rows = slice(w * win, (w + 1) * win)
        sw = jnp.where(sel[rows], 1.0, 0.0)
        excl = jnp.dot(tri, sw.astype(BF16), preferred_element_type=F32)
        rank_ref[0, rows, :] = jnp.where(sel[rows], excl + base, -1.0)
        bases.append(base)
        base = base + jnp.sum(sw, axis=0, keepdims=True)
    bases.append(base)
    bases += [jnp.zeros((1, LANES), F32)] * (CNT_ROWS - len(bases))
    cnt_ref[0] = jnp.concatenate(bases, axis=0).astype(jnp.int32)
    rankt_ref[0] = rank_ref[0].T[:CNT_ROWS]


def _route_call(h, w_router, tm):
    nb, n, d = h.shape
    n_experts = w_router.shape[1]
    assert n_experts <= CNT_ROWS and tm // MOE_WIN + 1 <= CNT_ROWS
    nt = n // tm
    wr = jnp.zeros((d, LANES), BF16).at[:, :n_experts].set(w_router.astype(BF16))
    tok = pl.BlockSpec((1, tm, LANES), lambda b, i: (b, i, 0))
    per_tile = lambda r, c: pl.BlockSpec((1, r, c), lambda b, i: (b * nt + i, 0, 0))
    return pl.pallas_call(
        functools.partial(_route_kernel, n_experts=n_experts, win=min(MOE_WIN, tm)),
        grid=(nb, nt),
        in_specs=[pl.BlockSpec((1, tm, d), lambda b, i: (b, i, 0)),
                  pl.BlockSpec((d, LANES), lambda b, i: (0, 0))],
        out_specs=[tok, tok, per_tile(CNT_ROWS, tm), per_tile(CNT_ROWS, LANES)],
        out_shape=[jax.ShapeDtypeStruct((nb, n, LANES), F32), jax.ShapeDtypeStruct((nb, n, LANES), F32),
                   jax.ShapeDtypeStruct((nb * nt, CNT_ROWS, tm), F32),
                   jax.ShapeDtypeStruct((nb * nt, CNT_ROWS, LANES), jnp.int32)],
        compiler_params=_cparams(("parallel", "parallel")),
        name="moe_route",
    )(h, wr)


def _moe_kernel(cnt_ref, x_ref, h_ref, comb_ref, rank_ref, rankt_ref, wg_hbm, wu_hbm, wd_hbm, g2_ref,
                ng_ref, sh_ref, sc_ref, *rest, final, n_experts, win, slab, rblk, unit, ff):
    out_refs, (hc_ref, yc_ref, wg_buf, wu_buf, wd_buf, sem) = rest[:-6], rest[-6:]
    acc_ref = out_refs[0].at[0]
    b, i, e = (pl.program_id(k) for k in range(3))
    tm, d = h_ref.shape[1], h_ref.shape[2]
    nw = tm // win
    nf = wg_hbm.shape[2] // ff
    tile = b * pl.num_programs(1) + i
    cbase = tile * (CNT_ROWS * n_experts)
    step = tile * n_experts + e
    last_step = pl.num_programs(0) * pl.num_programs(1) * n_experts - 1

    last_block = (last_step + 1) * nf - 1
    def weight_copies(expert, f, slot):
        cols = pl.ds(pl.multiple_of(f * ff, ff), ff)
        return (pltpu.make_async_copy(wg_hbm.at[expert, :, cols], wg_buf.at[slot], sem.at[slot, 0]),
                pltpu.make_async_copy(wu_hbm.at[expert, :, cols], wu_buf.at[slot], sem.at[slot, 1]),
                pltpu.make_async_copy(wd_hbm.at[expert, cols, :], wd_buf.at[slot], sem.at[slot, 2]))

    def fetch(expert, f, slot):
        for copy in weight_copies(expert, f, slot):
            copy.start()

    def fetch_block(g_off):
        over = g_off // nf
        fetch(lax.rem(e + over, n_experts), g_off - over * nf,
              lax.rem(step * nf + g_off, MOE_SLOTS))

    @pl.when(step == 0)
    def _():
        for g_off in range(MOE_SLOTS - 1):
            fetch_block(g_off)

    def count(w):
        return cnt_ref[cbase + w * n_experts + e]

    def slabs(w):
        row0 = (count(w) // BF16_SUBLANES) * BF16_SUBLANES
        return row0, (count(w + 1) - row0 + slab - 1) // slab

    @pl.when(e == 0)
    def _():
        acc_ref[...] = jnp.zeros_like(acc_ref)

    total = count(nw)

    def zero(s, carry):
        rows = pl.ds(pl.multiple_of(s * slab, slab), slab)
        hc_ref[rows, :] = jnp.zeros((slab, d), BF16)
        yc_ref[rows, :] = jnp.zeros((slab, d), F32)
        return carry

    lax.fori_loop(0, jnp.minimum((total + unit + 2 * slab) // slab, hc_ref.shape[0] // slab),
                  zero, 0)

    def gather(w):
        row0, nsl = slabs(w)
        rt = rankt_ref[0, 0, :, w * win:(w + 1) * win]
        hw = h_ref[0, w * win:(w + 1) * win, :]

        def body(s, carry):
            r0 = pl.multiple_of(row0 + s * slab, BF16_SUBLANES)
            rid = (r0 + lax.broadcasted_iota(jnp.int32, (slab, win), 0)).astype(F32)
            onehot = jnp.where(rt == rid, 1.0, 0.0).astype(BF16)
            rows = pl.ds(r0, slab)
            got = jnp.dot(onehot, hw, preferred_element_type=F32)
            hc_ref[rows, :] = (hc_ref[rows, :].astype(F32) + got).astype(BF16)
            return carry

        return body, nsl

    for w in range(nw):
        gather(w)[0](0, 0)
    for w in range(nw):
        body, nsl = gather(w)
        lax.fori_loop(1, nsl, body, 0)

    units = (total + unit - 1) // unit
    per_block = rblk // unit
    nfull = units // per_block
    rest = units - per_block * nfull

    def ff_block(f, carry):
        g = step * nf + f
        slot = lax.rem(g, MOE_SLOTS)
        ahead = f + (MOE_SLOTS - 1)
        wraps = ahead >= nf

        @pl.when(g + (MOE_SLOTS - 1) <= last_block)
        def _():
            fetch(jnp.where(wraps, lax.rem(e + 1, n_experts), e),
                  jnp.where(wraps, ahead - nf, ahead), lax.rem(g + (MOE_SLOTS - 1), MOE_SLOTS))

        for copy in weight_copies(e, f, slot):
            copy.wait()

        def ffn(r0, size):
            rows = pl.ds(pl.multiple_of(r0, unit), size)
            hb = hc_ref[rows, :]
            a = jnp.dot(hb, wg_buf[slot], preferred_element_type=F32)
            u = jnp.dot(hb, wu_buf[slot], preferred_element_type=F32)
            t = (_silu(a) * u).astype(BF16)
            yc_ref[rows, :] += jnp.dot(t, wd_buf[slot], preferred_element_type=F32)

        def full(r, c):
            ffn(r * rblk, rblk)
            return c

        lax.fori_loop(0, nfull, full, 0)
        for k in range(1, per_block):
            @pl.when(rest == k)
            def _(k=k):
                ffn(nfull * rblk, k * unit)

        return carry

    lax.fori_loop(0, nf, ff_block, 0)

    def spread(w):
        row0, nsl = slabs(w)
        trows = slice(w * win, (w + 1) * win)
        mine = lax.broadcasted_iota(jnp.int32, (win, LANES), 1) == e
        rcol = jnp.sum(jnp.where(mine, rank_ref[0, trows, :], 0.0), axis=-1, keepdims=True)
        wcol = jnp.sum(jnp.where(mine, comb_ref[0, trows, :], 0.0), axis=-1, keepdims=True)

        def body(s, carry):
            r0 = pl.multiple_of(row0 + s * slab, BF16_SUBLANES)
            cid = (r0 + lax.broadcasted_iota(jnp.int32, (win, slab), 1)).astype(F32)
            onehot = jnp.where(rcol == cid, 1.0, 0.0).astype(BF16)
            yb = yc_ref[pl.ds(r0, slab), :].astype(BF16)
            acc_ref[trows, :] += wcol * jnp.dot(onehot, yb, preferred_element_type=F32)
            return carry

        return body, nsl

    bodies = [spread(w) for w in range(nw)]
    for body, _ in bodies:
        body(0, 0)
    for body, nsl in bodies:
        lax.fori_loop(1, nsl, body, 0)

    @pl.when(e == n_experts - 1)
    def _():
        xn = x_ref[0] + g2_ref[0] * acc_ref[...]
        _finish(xn, final, ng_ref, sh_ref, sc_ref, out_refs)


def _moe_call(x, h, w_router, wg, wu, wd, g2, ng, sh, sc, *, final):
    nb, n, d = x.shape
    n_experts, _, dff = wg.shape
    tm = min(MOE_TILE, n)
    win = min(MOE_WIN, tm)
    nt = n // tm
    comb, rank, rankt, cnt = _route_call(h, w_router, tm)
    cnt = cnt[:, :, :n_experts].reshape(-1)
    rankt = rankt.reshape(nb * nt, CNT_ROWS, 1, tm)
    cap_ffn = -(-tm // MOE_UNIT) * MOE_UNIT
    cap = -(-max(cap_ffn, tm + MOE_SLAB) // MOE_SLAB) * MOE_SLAB

    idx = lambda b, i, e, c: (b, i, 0)
    vec = pl.BlockSpec((1, 1, d), lambda b, i, e, c: (b, 0, 0))
    hbm = pl.BlockSpec(memory_space=pl.ANY)
    out_specs, out_shape = _epilogue_specs(nb, n, d, tm, final, idx)
    grid_spec = pltpu.PrefetchScalarGridSpec(
        num_scalar_prefetch=1,
        grid=(nb, nt, n_experts),
        in_specs=[_resident((1, tm, d), idx), _resident((1, tm, d), idx),
                  _resident((1, tm, LANES), idx), _resident((1, tm, LANES), idx),
                  pl.BlockSpec((1, 1, 1, tm), lambda b, i, e, c: (b * nt + i, e, 0, 0)),
                  hbm, hbm, hbm,
                  vec, pl.BlockSpec((1, d), lambda b, i, e, c: (0, 0)), vec, vec],
        out_specs=out_specs,
        scratch_shapes=[pltpu.VMEM((cap, d), BF16), pltpu.VMEM((cap, d), F32),
                        pltpu.VMEM((MOE_SLOTS, d, MOE_FF), BF16),
                        pltpu.VMEM((MOE_SLOTS, d, MOE_FF), BF16),
                        pltpu.VMEM((MOE_SLOTS, MOE_FF, d), BF16),
                        pltpu.SemaphoreType.DMA((MOE_SLOTS, 3))],
    )
    return pl.pallas_call(
        functools.partial(_moe_kernel, final=final, n_experts=n_experts, win=win, slab=MOE_SLAB,
                          rblk=MOE_ROWS, unit=MOE_UNIT, ff=MOE_FF),
        grid_spec=grid_spec,
        out_shape=out_shape,
        compiler_params=_cparams(("arbitrary", "arbitrary", "arbitrary"), MOE_VMEM_LIMIT),
        name="moe_swiglu",
    )(cnt, x, h, comb, rank, rankt, wg, wu, wd, g2, ng.reshape(1, d), sh, sc)


def kernel(x, c, ctx, c_ctx, w_ada, b_ada, norm1_g, norm2_g, w_in, b_gate, a_ln_g, a_ln_b, a_ws,
           a_bs, b_conv, c_lambda, c_subln_g, w_a_out, w_b_out, w_c_out, w_o, ff_w_gate, ff_w_up,
           ff_w_down, moe_w_router, moe_w_gate, moe_w_up, moe_w_down, final_norm_g):
    bsz, n, d = x.shape
    nc = ctx.shape[1]
    depth = w_ada.shape[0]
    cols = _Cols(d)
    head_cb = COL_BLOCK // LANES
    gate_cb = cols.out(cols.gate) * COL_BLOCK // d
    mix_pos = tuple(cols.out(j) for j in (cols.au, cols.av, cols.bb, cols.bc, cols.bh))
    cq_l, ck_l, cv_l = (cols.out(j) * head_cb for j in (cols.cq, cols.ck, cols.cv))

    pad = (-(bsz + 1)) % 8
    cc = jnp.concatenate([c, c_ctx[None], jnp.zeros((pad, d), F32)], axis=0)
    mod = _ada_call(cc, w_ada, b_ada)

    def mods(l):
        lat = [mod[l, :bsz, k * d:(k + 1) * d].reshape(bsz, 1, d) for k in range(N_MOD)]
        con = [mod[l, bsz:bsz + 1, k * d:(k + 1) * d].reshape(1, 1, d) for k in range(N_MOD)]
        return lat, con

    tables = _rope_tables(n)
    w_in_b = w_in.astype(BF16)
    xl = x
    xc = ctx.reshape(1, bsz * nc, d)
    lat, con = mods(0)
    h = _mod_call(xl, norm1_g[0], lat[0], lat[1])
    hc = _mod_call(xc, norm1_g[0], con[0], con[1])

    for l in range(depth):
        last = l == depth - 1
        lam_init = 0.8 - 0.6 * math.exp(-0.3 * l)
        lat, con = mods(l)
        if not last:
            nlat, ncon = mods(l + 1)
            nxt_l = (norm1_g[l + 1], nlat[0], nlat[1])
            nxt_c = (norm1_g[l + 1], ncon[0], ncon[1])
        else:
            nxt_l = (final_norm_g, lat[0], lat[1])
            nxt_c = None

        p = _in_call(h, w_in_b[l], b_gate[l], tables, j0=0, nj=cols.end, rope=True)
        if last:
            pc = _in_call(hc, w_in_b[l], b_gate[l], tables, j0=cols.ck, nj=cols.gate - cols.ck,
                          rope=False)
            ck_c, cv_c = 0, (cols.cv - cols.ck) * head_cb
        else:
            pc = _in_call(hc, w_in_b[l], b_gate[l], tables, j0=0, nj=cols.end, rope=False)
            ck_c, cv_c = ck_l, cv_l
        pc_seq = pc.reshape(bsz, nc, -1)

        wa, wb, wc, wo = (w.astype(BF16) for w in (w_a_out[l], w_b_out[l], w_c_out[l], w_o[l]))

        def channel(xs, hs, g2, nxt, final):
            i = l // 2
            if l % 2 == 0:
                return _ffn_call(xs, hs, ff_w_gate[i].astype(BF16), ff_w_up[i].astype(BF16),
                                 ff_w_down[i].astype(BF16), g2, *nxt, final=final)
            return _moe_call(xs, hs, moe_w_router[i], moe_w_gate[i].astype(BF16),
                             moe_w_up[i].astype(BF16), moe_w_down[i].astype(BF16), g2, *nxt,
                             final=final)

        mixer_w = (a_ln_g[l], a_ln_b[l], a_ws[l], a_bs[l], b_conv[l])
        yc = _attn_call(p, cq_l, [(p, ck_l, cv_l), (pc_seq, ck_c, cv_c)],
                        c_lambda[l], c_subln_g[l], lam_init=lam_init)
        x1, h2 = _merge_call(xl, p, mix_pos, yc, gate_cb, mixer_w, wa, wb, wc, wo, lat[2],
                             norm2_g[l], lat[3], lat[4], seq=n)
        res = channel(x1, h2, lat[5], nxt_l, last)
        if last:
            return res[0]
        xl, h = res

        yc_c = _attn_call(pc_seq, cq_l, [(pc_seq, ck_c, cv_c)], c_lambda[l], c_subln_g[l],
                          lam_init=lam_init)
        xc1, hc2 = _merge_call(xc, pc, mix_pos, yc_c.reshape(1, bsz * nc, -1), gate_cb, mixer_w,
                               wa, wb, wc, wo, con[2], norm2_g[l], con[3], con[4], seq=nc)
        xc, hc = channel(xc1, hc2, con[5], nxt_c, False)
```
